```python
import math
import jax, jax.numpy as jnp
from jax import lax
import numpy as np

D_MODEL = 1024
BATCH = 8
SEQ = 8192
DEPTH = 4

ATT_HEADS = 8
KV_HEADS = 2
GQA_GROUP = ATT_HEADS // KV_HEADS
HEAD_DIM = 64
ATT_WIDTH = ATT_HEADS * HEAD_DIM
KV_WIDTH = KV_HEADS * HEAD_DIM
WINDOW = 128
BLOCK = 128
SSM_WIDTH = D_MODEL - ATT_WIDTH
SSM_GROUP = 16
SSM_GROUPS = SSM_WIDTH // SSM_GROUP
SSM_STATE = 64
DT_MIN = 1e-3
DT_MAX = 1e-1
MIX_WIDTH = ATT_WIDTH + SSM_WIDTH
IN_WIDTH = ATT_WIDTH + 2 * KV_WIDTH + SSM_WIDTH
D_FF = 4 * D_MODEL
EPS = 1e-6

kernel_name = "hymba_style_swa_s5_hybrid_encoder"


def rms_norm(x, gain):
    xf = x.astype(jnp.float32)
    y = xf * lax.rsqrt(jnp.mean(xf * xf, axis=-1, keepdims=True) + EPS)
    return (y * gain.astype(jnp.float32)).astype(x.dtype)


def alibi_slopes(n_heads):
    return jnp.exp2(-8.0 * jnp.arange(1, n_heads + 1, dtype=jnp.float32) / n_heads)


def windowed_gqa(q, k, v, q_gain, k_gain, sink):
    bsz, seq = q.shape[0], q.shape[1]
    nb = seq // BLOCK
    q = rms_norm(q, q_gain)
    k = rms_norm(k, k_gain)
    qb = q.reshape(bsz, nb, BLOCK, KV_HEADS, GQA_GROUP, HEAD_DIM)

    def band(t):
        tp = jnp.pad(t, ((0, 0), (BLOCK, BLOCK), (0, 0), (0, 0)))
        tb = tp.reshape(bsz, nb + 2, BLOCK, KV_HEADS, HEAD_DIM)
        return jnp.concatenate([tb[:, :-2], tb[:, 1:-1], tb[:, 2:]], axis=2)

    kb, vb = band(k), band(v)
    scores = jnp.einsum('bnqkgd,bnckd->bnkgqc', qb, kb,
                        preferred_element_type=jnp.float32) / math.sqrt(HEAD_DIM)
    q_idx = jnp.arange(BLOCK)[:, None]
    c_idx = jnp.arange(3 * BLOCK)[None, :]
    dist = jnp.abs(q_idx - c_idx + BLOCK)
    key_pos = (jnp.arange(nb)[:, None] - 1) * BLOCK + jnp.arange(3 * BLOCK)[None, :]
    valid = (dist <= WINDOW)[None] & ((key_pos >= 0) & (key_pos < seq))[:, None, :]
    slopes = alibi_slopes(ATT_HEADS).reshape(KV_HEADS, GQA_GROUP)
    bias = -slopes[:, :, None, None] * dist.astype(jnp.float32)
    neg = jnp.finfo(jnp.float32).min
    scores = jnp.where(valid[None, :, None, None], scores + bias, neg)
    sk = sink.astype(jnp.float32).reshape(1, 1, KV_HEADS, GQA_GROUP, 1, 1)
    m = jnp.maximum(jnp.max(scores, axis=-1, keepdims=True), sk)
    p = jnp.exp(scores - m)
    denom = jnp.sum(p, axis=-1, keepdims=True) + jnp.exp(sk - m)
    out = jnp.einsum('bnkgqc,bnckd->bnqkgd', (p / denom).astype(v.dtype), vb)
    return out.reshape(bsz, seq, ATT_WIDTH)


def complex_diag_scan(a_re, a_im, b_re, b_im, reverse):
    ar = jnp.broadcast_to(a_re, b_re.shape)
    ai = jnp.broadcast_to(a_im, b_re.shape)

    def combine(e1, e2):
        a1r, a1i, b1r, b1i = e1
        a2r, a2i, b2r, b2i = e2
        return (a1r * a2r - a1i * a2i,
                a1r * a2i + a1i * a2r,
                a2r * b1r - a2i * b1i + b2r,
                a2r * b1i + a2i * b1r + b2i)

    _, _, xr, xi = lax.associative_scan(combine, (ar, ai, b_re, b_im), reverse=reverse, axis=1)
    return xr, xi


def s5_mixer(u, lam_re, lam_im, log_dt, b_re, b_im, c_re, c_im, d_skip, w_glu):
    bsz, seq = u.shape[0], u.shape[1]
    uf = u.astype(jnp.float32).reshape(bsz, seq, SSM_GROUPS, SSM_GROUP)
    y = d_skip.astype(jnp.float32).reshape(SSM_GROUPS, SSM_GROUP) * uf
    br = b_re.astype(jnp.float32)
    bi = b_im.astype(jnp.float32)
    for direction, reverse in enumerate((False, True)):
        lr = lam_re[direction].astype(jnp.float32)
        li = lam_im[direction].astype(jnp.float32)
        dt = jnp.exp(log_dt[direction].astype(jnp.float32))[:, None]
        mag = jnp.exp(lr * dt)
        abr = mag * jnp.cos(li * dt)
        abi = mag * jnp.sin(li * dt)
        den = lr * lr + li * li
        zr = ((abr - 1.0) * lr + abi * li) / den
        zi = (abi * lr - (abr - 1.0) * li) / den
        bbr = zr[..., None] * br - zi[..., None] * bi
        bbi = zr[..., None] * bi + zi[..., None] * br
        bur = jnp.einsum('bsgh,gph->bsgp', uf, bbr)
        bui = jnp.einsum('bsgh,gph->bsgp', uf, bbi)
        xr, xi = complex_diag_scan(abr, abi, bur, bui, reverse)
        y = (y + jnp.einsum('bsgp,ghp->bsgh', xr, c_re[direction].astype(jnp.float32))
             - jnp.einsum('bsgp,ghp->bsgh', xi, c_im[direction].astype(jnp.float32)))
    y = jax.nn.gelu(y).reshape(bsz, seq, SSM_WIDTH).astype(u.dtype)
    g_val, g_gate = jnp.split(y @ w_glu, 2, axis=-1)
    return g_val * jax.nn.sigmoid(g_gate)


def _fwd_setup_inputs(seed: int = 0) -> dict:
    key = jax.random.key(seed)
    ks = jax.random.split(key, 20)
    nrm = jax.random.normal
    f32 = jnp.float32
    x = nrm(ks[0], (BATCH, SEQ, D_MODEL), f32)
    norm1 = 1.0 + 0.05 * nrm(ks[1], (DEPTH, D_MODEL), f32)
    w_in = nrm(ks[2], (DEPTH, D_MODEL, IN_WIDTH), f32) * D_MODEL ** -0.5
    q_gain = 1.0 + 0.05 * nrm(ks[3], (DEPTH, HEAD_DIM), f32)
    k_gain = 1.0 + 0.05 * nrm(ks[4], (DEPTH, HEAD_DIM), f32)
    sink = 0.5 * nrm(ks[5], (DEPTH, ATT_HEADS), f32)
    lam_re = -0.5 + 0.01 * nrm(ks[6], (DEPTH, 2, SSM_GROUPS, SSM_STATE), f32)
    lam_im = (math.pi * jnp.arange(SSM_STATE, dtype=f32)
              + 0.01 * nrm(ks[7], (DEPTH, 2, SSM_GROUPS, SSM_STATE), f32))
    log_dt = jax.random.uniform(ks[8], (DEPTH, 2, SSM_GROUPS), f32,
                                minval=math.log(DT_MIN), maxval=math.log(DT_MAX))
    b_re = nrm(ks[9], (DEPTH, SSM_GROUPS, SSM_STATE, SSM_GROUP), f32) * (2 * SSM_GROUP) ** -0.5
    b_im = nrm(ks[10], (DEPTH, SSM_GROUPS, SSM_STATE, SSM_GROUP), f32) * (2 * SSM_GROUP) ** -0.5
    c_re = nrm(ks[11], (DEPTH, 2, SSM_GROUPS, SSM_GROUP, SSM_STATE), f32) * SSM_STATE ** -0.5
    c_im = nrm(ks[12], (DEPTH, 2, SSM_GROUPS, SSM_GROUP, SSM_STATE), f32) * SSM_STATE ** -0.5
    d_skip = nrm(ks[13], (DEPTH, SSM_WIDTH), f32)
    w_glu = nrm(ks[14], (DEPTH, SSM_WIDTH, 2 * SSM_WIDTH), f32) * SSM_WIDTH ** -0.5
    w_out = nrm(ks[15], (DEPTH, MIX_WIDTH, D_MODEL), f32) * (0.5 * MIX_WIDTH ** -0.5)
    norm2 = 1.0 + 0.05 * nrm(ks[16], (DEPTH, D_MODEL), f32)
    w_ff1 = nrm(ks[17], (DEPTH, D_MODEL, D_FF), f32) * D_MODEL ** -0.5
    w_ff2 = nrm(ks[18], (DEPTH, D_FF, D_MODEL), f32) * (0.5 * D_FF ** -0.5)
    return {"x": x, "norm1": norm1, "w_in": w_in, "q_gain": q_gain, "k_gain": k_gain,
            "sink": sink, "lam_re": lam_re, "lam_im": lam_im, "log_dt": log_dt,
            "b_re": b_re, "b_im": b_im, "c_re": c_re, "c_im": c_im, "d_skip": d_skip,
            "w_glu": w_glu, "w_out": w_out, "norm2": norm2, "w_ff1": w_ff1, "w_ff2": w_ff2}


def _fwd_reference(x, norm1, w_in, q_gain, k_gain, sink, lam_re, lam_im, log_dt,
              b_re, b_im, c_re, c_im, d_skip, w_glu, w_out, norm2, w_ff1, w_ff2):
    bsz, seq = x.shape[0], x.shape[1]
    q_end = ATT_WIDTH
    k_end = q_end + KV_WIDTH
    v_end = k_end + KV_WIDTH
    for layer in range(DEPTH):
        h = rms_norm(x, norm1[layer])
        z = h @ w_in[layer]
        q = z[..., :q_end].reshape(bsz, seq, ATT_HEADS, HEAD_DIM)
        k = z[..., q_end:k_end].reshape(bsz, seq, KV_HEADS, HEAD_DIM)
        v = z[..., k_end:v_end].reshape(bsz, seq, KV_HEADS, HEAD_DIM)
        u = z[..., v_end:]
        att = windowed_gqa(q, k, v, q_gain[layer], k_gain[layer], sink[layer])
        ssm = s5_mixer(u, lam_re[layer], lam_im[layer], log_dt[layer], b_re[layer], b_im[layer],
                       c_re[layer], c_im[layer], d_skip[layer], w_glu[layer])
        x = x + jnp.concatenate([att, ssm], axis=-1) @ w_out[layer]
        h = rms_norm(x, norm2[layer])
        x = x + jnp.square(jax.nn.relu(h @ w_ff1[layer])) @ w_ff2[layer]
    return x


import jax as _jax
import jax.numpy as _jnp

TWIN_FORMAT = 'train_step'
FWD_PARAMS = ['x', 'norm1', 'w_in', 'q_gain', 'k_gain', 'sink', 'lam_re', 'lam_im', 'log_dt', 'b_re', 'b_im', 'c_re', 'c_im', 'd_skip', 'w_glu', 'w_out', 'norm2', 'w_ff1', 'w_ff2']
TWIN_WEIGHTS = ['norm1', 'w_in', 'q_gain', 'k_gain', 'sink', 'lam_re', 'lam_im', 'log_dt', 'b_re', 'b_im', 'c_re', 'c_im', 'd_skip', 'w_glu', 'w_out', 'norm2', 'w_ff1', 'w_ff2']
TWIN_DIFF_INPUT = 'x'
TWIN_INPUTS = ['x', 'norm1', 'w_in', 'q_gain', 'k_gain', 'sink', 'lam_re', 'lam_im', 'log_dt', 'b_re', 'b_im', 'c_re', 'c_im', 'd_skip', 'w_glu', 'w_out', 'norm2', 'w_ff1', 'w_ff2', 'loss_target', 'm_norm1', 'm_w_in', 'm_q_gain', 'm_k_gain', 'm_sink', 'm_lam_re', 'm_lam_im', 'm_log_dt', 'm_b_re', 'm_b_im', 'm_c_re', 'm_c_im', 'm_d_skip', 'm_w_glu', 'm_w_out', 'm_norm2', 'm_w_ff1', 'm_w_ff2', 'v_norm1', 'v_w_in', 'v_q_gain', 'v_k_gain', 'v_sink', 'v_lam_re', 'v_lam_im', 'v_log_dt', 'v_b_re', 'v_b_im', 'v_c_re', 'v_c_im', 'v_d_skip', 'v_w_glu', 'v_w_out', 'v_norm2', 'v_w_ff1', 'v_w_ff2']
TWIN_OUTPUTS = ['loss', 'grad_x', 'grad_norm1', 'grad_w_in', 'grad_q_gain', 'grad_k_gain', 'grad_sink', 'grad_lam_re', 'grad_lam_im', 'grad_log_dt', 'grad_b_re', 'grad_b_im', 'grad_c_re', 'grad_c_im', 'grad_d_skip', 'grad_w_glu', 'grad_w_out', 'grad_norm2', 'grad_w_ff1', 'grad_w_ff2', 'delta_norm1', 'delta_w_in', 'delta_q_gain', 'delta_k_gain', 'delta_sink', 'delta_lam_re', 'delta_lam_im', 'delta_log_dt', 'delta_b_re', 'delta_b_im', 'delta_c_re', 'delta_c_im', 'delta_d_skip', 'delta_w_glu', 'delta_w_out', 'delta_norm2', 'delta_w_ff1', 'delta_w_ff2', 'new_m_norm1', 'new_m_w_in', 'new_m_q_gain', 'new_m_k_gain', 'new_m_sink', 'new_m_lam_re', 'new_m_lam_im', 'new_m_log_dt', 'new_m_b_re', 'new_m_b_im', 'new_m_c_re', 'new_m_c_im', 'new_m_d_skip', 'new_m_w_glu', 'new_m_w_out', 'new_m_norm2', 'new_m_w_ff1', 'new_m_w_ff2', 'new_v_norm1', 'new_v_w_in', 'new_v_q_gain', 'new_v_k_gain', 'new_v_sink', 'new_v_lam_re', 'new_v_lam_im', 'new_v_log_dt', 'new_v_b_re', 'new_v_b_im', 'new_v_c_re', 'new_v_c_im', 'new_v_d_skip', 'new_v_w_glu', 'new_v_w_out', 'new_v_norm2', 'new_v_w_ff1', 'new_v_w_ff2']
TWIN_LEAF_KINDS = {'loss': 'loss', 'grad_x': 'grad_x', 'grad_norm1': 'grad_w', 'grad_w_in': 'grad_w', 'grad_q_gain': 'grad_w', 'grad_k_gain': 'grad_w', 'grad_sink': 'grad_w', 'grad_lam_re': 'grad_w', 'grad_lam_im': 'grad_w', 'grad_log_dt': 'grad_w', 'grad_b_re': 'grad_w', 'grad_b_im': 'grad_w', 'grad_c_re': 'grad_w', 'grad_c_im': 'grad_w', 'grad_d_skip': 'grad_w', 'grad_w_glu': 'grad_w', 'grad_w_out': 'grad_w', 'grad_norm2': 'grad_w', 'grad_w_ff1': 'grad_w', 'grad_w_ff2': 'grad_w', 'delta_norm1': 'delta_w', 'delta_w_in': 'delta_w', 'delta_q_gain': 'delta_w', 'delta_k_gain': 'delta_w', 'delta_sink': 'delta_w', 'delta_lam_re': 'delta_w', 'delta_lam_im': 'delta_w', 'delta_log_dt': 'delta_w', 'delta_b_re': 'delta_w', 'delta_b_im': 'delta_w', 'delta_c_re': 'delta_w', 'delta_c_im': 'delta_w', 'delta_d_skip': 'delta_w', 'delta_w_glu': 'delta_w', 'delta_w_out': 'delta_w', 'delta_norm2': 'delta_w', 'delta_w_ff1': 'delta_w', 'delta_w_ff2': 'delta_w', 'new_m_norm1': 'new_m', 'new_m_w_in': 'new_m', 'new_m_q_gain': 'new_m', 'new_m_k_gain': 'new_m', 'new_m_sink': 'new_m', 'new_m_lam_re': 'new_m', 'new_m_lam_im': 'new_m', 'new_m_log_dt': 'new_m', 'new_m_b_re': 'new_m', 'new_m_b_im': 'new_m', 'new_m_c_re': 'new_m', 'new_m_c_im': 'new_m', 'new_m_d_skip': 'new_m', 'new_m_w_glu': 'new_m', 'new_m_w_out': 'new_m', 'new_m_norm2': 'new_m', 'new_m_w_ff1': 'new_m', 'new_m_w_ff2': 'new_m', 'new_v_norm1': 'new_v', 'new_v_w_in': 'new_v', 'new_v_q_gain': 'new_v', 'new_v_k_gain': 'new_v', 'new_v_sink': 'new_v', 'new_v_lam_re': 'new_v', 'new_v_lam_im': 'new_v', 'new_v_log_dt': 'new_v', 'new_v_b_re': 'new_v', 'new_v_b_im': 'new_v', 'new_v_c_re': 'new_v', 'new_v_c_im': 'new_v', 'new_v_d_skip': 'new_v', 'new_v_w_glu': 'new_v', 'new_v_w_out': 'new_v', 'new_v_norm2': 'new_v', 'new_v_w_ff1': 'new_v', 'new_v_w_ff2': 'new_v'}


def _forward(args):
    return _fwd_reference(*[args[k] for k in FWD_PARAMS])


def _output_shape():
    def fwd():
        inp = _fwd_setup_inputs(0)
        return _fwd_reference(*[inp[k] for k in FWD_PARAMS])
    out = _jax.eval_shape(fwd)
    return out.shape, out.dtype

N_MICROBATCH = 1
ADAM_LR = 0.001
ADAM_B1 = 0.9
ADAM_B2 = 0.999
ADAM_EPS = 1e-08
ADAM_WD = 0.01
ADAM_STEP = 10
PER_EXAMPLE_BATCH_AXIS = {'x': 0, 'loss_target': 0}
SHARED_INPUTS = []
_WEIGHT_DTYPES = {'norm1': _jnp.float32, 'w_in': _jnp.float32, 'q_gain': _jnp.float32, 'k_gain': _jnp.float32, 'sink': _jnp.float32, 'lam_re': _jnp.float32, 'lam_im': _jnp.float32, 'log_dt': _jnp.float32, 'b_re': _jnp.float32, 'b_im': _jnp.float32, 'c_re': _jnp.float32, 'c_im': _jnp.float32, 'd_skip': _jnp.float32, 'w_glu': _jnp.float32, 'w_out': _jnp.float32, 'norm2': _jnp.float32, 'w_ff1': _jnp.float32, 'w_ff2': _jnp.float32}
MOMENT_SCALE = {'norm1': 4.747565e+00, 'w_in': 3.828157e+00, 'q_gain': 3.460669e+00, 'k_gain': 3.450972e+00, 'sink': 1.160553e+01, 'lam_re': 1.646643e-01, 'lam_im': 1.856934e-01, 'log_dt': 8.245192e+00, 'b_re': 1.720908e-01, 'b_im': 1.823236e-01, 'c_re': 1.868301e-01, 'c_im': 1.795893e-01, 'd_skip': 5.606354e+00, 'w_glu': 3.367603e+00, 'w_out': 1.022628e+01, 'norm2': 4.990290e+01, 'w_ff1': 3.040984e+00, 'w_ff2': 2.576102e+01}


def _to_microbatches(a, axis):
    t = _jnp.moveaxis(a, axis, 0)
    t = t.reshape((N_MICROBATCH, t.shape[0] // N_MICROBATCH) + t.shape[1:])
    return _jnp.moveaxis(t, 1, axis + 1)


def setup_inputs(seed: int = 0) -> dict:
    inp = _fwd_setup_inputs(seed)
    key = _jax.random.fold_in(_jax.random.key(seed), 7919)
    shape, _ = _output_shape()
    out = dict(inp)
    out["loss_target"] = _jax.random.normal(_jax.random.fold_in(key, 0), shape, _jnp.float32)
    for i, name in enumerate(TWIN_WEIGHTS):
        w = inp[name].astype(_jnp.float32)
        if MOMENT_SCALE is None:
            s = _jnp.sqrt(_jnp.mean(_jnp.square(w)) + 1e-30)
        else:
            s = MOMENT_SCALE[name]
        km, kv = _jax.random.split(_jax.random.fold_in(key, i + 1))
        out[name] = w
        out["m_" + name] = s * _jax.random.normal(km, w.shape, _jnp.float32)
        out["v_" + name] = (s * s) * _jax.random.uniform(kv, w.shape, _jnp.float32, 0.5, 1.5)
    if N_MICROBATCH > 1:
        for name, axis in PER_EXAMPLE_BATCH_AXIS.items():
            out[name] = _to_microbatches(out[name], axis)
    return {'x': out['x'], 'norm1': out['norm1'], 'w_in': out['w_in'], 'q_gain': out['q_gain'], 'k_gain': out['k_gain'], 'sink': out['sink'], 'lam_re': out['lam_re'], 'lam_im': out['lam_im'], 'log_dt': out['log_dt'], 'b_re': out['b_re'], 'b_im': out['b_im'], 'c_re': out['c_re'], 'c_im': out['c_im'], 'd_skip': out['d_skip'], 'w_glu': out['w_glu'], 'w_out': out['w_out'], 'norm2': out['norm2'], 'w_ff1': out['w_ff1'], 'w_ff2': out['w_ff2'], 'loss_target': out['loss_target'], 'm_norm1': out['m_norm1'], 'm_w_in': out['m_w_in'], 'm_q_gain': out['m_q_gain'], 'm_k_gain': out['m_k_gain'], 'm_sink': out['m_sink'], 'm_lam_re': out['m_lam_re'], 'm_lam_im': out['m_lam_im'], 'm_log_dt': out['m_log_dt'], 'm_b_re': out['m_b_re'], 'm_b_im': out['m_b_im'], 'm_c_re': out['m_c_re'], 'm_c_im': out['m_c_im'], 'm_d_skip': out['m_d_skip'], 'm_w_glu': out['m_w_glu'], 'm_w_out': out['m_w_out'], 'm_norm2': out['m_norm2'], 'm_w_ff1': out['m_w_ff1'], 'm_w_ff2': out['m_w_ff2'], 'v_norm1': out['v_norm1'], 'v_w_in': out['v_w_in'], 'v_q_gain': out['v_q_gain'], 'v_k_gain': out['v_k_gain'], 'v_sink': out['v_sink'], 'v_lam_re': out['v_lam_re'], 'v_lam_im': out['v_lam_im'], 'v_log_dt': out['v_log_dt'], 'v_b_re': out['v_b_re'], 'v_b_im': out['v_b_im'], 'v_c_re': out['v_c_re'], 'v_c_im': out['v_c_im'], 'v_d_skip': out['v_d_skip'], 'v_w_glu': out['v_w_glu'], 'v_w_out': out['v_w_out'], 'v_norm2': out['v_norm2'], 'v_w_ff1': out['v_w_ff1'], 'v_w_ff2': out['v_w_ff2']}


def _loss(weights, diff, rest, loss_target):
    with _jax.named_scope("forward"):
        args = {**rest, TWIN_DIFF_INPUT: diff, **{k: w.astype(_WEIGHT_DTYPES[k]) for k, w in weights.items()}}
        y = _forward(args)
    with _jax.named_scope("loss_head"):
        err = _jnp.square(y.astype(_jnp.float32) - loss_target)
        return 0.5 * _jnp.sum(_jnp.mean(err, axis=-1)) if err.ndim else 0.5 * err


def _adamw(w, g, m, v):
    m = ADAM_B1 * m + (1.0 - ADAM_B1) * g
    v = ADAM_B2 * v + (1.0 - ADAM_B2) * _jnp.square(g)
    m_hat = m / (1.0 - ADAM_B1 ** ADAM_STEP)
    v_hat = v / (1.0 - ADAM_B2 ** ADAM_STEP)
    delta = -ADAM_LR * (m_hat / (_jnp.sqrt(v_hat) + ADAM_EPS) + ADAM_WD * w)
    return delta, m, v


def reference(x, norm1, w_in, q_gain, k_gain, sink, lam_re, lam_im, log_dt, b_re, b_im, c_re, c_im, d_skip, w_glu, w_out, norm2, w_ff1, w_ff2, loss_target, m_norm1, m_w_in, m_q_gain, m_k_gain, m_sink, m_lam_re, m_lam_im, m_log_dt, m_b_re, m_b_im, m_c_re, m_c_im, m_d_skip, m_w_glu, m_w_out, m_norm2, m_w_ff1, m_w_ff2, v_norm1, v_w_in, v_q_gain, v_k_gain, v_sink, v_lam_re, v_lam_im, v_log_dt, v_b_re, v_b_im, v_c_re, v_c_im, v_d_skip, v_w_glu, v_w_out, v_norm2, v_w_ff1, v_w_ff2):
    given = dict(x=x, norm1=norm1, w_in=w_in, q_gain=q_gain, k_gain=k_gain, sink=sink, lam_re=lam_re, lam_im=lam_im, log_dt=log_dt, b_re=b_re, b_im=b_im, c_re=c_re, c_im=c_im, d_skip=d_skip, w_glu=w_glu, w_out=w_out, norm2=norm2, w_ff1=w_ff1, w_ff2=w_ff2, loss_target=loss_target, m_norm1=m_norm1, m_w_in=m_w_in, m_q_gain=m_q_gain, m_k_gain=m_k_gain, m_sink=m_sink, m_lam_re=m_lam_re, m_lam_im=m_lam_im, m_log_dt=m_log_dt, m_b_re=m_b_re, m_b_im=m_b_im, m_c_re=m_c_re, m_c_im=m_c_im, m_d_skip=m_d_skip, m_w_glu=m_w_glu, m_w_out=m_w_out, m_norm2=m_norm2, m_w_ff1=m_w_ff1, m_w_ff2=m_w_ff2, v_norm1=v_norm1, v_w_in=v_w_in, v_q_gain=v_q_gain, v_k_gain=v_k_gain, v_sink=v_sink, v_lam_re=v_lam_re, v_lam_im=v_lam_im, v_log_dt=v_log_dt, v_b_re=v_b_re, v_b_im=v_b_im, v_c_re=v_c_re, v_c_im=v_c_im, v_d_skip=v_d_skip, v_w_glu=v_w_glu, v_w_out=v_w_out, v_norm2=v_norm2, v_w_ff1=v_w_ff1, v_w_ff2=v_w_ff2)
    weights = {n: given[n] for n in TWIN_WEIGHTS}
    shared = {n: given[n] for n in SHARED_INPUTS}
    per_example = {n: given[n] for n in ['x']}
    grad_fn = _jax.value_and_grad(_loss, argnums=(0, 1))

    def one_microbatch(ex, loss_target):
        ex = dict(ex)
        diff = ex.pop(TWIN_DIFF_INPUT)
        return grad_fn(weights, diff, {**shared, **ex}, loss_target)

    if N_MICROBATCH == 1:
        loss, (grad_w, grad_x) = one_microbatch(per_example, given["loss_target"])
    else:
        def body(carry, xs):
            loss_sum, grad_sum = carry
            l_k, (gw_k, gx_k) = one_microbatch(xs[0], xs[1])
            with _jax.named_scope("update"):
                return (loss_sum + l_k, _jax.tree.map(_jnp.add, grad_sum, gw_k)), gx_k

        init = (_jnp.zeros((), _jnp.float32), _jax.tree.map(_jnp.zeros_like, weights))
        (loss, grad_w), grad_x = _jax.lax.scan(body, init, (per_example, given["loss_target"]))
    with _jax.named_scope("update"):
        delta_w, new_m, new_v = {}, {}, {}
        for n in TWIN_WEIGHTS:
            delta_w[n], new_m[n], new_v[n] = _adamw(weights[n], grad_w[n], given["m_" + n], given["v_" + n])
    return (loss, grad_x, *[grad_w[n] for n in TWIN_WEIGHTS], *[delta_w[n] for n in TWIN_WEIGHTS],
            *[new_m[n] for n in TWIN_WEIGHTS], *[new_v[n] for n in TWIN_WEIGHTS])
```

```python
import numpy as np
import jax
import jax.numpy as jnp
from jax import lax
from jax.experimental import pallas as pl
from jax.experimental.pallas import tpu as pltpu

F32, BF16 = jnp.float32, jnp.bfloat16
EPS = 1e-6
D_MODEL = 1024
DEPTH = 4
ATT_HEADS, KV_HEADS, GQA_GROUP, HEAD_DIM = 8, 2, 4, 64
ATT_W, KV_W, SSM_W = 512, 128, 512
Q_END, K_END, V_END, IN_W = 512, 640, 768, 1280
BLK = 128
SSM_G, SSM_H, SSM_P = 32, 16, 64
CH = 16
SEGS = 8
N_DEV = 8
NEG = float(np.finfo(np.float32).min)
SLOPES = tuple(2.0 ** (-8.0 * (h + 1) / ATT_HEADS) for h in range(ATT_HEADS))
VMEM_LIMIT = 56 * 1024 * 1024
TM = 512

ADAM_LR, ADAM_B1, ADAM_B2, ADAM_EPS, ADAM_WD, ADAM_STEP = 0.001, 0.9, 0.999, 1e-08, 0.01, 10

SMALL = ("norm1", "q_gain", "k_gain", "sink", "lam_re", "lam_im", "log_dt", "b_re", "b_im",
         "c_re", "c_im", "d_skip", "norm2")
BIG = ("w_in", "w_glu", "w_out", "w_ff1", "w_ff2")
WEIGHTS = ("norm1", "w_in", "q_gain", "k_gain", "sink", "lam_re", "lam_im", "log_dt", "b_re", "b_im",
           "c_re", "c_im", "d_skip", "w_glu", "w_out", "norm2", "w_ff1", "w_ff2")
SMALL_ROWS = 6656


def _params(*sem):
    return pltpu.CompilerParams(dimension_semantics=sem, vmem_limit_bytes=VMEM_LIMIT)


def _dot(a, b):
    return jnp.dot(a, b, preferred_element_type=F32)


def _dot_nt(a, b):
    return lax.dot_general(a, b, (((1,), (1,)), ((), ())), preferred_element_type=F32)


def _dot_tn(a, b):
    return lax.dot_general(a, b, (((0,), (0,)), ((), ())), preferred_element_type=F32)


def _rms(x):
    return lax.rsqrt(jnp.mean(x * x, axis=-1, keepdims=True) + EPS)


def _rms_bwd(xhat, r, dxhat):
    return r * (dxhat - xhat * jnp.mean(dxhat * xhat, axis=-1, keepdims=True))


def _sigmoid(x):
    return 1.0 / (1.0 + jnp.exp(-x))


_GC = 0.7978845608028654
_GA = 0.044715


def _gelu(x):
    return 0.5 * x * (1.0 + jnp.tanh(_GC * (x + _GA * x * x * x)))


def _gelu_grad(x):
    t = jnp.tanh(_GC * (x + _GA * x * x * x))
    return 0.5 * (1.0 + t) + 0.5 * x * (1.0 - t * t) * _GC * (1.0 + 3.0 * _GA * x * x)


def _row(i):
    return (i, 0)


def _fixed(i):
    return (0, 0)


def norm_matmul(x, gain, w, name):
    s, d = x.shape
    n = w.shape[1]

    def body(x_ref, g_ref, w_ref, z_ref):
        xv = x_ref[...]
        h = (xv * _rms(xv) * g_ref[...]).astype(BF16)
        z_ref[...] = _dot(h, w_ref[...])

    return pl.pallas_call(
        body, name=name, grid=(s // TM,),
        in_specs=[pl.BlockSpec((TM, d), _row), pl.BlockSpec((1, d), _fixed), pl.BlockSpec((d, n), _fixed)],
        out_specs=pl.BlockSpec((TM, n), _row),
        out_shape=jax.ShapeDtypeStruct((s, n), F32),
        compiler_params=_params("parallel"),
    )(x, gain, w)


def in_bwd(x, gain, w, dq, dk, dv, du_a, du_b, dres, name):
    s, d = x.shape
    n = w.shape[1]

    def body(x_ref, g_ref, w_ref, dq_ref, dk_ref, dv_ref, dua_ref, dub_ref, dres_ref, dx_ref, dw_ref, dg_ref):
        i = pl.program_id(0)

        @pl.when(i == 0)
        def _():
            dw_ref[...] = jnp.zeros_like(dw_ref)
            dg_ref[...] = jnp.zeros_like(dg_ref)

        xv = x_ref[...]
        r = _rms(xv)
        xhat = xv * r
        g = g_ref[...]
        h = (xhat * g).astype(BF16)
        dz = jnp.concatenate([dq_ref[...].astype(BF16), dk_ref[...].astype(BF16), dv_ref[...].astype(BF16),
                              (dua_ref[...] + dub_ref[...]).astype(BF16)], axis=1)
        dh = _dot_nt(dz, w_ref[...])
        dw_ref[...] += _dot_tn(h, dz)
        dg_ref[...] += jnp.sum(dh * xhat, axis=0, keepdims=True)
        dx_ref[...] = dres_ref[...] + _rms_bwd(xhat, r, dh * g)

    return pl.pallas_call(
        body, name=name, grid=(s // TM,),
        in_specs=[pl.BlockSpec((TM, d), _row), pl.BlockSpec((1, d), _fixed), pl.BlockSpec((d, n), _fixed),
                  pl.BlockSpec((TM, ATT_W), _row), pl.BlockSpec((TM, KV_W), _row), pl.BlockSpec((TM, KV_W), _row),
                  pl.BlockSpec((TM, SSM_W), _row), pl.BlockSpec((TM, SSM_W), _row), pl.BlockSpec((TM, d), _row)],
        out_specs=[pl.BlockSpec((TM, d), _row), pl.BlockSpec((d, n), _fixed), pl.BlockSpec((1, d), _fixed)],
        out_shape=[jax.ShapeDtypeStruct((s, d), F32), jax.ShapeDtypeStruct((d, n), F32),
                   jax.ShapeDtypeStruct((1, d), F32)],
        compiler_params=_params("arbitrary"),
    )(x, gain, w, dq, dk, dv, du_a, du_b, dres)


def _band_specs(nb):
    kcol, vcol = Q_END // BLK, K_END // BLK

    def prev(i):
        return jnp.maximum(i - 1, 0)

    def nxt(i):
        return jnp.minimum(i + 1, nb - 1)

    specs = [pl.BlockSpec((BLK, ATT_W), _row)]
    for col in (kcol, vcol):
        specs += [pl.BlockSpec((BLK, KV_W), lambda i, c=col: (prev(i), c)),
                  pl.BlockSpec((BLK, KV_W), lambda i, c=col: (i, c)),
                  pl.BlockSpec((BLK, KV_W), lambda i, c=col: (nxt(i), c))]
    return specs


def _band_mask(i, s):
    qi = lax.broadcasted_iota(jnp.int32, (BLK, 3 * BLK), 0)
    ci = lax.broadcasted_iota(jnp.int32, (BLK, 3 * BLK), 1)
    dist = jnp.abs(qi - ci + BLK)
    kpos = (i - 1) * BLK + ci
    ok = jnp.where(dist <= BLK, jnp.where(kpos >= 0, jnp.where(kpos < s, 1, 0), 0), 0)
    return ok > 0, dist.astype(F32)


def attn_fwd(z, qg, kg, sink, name):
    s = z.shape[0]
    nb = s // BLK

    def body(sink_ref, q_ref, kp, kc, kn, vp, vc, vn, qg_ref, kg_ref, o_ref, lse_ref):
        i = pl.program_id(0)
        valid, distf = _band_mask(i, s)
        k3 = jnp.concatenate([kp[...], kc[...], kn[...]], axis=0)
        v3 = jnp.concatenate([vp[...], vc[...], vn[...]], axis=0).astype(BF16)
        q = q_ref[...]
        for j in range(KV_HEADS):
            kj = k3[:, j * HEAD_DIM:(j + 1) * HEAD_DIM]
            knj = (kj * _rms(kj) * kg_ref[...]).astype(BF16)
            vj = v3[:, j * HEAD_DIM:(j + 1) * HEAD_DIM]
            for g in range(GQA_GROUP):
                h = j * GQA_GROUP + g
                qh = q[:, h * HEAD_DIM:(h + 1) * HEAD_DIM]
                qn = (qh * _rms(qh) * qg_ref[...]).astype(BF16)
                sc = _dot_nt(qn, knj) * 0.125 - SLOPES[h] * distf
                sc = jnp.where(valid, sc, NEG)
                sk = sink_ref[h]
                m = jnp.maximum(jnp.max(sc, axis=-1, keepdims=True), sk)
                p = jnp.exp(sc - m)
                den = jnp.sum(p, axis=-1, keepdims=True) + jnp.exp(sk - m)
                o_ref[:, h * HEAD_DIM:(h + 1) * HEAD_DIM] = _dot((p / den).astype(BF16), vj)
                lse_ref[:, h:h + 1] = m + jnp.log(den)

    return pl.pallas_call(
        body, name=name, grid=(nb,),
        in_specs=[pl.BlockSpec(memory_space=pltpu.SMEM)] + _band_specs(nb)
        + [pl.BlockSpec((1, HEAD_DIM), _fixed), pl.BlockSpec((1, HEAD_DIM), _fixed)],
        out_specs=[pl.BlockSpec((BLK, ATT_W), _row), pl.BlockSpec((BLK, ATT_HEADS), _row)],
        out_shape=[jax.ShapeDtypeStruct((s, ATT_W), F32), jax.ShapeDtypeStruct((s, ATT_HEADS), F32)],
        compiler_params=_params("parallel"),
    )(sink, z, z, z, z, z, z, z, qg, kg)


def attn_bwd(z, att, datt, lse, qg, kg, sink, name):
    s = z.shape[0]
    nb = s // BLK
    sp = s + 2 * BLK

    def body(sink_ref, q_ref, kp, kc, kn, vp, vc, vn, o_ref, do_ref, lse_ref, qg_ref, kg_ref,
             dq_ref, dk_ref, dv_ref, dqg_ref, dkg_ref, dsk_ref):
        i = pl.program_id(0)

        @pl.when(i == 0)
        def _():
            dk_ref[...] = jnp.zeros_like(dk_ref)
            dv_ref[...] = jnp.zeros_like(dv_ref)
            dqg_ref[...] = jnp.zeros_like(dqg_ref)
            dkg_ref[...] = jnp.zeros_like(dkg_ref)
            dsk_ref[...] = jnp.zeros_like(dsk_ref)

        valid, distf = _band_mask(i, s)
        k3 = jnp.concatenate([kp[...], kc[...], kn[...]], axis=0)
        v3 = jnp.concatenate([vp[...], vc[...], vn[...]], axis=0).astype(BF16)
        q = q_ref[...]
        o = o_ref[...]
        do = do_ref[...]
        qgv = qg_ref[...]
        kgv = kg_ref[...]
        rows = pl.ds(pl.multiple_of(i * BLK, BLK), 3 * BLK)
        dqg = jnp.zeros((1, HEAD_DIM), F32)
        dkg = jnp.zeros((1, HEAD_DIM), F32)
        for j in range(KV_HEADS):
            cols = slice(j * HEAD_DIM, (j + 1) * HEAD_DIM)
            kj = k3[:, cols]
            rk = _rms(kj)
            khat = kj * rk
            knj = (khat * kgv).astype(BF16)
            vj = v3[:, cols]
            dkn = jnp.zeros((3 * BLK, HEAD_DIM), F32)
            dvj = jnp.zeros((3 * BLK, HEAD_DIM), F32)
            for g in range(GQA_GROUP):
                h = j * GQA_GROUP + g
                hc = slice(h * HEAD_DIM, (h + 1) * HEAD_DIM)
                qh = q[:, hc]
                rq = _rms(qh)
                qhat = qh * rq
                qn = (qhat * qgv).astype(BF16)
                sc = _dot_nt(qn, knj) * 0.125 - SLOPES[h] * distf
                sc = jnp.where(valid, sc, NEG)
                lse_h = lse_ref[:, h:h + 1]
                p = jnp.exp(sc - lse_h)
                doh = do[:, hc]
                delta = jnp.sum(doh * o[:, hc], axis=-1, keepdims=True)
                dob = doh.astype(BF16)
                dp = _dot_nt(dob, vj)
                ds = p * (dp - delta)
                psink = jnp.exp(sink_ref[h] - lse_h)
                dsk_ref[:, h:h + 1] += -jnp.sum(psink * delta, axis=0, keepdims=True)
                dsb = (ds * 0.125).astype(BF16)
                dvj = dvj + _dot_tn(p.astype(BF16), dob)
                dqn = _dot(dsb, knj)
                dkn = dkn + _dot_tn(dsb, qn)
                dqg = dqg + jnp.sum(dqn * qhat, axis=0, keepdims=True)
                dq_ref[:, hc] = _rms_bwd(qhat, rq, dqn * qgv)
            dkg = dkg + jnp.sum(dkn * khat, axis=0, keepdims=True)
            dk_ref[rows, cols] += _rms_bwd(khat, rk, dkn * kgv)
            dv_ref[rows, cols] += dvj
        dqg_ref[...] += dqg
        dkg_ref[...] += dkg

    return pl.pallas_call(
        body, name=name, grid=(nb,),
        in_specs=[pl.BlockSpec(memory_space=pltpu.SMEM)] + _band_specs(nb)
        + [pl.BlockSpec((BLK, ATT_W), _row), pl.BlockSpec((BLK, ATT_W), _row), pl.BlockSpec((BLK, ATT_HEADS), _row),
           pl.BlockSpec((1, HEAD_DIM), _fixed), pl.BlockSpec((1, HEAD_DIM), _fixed)],
        out_specs=[pl.BlockSpec((BLK, ATT_W), _row), pl.BlockSpec((sp, KV_W), _fixed), pl.BlockSpec((sp, KV_W), _fixed),
                   pl.BlockSpec((1, HEAD_DIM), _fixed), pl.BlockSpec((1, HEAD_DIM), _fixed),
                   pl.BlockSpec((1, ATT_HEADS), _fixed)],
        out_shape=[jax.ShapeDtypeStruct((s, ATT_W), F32), jax.ShapeDtypeStruct((sp, KV_W), F32),
                   jax.ShapeDtypeStruct((sp, KV_W), F32), jax.ShapeDtypeStruct((1, HEAD_DIM), F32),
                   jax.ShapeDtypeStruct((1, HEAD_DIM), F32), jax.ShapeDtypeStruct((1, ATT_HEADS), F32)],
        compiler_params=_params("arbitrary"),
    )(sink, z, z, z, z, z, z, z, att, datt, lse, qg, kg)


def _to_groups(u):
    s = u.shape[0]
    nstep = s // CH // SEGS
    t = u.reshape(SEGS, nstep, CH, SSM_G, SSM_H).transpose(3, 1, 0, 2, 4)
    return t.reshape(SSM_G, s // CH, CH * SSM_H)


def _from_groups(y):
    nc = y.shape[1]
    nstep = nc // SEGS
    t = y.reshape(SSM_G, nstep, SEGS, CH, SSM_H).transpose(2, 1, 3, 0, 4)
    return t.reshape(nc * CH, SSM_W)


def _pair3(i):
    return (i, 0, 0)


def _state_blk(i):
    return (0, 0, i)


def ssm_in(ug, e, name):
    g, nc, k = ug.shape

    def body(u_ref, e_ref, s_ref):
        sv = _dot(u_ref[0], e_ref[0]) + _dot(u_ref[1], e_ref[1])
        for q in range(4):
            s_ref[q] = sv[:, q * 128:(q + 1) * 128]

    return pl.pallas_call(
        body, name=name, grid=(g // 2,),
        in_specs=[pl.BlockSpec((2, nc, k), _pair3), pl.BlockSpec((2, k, 512), _pair3)],
        out_specs=pl.BlockSpec((4, nc, 128), _state_blk),
        out_shape=jax.ShapeDtypeStruct((4, nc, g * SSM_P), F32),
        compiler_params=_params("parallel"),
    )(ug, e)


def chunk_scan(s4, a4, flip, name, xp4=None):
    _, nc, gp = s4.shape
    nstep = nc // SEGS
    assert nstep & (nstep - 1) == 0
    ct = 512
    with_da = xp4 is not None

    def body(*refs):
        if with_da:
            s_ref, a_ref, xp_ref, o_ref, da_ref = refs
        else:
            s_ref, a_ref, o_ref = refs
        rows = lax.broadcasted_iota(jnp.int32, (SEGS, ct), 0)
        zero = jnp.zeros((SEGS, ct), F32)
        for pair in range(2):
            asc = (pair == 0) != flip
            ir, ii = 2 * pair, 2 * pair + 1
            ar1 = a_ref[ir]
            ai1 = a_ref[ii]
            ar = jnp.broadcast_to(ar1, (SEGS, ct))
            ai = jnp.broadcast_to(ai1, (SEGS, ct))

            def tile(t):
                tt = t if asc else nstep - 1 - t
                return pl.ds(pl.multiple_of(tt * SEGS, SEGS), SEGS)

            def local(t, c):
                xr, xi = c
                sl = tile(t)
                return (ar * xr - ai * xi + s_ref[ir, sl, :], ar * xi + ai * xr + s_ref[ii, sl, :])

            er, ei = lax.fori_loop(0, nstep, local, (zero, zero))
            pr, pi = ar1, ai1
            for _ in range(nstep.bit_length() - 1):
                pr, pi = pr * pr - pi * pi, 2.0 * pr * pi
            cr = jnp.zeros((1, ct), F32)
            ci = jnp.zeros((1, ct), F32)
            xin_r, xin_i = zero, zero
            for k in range(SEGS):
                sg = k if asc else SEGS - 1 - k
                here = rows == sg
                xin_r = jnp.where(here, cr, xin_r)
                xin_i = jnp.where(here, ci, xin_i)
                lr = jnp.sum(jnp.where(here, er, 0.0), axis=0, keepdims=True)
                li = jnp.sum(jnp.where(here, ei, 0.0), axis=0, keepdims=True)
                cr, ci = pr * cr - pi * ci + lr, pr * ci + pi * cr + li

            def final(t, c):
                xr, xi, acr, aci = c
                sl = tile(t)
                o_ref[ir, sl, :] = xr
                o_ref[ii, sl, :] = xi
                if with_da:
                    br = xp_ref[ir, sl, :]
                    bi = xp_ref[ii, sl, :]
                    acr = acr + br * xr + bi * xi
                    aci = aci + br * xi - bi * xr
                return (ar * xr - ai * xi + s_ref[ir, sl, :], ar * xi + ai * xr + s_ref[ii, sl, :], acr, aci)

            _, _, acr, aci = lax.fori_loop(0, nstep, final, (xin_r, xin_i, zero, zero))
            if with_da:
                da_ref[ir] = jnp.sum(acr, axis=0, keepdims=True)
                da_ref[ii] = jnp.sum(aci, axis=0, keepdims=True)

    blk = pl.BlockSpec((4, nc, ct), _state_blk)
    ablk = pl.BlockSpec((4, 1, ct), _state_blk)
    out_shape = jax.ShapeDtypeStruct((4, nc, gp), F32)
    if with_da:
        return pl.pallas_call(
            body, name=name, grid=(gp // ct,), in_specs=[blk, ablk, blk], out_specs=[blk, ablk],
            out_shape=[out_shape, jax.ShapeDtypeStruct((4, 1, gp), F32)], compiler_params=_params("parallel"),
        )(s4, a4, xp4)
    return pl.pallas_call(
        body, name=name, grid=(gp // ct,), in_specs=[blk, ablk], out_specs=blk,
        out_shape=out_shape, compiler_params=_params("parallel"),
    )(s4, a4)


def _state_cat(ref):
    return jnp.concatenate([ref[q] for q in range(4)], axis=1).astype(BF16)


def ssm_out(ug, t, xp4, o, name):
    g, nc, k = ug.shape

    def body(u_ref, t_ref, xp_ref, o_ref, y_ref):
        x4 = _state_cat(xp_ref)
        for r in range(2):
            y_ref[r] = _dot(u_ref[r], t_ref[r]) + _dot(x4, o_ref[r])

    return pl.pallas_call(
        body, name=name, grid=(g // 2,),
        in_specs=[pl.BlockSpec((2, nc, k), _pair3), pl.BlockSpec((2, k, k), _pair3),
                  pl.BlockSpec((4, nc, 128), _state_blk), pl.BlockSpec((2, 512, k), _pair3)],
        out_specs=pl.BlockSpec((2, nc, k), _pair3),
        out_shape=jax.ShapeDtypeStruct((g, nc, k), F32),
        compiler_params=_params("parallel"),
    )(ug, t, xp4, o)


def ssm_out_bwd(dyg, xp4, o, name):
    g, nc, k = dyg.shape

    def body(dy_ref, xp_ref, o_ref, do_ref, dxp_ref):
        x4 = _state_cat(xp_ref)
        acc = jnp.zeros((nc, 512), F32)
        for r in range(2):
            do_ref[r] = _dot_tn(x4, dy_ref[r])
            acc = acc + _dot_nt(dy_ref[r], o_ref[r])
        for q in range(4):
            dxp_ref[q] = acc[:, q * 128:(q + 1) * 128]

    return pl.pallas_call(
        body, name=name, grid=(g // 2,),
        in_specs=[pl.BlockSpec((2, nc, k), _pair3), pl.BlockSpec((4, nc, 128), _state_blk),
                  pl.BlockSpec((2, 512, k), _pair3)],
        out_specs=[pl.BlockSpec((2, 512, k), _pair3), pl.BlockSpec((4, nc, 128), _state_blk)],
        out_shape=[jax.ShapeDtypeStruct((g, 512, k), F32), jax.ShapeDtypeStruct((4, nc, g * SSM_P), F32)],
        compiler_params=_params("parallel"),
    )(dyg, xp4, o)


def ssm_in_bwd(ug, dyg, ds4, t, e, name):
    g, nc, k = ug.shape

    def body(u_ref, dy_ref, ds_ref, t_ref, e_ref, dt_ref, de_ref, du_ref):
        ds = _state_cat(ds_ref)
        for r in range(2):
            dt_ref[r] = _dot_tn(u_ref[r], dy_ref[r])
            de_ref[r] = _dot_tn(u_ref[r], ds)
            du_ref[r] = _dot_nt(dy_ref[r], t_ref[r]) + _dot_nt(ds, e_ref[r])

    return pl.pallas_call(
        body, name=name, grid=(g // 2,),
        in_specs=[pl.BlockSpec((2, nc, k), _pair3), pl.BlockSpec((2, nc, k), _pair3),
                  pl.BlockSpec((4, nc, 128), _state_blk), pl.BlockSpec((2, k, k), _pair3),
                  pl.BlockSpec((2, k, 512), _pair3)],
        out_specs=[pl.BlockSpec((2, k, k), _pair3), pl.BlockSpec((2, k, 512), _pair3),
                   pl.BlockSpec((2, nc, k), _pair3)],
        out_shape=[jax.ShapeDtypeStruct((g, k, k), F32), jax.ShapeDtypeStruct((g, k, 512), F32),
                   jax.ShapeDtypeStruct((g, nc, k), F32)],
        compiler_params=_params("parallel"),
    )(ug, dyg, ds4, t, e)


def ssm_post_fwd(yc, z, dskip, wglu, name):
    s = yc.shape[0]

    def body(y_ref, u_ref, d_ref, w_ref, o_ref, yp_ref, g_ref):
        yp = y_ref[...] + d_ref[...] * u_ref[...]
        yp_ref[...] = yp
        gv = _dot(_gelu(yp).astype(BF16), w_ref[...])
        g_ref[...] = gv
        o_ref[...] = gv[:, :SSM_W] * _sigmoid(gv[:, SSM_W:])

    return pl.pallas_call(
        body, name=name, grid=(s // TM,),
        in_specs=[pl.BlockSpec((TM, SSM_W), _row), pl.BlockSpec((TM, SSM_W), _row),
                  pl.BlockSpec((1, SSM_W), _fixed), pl.BlockSpec((SSM_W, 2 * SSM_W), _fixed)],
        out_specs=[pl.BlockSpec((TM, SSM_W), _row), pl.BlockSpec((TM, SSM_W), _row),
                   pl.BlockSpec((TM, 2 * SSM_W), _row)],
        out_shape=[jax.ShapeDtypeStruct((s, SSM_W), F32), jax.ShapeDtypeStruct((s, SSM_W), F32),
                   jax.ShapeDtypeStruct((s, 2 * SSM_W), F32)],
        compiler_params=_params("parallel"),
    )(yc, z, dskip, wglu)


def ssm_post_bwd(dssm, gpre, ypre, u, dskip, wglu, name):
    s = dssm.shape[0]

    def body(do_ref, g_ref, yp_ref, u_ref, d_ref, w_ref, dy_ref, du_ref, dw_ref, dd_ref):
        i = pl.program_id(0)

        @pl.when(i == 0)
        def _():
            dw_ref[...] = jnp.zeros_like(dw_ref)
            dd_ref[...] = jnp.zeros_like(dd_ref)

        gv = g_ref[...]
        val = gv[:, :SSM_W]
        sg = _sigmoid(gv[:, SSM_W:])
        do = do_ref[...]
        dg = jnp.concatenate([do * sg, do * val * sg * (1.0 - sg)], axis=1).astype(BF16)
        yp = yp_ref[...]
        dgl = _dot_nt(dg, w_ref[...])
        dw_ref[...] += _dot_tn(_gelu(yp).astype(BF16), dg)
        dyp = dgl * _gelu_grad(yp)
        dy_ref[...] = dyp
        du_ref[...] = dyp * d_ref[...]
        dd_ref[...] += jnp.sum(dyp * u_ref[...], axis=0, keepdims=True)

    return pl.pallas_call(
        body, name=name, grid=(s // TM,),
        in_specs=[pl.BlockSpec((TM, SSM_W), _row), pl.BlockSpec((TM, 2 * SSM_W), _row),
                  pl.BlockSpec((TM, SSM_W), _row), pl.BlockSpec((TM, SSM_W), _row),
                  pl.BlockSpec((1, SSM_W), _fixed), pl.BlockSpec((SSM_W, 2 * SSM_W), _fixed)],
        out_specs=[pl.BlockSpec((TM, SSM_W), _row), pl.BlockSpec((TM, SSM_W), _row),
                   pl.BlockSpec((SSM_W, 2 * SSM_W), _fixed), pl.BlockSpec((1, SSM_W), _fixed)],
        out_shape=[jax.ShapeDtypeStruct((s, SSM_W), F32), jax.ShapeDtypeStruct((s, SSM_W), F32),
                   jax.ShapeDtypeStruct((SSM_W, 2 * SSM_W), F32), jax.ShapeDtypeStruct((1, SSM_W), F32)],
        compiler_params=_params("arbitrary"),
    )(dssm, gpre, ypre, u, dskip, wglu)


def ssm_mats(lam_re, lam_im, log_dt, b_re, b_im, c_re, c_im):
    g, p, hh = SSM_G, SSM_P, SSM_H
    j = jnp.arange(CH + 1, dtype=F32).reshape(CH + 1, 1, 1, 1)
    dt = jnp.exp(log_dt)[..., None]
    mag = jnp.exp(j * (lam_re * dt)[None])
    ang = j * (lam_im * dt)[None]
    pr, pi = mag * jnp.cos(ang), mag * jnp.sin(ang)
    abr, abi = pr[1], pi[1]
    den = lam_re * lam_re + lam_im * lam_im
    zr = ((abr - 1.0) * lam_re + abi * lam_im) / den
    zi = (abi * lam_re - (abr - 1.0) * lam_im) / den
    bbr = zr[..., None] * b_re[None] - zi[..., None] * b_im[None]
    bbi = zr[..., None] * b_im[None] + zi[..., None] * b_re[None]
    car = c_re[None] * pr[:, :, :, None, :] - c_im[None] * pi[:, :, :, None, :]
    cai = c_re[None] * pi[:, :, :, None, :] + c_im[None] * pr[:, :, :, None, :]
    hp = lax.Precision.HIGHEST
    kk = (jnp.einsum("jdghp,dgpk->jdghk", car[:CH], bbr, precision=hp)
          - jnp.einsum("jdghp,dgpk->jdghk", cai[:CH], bbi, precision=hp))
    si = np.arange(CH)[:, None]
    ti = np.arange(CH)[None, :]
    df = ti - si
    tf = jnp.where((df >= 0)[:, :, None, None, None], kk[:, 0][np.clip(df, 0, CH - 1)], 0.0)
    tb = jnp.where((df <= 0)[:, :, None, None, None], kk[:, 1][np.clip(-df, 0, CH - 1)], 0.0)
    tmat = (tf + tb).transpose(2, 0, 4, 1, 3).reshape(g, CH * hh, CH * hh)

    par = jax.nn.one_hot(np.arange(g) % 2, 2, dtype=F32)
    sidx = np.arange(CH)

    def e_part(pw_r, pw_i, d):
        er = pw_r[..., None] * bbr[d][None] - pw_i[..., None] * bbi[d][None]
        ei = pw_r[..., None] * bbi[d][None] + pw_i[..., None] * bbr[d][None]
        return [v.transpose(1, 0, 3, 2).reshape(g, CH * hh, p) for v in (er, ei)]

    eparts = e_part(pr[CH - 1 - sidx, 0], pi[CH - 1 - sidx, 0], 0) + e_part(pr[sidx, 1], pi[sidx, 1], 1)
    emat = jnp.stack(eparts, axis=2)
    emat = (emat[:, :, :, None, :] * par[:, None, None, :, None]).reshape(g, CH * hh, 8 * p)

    def o_part(d, pw):
        return [v.transpose(1, 3, 0, 2).reshape(g, p, CH * hh) for v in (car[pw, d], -cai[pw, d])]

    oparts = o_part(0, 1 + sidx) + o_part(1, CH - sidx)
    omat = jnp.stack(oparts, axis=1)
    omat = (omat[:, :, None, :, :] * par[:, None, :, None, None]).reshape(g, 8 * p, CH * hh)
    amat = jnp.stack([pr[CH, 0], pi[CH, 0], pr[CH, 1], pi[CH, 1]], axis=0).reshape(4, 1, g * p)
    return tmat, emat, omat, amat


def outproj_fwd(x, att, ssm, wo, name):
    s, d = x.shape

    def body(x_ref, a_ref, s_ref, w_ref, o_ref):
        o_ref[...] = (x_ref[...] + _dot(a_ref[...].astype(BF16), w_ref[0:ATT_W, :])
                      + _dot(s_ref[...].astype(BF16), w_ref[ATT_W:, :]))

    return pl.pallas_call(
        body, name=name, grid=(s // TM,),
        in_specs=[pl.BlockSpec((TM, d), _row), pl.BlockSpec((TM, ATT_W), _row), pl.BlockSpec((TM, SSM_W), _row),
                  pl.BlockSpec((ATT_W + SSM_W, d), _fixed)],
        out_specs=pl.BlockSpec((TM, d), _row),
        out_shape=jax.ShapeDtypeStruct((s, d), F32),
        compiler_params=_params("parallel"),
    )(x, att, ssm, wo)


def outproj_bwd(dx1, att, ssm, wo, name):
    s, d = dx1.shape

    def body(dx_ref, a_ref, s_ref, w_ref, da_ref, ds_ref, dw_ref):
        i = pl.program_id(0)

        @pl.when(i == 0)
        def _():
            dw_ref[...] = jnp.zeros_like(dw_ref)

        dxb = dx_ref[...].astype(BF16)
        da_ref[...] = _dot_nt(dxb, w_ref[0:ATT_W, :])
        ds_ref[...] = _dot_nt(dxb, w_ref[ATT_W:, :])
        dw_ref[0:ATT_W, :] += _dot_tn(a_ref[...].astype(BF16), dxb)
        dw_ref[ATT_W:, :] += _dot_tn(s_ref[...].astype(BF16), dxb)

    return pl.pallas_call(
        body, name=name, grid=(s // TM,),
        in_specs=[pl.BlockSpec((TM, d), _row), pl.BlockSpec((TM, ATT_W), _row), pl.BlockSpec((TM, SSM_W), _row),
                  pl.BlockSpec((ATT_W + SSM_W, d), _fixed)],
        out_specs=[pl.BlockSpec((TM, ATT_W), _row), pl.BlockSpec((TM, SSM_W), _row),
                   pl.BlockSpec((ATT_W + SSM_W, d), _fixed)],
        out_shape=[jax.ShapeDtypeStruct((s, ATT_W), F32), jax.ShapeDtypeStruct((s, SSM_W), F32),
                   jax.ShapeDtypeStruct((ATT_W + SSM_W, d), F32)],
        compiler_params=_params("arbitrary"),
    )(dx1, att, ssm, wo)


def ffn_fwd(x1, gain, w1, w2, name):
    s, d = x1.shape
    nch, _, fc = w1.shape

    def body(x_ref, g_ref, w1_ref, w2_ref, o_ref, h_ref, a_ref):
        k = pl.program_id(1)

        @pl.when(k == 0)
        def _():
            xv = x_ref[...]
            h_ref[...] = (xv * _rms(xv) * g_ref[...]).astype(BF16)
            o_ref[...] = xv

        a = _dot(h_ref[...], w1_ref[...])
        a_ref[...] = a.astype(BF16)
        o_ref[...] += _dot(jnp.square(jnp.maximum(a, 0.0)).astype(BF16), w2_ref[...])

    return pl.pallas_call(
        body, name=name, grid=(s // TM, nch),
        in_specs=[pl.BlockSpec((TM, d), lambda i, k: (i, 0)), pl.BlockSpec((1, d), lambda i, k: (0, 0)),
                  pl.BlockSpec((None, d, fc), lambda i, k: (k, 0, 0)),
                  pl.BlockSpec((None, fc, d), lambda i, k: (k, 0, 0))],
        out_specs=[pl.BlockSpec((TM, d), lambda i, k: (i, 0)), pl.BlockSpec((TM, d), lambda i, k: (i, 0)),
                   pl.BlockSpec((TM, fc), lambda i, k: (i, k))],
        out_shape=[jax.ShapeDtypeStruct((s, d), F32), jax.ShapeDtypeStruct((s, d), BF16),
                   jax.ShapeDtypeStruct((s, nch * fc), BF16)],
        compiler_params=_params("parallel", "arbitrary"),
    )(x1, gain, w1, w2)


def ffn_bwd_tok(dx2, x1, gain, a, w1, w2, name):
    s, d = x1.shape
    nch, _, fc = w1.shape

    def body(dx_ref, x_ref, g_ref, a_ref, w1_ref, w2_ref, da_ref, dx1_ref, dg_ref, dxb_ref, dh_ref):
        i = pl.program_id(0)
        k = pl.program_id(1)

        @pl.when(jnp.logical_and(i == 0, k == 0))
        def _():
            dg_ref[...] = jnp.zeros_like(dg_ref)

        @pl.when(k == 0)
        def _():
            dxb_ref[...] = dx_ref[...].astype(BF16)
            dh_ref[...] = jnp.zeros_like(dh_ref)

        dr = _dot_nt(dxb_ref[...], w2_ref[...])
        da = (dr * (2.0 * jnp.maximum(a_ref[...].astype(F32), 0.0))).astype(BF16)
        da_ref[...] = da
        dh_ref[...] += _dot_nt(da, w1_ref[...])

        @pl.when(k == nch - 1)
        def _():
            xv = x_ref[...]
            r = _rms(xv)
            xhat = xv * r
            dh = dh_ref[...]
            dg_ref[...] += jnp.sum(dh * xhat, axis=0, keepdims=True)
            dx1_ref[...] = dx_ref[...] + _rms_bwd(xhat, r, dh * g_ref[...])

    return pl.pallas_call(
        body, name=name, grid=(s // TM, nch),
        in_specs=[pl.BlockSpec((TM, d), lambda i, k: (i, 0)), pl.BlockSpec((TM, d), lambda i, k: (i, 0)),
                  pl.BlockSpec((1, d), lambda i, k: (0, 0)), pl.BlockSpec((TM, fc), lambda i, k: (i, k)),
                  pl.BlockSpec((None, d, fc), lambda i, k: (k, 0, 0)),
                  pl.BlockSpec((None, fc, d), lambda i, k: (k, 0, 0))],
        out_specs=[pl.BlockSpec((TM, fc), lambda i, k: (i, k)), pl.BlockSpec((TM, d), lambda i, k: (i, 0)),
                   pl.BlockSpec((1, d), lambda i, k: (0, 0)), pl.BlockSpec((TM, d), lambda i, k: (i, 0))],
        out_shape=[jax.ShapeDtypeStruct((s, nch * fc), BF16), jax.ShapeDtypeStruct((s, d), F32),
                   jax.ShapeDtypeStruct((1, d), F32), jax.ShapeDtypeStruct((s, d), BF16)],
        scratch_shapes=[pltpu.VMEM((TM, d), F32)],
        compiler_params=_params("arbitrary", "arbitrary"),
    )(dx2, x1, gain, a, w1, w2)


def ffn_bwd_w(h2, da, a, dxb, nch, name):
    s, d = h2.shape
    fc = a.shape[1] // nch

    def body(h_ref, da_ref, a_ref, dx_ref, dw1_ref, dw2_ref):
        t = pl.program_id(1)

        @pl.when(t == 0)
        def _():
            dw1_ref[...] = jnp.zeros_like(dw1_ref)
            dw2_ref[...] = jnp.zeros_like(dw2_ref)

        dw1_ref[...] += _dot_tn(h_ref[...], da_ref[...])
        r = jnp.square(jnp.maximum(a_ref[...].astype(F32), 0.0)).astype(BF16)
        dw2_ref[...] += _dot_tn(r, dx_ref[...])

    return pl.pallas_call(
        body, name=name, grid=(nch, s // TM),
        in_specs=[pl.BlockSpec((TM, d), lambda k, t: (t, 0)), pl.BlockSpec((TM, fc), lambda k, t: (t, k)),
                  pl.BlockSpec((TM, fc), lambda k, t: (t, k)), pl.BlockSpec((TM, d), lambda k, t: (t, 0))],
        out_specs=[pl.BlockSpec((None, d, fc), lambda k, t: (k, 0, 0)),
                   pl.BlockSpec((None, fc, d), lambda k, t: (k, 0, 0))],
        out_shape=[jax.ShapeDtypeStruct((nch, d, fc), F32), jax.ShapeDtypeStruct((nch, fc, d), F32)],
        compiler_params=_params("parallel", "arbitrary"),
    )(h2, da, a, dxb)


def loss_grad(xf, tgt, name):
    s, d = xf.shape
    nt = s // TM

    def body(x_ref, t_ref, dx_ref, l_ref, acc_ref):
        i = pl.program_id(0)

        @pl.when(i == 0)
        def _():
            acc_ref[...] = jnp.zeros_like(acc_ref)

        e = x_ref[...] - t_ref[...]
        dx_ref[...] = e * (1.0 / d)
        acc_ref[...] += jnp.sum(e * e, axis=0, keepdims=True)

        @pl.when(i == nt - 1)
        def _():
            l_ref[...] = jnp.sum(acc_ref[...], axis=1, keepdims=True) * (0.5 / d)

    return pl.pallas_call(
        body, name=name, grid=(nt,),
        in_specs=[pl.BlockSpec((TM, d), _row), pl.BlockSpec((TM, d), _row)],
        out_specs=[pl.BlockSpec((TM, d), _row), pl.BlockSpec((1, 1), _fixed)],
        out_shape=[jax.ShapeDtypeStruct((s, d), F32), jax.ShapeDtypeStruct((1, 1), F32)],
        scratch_shapes=[pltpu.VMEM((1, d), F32)],
        compiler_params=_params("arbitrary"),
    )(xf, tgt)


def adamw_sum(parts, w, m, v, br, name):
    r, c = w.shape
    c1 = 1.0 - ADAM_B1 ** ADAM_STEP
    c2 = 1.0 - ADAM_B2 ** ADAM_STEP

    def body(p_ref, w_ref, m_ref, v_ref, g_ref, d_ref, nm_ref, nv_ref):
        g = p_ref[0]
        for j in range(1, N_DEV):
            g = g + p_ref[j]
        m2 = ADAM_B1 * m_ref[...] + (1.0 - ADAM_B1) * g
        v2 = ADAM_B2 * v_ref[...] + (1.0 - ADAM_B2) * jnp.square(g)
        g_ref[...] = g
        nm_ref[...] = m2
        nv_ref[...] = v2
        d_ref[...] = -ADAM_LR * ((m2 / c1) / (jnp.sqrt(v2 / c2) + ADAM_EPS) + ADAM_WD * w_ref[...])

    blk = pl.BlockSpec((br, c), _row)
    sds = jax.ShapeDtypeStruct((r, c), F32)
    return pl.pallas_call(
        body, name=name, grid=(r // br,),
        in_specs=[pl.BlockSpec((N_DEV, br, c), lambda i: (0, i, 0)), blk, blk, blk],
        out_specs=[blk, blk, blk, blk], out_shape=[sds, sds, sds, sds],
        compiler_params=_params("parallel"),
    )(parts, w, m, v)


def _me_and_peers():
    x, y, c = lax.axis_index("x"), lax.axis_index("y"), lax.axis_index("c")
    me = 4 * x + 2 * y + c
    peers = []
    for k in range(1, N_DEV):
        px = jnp.bitwise_xor(x, (k >> 2) & 1)
        py = jnp.bitwise_xor(y, (k >> 1) & 1)
        pc = jnp.bitwise_xor(c, k & 1)
        peers.append(((px, py, pc), 4 * px + 2 * py + pc))
    return me, peers


def exchange(arrays, scatter, name):
    n = len(arrays)

    def body(*refs):
        ins, outs = refs[:n], refs[n:2 * n]
        send_sems, recv_sems, loc_sems = refs[2 * n:]
        me, peers = _me_and_peers()
        copies = []
        for a in range(n):
            src = ins[a].at[me] if scatter else ins[a]
            cp = pltpu.make_async_copy(src, outs[a].at[me], loc_sems.at[a])
            cp.start()
            copies.append(cp)
        sends = []
        for k, (dev, idx) in enumerate(peers):
            for a in range(n):
                src = ins[a].at[idx] if scatter else ins[a]
                rc = pltpu.make_async_remote_copy(
                    src_ref=src, dst_ref=outs[a].at[me], send_sem=send_sems.at[a, k], recv_sem=recv_sems.at[a, k],
                    device_id=dev, device_id_type=pl.DeviceIdType.MESH)
                rc.start()
                sends.append((rc, a, k, idx))
        for rc, a, k, idx in sends:
            src = ins[a].at[idx] if scatter else ins[a]
            pltpu.make_async_remote_copy(
                src_ref=src, dst_ref=outs[a].at[idx], send_sem=send_sems.at[a, k], recv_sem=recv_sems.at[a, k],
                device_id=peers[k][0], device_id_type=pl.DeviceIdType.MESH).wait_recv()
        for rc, a, k, idx in sends:
            rc.wait_send()
        for cp in copies:
            cp.wait()

    def out_sds(a):
        shape = a.shape if scatter else (N_DEV,) + a.shape
        return jax.ShapeDtypeStruct(shape, a.dtype)

    anyspec = pl.BlockSpec(memory_space=pl.ANY)
    return pl.pallas_call(
        body, name=name,
        in_specs=[anyspec] * n, out_specs=[anyspec] * n, out_shape=[out_sds(a) for a in arrays],
        scratch_shapes=[pltpu.SemaphoreType.DMA((n, N_DEV - 1)), pltpu.SemaphoreType.DMA((n, N_DEV - 1)),
                        pltpu.SemaphoreType.DMA((n,))],
        compiler_params=pltpu.CompilerParams(has_side_effects=True),
    )(*arrays)


def layer_fwd(x, p, mats, l):
    tmat, emat, omat, amat = mats
    z = norm_matmul(x, p["norm1"], p["w_in"], f"in_fwd{l}")
    att, lse = attn_fwd(z, p["q_gain"], p["k_gain"], p["sink"], f"attn_fwd{l}")
    ug = _to_groups(z[:, V_END:]).astype(BF16)
    s4 = ssm_in(ug, emat, f"ssm_in{l}")
    xp4 = chunk_scan(s4, amat, False, f"ssm_scan{l}")
    yc = _from_groups(ssm_out(ug, tmat, xp4, omat, f"ssm_out{l}"))
    u = z[:, V_END:]
    ssm, ypre, gpre = ssm_post_fwd(yc, u, p["d_skip"], p["w_glu"], f"ssm_post{l}")
    x1 = outproj_fwd(x, att, ssm, p["w_out"], f"out_fwd{l}")
    x2, h2, a = ffn_fwd(x1, p["norm2"], p["w_ff1"], p["w_ff2"], f"ffn_fwd{l}")
    saved = dict(x=x, z=z, u=u, att=att, lse=lse, ug=ug, xp4=xp4, ssm=ssm, ypre=ypre, gpre=gpre, x1=x1, h2=h2, a=a)
    return x2, saved


def layer_bwd(dx2, p, mats, sv, l):
    tmat, emat, omat, amat = mats
    nch = p["w_ff1"].shape[0]
    da, dx1, dnorm2, dxb = ffn_bwd_tok(dx2, sv["x1"], p["norm2"], sv["a"], p["w_ff1"], p["w_ff2"], f"ffn_bwd{l}")
    dw1, dw2 = ffn_bwd_w(sv["h2"], da, sv["a"], dxb, nch, f"ffn_bwdw{l}")
    datt, dssm, dwo = outproj_bwd(dx1, sv["att"], sv["ssm"], p["w_out"], f"out_bwd{l}")
    dyc, du_skip, dwglu, ddskip = ssm_post_bwd(dssm, sv["gpre"], sv["ypre"], sv["u"], p["d_skip"], p["w_glu"],
                                               f"ssm_post_bwd{l}")
    dyg = _to_groups(dyc).astype(BF16)
    domat, dxp4 = ssm_out_bwd(dyg, sv["xp4"], omat, f"ssm_out_bwd{l}")
    aconj = amat * jnp.array([1.0, -1.0, 1.0, -1.0], F32).reshape(4, 1, 1)
    ds4, damat = chunk_scan(dxp4, aconj, True, f"ssm_scan_bwd{l}", xp4=sv["xp4"])
    dtmat, demat, dug = ssm_in_bwd(sv["ug"], dyg, ds4, tmat, emat, f"ssm_in_bwd{l}")
    du_core = _from_groups(dug)
    dq, dkp, dvp, dqg, dkg, dsink = attn_bwd(sv["z"], sv["att"], datt, sv["lse"], p["q_gain"], p["k_gain"], p["sink"],
                                             f"attn_bwd{l}")
    dx, dwin, dnorm1 = in_bwd(sv["x"], p["norm1"], p["w_in"], dq, dkp[BLK:-BLK], dvp[BLK:-BLK], du_skip, du_core, dx1,
                              f"in_bwd{l}")
    grads = dict(norm1=dnorm1, w_in=dwin, q_gain=dqg, k_gain=dkg, sink=dsink, d_skip=ddskip, w_glu=dwglu,
                 w_out=dwo, norm2=dnorm2, w_ff1=dw1, w_ff2=dw2)
    return dx, grads, (dtmat, demat, domat, damat)


def local_step(xs, tgt, big, small):
    s5 = tuple(small[n] for n in ("lam_re", "lam_im", "log_dt", "b_re", "b_im", "c_re", "c_im"))
    mats, mats_vjp = jax.vjp(jax.vmap(ssm_mats), *s5)
    tmat, emat, omat, amat = mats
    tb, eb, ob = tmat.astype(BF16), emat.astype(BF16), omat.astype(BF16)
    x = xs
    saved, lp, lm = [], [], []
    for l in range(DEPTH):
        p = {n: big[n][l] for n in BIG}
        for n in ("norm1", "q_gain", "k_gain", "d_skip", "norm2"):
            p[n] = small[n][l].reshape(1, -1)
        p["sink"] = small["sink"][l]
        m = (tb[l], eb[l], ob[l], amat[l])
        x, sv = layer_fwd(x, p, m, l)
        saved.append(sv)
        lp.append(p)
        lm.append(m)
    dx, loss = loss_grad(x, tgt, "loss")
    grads = [None] * DEPTH
    dmats = [None] * DEPTH
    for l in reversed(range(DEPTH)):
        dx, grads[l], dmats[l] = layer_bwd(dx, lp[l], lm[l], saved[l], l)
    ds5 = mats_vjp(tuple(jnp.stack([dmats[l][i] for l in range(DEPTH)]) for i in range(4)))
    return loss, dx, grads, ds5


def _pack_small(vals):
    flat = jnp.concatenate([vals[n].reshape(-1).astype(F32) for n in SMALL])
    return jnp.pad(flat, (0, SMALL_ROWS * 128 - flat.shape[0])).reshape(SMALL_ROWS, 128)


def _unpack_small(packed, like):
    flat = packed.reshape(-1)
    out, off = {}, 0
    for n in SMALL:
        size = int(np.prod(like[n].shape))
        out[n] = flat[off:off + size].reshape(like[n].shape)
        off += size
    return out


def kernel(x, norm1, w_in, q_gain, k_gain, sink, lam_re, lam_im, log_dt, b_re, b_im, c_re, c_im, d_skip, w_glu, w_out, norm2, w_ff1, w_ff2, loss_target, m_norm1, m_w_in, m_q_gain, m_k_gain, m_sink, m_lam_re, m_lam_im, m_log_dt, m_b_re, m_b_im, m_c_re, m_c_im, m_d_skip, m_w_glu, m_w_out, m_norm2, m_w_ff1, m_w_ff2, v_norm1, v_w_in, v_q_gain, v_k_gain, v_sink, v_lam_re, v_lam_im, v_log_dt, v_b_re, v_b_im, v_c_re, v_c_im, v_d_skip, v_w_glu, v_w_out, v_norm2, v_w_ff1, v_w_ff2):
    w = dict(norm1=norm1, w_in=w_in, q_gain=q_gain, k_gain=k_gain, sink=sink, lam_re=lam_re, lam_im=lam_im,
             log_dt=log_dt, b_re=b_re, b_im=b_im, c_re=c_re, c_im=c_im, d_skip=d_skip, w_glu=w_glu, w_out=w_out,
             norm2=norm2, w_ff1=w_ff1, w_ff2=w_ff2)
    m = dict(norm1=m_norm1, w_in=m_w_in, q_gain=m_q_gain, k_gain=m_k_gain, sink=m_sink, lam_re=m_lam_re,
             lam_im=m_lam_im, log_dt=m_log_dt, b_re=m_b_re, b_im=m_b_im, c_re=m_c_re, c_im=m_c_im, d_skip=m_d_skip,
             w_glu=m_w_glu, w_out=m_w_out, norm2=m_norm2, w_ff1=m_w_ff1, w_ff2=m_w_ff2)
    v = dict(norm1=v_norm1, w_in=v_w_in, q_gain=v_q_gain, k_gain=v_k_gain, sink=v_sink, lam_re=v_lam_re,
             lam_im=v_lam_im, log_dt=v_log_dt, b_re=v_b_re, b_im=v_b_im, c_re=v_c_re, c_im=v_c_im, d_skip=v_d_skip,
             w_glu=v_w_glu, w_out=v_w_out, norm2=v_norm2, w_ff1=v_w_ff1, w_ff2=v_w_ff2)
    L = DEPTH

    g_in, g_glu, g_out, g_ff1, g_ff2 = exchange([w[n].astype(BF16) for n in BIG], False, "gather_weights")
    big = dict(
        w_in=g_in.transpose(1, 2, 0, 3).reshape(L, D_MODEL, IN_W),
        w_glu=g_glu.transpose(1, 2, 0, 3).reshape(L, SSM_W, 2 * SSM_W),
        w_out=g_out.transpose(1, 0, 2, 3).reshape(L, ATT_W + SSM_W, D_MODEL),
        w_ff1=g_ff1.transpose(1, 0, 2, 3),
        w_ff2=g_ff2.transpose(1, 0, 2, 3),
    )
    small = {n: w[n] for n in SMALL}
    loss_part, dx, grads, ds5 = local_step(x[0], loss_target[0], big, small)
    loss = lax.psum(loss_part[0, 0], ("x", "y", "c"))

    def stack(name):
        return jnp.stack([grads[l][name] for l in range(L)], axis=1)

    parts = exchange([
        jnp.stack([grads[l]["w_in"].reshape(D_MODEL, N_DEV, IN_W // N_DEV).transpose(1, 0, 2) for l in range(L)], axis=1),
        jnp.stack([grads[l]["w_glu"].reshape(SSM_W, N_DEV, 2 * SSM_W // N_DEV).transpose(1, 0, 2) for l in range(L)], axis=1),
        jnp.stack([grads[l]["w_out"].reshape(N_DEV, (ATT_W + SSM_W) // N_DEV, D_MODEL) for l in range(L)], axis=1),
        stack("w_ff1"), stack("w_ff2")], True, "exchange_grads")

    gs = {n: jnp.stack([grads[l][n].reshape(w[n].shape[1:]) for l in range(L)])
          for n in ("norm1", "q_gain", "k_gain", "sink", "d_skip", "norm2")}
    for n, g in zip(("lam_re", "lam_im", "log_dt", "b_re", "b_im", "c_re", "c_im"), ds5):
        gs[n] = g
    (small_parts,) = exchange([_pack_small(gs)], False, "gather_small_grads")

    out_g, out_d, out_m, out_v = {}, {}, {}, {}
    for n, part in zip(BIG, parts):
        c = w[n].shape[-1]
        r = int(np.prod(w[n].shape[:-1]))
        res = adamw_sum(part.reshape(N_DEV, r, c), w[n].reshape(r, c), m[n].reshape(r, c), v[n].reshape(r, c),
                        256 if c >= 512 else 1024, f"adamw_{n}")
        out_g[n], out_d[n], out_m[n], out_v[n] = (t.reshape(w[n].shape) for t in res)
    res = adamw_sum(small_parts, _pack_small(w), _pack_small(m), _pack_small(v), 512, "adamw_small")
    for dst, packed in zip((out_g, out_d, out_m, out_v), res):
        dst.update(_unpack_small(packed, w))

    return (loss, dx[None], *[out_g[n] for n in WEIGHTS], *[out_d[n] for n in WEIGHTS],
            *[out_m[n] for n in WEIGHTS], *[out_v[n] for n in WEIGHTS])
```

```python
import numpy as np
import jax
import jax.numpy as jnp
from jax import lax
from jax.experimental import pallas as pl
from jax.experimental.pallas import tpu as pltpu

F32, BF16 = jnp.float32, jnp.bfloat16
EPS = 1e-6
D_MODEL = 1024
ATT_HEADS, KV_HEADS, GQA_GROUP, HEAD_DIM = 8, 2, 4, 64
ATT_W, KV_W, SSM_W, IN_W = 512, 128, 512, 1280
V_END = 768
U0, Q0, K0, V0 = 0, 512, 1024, 1152
BLK = 128
SSM_G, SSM_H, SSM_P = 32, 16, 64
CH = 16
GW = CH * SSM_H
SEGS = 8
N_DEV = 8
NEG = float(np.finfo(np.float32).min)
SLOPES = tuple(2.0 ** (-8.0 * (h + 1) / ATT_HEADS) for h in range(ATT_HEADS))
VMEM_LIMIT = 56 * 1024 * 1024
TM = 512

ADAM_LR, ADAM_B1, ADAM_B2, ADAM_EPS, ADAM_WD, ADAM_STEP = 0.001, 0.9, 0.999, 1e-08, 0.01, 10

SMALL = ("norm1", "q_gain", "k_gain", "sink", "lam_re", "lam_im", "log_dt", "b_re", "b_im",
         "c_re", "c_im", "d_skip", "norm2")
S5 = ("lam_re", "lam_im", "log_dt", "b_re", "b_im", "c_re", "c_im")
BIG = ("w_in", "w_glu", "w_out", "w_ff1", "w_ff2")
WEIGHTS = ("norm1", "w_in", "q_gain", "k_gain", "sink", "lam_re", "lam_im", "log_dt", "b_re", "b_im",
           "c_re", "c_im", "d_skip", "w_glu", "w_out", "norm2", "w_ff1", "w_ff2")
ADAM_ROWS = {160: 256, 128: 512, 512: 128, 1024: 64}


def _dot(a, b):
    return jnp.dot(a, b, preferred_element_type=F32)


def _dot_nt(a, b):
    return lax.dot_general(a, b, (((1,), (1,)), ((), ())), preferred_element_type=F32)


def _dot_tn(a, b):
    return lax.dot_general(a, b, (((0,), (0,)), ((), ())), preferred_element_type=F32)


def _rms(x):
    return lax.rsqrt(jnp.mean(x * x, axis=-1, keepdims=True) + EPS)


def _rms_bwd(xhat, r, dxhat):
    return r * (dxhat - xhat * jnp.mean(dxhat * xhat, axis=-1, keepdims=True))


def _sigmoid(x):
    return 1.0 / (1.0 + jnp.exp(-x))


_GC = 0.7978845608028654
_GA = 0.044715


def _gelu(x):
    return 0.5 * x * (1.0 + jnp.tanh(_GC * (x + _GA * x * x * x)))


def _gelu_grad(x):
    t = jnp.tanh(_GC * (x + _GA * x * x * x))
    return 0.5 * (1.0 + t) + 0.5 * x * (1.0 - t * t) * _GC * (1.0 + 3.0 * _GA * x * x)


def _row(i):
    return (i, 0)


def _fixed(i):
    return (0, 0)


def _me_and_peers():
    x, y, c = lax.axis_index("x"), lax.axis_index("y"), lax.axis_index("c")
    me = 4 * x + 2 * y + c
    peers = []
    for k in range(1, N_DEV):
        px = jnp.bitwise_xor(x, (k >> 2) & 1)
        py = jnp.bitwise_xor(y, (k >> 1) & 1)
        pc = jnp.bitwise_xor(c, k & 1)
        peers.append(((px, py, pc), 4 * px + 2 * py + pc))
    return me, peers


def _xch_copies(ins, outs, send_sems, recv_sems, loc_sems, scatter):
    me, peers = _me_and_peers()
    local, sends, recvs = [], [], []
    for a in range(len(ins)):
        local.append(pltpu.make_async_copy(ins[a].at[me] if scatter else ins[a], outs[a].at[me], loc_sems.at[a]))
    for k, (dev, idx) in enumerate(peers):
        for a in range(len(ins)):
            src = ins[a].at[idx] if scatter else ins[a]
            for dst, group in ((outs[a].at[me], sends), (outs[a].at[idx], recvs)):
                group.append(pltpu.make_async_remote_copy(
                    src_ref=src, dst_ref=dst, send_sem=send_sems.at[a, k], recv_sem=recv_sems.at[a, k],
                    device_id=dev, device_id_type=pl.DeviceIdType.MESH))
    return local, sends, recvs


def _xch_start(copies):
    local, sends, _ = copies
    for cp in local + sends:
        cp.start()


def _xch_wait(copies):
    local, sends, recvs = copies
    for cp in recvs:
        cp.wait_recv()
    for cp in sends:
        cp.wait_send()
    for cp in local:
        cp.wait()


def _xch_shapes(arrays, scatter):
    return [jax.ShapeDtypeStruct(a.shape if scatter else (N_DEV,) + a.shape, a.dtype) for a in arrays]


def _xch_sems(n):
    return [pltpu.SemaphoreType.DMA((n, N_DEV - 1)), pltpu.SemaphoreType.DMA((n, N_DEV - 1)),
            pltpu.SemaphoreType.DMA((n,))]


_ANY = pl.BlockSpec(memory_space=pl.ANY)


def exchange(arrays, scatter, name):
    n = len(arrays)

    def body(*refs):
        copies = _xch_copies(refs[:n], refs[n:2 * n], *refs[2 * n:], scatter)
        _xch_start(copies)
        _xch_wait(copies)

    return pl.pallas_call(
        body, name=name, in_specs=[_ANY] * n, out_specs=[_ANY] * n, out_shape=_xch_shapes(arrays, scatter),
        scratch_shapes=_xch_sems(n), compiler_params=pltpu.CompilerParams(has_side_effects=True),
    )(*arrays)


def _call(body, name, grid, in_specs, out_specs, out_shape, sem, inputs, scratch=(), xch=None):
    params = pltpu.CompilerParams(dimension_semantics=sem, vmem_limit_bytes=VMEM_LIMIT)
    if xch is None:
        out = pl.pallas_call(body, name=name, grid=grid, in_specs=in_specs, out_specs=out_specs, out_shape=out_shape,
                             scratch_shapes=list(scratch), compiler_params=params)(*inputs)
        return list(out), None
    arrays, scatter = xch
    n, nin, nout, nsc = len(arrays), len(in_specs), len(out_specs), len(scratch)

    def wrapped(*refs):
        ins, refs = refs[:nin], refs[nin:]
        xin, refs = refs[:n], refs[n:]
        outs, refs = refs[:nout], refs[nout:]
        xout, refs = refs[:n], refs[n:]
        sc, sems = refs[:nsc], refs[nsc:]
        first = last = None
        for ax, size in enumerate(grid):
            f, e = pl.program_id(ax) == 0, pl.program_id(ax) == size - 1
            first = f if first is None else jnp.logical_and(first, f)
            last = e if last is None else jnp.logical_and(last, e)

        @pl.when(first)
        def _():
            _xch_start(_xch_copies(xin, xout, *sems, scatter))

        body(*ins, *outs, *sc)

        @pl.when(last)
        def _():
            _xch_wait(_xch_copies(xin, xout, *sems, scatter))

    out = pl.pallas_call(
        wrapped, name=name, grid=grid, in_specs=list(in_specs) + [_ANY] * n, out_specs=list(out_specs) + [_ANY] * n,
        out_shape=list(out_shape) + _xch_shapes(arrays, scatter), scratch_shapes=list(scratch) + _xch_sems(n),
        compiler_params=params)(*inputs, *arrays)
    return list(out[:nout]), list(out[nout:])


def norm_matmul(x, gain, w, name):
    s, d = x.shape
    n = w.shape[1]

    def body(x_ref, g_ref, w_ref, z_ref):
        xv = x_ref[...]
        h = (xv * _rms(xv) * g_ref[...]).astype(BF16)
        z_ref[...] = _dot(h, w_ref[...])

    (z,), _ = _call(body, name, (s // TM,),
                    [pl.BlockSpec((TM, d), _row), pl.BlockSpec((1, d), _fixed), pl.BlockSpec((d, n), _fixed)],
                    [pl.BlockSpec((TM, n), _row)], [jax.ShapeDtypeStruct((s, n), F32)], ("parallel",), (x, gain, w))
    return z


def in_bwd(x, gain, w, du_a, du_b, dq, dk, dv, dres, name):
    s, d = x.shape
    n = w.shape[1]

    def body(x_ref, g_ref, w_ref, dua_ref, dub_ref, dq_ref, dk_ref, dv_ref, dres_ref, dx_ref, dw_ref, dg_ref):
        @pl.when(pl.program_id(0) == 0)
        def _():
            dw_ref[...] = jnp.zeros_like(dw_ref)
            dg_ref[...] = jnp.zeros_like(dg_ref)

        xv = x_ref[...]
        r = _rms(xv)
        xhat = xv * r
        g = g_ref[...]
        h = (xhat * g).astype(BF16)
        dz = jnp.concatenate([(dua_ref[...] + _lanes4(dub_ref)).astype(BF16), dq_ref[...].astype(BF16),
                              dk_ref[...].astype(BF16), dv_ref[...].astype(BF16)], axis=1)
        dh = _dot_nt(dz, w_ref[...])
        dw_ref[...] += _dot_tn(h, dz)
        dg_ref[...] += jnp.sum(dh * xhat, axis=0, keepdims=True)
        dx_ref[...] = dres_ref[...] + _rms_bwd(xhat, r, dh * g)

    out, _ = _call(
        body, name, (s // TM,),
        [pl.BlockSpec((TM, d), _row), pl.BlockSpec((1, d), _fixed), pl.BlockSpec((d, n), _fixed),
         pl.BlockSpec((TM, SSM_W), _row), pl.BlockSpec((4, TM, 128), lambda i: (0, i, 0)), pl.BlockSpec((TM, ATT_W), _row),
         pl.BlockSpec((TM, KV_W), _row), pl.BlockSpec((TM, KV_W), _row), pl.BlockSpec((TM, d), _row)],
        [pl.BlockSpec((TM, d), _row), pl.BlockSpec((d, n), _fixed), pl.BlockSpec((1, d), _fixed)],
        [jax.ShapeDtypeStruct((s, d), F32), jax.ShapeDtypeStruct((d, n), F32), jax.ShapeDtypeStruct((1, d), F32)],
        ("arbitrary",), (x, gain, w, du_a, du_b, dq, dk, dv, dres))
    return out


def _band_specs(nb):
    def w0(i):
        return jnp.clip(i - 1, 0, nb - 3)

    specs = [pl.BlockSpec((BLK, ATT_W), lambda i: (i, Q0 // ATT_W))]
    for col in (K0 // KV_W, V0 // KV_W):
        specs += [pl.BlockSpec((BLK, KV_W), lambda i, c=col, o=o: (w0(i) + o, c)) for o in range(3)]
    return specs


def _band_mask(i, nb):
    w0 = jnp.clip(i - 1, 0, nb - 3)
    qi = lax.broadcasted_iota(jnp.int32, (BLK, 3 * BLK), 0)
    ci = lax.broadcasted_iota(jnp.int32, (BLK, 3 * BLK), 1)
    dist = jnp.abs((i - w0) * BLK + qi - ci)
    return dist <= BLK, dist.astype(F32), w0


def attn_fwd(z, qg, kg, sink, name):
    s = z.shape[0]
    nb = s // BLK

    def body(sink_ref, q_ref, k0, k1, k2, v0, v1, v2, qg_ref, kg_ref, o_ref, lse_ref):
        valid, distf, _ = _band_mask(pl.program_id(0), nb)
        k3 = jnp.concatenate([k0[...], k1[...], k2[...]], axis=0)
        v3 = jnp.concatenate([v0[...], v1[...], v2[...]], axis=0).astype(BF16)
        q = q_ref[...]
        for j in range(KV_HEADS):
            kj = k3[:, j * HEAD_DIM:(j + 1) * HEAD_DIM]
            knj = (kj * _rms(kj) * kg_ref[...]).astype(BF16)
            vj = v3[:, j * HEAD_DIM:(j + 1) * HEAD_DIM]
            for g in range(GQA_GROUP):
                h = j * GQA_GROUP + g
                qh = q[:, h * HEAD_DIM:(h + 1) * HEAD_DIM]
                qn = (qh * _rms(qh) * qg_ref[...]).astype(BF16)
                sc = _dot_nt(qn, knj) * 0.125 - SLOPES[h] * distf
                sc = jnp.where(valid, sc, NEG)
                sk = sink_ref[h]
                m = jnp.maximum(jnp.max(sc, axis=-1, keepdims=True), sk)
                p = jnp.exp(sc - m)
                den = jnp.sum(p, axis=-1, keepdims=True) + jnp.exp(sk - m)
                o_ref[:, h * HEAD_DIM:(h + 1) * HEAD_DIM] = _dot((p / den).astype(BF16), vj)
                lse_ref[:, h:h + 1] = m + jnp.log(den)

    out, _ = _call(
        body, name, (nb,),
        [pl.BlockSpec(memory_space=pltpu.SMEM)] + _band_specs(nb)
        + [pl.BlockSpec((1, HEAD_DIM), _fixed), pl.BlockSpec((1, HEAD_DIM), _fixed)],
        [pl.BlockSpec((BLK, ATT_W), _row), pl.BlockSpec((BLK, ATT_HEADS), _row)],
        [jax.ShapeDtypeStruct((s, ATT_W), F32), jax.ShapeDtypeStruct((s, ATT_HEADS), F32)],
        ("parallel",), (sink, z, z, z, z, z, z, z, qg, kg))
    return out


def attn_bwd(z, att, datt, lse, qg, kg, sink, name, xch=None):
    s = z.shape[0]
    nb = s // BLK

    def body(sink_ref, q_ref, k0, k1, k2, v0, v1, v2, o_ref, do_ref, lse_ref, qg_ref, kg_ref,
             dq_ref, dk_ref, dv_ref, dqg_ref, dkg_ref, dsk_ref):
        i = pl.program_id(0)

        @pl.when(i == 0)
        def _():
            dk_ref[...] = jnp.zeros_like(dk_ref)
            dv_ref[...] = jnp.zeros_like(dv_ref)
            dqg_ref[...] = jnp.zeros_like(dqg_ref)
            dkg_ref[...] = jnp.zeros_like(dkg_ref)
            dsk_ref[...] = jnp.zeros_like(dsk_ref)

        valid, distf, w0 = _band_mask(i, nb)
        k3 = jnp.concatenate([k0[...], k1[...], k2[...]], axis=0)
        v3 = jnp.concatenate([v0[...], v1[...], v2[...]], axis=0).astype(BF16)
        q = q_ref[...]
        o = o_ref[...]
        do = do_ref[...]
        qgv = qg_ref[...]
        kgv = kg_ref[...]
        rows = pl.ds(pl.multiple_of(w0 * BLK, BLK), 3 * BLK)
        dqg = jnp.zeros((1, HEAD_DIM), F32)
        dkg = jnp.zeros((1, HEAD_DIM), F32)
        for j in range(KV_HEADS):
            cols = slice(j * HEAD_DIM, (j + 1) * HEAD_DIM)
            kj = k3[:, cols]
            rk = _rms(kj)
            khat = kj * rk
            knj = (khat * kgv).astype(BF16)
            vj = v3[:, cols]
            dkn = jnp.zeros((3 * BLK, HEAD_DIM), F32)
            dvj = jnp.zeros((3 * BLK, HEAD_DIM), F32)
            for g in range(GQA_GROUP):
                h = j * GQA_GROUP + g
                hc = slice(h * HEAD_DIM, (h + 1) * HEAD_DIM)
                qh = q[:, hc]
                rq = _rms(qh)
                qhat = qh * rq
                qn = (qhat * qgv).astype(BF16)
                sc = _dot_nt(qn, knj) * 0.125 - SLOPES[h] * distf
                sc = jnp.where(valid, sc, NEG)
                lse_h = lse_ref[:, h:h + 1]
                p = jnp.exp(sc - lse_h)
                doh = do[:, hc]
                delta = jnp.sum(doh * o[:, hc], axis=-1, keepdims=True)
                dob = doh.astype(BF16)
                dp = _dot_nt(dob, vj)
                ds = p * (dp - delta)
                psink = jnp.exp(sink_ref[h] - lse_h)
                dsk_ref[:, h:h + 1] += -jnp.sum(psink * delta, axis=0, keepdims=True)
                dsb = (ds * 0.125).astype(BF16)
                dvj = dvj + _dot_tn(p.astype(BF16), dob)
                dqn = _dot(dsb, knj)
                dkn = dkn + _dot_tn(dsb, qn)
                dqg = dqg + jnp.sum(dqn * qhat, axis=0, keepdims=True)
                dq_ref[:, hc] = _rms_bwd(qhat, rq, dqn * qgv)
            dkg = dkg + jnp.sum(dkn * khat, axis=0, keepdims=True)
            dk_ref[rows, cols] += _rms_bwd(khat, rk, dkn * kgv)
            dv_ref[rows, cols] += dvj
        dqg_ref[...] += dqg
        dkg_ref[...] += dkg

    return _call(
        body, name, (nb,),
        [pl.BlockSpec(memory_space=pltpu.SMEM)] + _band_specs(nb)
        + [pl.BlockSpec((BLK, ATT_W), _row), pl.BlockSpec((BLK, ATT_W), _row), pl.BlockSpec((BLK, ATT_HEADS), _row),
           pl.BlockSpec((1, HEAD_DIM), _fixed), pl.BlockSpec((1, HEAD_DIM), _fixed)],
        [pl.BlockSpec((BLK, ATT_W), _row), pl.BlockSpec((s, KV_W), _fixed), pl.BlockSpec((s, KV_W), _fixed),
         pl.BlockSpec((1, HEAD_DIM), _fixed), pl.BlockSpec((1, HEAD_DIM), _fixed), pl.BlockSpec((1, ATT_HEADS), _fixed)],
        [jax.ShapeDtypeStruct((s, ATT_W), F32), jax.ShapeDtypeStruct((s, KV_W), F32), jax.ShapeDtypeStruct((s, KV_W), F32),
         jax.ShapeDtypeStruct((1, HEAD_DIM), F32), jax.ShapeDtypeStruct((1, HEAD_DIM), F32),
         jax.ShapeDtypeStruct((1, ATT_HEADS), F32)],
        ("arbitrary",), (sink, z, z, z, z, z, z, z, att, datt, lse, qg, kg), xch=xch)


def _group_steps(nc):
    nstep = nc // SEGS
    return nstep, min(32, nstep)


def _lanes4(ref):
    return jnp.concatenate([ref[q] for q in range(4)], axis=1)


def to_groups(src, col, name):
    s, w = src.shape
    nc = s // CH
    nstep, sb = _group_steps(nc)

    def body(u0, u1, u2, u3, o_ref):
        slot = lax.broadcasted_iota(jnp.int32, (sb, 128), 1) // SSM_H
        for seg in range(SEGS):
            for vc, u_ref in enumerate((u0, u1, u2, u3)):
                for sh in range(2):
                    accs = [None] * 8
                    for sl in range(8):
                        piece = u_ref[seg, pl.ds(sh * 8 + sl, sb, stride=CH), :]
                        for gl in range(8):
                            shift = ((sl - gl) * SSM_H) % 128
                            r = pltpu.roll(piece, shift, 1) if shift else piece
                            accs[gl] = r if sl == 0 else jnp.where(slot == sl, r, accs[gl])
                    for gl in range(8):
                        o_ref[(vc * 8 + gl) * 2 + sh, pl.ds(seg, sb, stride=SEGS), :] = accs[gl]

    src3 = src.reshape(SEGS, s // SEGS, w)
    (out,), _ = _call(body, name, (nstep // sb,),
                      [pl.BlockSpec((SEGS, sb * CH, 128), lambda i, c=col + vc: (0, i, c)) for vc in range(4)],
                      [pl.BlockSpec((2 * SSM_G, sb * SEGS, 128), lambda i: (0, i, 0))],
                      [jax.ShapeDtypeStruct((2 * SSM_G, nc, 128), F32)], ("parallel",), (src3,) * 4)
    return out


def from_groups(yc, name):
    nc = yc.shape[1]
    s = nc * CH
    nstep, sb = _group_steps(nc)

    def body(y_ref, o_ref):
        slot = lax.broadcasted_iota(jnp.int32, (sb, 128), 1) // SSM_H
        for seg in range(SEGS):
            for vc in range(4):
                for sh in range(2):
                    rows = [None] * 8
                    for gl in range(8):
                        piece = y_ref[(vc * 8 + gl) * 2 + sh, pl.ds(seg, sb, stride=SEGS), :]
                        for sl in range(8):
                            shift = ((gl - sl) * SSM_H) % 128
                            r = pltpu.roll(piece, shift, 1) if shift else piece
                            rows[sl] = r if gl == 0 else jnp.where(slot == gl, r, rows[sl])
                    for sl in range(8):
                        o_ref[vc, seg, pl.ds(sh * 8 + sl, sb, stride=CH), :] = rows[sl]

    (out,), _ = _call(body, name, (nstep // sb,),
                      [pl.BlockSpec((2 * SSM_G, sb * SEGS, 128), lambda i: (0, i, 0))],
                      [pl.BlockSpec((4, SEGS, sb * CH, 128), lambda i: (0, 0, i, 0))],
                      [jax.ShapeDtypeStruct((4, SEGS, s // SEGS, 128), F32)], ("parallel",), (yc,))
    return out.reshape(4, s, 128)


def _pair3(i):
    return (i, 0, 0)


def _state_blk(i):
    return (0, 0, i)


def ssm_in(ug, e, name):
    nc = ug.shape[1]

    def body(u_ref, e_ref, s_ref):
        u = _lanes4(u_ref).astype(BF16)
        sv = _dot(u[:, :GW], e_ref[0]) + _dot(u[:, GW:], e_ref[1])
        for q in range(4):
            s_ref[q] = sv[:, q * 128:(q + 1) * 128]

    (out,), _ = _call(body, name, (SSM_G // 2,),
                      [pl.BlockSpec((4, nc, 128), _pair3), pl.BlockSpec((2, GW, 512), _pair3)],
                      [pl.BlockSpec((4, nc, 128), _state_blk)],
                      [jax.ShapeDtypeStruct((4, nc, SSM_G * SSM_P), F32)], ("parallel",), (ug, e))
    return out


def chunk_scan(s4, a4, flip, name, xp4=None):
    _, nc, gp = s4.shape
    nstep = nc // SEGS
    assert nstep & (nstep - 1) == 0
    ct = 512
    with_da = xp4 is not None

    def body(*refs):
        if with_da:
            s_ref, a_ref, xp_ref, o_ref, da_ref = refs
        else:
            s_ref, a_ref, o_ref = refs
        rows = lax.broadcasted_iota(jnp.int32, (SEGS, ct), 0)
        zero = jnp.zeros((SEGS, ct), F32)
        for pair in range(2):
            asc = (pair == 0) != flip
            ir, ii = 2 * pair, 2 * pair + 1
            ar1 = a_ref[ir]
            ai1 = a_ref[ii]
            ar = jnp.broadcast_to(ar1, (SEGS, ct))
            ai = jnp.broadcast_to(ai1, (SEGS, ct))

            def tile(t):
                tt = t if asc else nstep - 1 - t
                return pl.ds(pl.multiple_of(tt * SEGS, SEGS), SEGS)

            def local(t, c):
                xr, xi = c
                sl = tile(t)
                return (ar * xr - ai * xi + s_ref[ir, sl, :], ar * xi + ai * xr + s_ref[ii, sl, :])

            er, ei = lax.fori_loop(0, nstep, local, (zero, zero))
            pr, pi = ar1, ai1
            for _ in range(nstep.bit_length() - 1):
                pr, pi = pr * pr - pi * pi, 2.0 * pr * pi
            cr = jnp.zeros((1, ct), F32)
            ci = jnp.zeros((1, ct), F32)
            xin_r, xin_i = zero, zero
            for k in range(SEGS):
                sg = k if asc else SEGS - 1 - k
                here = rows == sg
                xin_r = jnp.where(here, cr, xin_r)
                xin_i = jnp.where(here, ci, xin_i)
                lr = jnp.sum(jnp.where(here, er, 0.0), axis=0, keepdims=True)
                li = jnp.sum(jnp.where(here, ei, 0.0), axis=0, keepdims=True)
                cr, ci = pr * cr - pi * ci + lr, pr * ci + pi * cr + li

            def final(t, c):
                xr, xi, acr, aci = c
                sl = tile(t)
                o_ref[ir, sl, :] = xr
                o_ref[ii, sl, :] = xi
                if with_da:
                    br = xp_ref[ir, sl, :]
                    bi = xp_ref[ii, sl, :]
                    acr = acr + br * xr + bi * xi
                    aci = aci + br * xi - bi * xr
                return (ar * xr - ai * xi + s_ref[ir, sl, :], ar * xi + ai * xr + s_ref[ii, sl, :], acr, aci)

            _, _, acr, aci = lax.fori_loop(0, nstep, final, (xin_r, xin_i, zero, zero))
            if with_da:
                da_ref[ir] = jnp.sum(acr, axis=0, keepdims=True)
                da_ref[ii] = jnp.sum(aci, axis=0, keepdims=True)

    blk = pl.BlockSpec((4, nc, ct), _state_blk)
    ablk = pl.BlockSpec((4, 1, ct), _state_blk)
    sds = jax.ShapeDtypeStruct((4, nc, gp), F32)
    if with_da:
        out, _ = _call(body, name, (gp // ct,), [blk, ablk, blk], [blk, ablk],
                       [sds, jax.ShapeDtypeStruct((4, 1, gp), F32)], ("parallel",), (s4, a4, xp4))
        return out
    (out,), _ = _call(body, name, (gp // ct,), [blk, ablk], [blk], [sds], ("parallel",), (s4, a4))
    return out


def _state_cat(ref):
    return _lanes4(ref).astype(BF16)


def ssm_out(ug, t, xp4, o, name):
    nc = ug.shape[1]

    def body(u_ref, t_ref, xp_ref, o_ref, y_ref):
        x4 = _state_cat(xp_ref)
        u = _lanes4(u_ref).astype(BF16)
        for r in range(2):
            y = _dot(u[:, r * GW:(r + 1) * GW], t_ref[r]) + _dot(x4, o_ref[r])
            y_ref[2 * r] = y[:, :128]
            y_ref[2 * r + 1] = y[:, 128:]

    (out,), _ = _call(body, name, (SSM_G // 2,),
                      [pl.BlockSpec((4, nc, 128), _pair3), pl.BlockSpec((2, GW, GW), _pair3),
                       pl.BlockSpec((4, nc, 128), _state_blk), pl.BlockSpec((2, 512, GW), _pair3)],
                      [pl.BlockSpec((4, nc, 128), _pair3)],
                      [jax.ShapeDtypeStruct((2 * SSM_G, nc, 128), F32)], ("parallel",), (ug, t, xp4, o))
    return out


def ssm_out_bwd(dyg, xp4, o, name):
    nc = dyg.shape[1]

    def body(dy_ref, xp_ref, o_ref, do_ref, dxp_ref):
        x4 = _state_cat(xp_ref)
        dy = _lanes4(dy_ref).astype(BF16)
        do_ref[0] = _dot_tn(x4, dy[:, :GW])
        do_ref[1] = _dot_tn(x4, dy[:, GW:])
        acc = _dot_nt(dy[:, :GW], o_ref[0]) + _dot_nt(dy[:, GW:], o_ref[1])
        for q in range(4):
            dxp_ref[q] = acc[:, q * 128:(q + 1) * 128]

    out, _ = _call(body, name, (SSM_G // 2,),
                   [pl.BlockSpec((4, nc, 128), _pair3), pl.BlockSpec((4, nc, 128), _state_blk),
                    pl.BlockSpec((2, 512, GW), _pair3)],
                   [pl.BlockSpec((2, 512, GW), _pair3), pl.BlockSpec((4, nc, 128), _state_blk)],
                   [jax.ShapeDtypeStruct((SSM_G, 512, GW), F32), jax.ShapeDtypeStruct((4, nc, SSM_G * SSM_P), F32)],
                   ("parallel",), (dyg, xp4, o))
    return out


def ssm_in_bwd(ug, dyg, ds4, t, e, name):
    nc = ug.shape[1]

    def body(u_ref, dy_ref, ds_ref, t_ref, e_ref, dt_ref, de_ref, du_ref):
        ds = _state_cat(ds_ref)
        u = _lanes4(u_ref).astype(BF16)
        dy = _lanes4(dy_ref).astype(BF16)
        for r in range(2):
            cols = slice(r * GW, (r + 1) * GW)
            dt_ref[r] = _dot_tn(u[:, cols], dy[:, cols])
            de_ref[r] = _dot_tn(u[:, cols], ds)
            du = _dot_nt(dy[:, cols], t_ref[r]) + _dot_nt(ds, e_ref[r])
            du_ref[2 * r] = du[:, :128]
            du_ref[2 * r + 1] = du[:, 128:]

    out, _ = _call(body, name, (SSM_G // 2,),
                   [pl.BlockSpec((4, nc, 128), _pair3), pl.BlockSpec((4, nc, 128), _pair3),
                    pl.BlockSpec((4, nc, 128), _state_blk), pl.BlockSpec((2, GW, GW), _pair3),
                    pl.BlockSpec((2, GW, 512), _pair3)],
                   [pl.BlockSpec((2, GW, GW), _pair3), pl.BlockSpec((2, GW, 512), _pair3),
                    pl.BlockSpec((4, nc, 128), _pair3)],
                   [jax.ShapeDtypeStruct((SSM_G, GW, GW), F32), jax.ShapeDtypeStruct((SSM_G, GW, 512), F32),
                    jax.ShapeDtypeStruct((2 * SSM_G, nc, 128), F32)], ("parallel",), (ug, dyg, ds4, t, e))
    return out


def ssm_post_fwd(yc, z, dskip, wglu, name):
    s = yc.shape[1]

    def body(y_ref, u_ref, d_ref, w_ref, o_ref, yp_ref, g_ref):
        yp = _lanes4(y_ref) + d_ref[...] * u_ref[...]
        yp_ref[...] = yp
        gv = _dot(_gelu(yp).astype(BF16), w_ref[...])
        g_ref[...] = gv
        o_ref[...] = gv[:, :SSM_W] * _sigmoid(gv[:, SSM_W:])

    out, _ = _call(body, name, (s // TM,),
                   [pl.BlockSpec((4, TM, 128), lambda i: (0, i, 0)), pl.BlockSpec((TM, SSM_W), lambda i: (i, U0 // SSM_W)),
                    pl.BlockSpec((1, SSM_W), _fixed), pl.BlockSpec((SSM_W, 2 * SSM_W), _fixed)],
                   [pl.BlockSpec((TM, SSM_W), _row), pl.BlockSpec((TM, SSM_W), _row), pl.BlockSpec((TM, 2 * SSM_W), _row)],
                   [jax.ShapeDtypeStruct((s, SSM_W), F32), jax.ShapeDtypeStruct((s, SSM_W), F32),
                    jax.ShapeDtypeStruct((s, 2 * SSM_W), F32)], ("parallel",), (yc, z, dskip, wglu))
    return out


def ssm_post_bwd(dssm, gpre, ypre, z, dskip, wglu, name):
    s = dssm.shape[0]

    def body(do_ref, g_ref, yp_ref, u_ref, d_ref, w_ref, dy_ref, du_ref, dw_ref, dd_ref):
        @pl.when(pl.program_id(0) == 0)
        def _():
            dw_ref[...] = jnp.zeros_like(dw_ref)
            dd_ref[...] = jnp.zeros_like(dd_ref)

        gv = g_ref[...]
        val = gv[:, :SSM_W]
        sg = _sigmoid(gv[:, SSM_W:])
        do = do_ref[...]
        dg = jnp.concatenate([do * sg, do * val * sg * (1.0 - sg)], axis=1).astype(BF16)
        yp = yp_ref[...]
        dgl = _dot_nt(dg, w_ref[...])
        dw_ref[...] += _dot_tn(_gelu(yp).astype(BF16), dg)
        dyp = dgl * _gelu_grad(yp)
        dy_ref[...] = dyp
        du_ref[...] = dyp * d_ref[...]
        dd_ref[...] += jnp.sum(dyp * u_ref[...], axis=0, keepdims=True)

    out, _ = _call(body, name, (s // TM,),
                   [pl.BlockSpec((TM, SSM_W), _row), pl.BlockSpec((TM, 2 * SSM_W), _row), pl.BlockSpec((TM, SSM_W), _row),
                    pl.BlockSpec((TM, SSM_W), lambda i: (i, U0 // SSM_W)), pl.BlockSpec((1, SSM_W), _fixed),
                    pl.BlockSpec((SSM_W, 2 * SSM_W), _fixed)],
                   [pl.BlockSpec((TM, SSM_W), _row), pl.BlockSpec((TM, SSM_W), _row),
                    pl.BlockSpec((SSM_W, 2 * SSM_W), _fixed), pl.BlockSpec((1, SSM_W), _fixed)],
                   [jax.ShapeDtypeStruct((s, SSM_W), F32), jax.ShapeDtypeStruct((s, SSM_W), F32),
                    jax.ShapeDtypeStruct((SSM_W, 2 * SSM_W), F32), jax.ShapeDtypeStruct((1, SSM_W), F32)],
                   ("arbitrary",), (dssm, gpre, ypre, z, dskip, wglu))
    return out


def ssm_mats(lam_re, lam_im, log_dt, b_re, b_im, c_re, c_im):
    g, p, hh = SSM_G, SSM_P, SSM_H
    hp = lax.Precision.HIGHEST
    j = jnp.arange(CH + 1, dtype=F32).reshape(CH + 1, 1, 1, 1)
    dt = jnp.exp(log_dt)[..., None]
    mag = jnp.exp(j * (lam_re * dt)[None])
    ang = j * (lam_im * dt)[None]
    pr, pi = mag * jnp.cos(ang), mag * jnp.sin(ang)
    abr, abi = pr[1], pi[1]
    den = lam_re * lam_re + lam_im * lam_im
    zr = ((abr - 1.0) * lam_re + abi * lam_im) / den
    zi = (abi * lam_re - (abr - 1.0) * lam_im) / den
    bbr = zr[..., None] * b_re[None] - zi[..., None] * b_im[None]
    bbi = zr[..., None] * b_im[None] + zi[..., None] * b_re[None]
    car = c_re[None] * pr[:, :, :, None, :] - c_im[None] * pi[:, :, :, None, :]
    cai = c_re[None] * pi[:, :, :, None, :] + c_im[None] * pr[:, :, :, None, :]
    kk = (jnp.einsum("jdghp,dgpk->jdghk", car[:CH], bbr, precision=hp)
          - jnp.einsum("jdghp,dgpk->jdghk", cai[:CH], bbi, precision=hp))
    lag = np.arange(CH)[None, :, None] - np.arange(CH)[:, None, None]
    sel_f = (lag == np.arange(CH)[None, None, :]).astype(np.float32)
    sel_b = (-lag == np.arange(CH)[None, None, :]).astype(np.float32)
    tmat = (jnp.einsum("stj,jghk->gskth", sel_f, kk[:, 0], precision=hp)
            + jnp.einsum("stj,jghk->gskth", sel_b, kk[:, 1], precision=hp)).reshape(g, GW, GW)

    par = jax.nn.one_hot(np.arange(g) % 2, 2, dtype=F32)

    def e_part(pw_r, pw_i, d):
        er = pw_r[..., None] * bbr[d][None] - pw_i[..., None] * bbi[d][None]
        ei = pw_r[..., None] * bbi[d][None] + pw_i[..., None] * bbr[d][None]
        return [v.transpose(1, 0, 3, 2).reshape(g, GW, p) for v in (er, ei)]

    eparts = e_part(pr[:CH, 0][::-1], pi[:CH, 0][::-1], 0) + e_part(pr[:CH, 1], pi[:CH, 1], 1)
    emat = jnp.stack(eparts, axis=2)
    emat = (emat[:, :, :, None, :] * par[:, None, None, :, None]).reshape(g, GW, 8 * p)

    def o_part(cr_, ci_):
        return [v.transpose(1, 3, 0, 2).reshape(g, p, GW) for v in (cr_, -ci_)]

    oparts = o_part(car[1:, 0], cai[1:, 0]) + o_part(car[1:, 1][::-1], cai[1:, 1][::-1])
    omat = jnp.stack(oparts, axis=1)
    omat = (omat[:, :, None, :, :] * par[:, None, :, None, None]).reshape(g, 8 * p, GW)
    amat = jnp.stack([pr[CH, 0], pi[CH, 0], pr[CH, 1], pi[CH, 1]], axis=0).reshape(4, 1, g * p)
    return tmat, emat, omat, amat


def outproj_fwd(x, att, ssm, wo, name):
    s, d = x.shape

    def body(x_ref, a_ref, s_ref, w_ref, o_ref):
        o_ref[...] = (x_ref[...] + _dot(a_ref[...].astype(BF16), w_ref[0:ATT_W, :])
                      + _dot(s_ref[...].astype(BF16), w_ref[ATT_W:, :]))

    (out,), _ = _call(body, name, (s // TM,),
                      [pl.BlockSpec((TM, d), _row), pl.BlockSpec((TM, ATT_W), _row), pl.BlockSpec((TM, SSM_W), _row),
                       pl.BlockSpec((ATT_W + SSM_W, d), _fixed)],
                      [pl.BlockSpec((TM, d), _row)], [jax.ShapeDtypeStruct((s, d), F32)], ("parallel",),
                      (x, att, ssm, wo))
    return out


def outproj_bwd(dx1, att, ssm, wo, name):
    s, d = dx1.shape

    def body(dx_ref, a_ref, s_ref, w_ref, da_ref, ds_ref, dw_ref):
        @pl.when(pl.program_id(0) == 0)
        def _():
            dw_ref[...] = jnp.zeros_like(dw_ref)

        dxb = dx_ref[...].astype(BF16)
        da_ref[...] = _dot_nt(dxb, w_ref[0:ATT_W, :])
        ds_ref[...] = _dot_nt(dxb, w_ref[ATT_W:, :])
        dw_ref[0:ATT_W, :] += _dot_tn(a_ref[...].astype(BF16), dxb)
        dw_ref[ATT_W:, :] += _dot_tn(s_ref[...].astype(BF16), dxb)

    out, _ = _call(body, name, (s // TM,),
                   [pl.BlockSpec((TM, d), _row), pl.BlockSpec((TM, ATT_W), _row), pl.BlockSpec((TM, SSM_W), _row),
                    pl.BlockSpec((ATT_W + SSM_W, d), _fixed)],
                   [pl.BlockSpec((TM, ATT_W), _row), pl.BlockSpec((TM, SSM_W), _row),
                    pl.BlockSpec((ATT_W + SSM_W, d), _fixed)],
                   [jax.ShapeDtypeStruct((s, ATT_W), F32), jax.ShapeDtypeStruct((s, SSM_W), F32),
                    jax.ShapeDtypeStruct((ATT_W + SSM_W, d), F32)], ("arbitrary",), (dx1, att, ssm, wo))
    return out


def ffn_fwd(x1, gain, w1, w2, name, xch=None):
    s, d = x1.shape
    nch, _, fc = w1.shape

    def body(x_ref, g_ref, w1_ref, w2_ref, o_ref, h_ref, a_ref):
        @pl.when(pl.program_id(1) == 0)
        def _():
            xv = x_ref[...]
            h_ref[...] = (xv * _rms(xv) * g_ref[...]).astype(BF16)
            o_ref[...] = xv

        a = _dot(h_ref[...], w1_ref[...])
        a_ref[...] = a.astype(BF16)
        o_ref[...] += _dot(jnp.square(jnp.maximum(a, 0.0)).astype(BF16), w2_ref[...])

    return _call(
        body, name, (s // TM, nch),
        [pl.BlockSpec((TM, d), lambda i, k: (i, 0)), pl.BlockSpec((1, d), lambda i, k: (0, 0)),
         pl.BlockSpec((None, d, fc), lambda i, k: (k, 0, 0)), pl.BlockSpec((None, fc, d), lambda i, k: (k, 0, 0))],
        [pl.BlockSpec((TM, d), lambda i, k: (i, 0)), pl.BlockSpec((TM, d), lambda i, k: (i, 0)),
         pl.BlockSpec((TM, fc), lambda i, k: (i, k))],
        [jax.ShapeDtypeStruct((s, d), F32), jax.ShapeDtypeStruct((s, d), BF16), jax.ShapeDtypeStruct((s, nch * fc), BF16)],
        ("arbitrary", "arbitrary"), (x1, gain, w1, w2), xch=xch)


def ffn_bwd_tok(dx2, x1, gain, a, w1, w2, name, xch=None):
    s, d = x1.shape
    nch, _, fc = w1.shape

    def body(dx_ref, x_ref, g_ref, a_ref, w1_ref, w2_ref, da_ref, dx1_ref, dg_ref, dxb_ref, dh_ref):
        i = pl.program_id(0)
        k = pl.program_id(1)

        @pl.when(jnp.logical_and(i == 0, k == 0))
        def _():
            dg_ref[...] = jnp.zeros_like(dg_ref)

        @pl.when(k == 0)
        def _():
            dxb_ref[...] = dx_ref[...].astype(BF16)
            dh_ref[...] = jnp.zeros_like(dh_ref)

        dr = _dot_nt(dxb_ref[...], w2_ref[...])
        da = (dr * (2.0 * jnp.maximum(a_ref[...].astype(F32), 0.0))).astype(BF16)
        da_ref[...] = da
        dh_ref[...] += _dot_nt(da, w1_ref[...])

        @pl.when(k == nch - 1)
        def _():
            xv = x_ref[...]
            r = _rms(xv)
            xhat = xv * r
            dh = dh_ref[...]
            dg_ref[...] += jnp.sum(dh * xhat, axis=0, keepdims=True)
            dx1_ref[...] = dx_ref[...] + _rms_bwd(xhat, r, dh * g_ref[...])

    return _call(
        body, name, (s // TM, nch),
        [pl.BlockSpec((TM, d), lambda i, k: (i, 0)), pl.BlockSpec((TM, d), lambda i, k: (i, 0)),
         pl.BlockSpec((1, d), lambda i, k: (0, 0)), pl.BlockSpec((TM, fc), lambda i, k: (i, k)),
         pl.BlockSpec((None, d, fc), lambda i, k: (k, 0, 0)), pl.BlockSpec((None, fc, d), lambda i, k: (k, 0, 0))],
        [pl.BlockSpec((TM, fc), lambda i, k: (i, k)), pl.BlockSpec((TM, d), lambda i, k: (i, 0)),
         pl.BlockSpec((1, d), lambda i, k: (0, 0)), pl.BlockSpec((TM, d), lambda i, k: (i, 0))],
        [jax.ShapeDtypeStruct((s, nch * fc), BF16), jax.ShapeDtypeStruct((s, d), F32),
         jax.ShapeDtypeStruct((1, d), F32), jax.ShapeDtypeStruct((s, d), BF16)],
        ("arbitrary", "arbitrary"), (dx2, x1, gain, a, w1, w2), scratch=[pltpu.VMEM((TM, d), F32)], xch=xch)


def ffn_bwd_w(h2, da, a, dxb, nch, name, xch=None):
    s, d = h2.shape
    fc = a.shape[1] // nch

    def body(h_ref, da_ref, a_ref, dx_ref, dw1_ref, dw2_ref):
        @pl.when(pl.program_id(1) == 0)
        def _():
            dw1_ref[...] = jnp.zeros_like(dw1_ref)
            dw2_ref[...] = jnp.zeros_like(dw2_ref)

        dw1_ref[...] += _dot_tn(h_ref[...], da_ref[...])
        r = jnp.square(jnp.maximum(a_ref[...].astype(F32), 0.0)).astype(BF16)
        dw2_ref[...] += _dot_tn(r, dx_ref[...])

    return _call(
        body, name, (nch, s // TM),
        [pl.BlockSpec((TM, d), lambda k, t: (t, 0)), pl.BlockSpec((TM, fc), lambda k, t: (t, k)),
         pl.BlockSpec((TM, fc), lambda k, t: (t, k)), pl.BlockSpec((TM, d), lambda k, t: (t, 0))],
        [pl.BlockSpec((None, d, fc), lambda k, t: (k, 0, 0)), pl.BlockSpec((None, fc, d), lambda k, t: (k, 0, 0))],
        [jax.ShapeDtypeStruct((nch, d, fc), F32), jax.ShapeDtypeStruct((nch, fc, d), F32)],
        ("arbitrary", "arbitrary"), (h2, da, a, dxb), xch=xch)


def loss_grad(xf, tgt, name):
    s, d = xf.shape
    nt = s // TM

    def body(x_ref, t_ref, dx_ref, l_ref, acc_ref):
        i = pl.program_id(0)

        @pl.when(i == 0)
        def _():
            acc_ref[...] = jnp.zeros_like(acc_ref)

        e = x_ref[...] - t_ref[...]
        dx_ref[...] = e * (1.0 / d)
        acc_ref[...] += jnp.sum(e * e, axis=0, keepdims=True)

        @pl.when(i == nt - 1)
        def _():
            l_ref[...] = jnp.sum(acc_ref[...], axis=1, keepdims=True) * (0.5 / d)

    out, _ = _call(body, name, (nt,), [pl.BlockSpec((TM, d), _row), pl.BlockSpec((TM, d), _row)],
                   [pl.BlockSpec((TM, d), _row), pl.BlockSpec((1, 1), _fixed)],
                   [jax.ShapeDtypeStruct((s, d), F32), jax.ShapeDtypeStruct((1, 1), F32)], ("arbitrary",),
                   (xf, tgt), scratch=[pltpu.VMEM((1, d), F32)])
    return out


def adamw_sum(parts, w, m, v, br, name):
    nl = len(parts)
    _, r, c = parts[0].shape
    nb = r // br
    c1 = 1.0 - ADAM_B1 ** ADAM_STEP
    c2 = 1.0 - ADAM_B2 ** ADAM_STEP

    def body(*refs):
        p_refs = refs[:nl]
        w_ref, m_ref, v_ref, g_ref, d_ref, nm_ref, nv_ref = refs[nl:]
        for l in range(nl):
            @pl.when(pl.program_id(0) == l)
            def _(l=l):
                g = p_refs[l][0]
                for j in range(1, N_DEV):
                    g = g + p_refs[l][j]
                m2 = ADAM_B1 * m_ref[...] + (1.0 - ADAM_B1) * g
                v2 = ADAM_B2 * v_ref[...] + (1.0 - ADAM_B2) * jnp.square(g)
                g_ref[...] = g
                nm_ref[...] = m2
                nv_ref[...] = v2
                d_ref[...] = -ADAM_LR * ((m2 / c1) / (jnp.sqrt(v2 / c2) + ADAM_EPS) + ADAM_WD * w_ref[...])

    blk = pl.BlockSpec((br, c), lambda l, i: (l * nb + i, 0))
    pspecs = [pl.BlockSpec((N_DEV, br, c), lambda l, i, own=own: (0, jnp.where(l == own, i, 0), 0)) for own in range(nl)]
    sds = jax.ShapeDtypeStruct((nl * r, c), F32)
    out, _ = _call(body, name, (nl, nb), pspecs + [blk, blk, blk], [blk, blk, blk, blk], [sds, sds, sds, sds],
                   ("arbitrary", "arbitrary"), (*parts, w, m, v))
    return out


def layer_fwd(x, p, mats, l, xch):
    tmat, emat, omat, amat = mats
    z = norm_matmul(x, p["norm1"], p["w_in"], f"in_fwd{l}")
    att, lse = attn_fwd(z, p["q_gain"], p["k_gain"], p["sink"], f"attn_fwd{l}")
    ug = to_groups(z, U0 // 128, f"ssm_to_groups{l}")
    s4 = ssm_in(ug, emat, f"ssm_in{l}")
    xp4 = chunk_scan(s4, amat, False, f"ssm_scan{l}")
    yc = from_groups(ssm_out(ug, tmat, xp4, omat, f"ssm_out{l}"), f"ssm_from_groups{l}")
    ssm, ypre, gpre = ssm_post_fwd(yc, z, p["d_skip"], p["w_glu"], f"ssm_post{l}")
    x1 = outproj_fwd(x, att, ssm, p["w_out"], f"out_fwd{l}")
    (x2, h2, a), gathered = ffn_fwd(x1, p["norm2"], p["w_ff1"], p["w_ff2"], f"ffn_fwd{l}", xch)
    saved = dict(x=x, z=z, att=att, lse=lse, ug=ug, xp4=xp4, ssm=ssm, ypre=ypre, gpre=gpre, x1=x1, h2=h2, a=a)
    return x2, saved, gathered


def layer_bwd(dx2, p, mats, sv, l, pending):
    tmat, emat, omat, amat = mats
    nch = p["w_ff1"].shape[0]

    def xch(*names):
        return None if pending is None else ([pending[n] for n in names], True)

    (da, dx1, dnorm2, dxb), got1 = ffn_bwd_tok(dx2, sv["x1"], p["norm2"], sv["a"], p["w_ff1"], p["w_ff2"],
                                               f"ffn_bwd{l}", xch("w_ff1"))
    (dw1, dw2), got2 = ffn_bwd_w(sv["h2"], da, sv["a"], dxb, nch, f"ffn_bwdw{l}", xch("w_ff2"))
    datt, dssm, dwo = outproj_bwd(dx1, sv["att"], sv["ssm"], p["w_out"], f"out_bwd{l}")
    dyc, du_skip, dwglu, ddskip = ssm_post_bwd(dssm, sv["gpre"], sv["ypre"], sv["z"], p["d_skip"], p["w_glu"],
                                               f"ssm_post_bwd{l}")
    dyg = to_groups(dyc, 0, f"ssm_to_groups_bwd{l}")
    domat, dxp4 = ssm_out_bwd(dyg, sv["xp4"], omat, f"ssm_out_bwd{l}")
    aconj = amat * jnp.array([1.0, -1.0, 1.0, -1.0], F32).reshape(4, 1, 1)
    ds4, damat = chunk_scan(dxp4, aconj, True, f"ssm_scan_bwd{l}", xp4=sv["xp4"])
    dtmat, demat, dug = ssm_in_bwd(sv["ug"], dyg, ds4, tmat, emat, f"ssm_in_bwd{l}")
    du_core = from_groups(dug, f"ssm_from_groups_bwd{l}")
    (dq, dk, dv, dqg, dkg, dsink), got3 = attn_bwd(sv["z"], sv["att"], datt, sv["lse"], p["q_gain"], p["k_gain"],
                                                   p["sink"], f"attn_bwd{l}", xch("w_in", "w_glu", "w_out"))
    dx, dwin, dnorm1 = in_bwd(sv["x"], p["norm1"], p["w_in"], du_skip, du_core, dq, dk, dv, dx1, f"in_bwd{l}")
    grads = dict(norm1=dnorm1, w_in=dwin, q_gain=dqg, k_gain=dkg, sink=dsink, d_skip=ddskip, w_glu=dwglu,
                 w_out=dwo, norm2=dnorm2, w_ff1=dw1, w_ff2=dw2)
    got = None if pending is None else dict(w_ff1=got1[0], w_ff2=got2[0], w_in=got3[0], w_glu=got3[1], w_out=got3[2])
    return dx, grads, (dtmat, demat, domat, damat), got


def _whole_weights(gathered):
    g_in, g_glu, g_out, g_ff1, g_ff2 = gathered
    w_in = g_in.transpose(1, 0, 2).reshape(D_MODEL, IN_W)
    return dict(w_in=jnp.concatenate([w_in[:, V_END:], w_in[:, :V_END]], axis=1),
                w_glu=g_glu.transpose(1, 0, 2).reshape(SSM_W, 2 * SSM_W),
                w_out=g_out.reshape(ATT_W + SSM_W, D_MODEL), w_ff1=g_ff1, w_ff2=g_ff2)


def _grad_blocks(g):
    dwin = jnp.concatenate([g["w_in"][:, IN_W - V_END:], g["w_in"][:, :IN_W - V_END]], axis=1)
    return dict(w_in=dwin.reshape(D_MODEL, N_DEV, IN_W // N_DEV).transpose(1, 0, 2),
                w_glu=g["w_glu"].reshape(SSM_W, N_DEV, 2 * SSM_W // N_DEV).transpose(1, 0, 2),
                w_out=g["w_out"].reshape(N_DEV, (ATT_W + SSM_W) // N_DEV, D_MODEL),
                w_ff1=g["w_ff1"], w_ff2=g["w_ff2"])


def _small_rows(like):
    n = sum(int(np.prod(like[k].shape)) for k in SMALL)
    return -(-n // (128 * 512)) * 512


def _pack_small(vals, rows):
    flat = jnp.concatenate([vals[n].reshape(-1).astype(F32) for n in SMALL])
    return jnp.pad(flat, (0, rows * 128 - flat.shape[0])).reshape(rows, 128)


def _unpack_small(packed, like):
    flat = packed.reshape(-1)
    out, off = {}, 0
    for n in SMALL:
        size = int(np.prod(like[n].shape))
        out[n] = flat[off:off + size].reshape(like[n].shape)
        off += size
    return out


def kernel(x, norm1, w_in, q_gain, k_gain, sink, lam_re, lam_im, log_dt, b_re, b_im, c_re, c_im, d_skip, w_glu, w_out, norm2, w_ff1, w_ff2, loss_target, m_norm1, m_w_in, m_q_gain, m_k_gain, m_sink, m_lam_re, m_lam_im, m_log_dt, m_b_re, m_b_im, m_c_re, m_c_im, m_d_skip, m_w_glu, m_w_out, m_norm2, m_w_ff1, m_w_ff2, v_norm1, v_w_in, v_q_gain, v_k_gain, v_sink, v_lam_re, v_lam_im, v_log_dt, v_b_re, v_b_im, v_c_re, v_c_im, v_d_skip, v_w_glu, v_w_out, v_norm2, v_w_ff1, v_w_ff2):
    w = dict(norm1=norm1, w_in=w_in, q_gain=q_gain, k_gain=k_gain, sink=sink, lam_re=lam_re, lam_im=lam_im,
             log_dt=log_dt, b_re=b_re, b_im=b_im, c_re=c_re, c_im=c_im, d_skip=d_skip, w_glu=w_glu, w_out=w_out,
             norm2=norm2, w_ff1=w_ff1, w_ff2=w_ff2)
    m = dict(norm1=m_norm1, w_in=m_w_in, q_gain=m_q_gain, k_gain=m_k_gain, sink=m_sink, lam_re=m_lam_re,
             lam_im=m_lam_im, log_dt=m_log_dt, b_re=m_b_re, b_im=m_b_im, c_re=m_c_re, c_im=m_c_im, d_skip=m_d_skip,
             w_glu=m_w_glu, w_out=m_w_out, norm2=m_norm2, w_ff1=m_w_ff1, w_ff2=m_w_ff2)
    v = dict(norm1=v_norm1, w_in=v_w_in, q_gain=v_q_gain, k_gain=v_k_gain, sink=v_sink, lam_re=v_lam_re,
             lam_im=v_lam_im, log_dt=v_log_dt, b_re=v_b_re, b_im=v_b_im, c_re=v_c_re, c_im=v_c_im, d_skip=v_d_skip,
             w_glu=v_w_glu, w_out=v_w_out, norm2=v_norm2, w_ff1=v_w_ff1, w_ff2=v_w_ff2)
    nl = w_in.shape[0]
    shards = [[w[n][l].astype(BF16) for n in BIG] for l in range(nl)]
    (tmat, emat, omat, amat), mats_vjp = jax.vjp(jax.vmap(ssm_mats), *[w[n] for n in S5])
    tb, eb, ob = tmat.astype(BF16), emat.astype(BF16), omat.astype(BF16)

    gathered = exchange(shards[0], False, "gather_w0")
    xs = x[0]
    saved, lp, lm = [], [], []
    for l in range(nl):
        p = _whole_weights(gathered)
        for n in ("norm1", "q_gain", "k_gain", "d_skip", "norm2"):
            p[n] = w[n][l].reshape(1, -1)
        p["sink"] = sink[l]
        mats = (tb[l], eb[l], ob[l], amat[l])
        xs, sv, gathered = layer_fwd(xs, p, mats, l, (shards[l + 1], False) if l + 1 < nl else None)
        saved.append(sv)
        lp.append(p)
        lm.append(mats)
    dx, loss_part = loss_grad(xs, loss_target[0], "loss")
    loss = lax.psum(loss_part[0, 0], ("x", "y", "c"))

    grads, dmats, parts = [None] * nl, [None] * nl, [None] * nl
    pending = None
    for l in reversed(range(nl)):
        dx, grads[l], dmats[l], got = layer_bwd(dx, lp[l], lm[l], saved[l], l, pending)
        if got is not None:
            parts[l + 1] = got
        pending = _grad_blocks(grads[l])
    parts[0] = dict(zip(BIG, exchange([pending[n] for n in BIG], True, "exchange_g0")))

    gs = {n: jnp.stack([grads[l][n].reshape(w[n].shape[1:]) for l in range(nl)])
          for n in ("norm1", "q_gain", "k_gain", "sink", "d_skip", "norm2")}
    ds5 = mats_vjp(tuple(jnp.stack([dmats[l][i] for l in range(nl)]) for i in range(4)))
    gs.update(zip(S5, ds5))
    rows = _small_rows(w)
    (small_parts,) = exchange([_pack_small(gs, rows)], False, "gather_small_grads")

    out_g, out_d, out_m, out_v = {}, {}, {}, {}
    for n in BIG:
        c = w[n].shape[-1]
        r = int(np.prod(w[n].shape[:-1]))
        res = adamw_sum([parts[l][n] for l in range(nl)], w[n].reshape(r, c), m[n].reshape(r, c), v[n].reshape(r, c),
                        ADAM_ROWS[c], f"adamw_{n}")
        out_g[n], out_d[n], out_m[n], out_v[n] = (t.reshape(w[n].shape) for t in res)
    res = adamw_sum([small_parts], _pack_small(w, rows), _pack_small(m, rows), _pack_small(v, rows), 512, "adamw_small")
    for dst, packed in zip((out_g, out_d, out_m, out_v), res):
        dst.update(_unpack_small(packed, w))

    return (loss, dx[None], *[out_g[n] for n in WEIGHTS], *[out_d[n] for n in WEIGHTS],
            *[out_m[n] for n in WEIGHTS], *[out_v[n] for n in WEIGHTS])
```

```python
import numpy as np
import jax
import jax.numpy as jnp
from jax import lax
from jax.experimental import pallas as pl
from jax.experimental.pallas import tpu as pltpu

F32, BF16 = jnp.float32, jnp.bfloat16
EPS = 1e-6
D_MODEL = 1024
ATT_HEADS, KV_HEADS, GQA_GROUP, HEAD_DIM = 8, 2, 4, 64
ATT_W, KV_W, SSM_W, IN_W = 512, 128, 512, 1280
V_END = 768
U0, Q0, K0, V0 = 0, 512, 1024, 1152
BLK = 128
SSM_G, SSM_H, SSM_P = 32, 16, 64
CH = 16
GW = CH * SSM_H
SEGS = 8
N_DEV = 8
NEG = float(np.finfo(np.float32).min)
SLOPES = tuple(2.0 ** (-8.0 * (h + 1) / ATT_HEADS) for h in range(ATT_HEADS))
VMEM_LIMIT = 56 * 1024 * 1024
TM = 512

ADAM_LR, ADAM_B1, ADAM_B2, ADAM_EPS, ADAM_WD, ADAM_STEP = 0.001, 0.9, 0.999, 1e-08, 0.01, 10

SMALL = ("norm1", "q_gain", "k_gain", "sink", "lam_re", "lam_im", "log_dt", "b_re", "b_im",
         "c_re", "c_im", "d_skip", "norm2")
S5 = ("lam_re", "lam_im", "log_dt", "b_re", "b_im", "c_re", "c_im")
BIG = ("w_in", "w_glu", "w_out", "w_ff1", "w_ff2")
WEIGHTS = ("norm1", "w_in", "q_gain", "k_gain", "sink", "lam_re", "lam_im", "log_dt", "b_re", "b_im",
           "c_re", "c_im", "d_skip", "w_glu", "w_out", "norm2", "w_ff1", "w_ff2")
ADAM_ROWS = {160: 256, 128: 512, 512: 128, 1024: 64}


def _dot(a, b):
    return jnp.dot(a, b, preferred_element_type=F32)


def _dot_nt(a, b):
    return lax.dot_general(a, b, (((1,), (1,)), ((), ())), preferred_element_type=F32)


def _dot_tn(a, b):
    return lax.dot_general(a, b, (((0,), (0,)), ((), ())), preferred_element_type=F32)


def _rms(x):
    return lax.rsqrt(jnp.mean(x * x, axis=-1, keepdims=True) + EPS)


def _rms_bwd(xhat, r, dxhat):
    return r * (dxhat - xhat * jnp.mean(dxhat * xhat, axis=-1, keepdims=True))


def _sigmoid(x):
    return 1.0 / (1.0 + jnp.exp(-x))


_GC = 0.7978845608028654
_GA = 0.044715


def _gelu(x):
    return 0.5 * x * (1.0 + jnp.tanh(_GC * (x + _GA * x * x * x)))


def _gelu_grad(x):
    t = jnp.tanh(_GC * (x + _GA * x * x * x))
    return 0.5 * (1.0 + t) + 0.5 * x * (1.0 - t * t) * _GC * (1.0 + 3.0 * _GA * x * x)


def _row(i):
    return (i, 0)


def _fixed(i):
    return (0, 0)


def _me_and_peers():
    x, y, c = lax.axis_index("x"), lax.axis_index("y"), lax.axis_index("c")
    me = 4 * x + 2 * y + c
    peers = []
    for k in range(1, N_DEV):
        px = jnp.bitwise_xor(x, (k >> 2) & 1)
        py = jnp.bitwise_xor(y, (k >> 1) & 1)
        pc = jnp.bitwise_xor(c, k & 1)
        peers.append(((px, py, pc), 4 * px + 2 * py + pc))
    return me, peers


def _xch_copies(ins, outs, send_sems, recv_sems, loc_sems, scatter):
    me, peers = _me_and_peers()
    local, sends, recvs = [], [], []
    for a in range(len(ins)):
        local.append(pltpu.make_async_copy(ins[a].at[me] if scatter else ins[a], outs[a].at[me], loc_sems.at[a]))
    for k, (dev, idx) in enumerate(peers):
        for a in range(len(ins)):
            src = ins[a].at[idx] if scatter else ins[a]
            for dst, group in ((outs[a].at[me], sends), (outs[a].at[idx], recvs)):
                group.append(pltpu.make_async_remote_copy(
                    src_ref=src, dst_ref=dst, send_sem=send_sems.at[a, k], recv_sem=recv_sems.at[a, k],
                    device_id=dev, device_id_type=pl.DeviceIdType.MESH))
    return local, sends, recvs


def _xch_start(copies):
    local, sends, _ = copies
    for cp in local + sends:
        cp.start()


def _xch_wait(copies):
    local, sends, recvs = copies
    for cp in recvs:
        cp.wait_recv()
    for cp in sends:
        cp.wait_send()
    for cp in local:
        cp.wait()


def _xch_shapes(arrays, scatter):
    return [jax.ShapeDtypeStruct(a.shape if scatter else (N_DEV,) + a.shape, a.dtype) for a in arrays]


def _xch_sems(n):
    return [pltpu.SemaphoreType.DMA((n, N_DEV - 1)), pltpu.SemaphoreType.DMA((n, N_DEV - 1)),
            pltpu.SemaphoreType.DMA((n,))]


_ANY = pl.BlockSpec(memory_space=pl.ANY)


def exchange(arrays, scatter, name):
    n = len(arrays)

    def body(*refs):
        copies = _xch_copies(refs[:n], refs[n:2 * n], *refs[2 * n:], scatter)
        _xch_start(copies)
        _xch_wait(copies)

    return pl.pallas_call(
        body, name=name, in_specs=[_ANY] * n, out_specs=[_ANY] * n, out_shape=_xch_shapes(arrays, scatter),
        scratch_shapes=_xch_sems(n), compiler_params=pltpu.CompilerParams(has_side_effects=True),
    )(*arrays)


def _call(body, name, grid, in_specs, out_specs, out_shape, sem, inputs, scratch=(), xch=None):
    params = pltpu.CompilerParams(dimension_semantics=sem, vmem_limit_bytes=VMEM_LIMIT)
    if xch is None:
        out = pl.pallas_call(body, name=name, grid=grid, in_specs=in_specs, out_specs=out_specs, out_shape=out_shape,
                             scratch_shapes=list(scratch), compiler_params=params)(*inputs)
        return list(out), None
    arrays, scatter = xch
    n, nin, nout, nsc = len(arrays), len(in_specs), len(out_specs), len(scratch)

    def wrapped(*refs):
        ins, refs = refs[:nin], refs[nin:]
        xin, refs = refs[:n], refs[n:]
        outs, refs = refs[:nout], refs[nout:]
        xout, refs = refs[:n], refs[n:]
        sc, sems = refs[:nsc], refs[nsc:]
        first = last = None
        for ax, size in enumerate(grid):
            f, e = pl.program_id(ax) == 0, pl.program_id(ax) == size - 1
            first = f if first is None else jnp.logical_and(first, f)
            last = e if last is None else jnp.logical_and(last, e)

        @pl.when(first)
        def _():
            _xch_start(_xch_copies(xin, xout, *sems, scatter))

        body(*ins, *outs, *sc)

        @pl.when(last)
        def _():
            _xch_wait(_xch_copies(xin, xout, *sems, scatter))

    out = pl.pallas_call(
        wrapped, name=name, grid=grid, in_specs=list(in_specs) + [_ANY] * n, out_specs=list(out_specs) + [_ANY] * n,
        out_shape=list(out_shape) + _xch_shapes(arrays, scatter), scratch_shapes=list(scratch) + _xch_sems(n),
        compiler_params=params)(*inputs, *arrays)
    return list(out[:nout]), list(out[nout:])


def norm_matmul(x, gain, w, name):
    s, d = x.shape
    n = w.shape[1]

    def body(x_ref, g_ref, w_ref, z_ref):
        xv = x_ref[...]
        h = (xv * _rms(xv) * g_ref[...]).astype(BF16)
        z_ref[...] = _dot(h, w_ref[...])

    (z,), _ = _call(body, name, (s // TM,),
                    [pl.BlockSpec((TM, d), _row), pl.BlockSpec((1, d), _fixed), pl.BlockSpec((d, n), _fixed)],
                    [pl.BlockSpec((TM, n), _row)], [jax.ShapeDtypeStruct((s, n), F32)], ("parallel",), (x, gain, w))
    return z


def in_bwd(x, gain, w, du_a, du_b, dq, dk, dv, dres, name, xch=None):
    s, d = x.shape
    n = w.shape[1]

    def body(x_ref, g_ref, w_ref, dua_ref, dub_ref, dq_ref, dk_ref, dv_ref, dres_ref, dx_ref, dw_ref, dg_ref):
        @pl.when(pl.program_id(0) == 0)
        def _():
            dw_ref[...] = jnp.zeros_like(dw_ref)
            dg_ref[...] = jnp.zeros_like(dg_ref)

        xv = x_ref[...]
        r = _rms(xv)
        xhat = xv * r
        g = g_ref[...]
        h = (xhat * g).astype(BF16)
        dz = jnp.concatenate([(dua_ref[...] + _lanes4(dub_ref)).astype(BF16), dq_ref[...].astype(BF16),
                              dk_ref[...].astype(BF16), dv_ref[...].astype(BF16)], axis=1)
        dh = _dot_nt(dz, w_ref[...])
        dw_ref[...] += _dot_tn(h, dz)
        dg_ref[...] += jnp.sum(dh * xhat, axis=0, keepdims=True)
        dx_ref[...] = dres_ref[...] + _rms_bwd(xhat, r, dh * g)

    return _call(
        body, name, (s // TM,),
        [pl.BlockSpec((TM, d), _row), pl.BlockSpec((1, d), _fixed), pl.BlockSpec((d, n), _fixed),
         pl.BlockSpec((TM, SSM_W), _row), pl.BlockSpec((4, TM, 128), lambda i: (0, i, 0)), pl.BlockSpec((TM, ATT_W), _row),
         pl.BlockSpec((TM, KV_W), _row), pl.BlockSpec((TM, KV_W), _row), pl.BlockSpec((TM, d), _row)],
        [pl.BlockSpec((TM, d), _row), pl.BlockSpec((d, n), _fixed), pl.BlockSpec((1, d), _fixed)],
        [jax.ShapeDtypeStruct((s, d), F32), jax.ShapeDtypeStruct((d, n), F32), jax.ShapeDtypeStruct((1, d), F32)],
        ("arbitrary",), (x, gain, w, du_a, du_b, dq, dk, dv, dres), xch=xch)


def _band_specs(nb):
    def w0(i):
        return jnp.clip(i - 1, 0, nb - 3)

    specs = [pl.BlockSpec((BLK, ATT_W), lambda i: (i, Q0 // ATT_W))]
    for col in (K0 // KV_W, V0 // KV_W):
        specs += [pl.BlockSpec((BLK, KV_W), lambda i, c=col, o=o: (w0(i) + o, c)) for o in range(3)]
    return specs


def _band_mask(i, nb):
    w0 = jnp.clip(i - 1, 0, nb - 3)
    qi = lax.broadcasted_iota(jnp.int32, (BLK, 3 * BLK), 0)
    ci = lax.broadcasted_iota(jnp.int32, (BLK, 3 * BLK), 1)
    dist = jnp.abs((i - w0) * BLK + qi - ci)
    return dist <= BLK, dist.astype(F32), w0


def attn_fwd(z, qg, kg, sink, name, xch=None):
    s = z.shape[0]
    nb = s // BLK

    def body(sink_ref, q_ref, k0, k1, k2, v0, v1, v2, qg_ref, kg_ref, o_ref, lse_ref):
        valid, distf, _ = _band_mask(pl.program_id(0), nb)
        k3 = jnp.concatenate([k0[...], k1[...], k2[...]], axis=0)
        v3 = jnp.concatenate([v0[...], v1[...], v2[...]], axis=0).astype(BF16)
        q = q_ref[...]
        for j in range(KV_HEADS):
            kj = k3[:, j * HEAD_DIM:(j + 1) * HEAD_DIM]
            knj = (kj * _rms(kj) * kg_ref[...]).astype(BF16)
            vj = v3[:, j * HEAD_DIM:(j + 1) * HEAD_DIM]
            for g in range(GQA_GROUP):
                h = j * GQA_GROUP + g
                qh = q[:, h * HEAD_DIM:(h + 1) * HEAD_DIM]
                qn = (qh * _rms(qh) * qg_ref[...]).astype(BF16)
                sc = _dot_nt(qn, knj) * 0.125 - SLOPES[h] * distf
                sc = jnp.where(valid, sc, NEG)
                sk = sink_ref[h]
                m = jnp.maximum(jnp.max(sc, axis=-1, keepdims=True), sk)
                p = jnp.exp(sc - m)
                den = jnp.sum(p, axis=-1, keepdims=True) + jnp.exp(sk - m)
                o_ref[:, h * HEAD_DIM:(h + 1) * HEAD_DIM] = _dot((p / den).astype(BF16), vj)
                lse_ref[:, h:h + 1] = m + jnp.log(den)

    return _call(
        body, name, (nb,),
        [pl.BlockSpec(memory_space=pltpu.SMEM)] + _band_specs(nb)
        + [pl.BlockSpec((1, HEAD_DIM), _fixed), pl.BlockSpec((1, HEAD_DIM), _fixed)],
        [pl.BlockSpec((BLK, ATT_W), _row), pl.BlockSpec((BLK, ATT_HEADS), _row)],
        [jax.ShapeDtypeStruct((s, ATT_W), F32), jax.ShapeDtypeStruct((s, ATT_HEADS), F32)],
        ("arbitrary",), (sink, z, z, z, z, z, z, z, qg, kg), xch=xch)


def attn_bwd(z, att, datt, lse, qg, kg, sink, name, xch=None):
    s = z.shape[0]
    nb = s // BLK

    def body(sink_ref, q_ref, k0, k1, k2, v0, v1, v2, o_ref, do_ref, lse_ref, qg_ref, kg_ref,
             dq_ref, dk_ref, dv_ref, dqg_ref, dkg_ref, dsk_ref):
        i = pl.program_id(0)

        @pl.when(i == 0)
        def _():
            dk_ref[...] = jnp.zeros_like(dk_ref)
            dv_ref[...] = jnp.zeros_like(dv_ref)
            dqg_ref[...] = jnp.zeros_like(dqg_ref)
            dkg_ref[...] = jnp.zeros_like(dkg_ref)
            dsk_ref[...] = jnp.zeros_like(dsk_ref)

        valid, distf, w0 = _band_mask(i, nb)
        k3 = jnp.concatenate([k0[...], k1[...], k2[...]], axis=0)
        v3 = jnp.concatenate([v0[...], v1[...], v2[...]], axis=0).astype(BF16)
        q = q_ref[...]
        o = o_ref[...]
        do = do_ref[...]
        qgv = qg_ref[...]
        kgv = kg_ref[...]
        rows = pl.ds(pl.multiple_of(w0 * BLK, BLK), 3 * BLK)
        dqg = jnp.zeros((1, HEAD_DIM), F32)
        dkg = jnp.zeros((1, HEAD_DIM), F32)
        for j in range(KV_HEADS):
            cols = slice(j * HEAD_DIM, (j + 1) * HEAD_DIM)
            kj = k3[:, cols]
            rk = _rms(kj)
            khat = kj * rk
            knj = (khat * kgv).astype(BF16)
            vj = v3[:, cols]
            dkn = jnp.zeros((3 * BLK, HEAD_DIM), F32)
            dvj = jnp.zeros((3 * BLK, HEAD_DIM), F32)
            for g in range(GQA_GROUP):
                h = j * GQA_GROUP + g
                hc = slice(h * HEAD_DIM, (h + 1) * HEAD_DIM)
                qh = q[:, hc]
                rq = _rms(qh)
                qhat = qh * rq
                qn = (qhat * qgv).astype(BF16)
                sc = _dot_nt(qn, knj) * 0.125 - SLOPES[h] * distf
                sc = jnp.where(valid, sc, NEG)
                lse_h = lse_ref[:, h:h + 1]
                p = jnp.exp(sc - lse_h)
                doh = do[:, hc]
                delta = jnp.sum(doh * o[:, hc], axis=-1, keepdims=True)
                dob = doh.astype(BF16)
                dp = _dot_nt(dob, vj)
                ds = p * (dp - delta)
                psink = jnp.exp(sink_ref[h] - lse_h)
                dsk_ref[:, h:h + 1] += -jnp.sum(psink * delta, axis=0, keepdims=True)
                dsb = (ds * 0.125).astype(BF16)
                dvj = dvj + _dot_tn(p.astype(BF16), dob)
                dqn = _dot(dsb, knj)
                dkn = dkn + _dot_tn(dsb, qn)
                dqg = dqg + jnp.sum(dqn * qhat, axis=0, keepdims=True)
                dq_ref[:, hc] = _rms_bwd(qhat, rq, dqn * qgv)
            dkg = dkg + jnp.sum(dkn * khat, axis=0, keepdims=True)
            dk_ref[rows, cols] += _rms_bwd(khat, rk, dkn * kgv)
            dv_ref[rows, cols] += dvj
        dqg_ref[...] += dqg
        dkg_ref[...] += dkg

    return _call(
        body, name, (nb,),
        [pl.BlockSpec(memory_space=pltpu.SMEM)] + _band_specs(nb)
        + [pl.BlockSpec((BLK, ATT_W), _row), pl.BlockSpec((BLK, ATT_W), _row), pl.BlockSpec((BLK, ATT_HEADS), _row),
           pl.BlockSpec((1, HEAD_DIM), _fixed), pl.BlockSpec((1, HEAD_DIM), _fixed)],
        [pl.BlockSpec((BLK, ATT_W), _row), pl.BlockSpec((s, KV_W), _fixed), pl.BlockSpec((s, KV_W), _fixed),
         pl.BlockSpec((1, HEAD_DIM), _fixed), pl.BlockSpec((1, HEAD_DIM), _fixed), pl.BlockSpec((1, ATT_HEADS), _fixed)],
        [jax.ShapeDtypeStruct((s, ATT_W), F32), jax.ShapeDtypeStruct((s, KV_W), F32), jax.ShapeDtypeStruct((s, KV_W), F32),
         jax.ShapeDtypeStruct((1, HEAD_DIM), F32), jax.ShapeDtypeStruct((1, HEAD_DIM), F32),
         jax.ShapeDtypeStruct((1, ATT_HEADS), F32)],
        ("arbitrary",), (sink, z, z, z, z, z, z, z, att, datt, lse, qg, kg), xch=xch)


def _group_steps(nc):
    nstep = nc // SEGS
    return nstep, min(32, nstep)


def _lanes4(ref):
    return jnp.concatenate([ref[q] for q in range(4)], axis=1)


def to_groups(src, col, name):
    s, w = src.shape
    nc = s // CH
    nstep, sb = _group_steps(nc)

    def body(u0, u1, u2, u3, o_ref):
        slot = lax.broadcasted_iota(jnp.int32, (sb, 128), 1) // SSM_H
        for seg in range(SEGS):
            for vc, u_ref in enumerate((u0, u1, u2, u3)):
                for sh in range(2):
                    accs = [None] * 8
                    for sl in range(8):
                        piece = u_ref[seg, pl.ds(sh * 8 + sl, sb, stride=CH), :]
                        for gl in range(8):
                            shift = ((sl - gl) * SSM_H) % 128
                            r = pltpu.roll(piece, shift, 1) if shift else piece
                            accs[gl] = r if sl == 0 else jnp.where(slot == sl, r, accs[gl])
                    for gl in range(8):
                        o_ref[(vc * 8 + gl) * 2 + sh, pl.ds(seg, sb, stride=SEGS), :] = accs[gl]

    src3 = src.reshape(SEGS, s // SEGS, w)
    (out,), _ = _call(body, name, (nstep // sb,),
                      [pl.BlockSpec((SEGS, sb * CH, 128), lambda i, c=col + vc: (0, i, c)) for vc in range(4)],
                      [pl.BlockSpec((2 * SSM_G, sb * SEGS, 128), lambda i: (0, i, 0))],
                      [jax.ShapeDtypeStruct((2 * SSM_G, nc, 128), F32)], ("parallel",), (src3,) * 4)
    return out


def from_groups(yc, name):
    nc = yc.shape[1]
    s = nc * CH
    nstep, sb = _group_steps(nc)

    def body(y_ref, o_ref):
        slot = lax.broadcasted_iota(jnp.int32, (sb, 128), 1) // SSM_H
        for seg in range(SEGS):
            for vc in range(4):
                for sh in range(2):
                    rows = [None] * 8
                    for gl in range(8):
                        piece = y_ref[(vc * 8 + gl) * 2 + sh, pl.ds(seg, sb, stride=SEGS), :]
                        for sl in range(8):
                            shift = ((gl - sl) * SSM_H) % 128
                            r = pltpu.roll(piece, shift, 1) if shift else piece
                            rows[sl] = r if gl == 0 else jnp.where(slot == gl, r, rows[sl])
                    for sl in range(8):
                        o_ref[vc, seg, pl.ds(sh * 8 + sl, sb, stride=CH), :] = rows[sl]

    (out,), _ = _call(body, name, (nstep // sb,),
                      [pl.BlockSpec((2 * SSM_G, sb * SEGS, 128), lambda i: (0, i, 0))],
                      [pl.BlockSpec((4, SEGS, sb * CH, 128), lambda i: (0, 0, i, 0))],
                      [jax.ShapeDtypeStruct((4, SEGS, s // SEGS, 128), F32)], ("parallel",), (yc,))
    return out.reshape(4, s, 128)


def _pair3(i):
    return (i, 0, 0)


def _state_blk(i):
    return (0, 0, i)


def ssm_in(ug, e, name):
    nc = ug.shape[1]

    def body(u_ref, e_ref, s_ref):
        u = _lanes4(u_ref).astype(BF16)
        sv = _dot(u[:, :GW], e_ref[0]) + _dot(u[:, GW:], e_ref[1])
        for q in range(4):
            s_ref[q] = sv[:, q * 128:(q + 1) * 128]

    (out,), _ = _call(body, name, (SSM_G // 2,),
                      [pl.BlockSpec((4, nc, 128), _pair3), pl.BlockSpec((2, GW, 512), _pair3)],
                      [pl.BlockSpec((4, nc, 128), _state_blk)],
                      [jax.ShapeDtypeStruct((4, nc, SSM_G * SSM_P), F32)], ("parallel",), (ug, e))
    return out


def chunk_scan(s4, a4, flip, name, xp4=None):
    _, nc, gp = s4.shape
    nstep = nc // SEGS
    assert nstep & (nstep - 1) == 0
    ct = 512
    with_da = xp4 is not None

    def body(*refs):
        if with_da:
            s_ref, a_ref, xp_ref, o_ref, da_ref = refs
        else:
            s_ref, a_ref, o_ref = refs
        rows = lax.broadcasted_iota(jnp.int32, (SEGS, ct), 0)
        zero = jnp.zeros((SEGS, ct), F32)
        for pair in range(2):
            asc = (pair == 0) != flip
            ir, ii = 2 * pair, 2 * pair + 1
            ar1 = a_ref[ir]
            ai1 = a_ref[ii]
            ar = jnp.broadcast_to(ar1, (SEGS, ct))
            ai = jnp.broadcast_to(ai1, (SEGS, ct))

            def tile(t):
                tt = t if asc else nstep - 1 - t
                return pl.ds(pl.multiple_of(tt * SEGS, SEGS), SEGS)

            def local(t, c):
                xr, xi = c
                sl = tile(t)
                return (ar * xr - ai * xi + s_ref[ir, sl, :], ar * xi + ai * xr + s_ref[ii, sl, :])

            er, ei = lax.fori_loop(0, nstep, local, (zero, zero))
            pr, pi = ar1, ai1
            for _ in range(nstep.bit_length() - 1):
                pr, pi = pr * pr - pi * pi, 2.0 * pr * pi
            cr = jnp.zeros((1, ct), F32)
            ci = jnp.zeros((1, ct), F32)
            xin_r, xin_i = zero, zero
            for k in range(SEGS):
                sg = k if asc else SEGS - 1 - k
                here = rows == sg
                xin_r = jnp.where(here, cr, xin_r)
                xin_i = jnp.where(here, ci, xin_i)
                lr = jnp.sum(jnp.where(here, er, 0.0), axis=0, keepdims=True)
                li = jnp.sum(jnp.where(here, ei, 0.0), axis=0, keepdims=True)
                cr, ci = pr * cr - pi * ci + lr, pr * ci + pi * cr + li

            def final(t, c):
                xr, xi, acr, aci = c
                sl = tile(t)
                o_ref[ir, sl, :] = xr
                o_ref[ii, sl, :] = xi
                if with_da:
                    br = xp_ref[ir, sl, :]
                    bi = xp_ref[ii, sl, :]
                    acr = acr + br * xr + bi * xi
                    aci = aci + br * xi - bi * xr
                return (ar * xr - ai * xi + s_ref[ir, sl, :], ar * xi + ai * xr + s_ref[ii, sl, :], acr, aci)

            _, _, acr, aci = lax.fori_loop(0, nstep, final, (xin_r, xin_i, zero, zero))
            if with_da:
                da_ref[ir] = jnp.sum(acr, axis=0, keepdims=True)
                da_ref[ii] = jnp.sum(aci, axis=0, keepdims=True)

    blk = pl.BlockSpec((4, nc, ct), _state_blk)
    ablk = pl.BlockSpec((4, 1, ct), _state_blk)
    sds = jax.ShapeDtypeStruct((4, nc, gp), F32)
    if with_da:
        out, _ = _call(body, name, (gp // ct,), [blk, ablk, blk], [blk, ablk],
                       [sds, jax.ShapeDtypeStruct((4, 1, gp), F32)], ("parallel",), (s4, a4, xp4))
        return out
    (out,), _ = _call(body, name, (gp // ct,), [blk, ablk], [blk], [sds], ("parallel",), (s4, a4))
    return out


def _state_cat(ref):
    return _lanes4(ref).astype(BF16)


def ssm_out(ug, t, xp4, o, name):
    nc = ug.shape[1]

    def body(u_ref, t_ref, xp_ref, o_ref, y_ref):
        x4 = _state_cat(xp_ref)
        u = _lanes4(u_ref).astype(BF16)
        for r in range(2):
            y = _dot(u[:, r * GW:(r + 1) * GW], t_ref[r]) + _dot(x4, o_ref[r])
            y_ref[2 * r] = y[:, :128]
            y_ref[2 * r + 1] = y[:, 128:]

    (out,), _ = _call(body, name, (SSM_G // 2,),
                      [pl.BlockSpec((4, nc, 128), _pair3), pl.BlockSpec((2, GW, GW), _pair3),
                       pl.BlockSpec((4, nc, 128), _state_blk), pl.BlockSpec((2, 512, GW), _pair3)],
                      [pl.BlockSpec((4, nc, 128), _pair3)],
                      [jax.ShapeDtypeStruct((2 * SSM_G, nc, 128), F32)], ("parallel",), (ug, t, xp4, o))
    return out


def ssm_out_bwd(dyg, xp4, o, name):
    nc = dyg.shape[1]

    def body(dy_ref, xp_ref, o_ref, do_ref, dxp_ref):
        x4 = _state_cat(xp_ref)
        dy = _lanes4(dy_ref).astype(BF16)
        do_ref[0] = _dot_tn(x4, dy[:, :GW])
        do_ref[1] = _dot_tn(x4, dy[:, GW:])
        acc = _dot_nt(dy[:, :GW], o_ref[0]) + _dot_nt(dy[:, GW:], o_ref[1])
        for q in range(4):
            dxp_ref[q] = acc[:, q * 128:(q + 1) * 128]

    out, _ = _call(body, name, (SSM_G // 2,),
                   [pl.BlockSpec((4, nc, 128), _pair3), pl.BlockSpec((4, nc, 128), _state_blk),
                    pl.BlockSpec((2, 512, GW), _pair3)],
                   [pl.BlockSpec((2, 512, GW), _pair3), pl.BlockSpec((4, nc, 128), _state_blk)],
                   [jax.ShapeDtypeStruct((SSM_G, 512, GW), F32), jax.ShapeDtypeStruct((4, nc, SSM_G * SSM_P), F32)],
                   ("parallel",), (dyg, xp4, o))
    return out


def ssm_in_bwd(ug, dyg, ds4, t, e, name):
    nc = ug.shape[1]

    def body(u_ref, dy_ref, ds_ref, t_ref, e_ref, dt_ref, de_ref, du_ref):
        ds = _state_cat(ds_ref)
        u = _lanes4(u_ref).astype(BF16)
        dy = _lanes4(dy_ref).astype(BF16)
        for r in range(2):
            cols = slice(r * GW, (r + 1) * GW)
            dt_ref[r] = _dot_tn(u[:, cols], dy[:, cols])
            de_ref[r] = _dot_tn(u[:, cols], ds)
            du = _dot_nt(dy[:, cols], t_ref[r]) + _dot_nt(ds, e_ref[r])
            du_ref[2 * r] = du[:, :128]
            du_ref[2 * r + 1] = du[:, 128:]

    out, _ = _call(body, name, (SSM_G // 2,),
                   [pl.BlockSpec((4, nc, 128), _pair3), pl.BlockSpec((4, nc, 128), _pair3),
                    pl.BlockSpec((4, nc, 128), _state_blk), pl.BlockSpec((2, GW, GW), _pair3),
                    pl.BlockSpec((2, GW, 512), _pair3)],
                   [pl.BlockSpec((2, GW, GW), _pair3), pl.BlockSpec((2, GW, 512), _pair3),
                    pl.BlockSpec((4, nc, 128), _pair3)],
                   [jax.ShapeDtypeStruct((SSM_G, GW, GW), F32), jax.ShapeDtypeStruct((SSM_G, GW, 512), F32),
                    jax.ShapeDtypeStruct((2 * SSM_G, nc, 128), F32)], ("parallel",), (ug, dyg, ds4, t, e))
    return out


def ssm_post_fwd(yc, z, dskip, wglu, name):
    s = yc.shape[1]

    def body(y_ref, u_ref, d_ref, w_ref, o_ref, yp_ref, g_ref):
        yp = _lanes4(y_ref) + d_ref[...] * u_ref[...]
        yp_ref[...] = yp
        gv = _dot(_gelu(yp).astype(BF16), w_ref[...])
        g_ref[...] = gv
        o_ref[...] = gv[:, :SSM_W] * _sigmoid(gv[:, SSM_W:])

    out, _ = _call(body, name, (s // TM,),
                   [pl.BlockSpec((4, TM, 128), lambda i: (0, i, 0)), pl.BlockSpec((TM, SSM_W), lambda i: (i, U0 // SSM_W)),
                    pl.BlockSpec((1, SSM_W), _fixed), pl.BlockSpec((SSM_W, 2 * SSM_W), _fixed)],
                   [pl.BlockSpec((TM, SSM_W), _row), pl.BlockSpec((TM, SSM_W), _row), pl.BlockSpec((TM, 2 * SSM_W), _row)],
                   [jax.ShapeDtypeStruct((s, SSM_W), F32), jax.ShapeDtypeStruct((s, SSM_W), F32),
                    jax.ShapeDtypeStruct((s, 2 * SSM_W), F32)], ("parallel",), (yc, z, dskip, wglu))
    return out


def ssm_post_bwd(dssm, gpre, ypre, z, dskip, wglu, name):
    s = dssm.shape[0]

    def body(do_ref, g_ref, yp_ref, u_ref, d_ref, w_ref, dy_ref, du_ref, dw_ref, dd_ref):
        @pl.when(pl.program_id(0) == 0)
        def _():
            dw_ref[...] = jnp.zeros_like(dw_ref)
            dd_ref[...] = jnp.zeros_like(dd_ref)

        gv = g_ref[...]
        val = gv[:, :SSM_W]
        sg = _sigmoid(gv[:, SSM_W:])
        do = do_ref[...]
        dg = jnp.concatenate([do * sg, do * val * sg * (1.0 - sg)], axis=1).astype(BF16)
        yp = yp_ref[...]
        dgl = _dot_nt(dg, w_ref[...])
        dw_ref[...] += _dot_tn(_gelu(yp).astype(BF16), dg)
        dyp = dgl * _gelu_grad(yp)
        dy_ref[...] = dyp
        du_ref[...] = dyp * d_ref[...]
        dd_ref[...] += jnp.sum(dyp * u_ref[...], axis=0, keepdims=True)

    out, _ = _call(body, name, (s // TM,),
                   [pl.BlockSpec((TM, SSM_W), _row), pl.BlockSpec((TM, 2 * SSM_W), _row), pl.BlockSpec((TM, SSM_W), _row),
                    pl.BlockSpec((TM, SSM_W), lambda i: (i, U0 // SSM_W)), pl.BlockSpec((1, SSM_W), _fixed),
                    pl.BlockSpec((SSM_W, 2 * SSM_W), _fixed)],
                   [pl.BlockSpec((TM, SSM_W), _row), pl.BlockSpec((TM, SSM_W), _row),
                    pl.BlockSpec((SSM_W, 2 * SSM_W), _fixed), pl.BlockSpec((1, SSM_W), _fixed)],
                   [jax.ShapeDtypeStruct((s, SSM_W), F32), jax.ShapeDtypeStruct((s, SSM_W), F32),
                    jax.ShapeDtypeStruct((SSM_W, 2 * SSM_W), F32), jax.ShapeDtypeStruct((1, SSM_W), F32)],
                   ("arbitrary",), (dssm, gpre, ypre, z, dskip, wglu))
    return out


def _toeplitz_select():
    row = lax.broadcasted_iota(jnp.int32, (GW, CH * GW), 0)
    col = lax.broadcasted_iota(jnp.int32, (GW, CH * GW), 1)
    j, h2 = row // SSM_H, row % SSM_H
    s, t, h = col // GW, (col % GW) // SSM_H, col % SSM_H
    same = h2 == h
    return jnp.concatenate([same & (t - s == j), same & (s - t == j)], axis=0).astype(F32)


def ssm_mats(lam_re, lam_im, log_dt, b_re, b_im, c_re, c_im):
    g, p, hh = SSM_G, SSM_P, SSM_H
    hp = lax.Precision.HIGHEST
    jj = jnp.arange(CH + 1, dtype=F32)
    dt = jnp.exp(log_dt)[..., None]
    mag = jnp.exp((lam_re * dt)[..., None] * jj)
    ang = (lam_im * dt)[..., None] * jj
    pr, pi = mag * jnp.cos(ang), mag * jnp.sin(ang)
    abr, abi = pr[..., 1], pi[..., 1]
    den = lam_re * lam_re + lam_im * lam_im
    zr = ((abr - 1.0) * lam_re + abi * lam_im) / den
    zi = (abi * lam_re - (abr - 1.0) * lam_im) / den
    bbr = zr[..., None] * b_re[None] - zi[..., None] * b_im[None]
    bbi = zr[..., None] * b_im[None] + zi[..., None] * b_re[None]
    crt, cit = c_re.transpose(0, 1, 3, 2), c_im.transpose(0, 1, 3, 2)
    car = pr[..., None] * crt[..., None, :] - pi[..., None] * cit[..., None, :]
    cai = pr[..., None] * cit[..., None, :] + pi[..., None] * crt[..., None, :]
    lhs = jnp.concatenate([bbr, -bbi], axis=2).transpose(0, 1, 3, 2)
    rhs = jnp.concatenate([car[..., :CH, :], cai[..., :CH, :]], axis=2).reshape(2, g, 2 * p, GW)
    kt = jnp.einsum("dgkp,dgpn->dgkn", lhs, rhs, precision=hp)
    kcat = jnp.concatenate([kt[0], kt[1]], axis=-1).reshape(g * hh, 2 * GW)
    tmat = jnp.dot(kcat, _toeplitz_select(), precision=hp)
    tmat = tmat.reshape(g, hh, CH, GW).transpose(0, 2, 1, 3).reshape(g, GW, GW)

    par = jax.nn.one_hot(np.arange(g) % 2, 2, dtype=F32)

    def e_part(d, pw_r, pw_i):
        pw_r, pw_i = pw_r.transpose(0, 2, 1)[:, :, None, :], pw_i.transpose(0, 2, 1)[:, :, None, :]
        br, bi = bbr[d].transpose(0, 2, 1)[:, None], bbi[d].transpose(0, 2, 1)[:, None]
        return [pw_r * br - pw_i * bi, pw_r * bi + pw_i * br]

    eparts = (e_part(0, pr[0, ..., :CH][..., ::-1], pi[0, ..., :CH][..., ::-1])
              + e_part(1, pr[1, ..., :CH], pi[1, ..., :CH]))
    emat = jnp.stack(eparts, axis=3)
    emat = (emat[:, :, :, :, None, :] * par[:, None, None, None, :, None]).reshape(g, GW, 8 * p)

    oparts = [car[0, ..., 1:, :], -cai[0, ..., 1:, :], car[1, ..., 1:, :][..., ::-1, :], -cai[1, ..., 1:, :][..., ::-1, :]]
    omat = jnp.stack([v.reshape(g, p, GW) for v in oparts], axis=1)
    omat = (omat[:, :, None, :, :] * par[:, None, :, None, None]).reshape(g, 8 * p, GW)
    amat = jnp.stack([pr[0, ..., CH], pi[0, ..., CH], pr[1, ..., CH], pi[1, ..., CH]], axis=0).reshape(4, 1, g * p)
    return tmat, emat, omat, amat


def outproj_fwd(x, att, ssm, wo, name):
    s, d = x.shape

    def body(x_ref, a_ref, s_ref, w_ref, o_ref):
        o_ref[...] = (x_ref[...] + _dot(a_ref[...].astype(BF16), w_ref[0:ATT_W, :])
                      + _dot(s_ref[...].astype(BF16), w_ref[ATT_W:, :]))

    (out,), _ = _call(body, name, (s // TM,),
                      [pl.BlockSpec((TM, d), _row), pl.BlockSpec((TM, ATT_W), _row), pl.BlockSpec((TM, SSM_W), _row),
                       pl.BlockSpec((ATT_W + SSM_W, d), _fixed)],
                      [pl.BlockSpec((TM, d), _row)], [jax.ShapeDtypeStruct((s, d), F32)], ("parallel",),
                      (x, att, ssm, wo))
    return out


def outproj_bwd(dx1, att, ssm, wo, name):
    s, d = dx1.shape

    def body(dx_ref, a_ref, s_ref, w_ref, da_ref, ds_ref, dw_ref):
        @pl.when(pl.program_id(0) == 0)
        def _():
            dw_ref[...] = jnp.zeros_like(dw_ref)

        dxb = dx_ref[...].astype(BF16)
        da_ref[...] = _dot_nt(dxb, w_ref[0:ATT_W, :])
        ds_ref[...] = _dot_nt(dxb, w_ref[ATT_W:, :])
        dw_ref[0:ATT_W, :] += _dot_tn(a_ref[...].astype(BF16), dxb)
        dw_ref[ATT_W:, :] += _dot_tn(s_ref[...].astype(BF16), dxb)

    out, _ = _call(body, name, (s // TM,),
                   [pl.BlockSpec((TM, d), _row), pl.BlockSpec((TM, ATT_W), _row), pl.BlockSpec((TM, SSM_W), _row),
                    pl.BlockSpec((ATT_W + SSM_W, d), _fixed)],
                   [pl.BlockSpec((TM, ATT_W), _row), pl.BlockSpec((TM, SSM_W), _row),
                    pl.BlockSpec((ATT_W + SSM_W, d), _fixed)],
                   [jax.ShapeDtypeStruct((s, ATT_W), F32), jax.ShapeDtypeStruct((s, SSM_W), F32),
                    jax.ShapeDtypeStruct((ATT_W + SSM_W, d), F32)], ("arbitrary",), (dx1, att, ssm, wo))
    return out


def ffn_fwd(x1, gain, w1, w2, name, xch=None):
    s, d = x1.shape
    nch, _, fc = w1.shape

    def body(x_ref, g_ref, w1_ref, w2_ref, o_ref, h_ref, a_ref):
        @pl.when(pl.program_id(1) == 0)
        def _():
            xv = x_ref[...]
            h_ref[...] = (xv * _rms(xv) * g_ref[...]).astype(BF16)
            o_ref[...] = xv

        a = _dot(h_ref[...], w1_ref[...])
        a_ref[...] = a.astype(BF16)
        o_ref[...] += _dot(jnp.square(jnp.maximum(a, 0.0)).astype(BF16), w2_ref[...])

    return _call(
        body, name, (s // TM, nch),
        [pl.BlockSpec((TM, d), lambda i, k: (i, 0)), pl.BlockSpec((1, d), lambda i, k: (0, 0)),
         pl.BlockSpec((None, d, fc), lambda i, k: (k, 0, 0)), pl.BlockSpec((None, fc, d), lambda i, k: (k, 0, 0))],
        [pl.BlockSpec((TM, d), lambda i, k: (i, 0)), pl.BlockSpec((TM, d), lambda i, k: (i, 0)),
         pl.BlockSpec((TM, fc), lambda i, k: (i, k))],
        [jax.ShapeDtypeStruct((s, d), F32), jax.ShapeDtypeStruct((s, d), BF16), jax.ShapeDtypeStruct((s, nch * fc), BF16)],
        ("arbitrary", "arbitrary"), (x1, gain, w1, w2), xch=xch)


def ffn_bwd_tok(dx2, x1, gain, a, w1, w2, name, xch=None):
    s, d = x1.shape
    nch, _, fc = w1.shape

    def body(dx_ref, x_ref, g_ref, a_ref, w1_ref, w2_ref, da_ref, dx1_ref, dg_ref, dxb_ref, dh_ref):
        i = pl.program_id(0)
        k = pl.program_id(1)

        @pl.when(jnp.logical_and(i == 0, k == 0))
        def _():
            dg_ref[...] = jnp.zeros_like(dg_ref)

        @pl.when(k == 0)
        def _():
            dxb_ref[...] = dx_ref[...].astype(BF16)
            dh_ref[...] = jnp.zeros_like(dh_ref)

        dr = _dot_nt(dxb_ref[...], w2_ref[...])
        da = (dr * (2.0 * jnp.maximum(a_ref[...].astype(F32), 0.0))).astype(BF16)
        da_ref[...] = da
        dh_ref[...] += _dot_nt(da, w1_ref[...])

        @pl.when(k == nch - 1)
        def _():
            xv = x_ref[...]
            r = _rms(xv)
            xhat = xv * r
            dh = dh_ref[...]
            dg_ref[...] += jnp.sum(dh * xhat, axis=0, keepdims=True)
            dx1_ref[...] = dx_ref[...] + _rms_bwd(xhat, r, dh * g_ref[...])

    return _call(
        body, name, (s // TM, nch),
        [pl.BlockSpec((TM, d), lambda i, k: (i, 0)), pl.BlockSpec((TM, d), lambda i, k: (i, 0)),
         pl.BlockSpec((1, d), lambda i, k: (0, 0)), pl.BlockSpec((TM, fc), lambda i, k: (i, k)),
         pl.BlockSpec((None, d, fc), lambda i, k: (k, 0, 0)), pl.BlockSpec((None, fc, d), lambda i, k: (k, 0, 0))],
        [pl.BlockSpec((TM, fc), lambda i, k: (i, k)), pl.BlockSpec((TM, d), lambda i, k: (i, 0)),
         pl.BlockSpec((1, d), lambda i, k: (0, 0)), pl.BlockSpec((TM, d), lambda i, k: (i, 0))],
        [jax.ShapeDtypeStruct((s, nch * fc), BF16), jax.ShapeDtypeStruct((s, d), F32),
         jax.ShapeDtypeStruct((1, d), F32), jax.ShapeDtypeStruct((s, d), BF16)],
        ("arbitrary", "arbitrary"), (dx2, x1, gain, a, w1, w2), scratch=[pltpu.VMEM((TM, d), F32)], xch=xch)


def ffn_bwd_w(h2, da, a, dxb, nch, name, xch=None):
    s, d = h2.shape
    fc = a.shape[1] // nch

    def body(h_ref, da_ref, a_ref, dx_ref, dw1_ref, dw2_ref):
        @pl.when(pl.program_id(1) == 0)
        def _():
            dw1_ref[...] = jnp.zeros_like(dw1_ref)
            dw2_ref[...] = jnp.zeros_like(dw2_ref)

        dw1_ref[...] += _dot_tn(h_ref[...], da_ref[...])
        r = jnp.square(jnp.maximum(a_ref[...].astype(F32), 0.0)).astype(BF16)
        dw2_ref[...] += _dot_tn(r, dx_ref[...])

    return _call(
        body, name, (nch, s // TM),
        [pl.BlockSpec((TM, d), lambda k, t: (t, 0)), pl.BlockSpec((TM, fc), lambda k, t: (t, k)),
         pl.BlockSpec((TM, fc), lambda k, t: (t, k)), pl.BlockSpec((TM, d), lambda k, t: (t, 0))],
        [pl.BlockSpec((None, d, fc), lambda k, t: (k, 0, 0)), pl.BlockSpec((None, fc, d), lambda k, t: (k, 0, 0))],
        [jax.ShapeDtypeStruct((nch, d, fc), F32), jax.ShapeDtypeStruct((nch, fc, d), F32)],
        ("arbitrary", "arbitrary"), (h2, da, a, dxb), xch=xch)


def loss_grad(xf, tgt, name):
    s, d = xf.shape
    nt = s // TM

    def body(x_ref, t_ref, dx_ref, l_ref, acc_ref):
        i = pl.program_id(0)

        @pl.when(i == 0)
        def _():
            acc_ref[...] = jnp.zeros_like(acc_ref)

        e = x_ref[...] - t_ref[...]
        dx_ref[...] = e * (1.0 / d)
        acc_ref[...] += jnp.sum(e * e, axis=0, keepdims=True)

        @pl.when(i == nt - 1)
        def _():
            l_ref[...] = jnp.sum(acc_ref[...], axis=1, keepdims=True) * (0.5 / d)

    out, _ = _call(body, name, (nt,), [pl.BlockSpec((TM, d), _row), pl.BlockSpec((TM, d), _row)],
                   [pl.BlockSpec((TM, d), _row), pl.BlockSpec((1, 1), _fixed)],
                   [jax.ShapeDtypeStruct((s, d), F32), jax.ShapeDtypeStruct((1, 1), F32)], ("arbitrary",),
                   (xf, tgt), scratch=[pltpu.VMEM((1, d), F32)])
    return out


def adamw_sum(parts, w, m, v, br, name):
    nl = len(parts)
    npart, r, c = parts[0].shape
    nb = r // br
    c1 = 1.0 - ADAM_B1 ** ADAM_STEP
    c2 = 1.0 - ADAM_B2 ** ADAM_STEP

    def body(*refs):
        p_refs = refs[:nl]
        w_ref, m_ref, v_ref, g_ref, d_ref, nm_ref, nv_ref = refs[nl:]
        for l in range(nl):
            @pl.when(pl.program_id(0) == l)
            def _(l=l):
                g = p_refs[l][0]
                for j in range(1, npart):
                    g = g + p_refs[l][j]
                m2 = ADAM_B1 * m_ref[...] + (1.0 - ADAM_B1) * g
                v2 = ADAM_B2 * v_ref[...] + (1.0 - ADAM_B2) * jnp.square(g)
                g_ref[...] = g
                nm_ref[...] = m2
                nv_ref[...] = v2
                d_ref[...] = -ADAM_LR * ((m2 / c1) / (jnp.sqrt(v2 / c2) + ADAM_EPS) + ADAM_WD * w_ref[...])

    blk = pl.BlockSpec((br, c), lambda l, i: (l * nb + i, 0))
    pspecs = [pl.BlockSpec((npart, br, c), lambda l, i, own=own: (0, jnp.where(l == own, i, 0), 0)) for own in range(nl)]
    sds = jax.ShapeDtypeStruct((nl * r, c), F32)
    out, _ = _call(body, name, (nl, nb), pspecs + [blk, blk, blk], [blk, blk, blk, blk], [sds, sds, sds, sds],
                   ("arbitrary", "arbitrary"), (*parts, w, m, v))
    return out


def sum_parts(parts, name):
    n, r, c = parts.shape

    def body(p_ref, o_ref):
        g = p_ref[0]
        for j in range(1, n):
            g = g + p_ref[j]
        o_ref[...] = g

    (out,), _ = _call(body, name, (1,), [pl.BlockSpec((n, r, c), lambda i: (0, 0, 0))], [pl.BlockSpec((r, c), _fixed)],
                      [jax.ShapeDtypeStruct((r, c), F32)], ("arbitrary",), (parts,))
    return out


def layer_fwd(x, w_in, p, mats, l, late, nxt):
    tmat, emat, omat, amat = mats
    z = norm_matmul(x, p["norm1"], w_in, f"in_fwd{l}")
    (att, lse), got = attn_fwd(z, p["q_gain"], p["k_gain"], p["sink"], f"attn_fwd{l}", (late, False))
    p = dict(p, **_whole_weights(dict(zip(BIG[-len(got):], got))))
    ug = to_groups(z, U0 // 128, f"ssm_to_groups{l}")
    s4 = ssm_in(ug, emat, f"ssm_in{l}")
    xp4 = chunk_scan(s4, amat, False, f"ssm_scan{l}")
    yc = from_groups(ssm_out(ug, tmat, xp4, omat, f"ssm_out{l}"), f"ssm_from_groups{l}")
    ssm, ypre, gpre = ssm_post_fwd(yc, z, p["d_skip"], p["w_glu"], f"ssm_post{l}")
    x1 = outproj_fwd(x, att, ssm, p["w_out"], f"out_fwd{l}")
    (x2, h2, a), gathered = ffn_fwd(x1, p["norm2"], p["w_ff1"], p["w_ff2"], f"ffn_fwd{l}",
                                    None if nxt is None else (nxt, False))
    saved = dict(x=x, z=z, att=att, lse=lse, ug=ug, xp4=xp4, ssm=ssm, ypre=ypre, gpre=gpre, x1=x1, h2=h2, a=a)
    return x2, saved, dict(p, w_in=w_in), gathered


def layer_bwd(dx2, p, mats, sv, l, pending):
    tmat, emat, omat, amat = mats
    nch = p["w_ff1"].shape[0]
    (da, dx1, dnorm2, dxb), got_in = ffn_bwd_tok(dx2, sv["x1"], p["norm2"], sv["a"], p["w_ff1"], p["w_ff2"],
                                                 f"ffn_bwd{l}", None if pending is None else ([pending], True))
    (dw1, dw2), _ = ffn_bwd_w(sv["h2"], da, sv["a"], dxb, nch, f"ffn_bwdw{l}")
    datt, dssm, dwo = outproj_bwd(dx1, sv["att"], sv["ssm"], p["w_out"], f"out_bwd{l}")
    dyc, du_skip, dwglu, ddskip = ssm_post_bwd(dssm, sv["gpre"], sv["ypre"], sv["z"], p["d_skip"], p["w_glu"],
                                               f"ssm_post_bwd{l}")
    dyg = to_groups(dyc, 0, f"ssm_to_groups_bwd{l}")
    domat, dxp4 = ssm_out_bwd(dyg, sv["xp4"], omat, f"ssm_out_bwd{l}")
    aconj = amat * jnp.array([1.0, -1.0, 1.0, -1.0], F32).reshape(4, 1, 1)
    ds4, damat = chunk_scan(dxp4, aconj, True, f"ssm_scan_bwd{l}", xp4=sv["xp4"])
    dtmat, demat, dug = ssm_in_bwd(sv["ug"], dyg, ds4, tmat, emat, f"ssm_in_bwd{l}")
    du_core = from_groups(dug, f"ssm_from_groups_bwd{l}")
    (dq, dk, dv, dqg, dkg, dsink), got_ff = attn_bwd(sv["z"], sv["att"], datt, sv["lse"], p["q_gain"], p["k_gain"],
                                                     p["sink"], f"attn_bwd{l}", ([dw1, dw2], True))
    blocks = _grad_blocks(dict(w_glu=dwglu, w_out=dwo))
    (dx, dwin, dnorm1), got_mix = in_bwd(sv["x"], p["norm1"], p["w_in"], du_skip, du_core, dq, dk, dv, dx1, f"in_bwd{l}",
                                         ([blocks["w_glu"], blocks["w_out"]], True))
    grads = dict(norm1=dnorm1, q_gain=dqg, k_gain=dkg, sink=dsink, d_skip=ddskip, norm2=dnorm2)
    got = dict(w_ff1=got_ff[0], w_ff2=got_ff[1], w_glu=got_mix[0], w_out=got_mix[1],
               w_in=None if pending is None else got_in[0])
    return dx, grads, (dtmat, demat, domat, damat), got, _grad_blocks(dict(w_in=dwin))["w_in"]


def _whole_weights(gathered):
    out = {}
    for n, g in gathered.items():
        if n == "w_in":
            w_in = g.transpose(1, 0, 2).reshape(D_MODEL, IN_W)
            out[n] = jnp.concatenate([w_in[:, V_END:], w_in[:, :V_END]], axis=1)
        elif n == "w_glu":
            out[n] = g.transpose(1, 0, 2).reshape(SSM_W, 2 * SSM_W)
        elif n == "w_out":
            out[n] = g.reshape(ATT_W + SSM_W, D_MODEL)
        else:
            out[n] = g
    return out


def _grad_blocks(grads):
    out = {}
    for n, g in grads.items():
        if n == "w_in":
            g = jnp.concatenate([g[:, IN_W - V_END:], g[:, :IN_W - V_END]], axis=1)
            out[n] = g.reshape(D_MODEL, N_DEV, IN_W // N_DEV).transpose(1, 0, 2)
        elif n == "w_glu":
            out[n] = g.reshape(SSM_W, N_DEV, 2 * SSM_W // N_DEV).transpose(1, 0, 2)
        elif n == "w_out":
            out[n] = g.reshape(N_DEV, (ATT_W + SSM_W) // N_DEV, D_MODEL)
        else:
            out[n] = g
    return out


def _small_rows(like):
    n = sum(int(np.prod(like[k].shape)) for k in SMALL)
    return -(-n // (128 * 512)) * 512


def _pack_small(vals, rows):
    flat = jnp.concatenate([vals[n].reshape(-1).astype(F32) for n in SMALL])
    return jnp.pad(flat, (0, rows * 128 - flat.shape[0])).reshape(rows, 128)


def _unpack_small(packed, like):
    flat = packed.reshape(-1)
    out, off = {}, 0
    for n in SMALL:
        size = int(np.prod(like[n].shape))
        out[n] = flat[off:off + size].reshape(like[n].shape)
        off += size
    return out


def kernel(x, norm1, w_in, q_gain, k_gain, sink, lam_re, lam_im, log_dt, b_re, b_im, c_re, c_im, d_skip, w_glu, w_out, norm2, w_ff1, w_ff2, loss_target, m_norm1, m_w_in, m_q_gain, m_k_gain, m_sink, m_lam_re, m_lam_im, m_log_dt, m_b_re, m_b_im, m_c_re, m_c_im, m_d_skip, m_w_glu, m_w_out, m_norm2, m_w_ff1, m_w_ff2, v_norm1, v_w_in, v_q_gain, v_k_gain, v_sink, v_lam_re, v_lam_im, v_log_dt, v_b_re, v_b_im, v_c_re, v_c_im, v_d_skip, v_w_glu, v_w_out, v_norm2, v_w_ff1, v_w_ff2):
    w = dict(norm1=norm1, w_in=w_in, q_gain=q_gain, k_gain=k_gain, sink=sink, lam_re=lam_re, lam_im=lam_im,
             log_dt=log_dt, b_re=b_re, b_im=b_im, c_re=c_re, c_im=c_im, d_skip=d_skip, w_glu=w_glu, w_out=w_out,
             norm2=norm2, w_ff1=w_ff1, w_ff2=w_ff2)
    m = dict(norm1=m_norm1, w_in=m_w_in, q_gain=m_q_gain, k_gain=m_k_gain, sink=m_sink, lam_re=m_lam_re,
             lam_im=m_lam_im, log_dt=m_log_dt, b_re=m_b_re, b_im=m_b_im, c_re=m_c_re, c_im=m_c_im, d_skip=m_d_skip,
             w_glu=m_w_glu, w_out=m_w_out, norm2=m_norm2, w_ff1=m_w_ff1, w_ff2=m_w_ff2)
    v = dict(norm1=v_norm1, w_in=v_w_in, q_gain=v_q_gain, k_gain=v_k_gain, sink=v_sink, lam_re=v_lam_re,
             lam_im=v_lam_im, log_dt=v_log_dt, b_re=v_b_re, b_im=v_b_im, c_re=v_c_re, c_im=v_c_im, d_skip=v_d_skip,
             w_glu=v_w_glu, w_out=v_w_out, norm2=v_norm2, w_ff1=v_w_ff1, w_ff2=v_w_ff2)
    nl = w_in.shape[0]
    shards = [[w[n][l].astype(BF16) for n in BIG] for l in range(nl)]
    (tmat, emat, omat, amat), mats_vjp = jax.vjp(jax.vmap(ssm_mats), *[w[n] for n in S5])
    tb, eb, ob = tmat.astype(BF16), emat.astype(BF16), omat.astype(BF16)

    (g_in,) = exchange(shards[0][:1], False, "gather_w_in0")
    have = _whole_weights(dict(w_in=g_in))
    late = shards[0][1:]
    xs = x[0]
    saved, lp, lm = [], [], []
    for l in range(nl):
        p = {n: have[n] for n in have if n != "w_in"}
        for n in ("norm1", "q_gain", "k_gain", "d_skip", "norm2"):
            p[n] = w[n][l].reshape(1, -1)
        p["sink"] = sink[l]
        mats = (tb[l], eb[l], ob[l], amat[l])
        xs, sv, p, got = layer_fwd(xs, have["w_in"], p, mats, l, late, shards[l + 1][:3] if l + 1 < nl else None)
        if l + 1 < nl:
            have = _whole_weights(dict(zip(BIG[:3], got)))
            late = shards[l + 1][3:]
        saved.append(sv)
        lp.append(p)
        lm.append(mats)
    dx, loss_part = loss_grad(xs, loss_target[0], "loss")
    loss = lax.psum(loss_part[0, 0], ("x", "y", "c"))

    grads, dmats, parts = [None] * nl, [None] * nl, [dict() for _ in range(nl)]
    pending = None
    for l in reversed(range(nl)):
        dx, grads[l], dmats[l], got, pending = layer_bwd(dx, lp[l], lm[l], saved[l], l, pending)
        for n in ("w_ff1", "w_ff2", "w_glu", "w_out"):
            parts[l][n] = got[n]
        if l + 1 < nl:
            parts[l + 1]["w_in"] = got["w_in"]
    (parts[0]["w_in"],) = exchange([pending], True, "exchange_g_in0")

    gs = {n: jnp.stack([grads[l][n].reshape(w[n].shape[1:]) for l in range(nl)])
          for n in ("norm1", "q_gain", "k_gain", "sink", "d_skip", "norm2")}
    ds5 = mats_vjp(tuple(jnp.stack([dmats[l][i] for l in range(nl)]) for i in range(4)))
    gs.update(zip(S5, ds5))
    rows = _small_rows(w)
    (mine,) = exchange([_pack_small(gs, rows).reshape(N_DEV, rows // N_DEV, 128)], True, "scatter_small_grads")
    (small_sum,) = exchange([sum_parts(mine, "sum_small_grads")], False, "gather_small_grads")

    out_g, out_d, out_m, out_v = {}, {}, {}, {}
    for n in BIG:
        c = w[n].shape[-1]
        r = int(np.prod(w[n].shape[:-1]))
        res = adamw_sum([parts[l][n] for l in range(nl)], w[n].reshape(r, c), m[n].reshape(r, c), v[n].reshape(r, c),
                        ADAM_ROWS[c], f"adamw_{n}")
        out_g[n], out_d[n], out_m[n], out_v[n] = (t.reshape(w[n].shape) for t in res)
    res = adamw_sum([small_sum.reshape(1, rows, 128)], _pack_small(w, rows), _pack_small(m, rows),
                    _pack_small(v, rows), 512, "adamw_small")
    for dst, packed in zip((out_g, out_d, out_m, out_v), res):
        dst.update(_unpack_small(packed, w))

    return (loss, dx[None], *[out_g[n] for n in WEIGHTS], *[out_d[n] for n in WEIGHTS],
            *[out_m[n] for n in WEIGHTS], *[out_v[n] for n in WEIGHTS])
```

```python
import numpy as np
import jax
import jax.numpy as jnp
from jax import lax
from jax.experimental import pallas as pl
from jax.experimental.pallas import tpu as pltpu

F32, BF16 = jnp.float32, jnp.bfloat16
EPS = 1e-6
D_MODEL = 1024
ATT_HEADS, KV_HEADS, GQA_GROUP, HEAD_DIM = 8, 2, 4, 64
ATT_W, KV_W, SSM_W, IN_W = 512, 128, 512, 1280
V_END = 768
U0, Q0, K0, V0 = 0, 512, 1024, 1152
BLK = 128
SSM_G, SSM_H, SSM_P = 32, 16, 64
CH = 16
GW = CH * SSM_H
SEGS = 8
N_DEV = 8
NEG = float(np.finfo(np.float32).min)
SLOPES = tuple(2.0 ** (-8.0 * (h + 1) / ATT_HEADS) for h in range(ATT_HEADS))
VMEM_LIMIT = 56 * 1024 * 1024
TM = 512

ADAM_LR, ADAM_B1, ADAM_B2, ADAM_EPS, ADAM_WD, ADAM_STEP = 0.001, 0.9, 0.999, 1e-08, 0.01, 10

SMALL = ("norm1", "q_gain", "k_gain", "sink", "lam_re", "lam_im", "log_dt", "b_re", "b_im",
         "c_re", "c_im", "d_skip", "norm2")
S5 = ("lam_re", "lam_im", "log_dt", "b_re", "b_im", "c_re", "c_im")
BIG = ("w_in", "w_glu", "w_out", "w_ff1", "w_ff2")
WEIGHTS = ("norm1", "w_in", "q_gain", "k_gain", "sink", "lam_re", "lam_im", "log_dt", "b_re", "b_im",
           "c_re", "c_im", "d_skip", "w_glu", "w_out", "norm2", "w_ff1", "w_ff2")
ADAM_ROWS = {160: 256, 128: 512, 512: 128, 1024: 64}


def _dot(a, b):
    return jnp.dot(a, b, preferred_element_type=F32)


def _dot_nt(a, b):
    return lax.dot_general(a, b, (((1,), (1,)), ((), ())), preferred_element_type=F32)


def _dot_tn(a, b):
    return lax.dot_general(a, b, (((0,), (0,)), ((), ())), preferred_element_type=F32)


def _rms(x):
    return lax.rsqrt(jnp.mean(x * x, axis=-1, keepdims=True) + EPS)


def _rms_bwd(xhat, r, dxhat):
    return r * (dxhat - xhat * jnp.mean(dxhat * xhat, axis=-1, keepdims=True))


def _sigmoid(x):
    return 1.0 / (1.0 + jnp.exp(-x))


_GC = 0.7978845608028654
_GA = 0.044715


def _gelu(x):
    return 0.5 * x * (1.0 + jnp.tanh(_GC * (x + _GA * x * x * x)))


def _gelu_grad(x):
    t = jnp.tanh(_GC * (x + _GA * x * x * x))
    return 0.5 * (1.0 + t) + 0.5 * x * (1.0 - t * t) * _GC * (1.0 + 3.0 * _GA * x * x)


def _row(i):
    return (i, 0)


def _fixed(i):
    return (0, 0)


def _me_and_peers():
    x, y, c = lax.axis_index("x"), lax.axis_index("y"), lax.axis_index("c")
    me = 4 * x + 2 * y + c
    peers = []
    for k in range(1, N_DEV):
        px = jnp.bitwise_xor(x, (k >> 2) & 1)
        py = jnp.bitwise_xor(y, (k >> 1) & 1)
        pc = jnp.bitwise_xor(c, k & 1)
        peers.append(((px, py, pc), 4 * px + 2 * py + pc))
    return me, peers


def _xch_copies(ins, outs, send_sems, recv_sems, loc_sems, scatter):
    me, peers = _me_and_peers()
    local, sends, recvs = [], [], []
    for a in range(len(ins)):
        local.append(pltpu.make_async_copy(ins[a].at[me] if scatter else ins[a], outs[a].at[me], loc_sems.at[a]))
    for k, (dev, idx) in enumerate(peers):
        for a in range(len(ins)):
            src = ins[a].at[idx] if scatter else ins[a]
            for dst, group in ((outs[a].at[me], sends), (outs[a].at[idx], recvs)):
                group.append(pltpu.make_async_remote_copy(
                    src_ref=src, dst_ref=dst, send_sem=send_sems.at[a, k], recv_sem=recv_sems.at[a, k],
                    device_id=dev, device_id_type=pl.DeviceIdType.MESH))
    return local, sends, recvs


def _xch_start(copies):
    local, sends, _ = copies
    for cp in local + sends:
        cp.start()


def _xch_wait(copies):
    local, sends, recvs = copies
    for cp in recvs:
        cp.wait_recv()
    for cp in sends:
        cp.wait_send()
    for cp in local:
        cp.wait()


def _xch_shapes(arrays, scatter):
    return [jax.ShapeDtypeStruct(a.shape if scatter else (N_DEV,) + a.shape, a.dtype) for a in arrays]


def _xch_sems(n):
    return [pltpu.SemaphoreType.DMA((n, N_DEV - 1)), pltpu.SemaphoreType.DMA((n, N_DEV - 1)),
            pltpu.SemaphoreType.DMA((n,))]


_ANY = pl.BlockSpec(memory_space=pl.ANY)


def exchange(arrays, scatter, name):
    n = len(arrays)

    def body(*refs):
        copies = _xch_copies(refs[:n], refs[n:2 * n], *refs[2 * n:], scatter)
        _xch_start(copies)
        _xch_wait(copies)

    return pl.pallas_call(
        body, name=name, in_specs=[_ANY] * n, out_specs=[_ANY] * n, out_shape=_xch_shapes(arrays, scatter),
        scratch_shapes=_xch_sems(n), compiler_params=pltpu.CompilerParams(has_side_effects=True),
    )(*arrays)


def _call(body, name, grid, in_specs, out_specs, out_shape, sem, inputs, scratch=(), xch=None):
    params = pltpu.CompilerParams(dimension_semantics=sem, vmem_limit_bytes=VMEM_LIMIT)
    if xch is None:
        out = pl.pallas_call(body, name=name, grid=grid, in_specs=in_specs, out_specs=out_specs, out_shape=out_shape,
                             scratch_shapes=list(scratch), compiler_params=params)(*inputs)
        return list(out), None
    arrays, scatter = xch
    n, nin, nout, nsc = len(arrays), len(in_specs), len(out_specs), len(scratch)

    def wrapped(*refs):
        ins, refs = refs[:nin], refs[nin:]
        xin, refs = refs[:n], refs[n:]
        outs, refs = refs[:nout], refs[nout:]
        xout, refs = refs[:n], refs[n:]
        sc, sems = refs[:nsc], refs[nsc:]
        first = last = None
        for ax, size in enumerate(grid):
            f, e = pl.program_id(ax) == 0, pl.program_id(ax) == size - 1
            first = f if first is None else jnp.logical_and(first, f)
            last = e if last is None else jnp.logical_and(last, e)

        @pl.when(first)
        def _():
            _xch_start(_xch_copies(xin, xout, *sems, scatter))

        body(*ins, *outs, *sc)

        @pl.when(last)
        def _():
            _xch_wait(_xch_copies(xin, xout, *sems, scatter))

    out = pl.pallas_call(
        wrapped, name=name, grid=grid, in_specs=list(in_specs) + [_ANY] * n, out_specs=list(out_specs) + [_ANY] * n,
        out_shape=list(out_shape) + _xch_shapes(arrays, scatter), scratch_shapes=list(scratch) + _xch_sems(n),
        compiler_params=params)(*inputs, *arrays)
    return list(out[:nout]), list(out[nout:])


def norm_matmul(x, gain, w, name):
    s, d = x.shape
    n = w.shape[1]

    def body(x_ref, g_ref, w_ref, z_ref):
        xv = x_ref[...]
        h = (xv * _rms(xv) * g_ref[...]).astype(BF16)
        z_ref[...] = _dot(h, w_ref[...])

    (z,), _ = _call(body, name, (s // TM,),
                    [pl.BlockSpec((TM, d), _row), pl.BlockSpec((1, d), _fixed), pl.BlockSpec((d, n), _fixed)],
                    [pl.BlockSpec((TM, n), _row)], [jax.ShapeDtypeStruct((s, n), F32)], ("parallel",), (x, gain, w))
    return z


def in_bwd(x, gain, w, du_a, du_b, dq, dk, dv, dres, name, xch=None):
    s, d = x.shape
    n = w.shape[1]

    def body(x_ref, g_ref, w_ref, dua_ref, dub_ref, dq_ref, dk_ref, dv_ref, dres_ref, dx_ref, dw_ref, dg_ref):
        @pl.when(pl.program_id(0) == 0)
        def _():
            dw_ref[...] = jnp.zeros_like(dw_ref)
            dg_ref[...] = jnp.zeros_like(dg_ref)

        xv = x_ref[...]
        r = _rms(xv)
        xhat = xv * r
        g = g_ref[...]
        h = (xhat * g).astype(BF16)
        dz = jnp.concatenate([(dua_ref[...] + _lanes4(dub_ref)).astype(BF16), dq_ref[...].astype(BF16),
                              dk_ref[...].astype(BF16), dv_ref[...].astype(BF16)], axis=1)
        dh = _dot_nt(dz, w_ref[...])
        dw_ref[...] += _dot_tn(h, dz)
        dg_ref[...] += jnp.sum(dh * xhat, axis=0, keepdims=True)
        dx_ref[...] = dres_ref[...] + _rms_bwd(xhat, r, dh * g)

    return _call(
        body, name, (s // TM,),
        [pl.BlockSpec((TM, d), _row), pl.BlockSpec((1, d), _fixed), pl.BlockSpec((d, n), _fixed),
         pl.BlockSpec((TM, SSM_W), _row), pl.BlockSpec((4, TM, 128), lambda i: (0, i, 0)), pl.BlockSpec((TM, ATT_W), _row),
         pl.BlockSpec((TM, KV_W), _row), pl.BlockSpec((TM, KV_W), _row), pl.BlockSpec((TM, d), _row)],
        [pl.BlockSpec((TM, d), _row), pl.BlockSpec((d, n), _fixed), pl.BlockSpec((1, d), _fixed)],
        [jax.ShapeDtypeStruct((s, d), F32), jax.ShapeDtypeStruct((d, n), F32), jax.ShapeDtypeStruct((1, d), F32)],
        ("arbitrary",), (x, gain, w, du_a, du_b, dq, dk, dv, dres), xch=xch)


def _band_specs(nb):
    def w0(i):
        return jnp.clip(i - 1, 0, nb - 3)

    specs = [pl.BlockSpec((BLK, ATT_W), lambda i: (i, Q0 // ATT_W))]
    for col in (K0 // KV_W, V0 // KV_W):
        specs += [pl.BlockSpec((BLK, KV_W), lambda i, c=col, o=o: (w0(i) + o, c)) for o in range(3)]
    return specs


def _band_mask(i, nb):
    w0 = jnp.clip(i - 1, 0, nb - 3)
    qi = lax.broadcasted_iota(jnp.int32, (BLK, 3 * BLK), 0)
    ci = lax.broadcasted_iota(jnp.int32, (BLK, 3 * BLK), 1)
    dist = jnp.abs((i - w0) * BLK + qi - ci)
    return dist <= BLK, dist.astype(F32), w0


def attn_fwd(z, qg, kg, sink, name, xch=None):
    s = z.shape[0]
    nb = s // BLK

    def body(sink_ref, q_ref, k0, k1, k2, v0, v1, v2, qg_ref, kg_ref, o_ref, lse_ref):
        valid, distf, _ = _band_mask(pl.program_id(0), nb)
        k3 = jnp.concatenate([k0[...], k1[...], k2[...]], axis=0)
        v3 = jnp.concatenate([v0[...], v1[...], v2[...]], axis=0).astype(BF16)
        q = q_ref[...]
        for j in range(KV_HEADS):
            kj = k3[:, j * HEAD_DIM:(j + 1) * HEAD_DIM]
            knj = (kj * _rms(kj) * kg_ref[...]).astype(BF16)
            vj = v3[:, j * HEAD_DIM:(j + 1) * HEAD_DIM]
            for g in range(GQA_GROUP):
                h = j * GQA_GROUP + g
                qh = q[:, h * HEAD_DIM:(h + 1) * HEAD_DIM]
                qn = (qh * _rms(qh) * qg_ref[...]).astype(BF16)
                sc = _dot_nt(qn, knj) * 0.125 - SLOPES[h] * distf
                sc = jnp.where(valid, sc, NEG)
                sk = sink_ref[h]
                m = jnp.maximum(jnp.max(sc, axis=-1, keepdims=True), sk)
                p = jnp.exp(sc - m)
                den = jnp.sum(p, axis=-1, keepdims=True) + jnp.exp(sk - m)
                o_ref[:, h * HEAD_DIM:(h + 1) * HEAD_DIM] = _dot((p / den).astype(BF16), vj)
                lse_ref[:, h:h + 1] = m + jnp.log(den)

    return _call(
        body, name, (nb,),
        [pl.BlockSpec(memory_space=pltpu.SMEM)] + _band_specs(nb)
        + [pl.BlockSpec((1, HEAD_DIM), _fixed), pl.BlockSpec((1, HEAD_DIM), _fixed)],
        [pl.BlockSpec((BLK, ATT_W), _row), pl.BlockSpec((BLK, ATT_HEADS), _row)],
        [jax.ShapeDtypeStruct((s, ATT_W), F32), jax.ShapeDtypeStruct((s, ATT_HEADS), F32)],
        ("arbitrary",), (sink, z, z, z, z, z, z, z, qg, kg), xch=xch)


def attn_bwd(z, att, datt, lse, qg, kg, sink, name, xch=None):
    s = z.shape[0]
    nb = s // BLK

    def body(sink_ref, q_ref, k0, k1, k2, v0, v1, v2, o_ref, do_ref, lse_ref, qg_ref, kg_ref,
             dq_ref, dk_ref, dv_ref, dqg_ref, dkg_ref, dsk_ref):
        i = pl.program_id(0)

        @pl.when(i == 0)
        def _():
            dk_ref[...] = jnp.zeros_like(dk_ref)
            dv_ref[...] = jnp.zeros_like(dv_ref)
            dqg_ref[...] = jnp.zeros_like(dqg_ref)
            dkg_ref[...] = jnp.zeros_like(dkg_ref)
            dsk_ref[...] = jnp.zeros_like(dsk_ref)

        valid, distf, w0 = _band_mask(i, nb)
        k3 = jnp.concatenate([k0[...], k1[...], k2[...]], axis=0)
        v3 = jnp.concatenate([v0[...], v1[...], v2[...]], axis=0).astype(BF16)
        q = q_ref[...]
        o = o_ref[...]
        do = do_ref[...]
        qgv = qg_ref[...]
        kgv = kg_ref[...]
        rows = pl.ds(pl.multiple_of(w0 * BLK, BLK), 3 * BLK)
        dqg = jnp.zeros((1, HEAD_DIM), F32)
        dkg = jnp.zeros((1, HEAD_DIM), F32)
        for j in range(KV_HEADS):
            cols = slice(j * HEAD_DIM, (j + 1) * HEAD_DIM)
            kj = k3[:, cols]
            rk = _rms(kj)
            khat = kj * rk
            knj = (khat * kgv).astype(BF16)
            vj = v3[:, cols]
            dkn = jnp.zeros((3 * BLK, HEAD_DIM), F32)
            dvj = jnp.zeros((3 * BLK, HEAD_DIM), F32)
            for g in range(GQA_GROUP):
                h = j * GQA_GROUP + g
                hc = slice(h * HEAD_DIM, (h + 1) * HEAD_DIM)
                qh = q[:, hc]
                rq = _rms(qh)
                qhat = qh * rq
                qn = (qhat * qgv).astype(BF16)
                sc = _dot_nt(qn, knj) * 0.125 - SLOPES[h] * distf
                sc = jnp.where(valid, sc, NEG)
                lse_h = lse_ref[:, h:h + 1]
                p = jnp.exp(sc - lse_h)
                doh = do[:, hc]
                delta = jnp.sum(doh * o[:, hc], axis=-1, keepdims=True)
                dob = doh.astype(BF16)
                dp = _dot_nt(dob, vj)
                ds = p * (dp - delta)
                psink = jnp.exp(sink_ref[h] - lse_h)
                dsk_ref[:, h:h + 1] += -jnp.sum(psink * delta, axis=0, keepdims=True)
                dsb = (ds * 0.125).astype(BF16)
                dvj = dvj + _dot_tn(p.astype(BF16), dob)
                dqn = _dot(dsb, knj)
                dkn = dkn + _dot_tn(dsb, qn)
                dqg = dqg + jnp.sum(dqn * qhat, axis=0, keepdims=True)
                dq_ref[:, hc] = _rms_bwd(qhat, rq, dqn * qgv)
            dkg = dkg + jnp.sum(dkn * khat, axis=0, keepdims=True)
            dk_ref[rows, cols] += _rms_bwd(khat, rk, dkn * kgv)
            dv_ref[rows, cols] += dvj
        dqg_ref[...] += dqg
        dkg_ref[...] += dkg

    return _call(
        body, name, (nb,),
        [pl.BlockSpec(memory_space=pltpu.SMEM)] + _band_specs(nb)
        + [pl.BlockSpec((BLK, ATT_W), _row), pl.BlockSpec((BLK, ATT_W), _row), pl.BlockSpec((BLK, ATT_HEADS), _row),
           pl.BlockSpec((1, HEAD_DIM), _fixed), pl.BlockSpec((1, HEAD_DIM), _fixed)],
        [pl.BlockSpec((BLK, ATT_W), _row), pl.BlockSpec((s, KV_W), _fixed), pl.BlockSpec((s, KV_W), _fixed),
         pl.BlockSpec((1, HEAD_DIM), _fixed), pl.BlockSpec((1, HEAD_DIM), _fixed), pl.BlockSpec((1, ATT_HEADS), _fixed)],
        [jax.ShapeDtypeStruct((s, ATT_W), F32), jax.ShapeDtypeStruct((s, KV_W), F32), jax.ShapeDtypeStruct((s, KV_W), F32),
         jax.ShapeDtypeStruct((1, HEAD_DIM), F32), jax.ShapeDtypeStruct((1, HEAD_DIM), F32),
         jax.ShapeDtypeStruct((1, ATT_HEADS), F32)],
        ("arbitrary",), (sink, z, z, z, z, z, z, z, att, datt, lse, qg, kg), xch=xch)


def _group_steps(nc):
    nstep = nc // SEGS
    return nstep, min(32, nstep)


def _lanes4(ref):
    return jnp.concatenate([ref[q] for q in range(4)], axis=1)


def _pair_split(x4):
    return [jnp.concatenate([x4[:, q * 128 + r * SSM_P:q * 128 + (r + 1) * SSM_P] for q in range(4)], axis=1)
            for r in range(2)]


def _pair_merge(a0, a1):
    return [jnp.concatenate([a[:, q * SSM_P:(q + 1) * SSM_P] for a in (a0, a1)], axis=1) for q in range(4)]


def to_groups(src, col, name):
    s, w = src.shape
    nc = s // CH
    nstep, sb = _group_steps(nc)

    def body(u0, u1, u2, u3, o_ref):
        slot = lax.broadcasted_iota(jnp.int32, (sb, 128), 1) // SSM_H
        for seg in range(SEGS):
            for vc, u_ref in enumerate((u0, u1, u2, u3)):
                for sh in range(2):
                    accs = [None] * 8
                    for sl in range(8):
                        piece = u_ref[seg, pl.ds(sh * 8 + sl, sb, stride=CH), :]
                        for gl in range(8):
                            shift = ((sl - gl) * SSM_H) % 128
                            r = pltpu.roll(piece, shift, 1) if shift else piece
                            accs[gl] = r if sl == 0 else jnp.where(slot == sl, r, accs[gl])
                    for gl in range(8):
                        o_ref[(vc * 8 + gl) * 2 + sh, pl.ds(seg, sb, stride=SEGS), :] = accs[gl]

    src3 = src.reshape(SEGS, s // SEGS, w)
    (out,), _ = _call(body, name, (nstep // sb,),
                      [pl.BlockSpec((SEGS, sb * CH, 128), lambda i, c=col + vc: (0, i, c)) for vc in range(4)],
                      [pl.BlockSpec((2 * SSM_G, sb * SEGS, 128), lambda i: (0, i, 0))],
                      [jax.ShapeDtypeStruct((2 * SSM_G, nc, 128), F32)], ("parallel",), (src3,) * 4)
    return out


def from_groups(yc, name):
    nc = yc.shape[1]
    s = nc * CH
    nstep, sb = _group_steps(nc)

    def body(y_ref, o_ref):
        slot = lax.broadcasted_iota(jnp.int32, (sb, 128), 1) // SSM_H
        for seg in range(SEGS):
            for vc in range(4):
                for sh in range(2):
                    rows = [None] * 8
                    for gl in range(8):
                        piece = y_ref[(vc * 8 + gl) * 2 + sh, pl.ds(seg, sb, stride=SEGS), :]
                        for sl in range(8):
                            shift = ((gl - sl) * SSM_H) % 128
                            r = pltpu.roll(piece, shift, 1) if shift else piece
                            rows[sl] = r if gl == 0 else jnp.where(slot == gl, r, rows[sl])
                    for sl in range(8):
                        o_ref[vc, seg, pl.ds(sh * 8 + sl, sb, stride=CH), :] = rows[sl]

    (out,), _ = _call(body, name, (nstep // sb,),
                      [pl.BlockSpec((2 * SSM_G, sb * SEGS, 128), lambda i: (0, i, 0))],
                      [pl.BlockSpec((4, SEGS, sb * CH, 128), lambda i: (0, 0, i, 0))],
                      [jax.ShapeDtypeStruct((4, SEGS, s // SEGS, 128), F32)], ("parallel",), (yc,))
    return out.reshape(4, s, 128)


def _pair3(i):
    return (i, 0, 0)


def _state_blk(i):
    return (0, 0, i)


def ssm_in(ug, e, name):
    nc = ug.shape[1]

    def body(u_ref, e_ref, s_ref):
        u = _lanes4(u_ref).astype(BF16)
        for q, blk in enumerate(_pair_merge(_dot(u[:, :GW], e_ref[0]), _dot(u[:, GW:], e_ref[1]))):
            s_ref[q] = blk

    (out,), _ = _call(body, name, (SSM_G // 2,),
                      [pl.BlockSpec((4, nc, 128), _pair3), pl.BlockSpec((2, GW, 4 * SSM_P), _pair3)],
                      [pl.BlockSpec((4, nc, 128), _state_blk)],
                      [jax.ShapeDtypeStruct((4, nc, SSM_G * SSM_P), F32)], ("parallel",), (ug, e))
    return out


def chunk_scan(s4, a4, flip, name, xp4=None):
    _, nc, gp = s4.shape
    nstep = nc // SEGS
    assert nstep & (nstep - 1) == 0
    ct = 512
    with_da = xp4 is not None

    def body(*refs):
        if with_da:
            s_ref, a_ref, xp_ref, o_ref, da_ref = refs
        else:
            s_ref, a_ref, o_ref = refs
        rows = lax.broadcasted_iota(jnp.int32, (SEGS, ct), 0)
        zero = jnp.zeros((SEGS, ct), F32)
        for pair in range(2):
            asc = (pair == 0) != flip
            ir, ii = 2 * pair, 2 * pair + 1
            ar1 = a_ref[ir]
            ai1 = a_ref[ii]
            ar = jnp.broadcast_to(ar1, (SEGS, ct))
            ai = jnp.broadcast_to(ai1, (SEGS, ct))

            def tile(t):
                tt = t if asc else nstep - 1 - t
                return pl.ds(pl.multiple_of(tt * SEGS, SEGS), SEGS)

            def local(t, c):
                xr, xi = c
                sl = tile(t)
                return (ar * xr - ai * xi + s_ref[ir, sl, :], ar * xi + ai * xr + s_ref[ii, sl, :])

            er, ei = lax.fori_loop(0, nstep, local, (zero, zero))
            pr, pi = ar1, ai1
            for _ in range(nstep.bit_length() - 1):
                pr, pi = pr * pr - pi * pi, 2.0 * pr * pi
            cr = jnp.zeros((1, ct), F32)
            ci = jnp.zeros((1, ct), F32)
            xin_r, xin_i = zero, zero
            for k in range(SEGS):
                sg = k if asc else SEGS - 1 - k
                here = rows == sg
                xin_r = jnp.where(here, cr, xin_r)
                xin_i = jnp.where(here, ci, xin_i)
                lr = jnp.sum(jnp.where(here, er, 0.0), axis=0, keepdims=True)
                li = jnp.sum(jnp.where(here, ei, 0.0), axis=0, keepdims=True)
                cr, ci = pr * cr - pi * ci + lr, pr * ci + pi * cr + li

            def final(t, c):
                xr, xi, acr, aci = c
                sl = tile(t)
                o_ref[ir, sl, :] = xr
                o_ref[ii, sl, :] = xi
                if with_da:
                    br = xp_ref[ir, sl, :]
                    bi = xp_ref[ii, sl, :]
                    acr = acr + br * xr + bi * xi
                    aci = aci + br * xi - bi * xr
                return (ar * xr - ai * xi + s_ref[ir, sl, :], ar * xi + ai * xr + s_ref[ii, sl, :], acr, aci)

            _, _, acr, aci = lax.fori_loop(0, nstep, final, (xin_r, xin_i, zero, zero))
            if with_da:
                da_ref[ir] = jnp.sum(acr, axis=0, keepdims=True)
                da_ref[ii] = jnp.sum(aci, axis=0, keepdims=True)

    blk = pl.BlockSpec((4, nc, ct), _state_blk)
    ablk = pl.BlockSpec((4, 1, ct), _state_blk)
    sds = jax.ShapeDtypeStruct((4, nc, gp), F32)
    if with_da:
        out, _ = _call(body, name, (gp // ct,), [blk, ablk, blk], [blk, ablk],
                       [sds, jax.ShapeDtypeStruct((4, 1, gp), F32)], ("parallel",), (s4, a4, xp4))
        return out
    (out,), _ = _call(body, name, (gp // ct,), [blk, ablk], [blk], [sds], ("parallel",), (s4, a4))
    return out


def _state_cat(ref):
    return _lanes4(ref).astype(BF16)


def ssm_out(ug, t, xp4, o, name):
    nc = ug.shape[1]

    def body(u_ref, t_ref, xp_ref, o_ref, y_ref):
        xs = _pair_split(_lanes4(xp_ref))
        u = _lanes4(u_ref).astype(BF16)
        for r in range(2):
            y = _dot(u[:, r * GW:(r + 1) * GW], t_ref[r]) + _dot(xs[r].astype(BF16), o_ref[r])
            y_ref[2 * r] = y[:, :128]
            y_ref[2 * r + 1] = y[:, 128:]

    (out,), _ = _call(body, name, (SSM_G // 2,),
                      [pl.BlockSpec((4, nc, 128), _pair3), pl.BlockSpec((2, GW, GW), _pair3),
                       pl.BlockSpec((4, nc, 128), _state_blk), pl.BlockSpec((2, 4 * SSM_P, GW), _pair3)],
                      [pl.BlockSpec((4, nc, 128), _pair3)],
                      [jax.ShapeDtypeStruct((2 * SSM_G, nc, 128), F32)], ("parallel",), (ug, t, xp4, o))
    return out


def ssm_out_bwd(dyg, xp4, o, name):
    nc = dyg.shape[1]

    def body(dy_ref, xp_ref, o_ref, do_ref, dxp_ref):
        xs = _pair_split(_lanes4(xp_ref))
        dy = _lanes4(dy_ref).astype(BF16)
        dxs = []
        for r in range(2):
            dyr = dy[:, r * GW:(r + 1) * GW]
            do_ref[r] = _dot_tn(xs[r].astype(BF16), dyr)
            dxs.append(_dot_nt(dyr, o_ref[r]))
        for q, blk in enumerate(_pair_merge(*dxs)):
            dxp_ref[q] = blk

    out, _ = _call(body, name, (SSM_G // 2,),
                   [pl.BlockSpec((4, nc, 128), _pair3), pl.BlockSpec((4, nc, 128), _state_blk),
                    pl.BlockSpec((2, 4 * SSM_P, GW), _pair3)],
                   [pl.BlockSpec((2, 4 * SSM_P, GW), _pair3), pl.BlockSpec((4, nc, 128), _state_blk)],
                   [jax.ShapeDtypeStruct((SSM_G, 4 * SSM_P, GW), F32), jax.ShapeDtypeStruct((4, nc, SSM_G * SSM_P), F32)],
                   ("parallel",), (dyg, xp4, o))
    return out


def ssm_in_bwd(ug, dyg, ds4, t, e, name):
    nc = ug.shape[1]

    def body(u_ref, dy_ref, ds_ref, t_ref, e_ref, dt_ref, de_ref, du_ref):
        dss = _pair_split(_lanes4(ds_ref))
        u = _lanes4(u_ref).astype(BF16)
        dy = _lanes4(dy_ref).astype(BF16)
        for r in range(2):
            cols = slice(r * GW, (r + 1) * GW)
            ds = dss[r].astype(BF16)
            dt_ref[r] = _dot_tn(u[:, cols], dy[:, cols])
            de_ref[r] = _dot_tn(u[:, cols], ds)
            du = _dot_nt(dy[:, cols], t_ref[r]) + _dot_nt(ds, e_ref[r])
            du_ref[2 * r] = du[:, :128]
            du_ref[2 * r + 1] = du[:, 128:]

    out, _ = _call(body, name, (SSM_G // 2,),
                   [pl.BlockSpec((4, nc, 128), _pair3), pl.BlockSpec((4, nc, 128), _pair3),
                    pl.BlockSpec((4, nc, 128), _state_blk), pl.BlockSpec((2, GW, GW), _pair3),
                    pl.BlockSpec((2, GW, 4 * SSM_P), _pair3)],
                   [pl.BlockSpec((2, GW, GW), _pair3), pl.BlockSpec((2, GW, 4 * SSM_P), _pair3),
                    pl.BlockSpec((4, nc, 128), _pair3)],
                   [jax.ShapeDtypeStruct((SSM_G, GW, GW), F32), jax.ShapeDtypeStruct((SSM_G, GW, 4 * SSM_P), F32),
                    jax.ShapeDtypeStruct((2 * SSM_G, nc, 128), F32)], ("parallel",), (ug, dyg, ds4, t, e))
    return out


def ssm_post_fwd(yc, z, dskip, wglu, name):
    s = yc.shape[1]

    def body(y_ref, u_ref, d_ref, w_ref, o_ref, yp_ref, g_ref):
        yp = _lanes4(y_ref) + d_ref[...] * u_ref[...]
        yp_ref[...] = yp
        gv = _dot(_gelu(yp).astype(BF16), w_ref[...])
        g_ref[...] = gv
        o_ref[...] = gv[:, :SSM_W] * _sigmoid(gv[:, SSM_W:])

    out, _ = _call(body, name, (s // TM,),
                   [pl.BlockSpec((4, TM, 128), lambda i: (0, i, 0)), pl.BlockSpec((TM, SSM_W), lambda i: (i, U0 // SSM_W)),
                    pl.BlockSpec((1, SSM_W), _fixed), pl.BlockSpec((SSM_W, 2 * SSM_W), _fixed)],
                   [pl.BlockSpec((TM, SSM_W), _row), pl.BlockSpec((TM, SSM_W), _row), pl.BlockSpec((TM, 2 * SSM_W), _row)],
                   [jax.ShapeDtypeStruct((s, SSM_W), F32), jax.ShapeDtypeStruct((s, SSM_W), F32),
                    jax.ShapeDtypeStruct((s, 2 * SSM_W), F32)], ("parallel",), (yc, z, dskip, wglu))
    return out


def ssm_post_bwd(dssm, gpre, ypre, z, dskip, wglu, name):
    s = dssm.shape[0]

    def body(do_ref, g_ref, yp_ref, u_ref, d_ref, w_ref, dy_ref, du_ref, dw_ref, dd_ref):
        @pl.when(pl.program_id(0) == 0)
        def _():
            dw_ref[...] = jnp.zeros_like(dw_ref)
            dd_ref[...] = jnp.zeros_like(dd_ref)

        gv = g_ref[...]
        val = gv[:, :SSM_W]
        sg = _sigmoid(gv[:, SSM_W:])
        do = do_ref[...]
        dg = jnp.concatenate([do * sg, do * val * sg * (1.0 - sg)], axis=1).astype(BF16)
        yp = yp_ref[...]
        dgl = _dot_nt(dg, w_ref[...])
        dw_ref[...] += _dot_tn(_gelu(yp).astype(BF16), dg)
        dyp = dgl * _gelu_grad(yp)
        dy_ref[...] = dyp
        du_ref[...] = dyp * d_ref[...]
        dd_ref[...] += jnp.sum(dyp * u_ref[...], axis=0, keepdims=True)

    out, _ = _call(body, name, (s // TM,),
                   [pl.BlockSpec((TM, SSM_W), _row), pl.BlockSpec((TM, 2 * SSM_W), _row), pl.BlockSpec((TM, SSM_W), _row),
                    pl.BlockSpec((TM, SSM_W), lambda i: (i, U0 // SSM_W)), pl.BlockSpec((1, SSM_W), _fixed),
                    pl.BlockSpec((SSM_W, 2 * SSM_W), _fixed)],
                   [pl.BlockSpec((TM, SSM_W), _row), pl.BlockSpec((TM, SSM_W), _row),
                    pl.BlockSpec((SSM_W, 2 * SSM_W), _fixed), pl.BlockSpec((1, SSM_W), _fixed)],
                   [jax.ShapeDtypeStruct((s, SSM_W), F32), jax.ShapeDtypeStruct((s, SSM_W), F32),
                    jax.ShapeDtypeStruct((SSM_W, 2 * SSM_W), F32), jax.ShapeDtypeStruct((1, SSM_W), F32)],
                   ("arbitrary",), (dssm, gpre, ypre, z, dskip, wglu))
    return out


def _toeplitz_select():
    row = lax.broadcasted_iota(jnp.int32, (GW, CH * GW), 0)
    col = lax.broadcasted_iota(jnp.int32, (GW, CH * GW), 1)
    j, h2 = row // SSM_H, row % SSM_H
    s, t, h = col // GW, (col % GW) // SSM_H, col % SSM_H
    same = h2 == h
    return jnp.concatenate([same & (t - s == j), same & (s - t == j)], axis=0).astype(F32)


def ssm_mats(lam_re, lam_im, log_dt, b_re, b_im, c_re, c_im):
    g, p, hh = SSM_G, SSM_P, SSM_H
    hp = lax.Precision.HIGHEST
    jj = jnp.arange(CH + 1, dtype=F32)
    dt = jnp.exp(log_dt)[..., None]
    mag = jnp.exp((lam_re * dt)[..., None] * jj)
    ang = (lam_im * dt)[..., None] * jj
    pr, pi = mag * jnp.cos(ang), mag * jnp.sin(ang)
    abr, abi = pr[..., 1], pi[..., 1]
    den = lam_re * lam_re + lam_im * lam_im
    zr = ((abr - 1.0) * lam_re + abi * lam_im) / den
    zi = (abi * lam_re - (abr - 1.0) * lam_im) / den
    bbr = zr[..., None] * b_re[None] - zi[..., None] * b_im[None]
    bbi = zr[..., None] * b_im[None] + zi[..., None] * b_re[None]
    crt, cit = c_re.transpose(0, 1, 3, 2), c_im.transpose(0, 1, 3, 2)
    car = pr[..., None] * crt[..., None, :] - pi[..., None] * cit[..., None, :]
    cai = pr[..., None] * cit[..., None, :] + pi[..., None] * crt[..., None, :]
    lhs = jnp.concatenate([bbr, -bbi], axis=2).transpose(0, 1, 3, 2)
    rhs = jnp.concatenate([car[..., :CH, :], cai[..., :CH, :]], axis=2).reshape(2, g, 2 * p, GW)
    kt = jnp.einsum("dgkp,dgpn->dgkn", lhs, rhs, precision=hp)
    kcat = jnp.concatenate([kt[0], kt[1]], axis=-1).reshape(g * hh, 2 * GW)
    tmat = jnp.dot(kcat, _toeplitz_select(), precision=hp)
    tmat = tmat.reshape(g, hh, CH, GW).transpose(0, 2, 1, 3).reshape(g, GW, GW)

    def e_part(d, pw_r, pw_i):
        pw_r, pw_i = pw_r.transpose(0, 2, 1)[:, :, None, :], pw_i.transpose(0, 2, 1)[:, :, None, :]
        br, bi = bbr[d].transpose(0, 2, 1)[:, None], bbi[d].transpose(0, 2, 1)[:, None]
        return [pw_r * br - pw_i * bi, pw_r * bi + pw_i * br]

    eparts = (e_part(0, pr[0, ..., :CH][..., ::-1], pi[0, ..., :CH][..., ::-1])
              + e_part(1, pr[1, ..., :CH], pi[1, ..., :CH]))
    emat = jnp.stack(eparts, axis=3).reshape(g, GW, 4 * p)

    oparts = [car[0, ..., 1:, :], -cai[0, ..., 1:, :], car[1, ..., 1:, :][..., ::-1, :], -cai[1, ..., 1:, :][..., ::-1, :]]
    omat = jnp.stack([v.reshape(g, p, GW) for v in oparts], axis=1).reshape(g, 4 * p, GW)
    amat = jnp.stack([pr[0, ..., CH], pi[0, ..., CH], pr[1, ..., CH], pi[1, ..., CH]], axis=0).reshape(4, 1, g * p)
    return tmat, emat, omat, amat


def outproj_fwd(x, att, ssm, wo, name):
    s, d = x.shape

    def body(x_ref, a_ref, s_ref, w_ref, o_ref):
        o_ref[...] = (x_ref[...] + _dot(a_ref[...].astype(BF16), w_ref[0:ATT_W, :])
                      + _dot(s_ref[...].astype(BF16), w_ref[ATT_W:, :]))

    (out,), _ = _call(body, name, (s // TM,),
                      [pl.BlockSpec((TM, d), _row), pl.BlockSpec((TM, ATT_W), _row), pl.BlockSpec((TM, SSM_W), _row),
                       pl.BlockSpec((ATT_W + SSM_W, d), _fixed)],
                      [pl.BlockSpec((TM, d), _row)], [jax.ShapeDtypeStruct((s, d), F32)], ("parallel",),
                      (x, att, ssm, wo))
    return out


def outproj_bwd(dx1, att, ssm, wo, name):
    s, d = dx1.shape

    def body(dx_ref, a_ref, s_ref, w_ref, da_ref, ds_ref, dw_ref):
        @pl.when(pl.program_id(0) == 0)
        def _():
            dw_ref[...] = jnp.zeros_like(dw_ref)

        dxb = dx_ref[...].astype(BF16)
        da_ref[...] = _dot_nt(dxb, w_ref[0:ATT_W, :])
        ds_ref[...] = _dot_nt(dxb, w_ref[ATT_W:, :])
        dw_ref[0:ATT_W, :] += _dot_tn(a_ref[...].astype(BF16), dxb)
        dw_ref[ATT_W:, :] += _dot_tn(s_ref[...].astype(BF16), dxb)

    out, _ = _call(body, name, (s // TM,),
                   [pl.BlockSpec((TM, d), _row), pl.BlockSpec((TM, ATT_W), _row), pl.BlockSpec((TM, SSM_W), _row),
                    pl.BlockSpec((ATT_W + SSM_W, d), _fixed)],
                   [pl.BlockSpec((TM, ATT_W), _row), pl.BlockSpec((TM, SSM_W), _row),
                    pl.BlockSpec((ATT_W + SSM_W, d), _fixed)],
                   [jax.ShapeDtypeStruct((s, ATT_W), F32), jax.ShapeDtypeStruct((s, SSM_W), F32),
                    jax.ShapeDtypeStruct((ATT_W + SSM_W, d), F32)], ("arbitrary",), (dx1, att, ssm, wo))
    return out


def ffn_fwd(x1, gain, w1, w2, name, xch=None):
    s, d = x1.shape
    nch, _, fc = w1.shape

    def body(x_ref, g_ref, w1_ref, w2_ref, o_ref, h_ref, a_ref):
        @pl.when(pl.program_id(1) == 0)
        def _():
            xv = x_ref[...]
            h_ref[...] = (xv * _rms(xv) * g_ref[...]).astype(BF16)
            o_ref[...] = xv

        a = _dot(h_ref[...], w1_ref[...])
        a_ref[...] = a.astype(BF16)
        o_ref[...] += _dot(jnp.square(jnp.maximum(a, 0.0)).astype(BF16), w2_ref[...])

    return _call(
        body, name, (s // TM, nch),
        [pl.BlockSpec((TM, d), lambda i, k: (i, 0)), pl.BlockSpec((1, d), lambda i, k: (0, 0)),
         pl.BlockSpec((None, d, fc), lambda i, k: (k, 0, 0)), pl.BlockSpec((None, fc, d), lambda i, k: (k, 0, 0))],
        [pl.BlockSpec((TM, d), lambda i, k: (i, 0)), pl.BlockSpec((TM, d), lambda i, k: (i, 0)),
         pl.BlockSpec((TM, fc), lambda i, k: (i, k))],
        [jax.ShapeDtypeStruct((s, d), F32), jax.ShapeDtypeStruct((s, d), BF16), jax.ShapeDtypeStruct((s, nch * fc), BF16)],
        ("arbitrary", "arbitrary"), (x1, gain, w1, w2), xch=xch)


def ffn_bwd_tok(dx2, x1, gain, a, w1, w2, name, xch=None):
    s, d = x1.shape
    nch, _, fc = w1.shape

    def body(dx_ref, x_ref, g_ref, a_ref, w1_ref, w2_ref, da_ref, dx1_ref, dg_ref, dxb_ref, dh_ref):
        i = pl.program_id(0)
        k = pl.program_id(1)

        @pl.when(jnp.logical_and(i == 0, k == 0))
        def _():
            dg_ref[...] = jnp.zeros_like(dg_ref)

        @pl.when(k == 0)
        def _():
            dxb_ref[...] = dx_ref[...].astype(BF16)
            dh_ref[...] = jnp.zeros_like(dh_ref)

        dr = _dot_nt(dxb_ref[...], w2_ref[...])
        da = (dr * (2.0 * jnp.maximum(a_ref[...].astype(F32), 0.0))).astype(BF16)
        da_ref[...] = da
        dh_ref[...] += _dot_nt(da, w1_ref[...])

        @pl.when(k == nch - 1)
        def _():
            xv = x_ref[...]
            r = _rms(xv)
            xhat = xv * r
            dh = dh_ref[...]
            dg_ref[...] += jnp.sum(dh * xhat, axis=0, keepdims=True)
            dx1_ref[...] = dx_ref[...] + _rms_bwd(xhat, r, dh * g_ref[...])

    return _call(
        body, name, (s // TM, nch),
        [pl.BlockSpec((TM, d), lambda i, k: (i, 0)), pl.BlockSpec((TM, d), lambda i, k: (i, 0)),
         pl.BlockSpec((1, d), lambda i, k: (0, 0)), pl.BlockSpec((TM, fc), lambda i, k: (i, k)),
         pl.BlockSpec((None, d, fc), lambda i, k: (k, 0, 0)), pl.BlockSpec((None, fc, d), lambda i, k: (k, 0, 0))],
        [pl.BlockSpec((TM, fc), lambda i, k: (i, k)), pl.BlockSpec((TM, d), lambda i, k: (i, 0)),
         pl.BlockSpec((1, d), lambda i, k: (0, 0)), pl.BlockSpec((TM, d), lambda i, k: (i, 0))],
        [jax.ShapeDtypeStruct((s, nch * fc), BF16), jax.ShapeDtypeStruct((s, d), F32),
         jax.ShapeDtypeStruct((1, d), F32), jax.ShapeDtypeStruct((s, d), BF16)],
        ("arbitrary", "arbitrary"), (dx2, x1, gain, a, w1, w2), scratch=[pltpu.VMEM((TM, d), F32)], xch=xch)


def ffn_bwd_w(h2, da, a, dxb, nch, name, xch=None):
    s, d = h2.shape
    fc = a.shape[1] // nch

    def body(h_ref, da_ref, a_ref, dx_ref, dw1_ref, dw2_ref):
        @pl.when(pl.program_id(1) == 0)
        def _():
            dw1_ref[...] = jnp.zeros_like(dw1_ref)
            dw2_ref[...] = jnp.zeros_like(dw2_ref)

        dw1_ref[...] += _dot_tn(h_ref[...], da_ref[...])
        r = jnp.square(jnp.maximum(a_ref[...].astype(F32), 0.0)).astype(BF16)
        dw2_ref[...] += _dot_tn(r, dx_ref[...])

    return _call(
        body, name, (nch, s // TM),
        [pl.BlockSpec((TM, d), lambda k, t: (t, 0)), pl.BlockSpec((TM, fc), lambda k, t: (t, k)),
         pl.BlockSpec((TM, fc), lambda k, t: (t, k)), pl.BlockSpec((TM, d), lambda k, t: (t, 0))],
        [pl.BlockSpec((None, d, fc), lambda k, t: (k, 0, 0)), pl.BlockSpec((None, fc, d), lambda k, t: (k, 0, 0))],
        [jax.ShapeDtypeStruct((nch, d, fc), F32), jax.ShapeDtypeStruct((nch, fc, d), F32)],
        ("arbitrary", "arbitrary"), (h2, da, a, dxb), xch=xch)


def loss_grad(xf, tgt, name):
    s, d = xf.shape
    nt = s // TM

    def body(x_ref, t_ref, dx_ref, l_ref, acc_ref):
        i = pl.program_id(0)

        @pl.when(i == 0)
        def _():
            acc_ref[...] = jnp.zeros_like(acc_ref)

        e = x_ref[...] - t_ref[...]
        dx_ref[...] = e * (1.0 / d)
        acc_ref[...] += jnp.sum(e * e, axis=0, keepdims=True)

        @pl.when(i == nt - 1)
        def _():
            l_ref[...] = jnp.sum(acc_ref[...], axis=1, keepdims=True) * (0.5 / d)

    out, _ = _call(body, name, (nt,), [pl.BlockSpec((TM, d), _row), pl.BlockSpec((TM, d), _row)],
                   [pl.BlockSpec((TM, d), _row), pl.BlockSpec((1, 1), _fixed)],
                   [jax.ShapeDtypeStruct((s, d), F32), jax.ShapeDtypeStruct((1, 1), F32)], ("arbitrary",),
                   (xf, tgt), scratch=[pltpu.VMEM((1, d), F32)])
    return out


def adamw_sum(parts, w, m, v, br, name):
    nl = len(parts)
    npart, r, c = parts[0].shape
    nb = r // br
    c1 = 1.0 - ADAM_B1 ** ADAM_STEP
    c2 = 1.0 - ADAM_B2 ** ADAM_STEP

    def body(*refs):
        p_refs = refs[:nl]
        w_ref, m_ref, v_ref, g_ref, d_ref, nm_ref, nv_ref = refs[nl:]
        for l in range(nl):
            @pl.when(pl.program_id(0) == l)
            def _(l=l):
                g = p_refs[l][0]
                for j in range(1, npart):
                    g = g + p_refs[l][j]
                m2 = ADAM_B1 * m_ref[...] + (1.0 - ADAM_B1) * g
                v2 = ADAM_B2 * v_ref[...] + (1.0 - ADAM_B2) * jnp.square(g)
                g_ref[...] = g
                nm_ref[...] = m2
                nv_ref[...] = v2
                d_ref[...] = -ADAM_LR * ((m2 / c1) / (jnp.sqrt(v2 / c2) + ADAM_EPS) + ADAM_WD * w_ref[...])

    blk = pl.BlockSpec((br, c), lambda l, i: (l * nb + i, 0))
    pspecs = [pl.BlockSpec((npart, br, c), lambda l, i, own=own: (0, jnp.where(l == own, i, 0), 0)) for own in range(nl)]
    sds = jax.ShapeDtypeStruct((nl * r, c), F32)
    out, _ = _call(body, name, (nl, nb), pspecs + [blk, blk, blk], [blk, blk, blk, blk], [sds, sds, sds, sds],
                   ("arbitrary", "arbitrary"), (*parts, w, m, v))
    return out


def sum_parts(parts, name):
    n, r, c = parts.shape

    def body(p_ref, o_ref):
        g = p_ref[0]
        for j in range(1, n):
            g = g + p_ref[j]
        o_ref[...] = g

    (out,), _ = _call(body, name, (1,), [pl.BlockSpec((n, r, c), lambda i: (0, 0, 0))], [pl.BlockSpec((r, c), _fixed)],
                      [jax.ShapeDtypeStruct((r, c), F32)], ("arbitrary",), (parts,))
    return out


def layer_fwd(x, w_in, p, mats, l, late, nxt):
    tmat, emat, omat, amat = mats
    z = norm_matmul(x, p["norm1"], w_in, f"in_fwd{l}")
    (att, lse), got = attn_fwd(z, p["q_gain"], p["k_gain"], p["sink"], f"attn_fwd{l}", (late, False))
    p = dict(p, **_whole_weights(dict(zip(BIG[-len(got):], got))))
    ug = to_groups(z, U0 // 128, f"ssm_to_groups{l}")
    s4 = ssm_in(ug, emat, f"ssm_in{l}")
    xp4 = chunk_scan(s4, amat, False, f"ssm_scan{l}")
    yc = from_groups(ssm_out(ug, tmat, xp4, omat, f"ssm_out{l}"), f"ssm_from_groups{l}")
    ssm, ypre, gpre = ssm_post_fwd(yc, z, p["d_skip"], p["w_glu"], f"ssm_post{l}")
    x1 = outproj_fwd(x, att, ssm, p["w_out"], f"out_fwd{l}")
    (x2, h2, a), gathered = ffn_fwd(x1, p["norm2"], p["w_ff1"], p["w_ff2"], f"ffn_fwd{l}",
                                    None if nxt is None else (nxt, False))
    saved = dict(x=x, z=z, att=att, lse=lse, ug=ug, xp4=xp4, ssm=ssm, ypre=ypre, gpre=gpre, x1=x1, h2=h2, a=a)
    return x2, saved, dict(p, w_in=w_in), gathered


def layer_bwd(dx2, p, mats, sv, l, above):
    tmat, emat, omat, amat = mats
    nch = p["w_ff1"].shape[0]
    last = l == 0
    (da, dx1, dnorm2, dxb), got_tok = ffn_bwd_tok(dx2, sv["x1"], p["norm2"], sv["a"], p["w_ff1"], p["w_ff2"],
                                                  f"ffn_bwd{l}", None if above is None else (above["tok"], True))
    (dw1, dw2), got_w = ffn_bwd_w(sv["h2"], da, sv["a"], dxb, nch, f"ffn_bwdw{l}",
                                  None if above is None else (above["w"], True))
    datt, dssm, dwo = outproj_bwd(dx1, sv["att"], sv["ssm"], p["w_out"], f"out_bwd{l}")
    dyc, du_skip, dwglu, ddskip = ssm_post_bwd(dssm, sv["gpre"], sv["ypre"], sv["z"], p["d_skip"], p["w_glu"],
                                               f"ssm_post_bwd{l}")
    dyg = to_groups(dyc, 0, f"ssm_to_groups_bwd{l}")
    domat, dxp4 = ssm_out_bwd(dyg, sv["xp4"], omat, f"ssm_out_bwd{l}")
    aconj = amat * jnp.array([1.0, -1.0, 1.0, -1.0], F32).reshape(4, 1, 1)
    ds4, damat = chunk_scan(dxp4, aconj, True, f"ssm_scan_bwd{l}", xp4=sv["xp4"])
    dtmat, demat, dug = ssm_in_bwd(sv["ug"], dyg, ds4, tmat, emat, f"ssm_in_bwd{l}")
    du_core = from_groups(dug, f"ssm_from_groups_bwd{l}")
    (dq, dk, dv, dqg, dkg, dsink), got_ff = attn_bwd(sv["z"], sv["att"], datt, sv["lse"], p["q_gain"], p["k_gain"],
                                                     p["sink"], f"attn_bwd{l}", ([dw1, dw2] if last else [dw1], True))
    blocks = _grad_blocks(dict(w_glu=dwglu, w_out=dwo))
    (dx, dwin, dnorm1), got_mix = in_bwd(sv["x"], p["norm1"], p["w_in"], du_skip, du_core, dq, dk, dv, dx1, f"in_bwd{l}",
                                         ([blocks["w_glu"], blocks["w_out"]], True) if last else None)
    grads = dict(norm1=dnorm1, q_gain=dqg, k_gain=dkg, sink=dsink, d_skip=ddskip, norm2=dnorm2)
    win_blocks = _grad_blocks(dict(w_in=dwin))["w_in"]
    mine = dict(w_ff1=got_ff[0])
    if last:
        mine.update(w_ff2=got_ff[1], w_glu=got_mix[0], w_out=got_mix[1])
        below = win_blocks
    else:
        below = dict(tok=[win_blocks, blocks["w_glu"], blocks["w_out"]], w=[dw2])
    theirs = {} if above is None else dict(w_in=got_tok[0], w_glu=got_tok[1], w_out=got_tok[2], w_ff2=got_w[0])
    return dx, grads, (dtmat, demat, domat, damat), mine, theirs, below


def _whole_weights(gathered):
    out = {}
    for n, g in gathered.items():
        if n == "w_in":
            w_in = g.transpose(1, 0, 2).reshape(D_MODEL, IN_W)
            out[n] = jnp.concatenate([w_in[:, V_END:], w_in[:, :V_END]], axis=1)
        elif n == "w_glu":
            out[n] = g.transpose(1, 0, 2).reshape(SSM_W, 2 * SSM_W)
        elif n == "w_out":
            out[n] = g.reshape(ATT_W + SSM_W, D_MODEL)
        else:
            out[n] = g
    return out


def _grad_blocks(grads):
    out = {}
    for n, g in grads.items():
        if n == "w_in":
            g = jnp.concatenate([g[:, IN_W - V_END:], g[:, :IN_W - V_END]], axis=1)
            out[n] = g.reshape(D_MODEL, N_DEV, IN_W // N_DEV).transpose(1, 0, 2)
        elif n == "w_glu":
            out[n] = g.reshape(SSM_W, N_DEV, 2 * SSM_W // N_DEV).transpose(1, 0, 2)
        elif n == "w_out":
            out[n] = g.reshape(N_DEV, (ATT_W + SSM_W) // N_DEV, D_MODEL)
        else:
            out[n] = g
    return out


def _small_layout(like):
    layout, row = {}, 0
    for n in SMALL:
        size = int(np.prod(like[n].shape))
        nrow = -(-size // (8 * 128)) * 8
        layout[n] = (row, nrow, size)
        row += nrow
    return layout, -(-row // (8 * N_DEV)) * 8 * N_DEV


def _pack_small(vals, layout, rows):
    pieces, used = [], 0
    for n in SMALL:
        _, nrow, size = layout[n]
        flat = vals[n].reshape(-1).astype(F32)
        pieces.append(jnp.pad(flat, (0, nrow * 128 - size)).reshape(nrow, 128))
        used += nrow
    if rows > used:
        pieces.append(jnp.zeros((rows - used, 128), F32))
    return jnp.concatenate(pieces, axis=0)


def _unpack_small(packed, like, layout):
    out = {}
    for n in SMALL:
        row, nrow, size = layout[n]
        out[n] = packed[row:row + nrow].reshape(-1)[:size].reshape(like[n].shape)
    return out


def kernel(x, norm1, w_in, q_gain, k_gain, sink, lam_re, lam_im, log_dt, b_re, b_im, c_re, c_im, d_skip, w_glu, w_out, norm2, w_ff1, w_ff2, loss_target, m_norm1, m_w_in, m_q_gain, m_k_gain, m_sink, m_lam_re, m_lam_im, m_log_dt, m_b_re, m_b_im, m_c_re, m_c_im, m_d_skip, m_w_glu, m_w_out, m_norm2, m_w_ff1, m_w_ff2, v_norm1, v_w_in, v_q_gain, v_k_gain, v_sink, v_lam_re, v_lam_im, v_log_dt, v_b_re, v_b_im, v_c_re, v_c_im, v_d_skip, v_w_glu, v_w_out, v_norm2, v_w_ff1, v_w_ff2):
    w = dict(norm1=norm1, w_in=w_in, q_gain=q_gain, k_gain=k_gain, sink=sink, lam_re=lam_re, lam_im=lam_im,
             log_dt=log_dt, b_re=b_re, b_im=b_im, c_re=c_re, c_im=c_im, d_skip=d_skip, w_glu=w_glu, w_out=w_out,
             norm2=norm2, w_ff1=w_ff1, w_ff2=w_ff2)
    m = dict(norm1=m_norm1, w_in=m_w_in, q_gain=m_q_gain, k_gain=m_k_gain, sink=m_sink, lam_re=m_lam_re,
             lam_im=m_lam_im, log_dt=m_log_dt, b_re=m_b_re, b_im=m_b_im, c_re=m_c_re, c_im=m_c_im, d_skip=m_d_skip,
             w_glu=m_w_glu, w_out=m_w_out, norm2=m_norm2, w_ff1=m_w_ff1, w_ff2=m_w_ff2)
    v = dict(norm1=v_norm1, w_in=v_w_in, q_gain=v_q_gain, k_gain=v_k_gain, sink=v_sink, lam_re=v_lam_re,
             lam_im=v_lam_im, log_dt=v_log_dt, b_re=v_b_re, b_im=v_b_im, c_re=v_c_re, c_im=v_c_im, d_skip=v_d_skip,
             w_glu=v_w_glu, w_out=v_w_out, norm2=v_norm2, w_ff1=v_w_ff1, w_ff2=v_w_ff2)
    nl = w_in.shape[0]
    shards = [[w[n][l].astype(BF16) for n in BIG] for l in range(nl)]
    (tmat, emat, omat, amat), mats_vjp = jax.vjp(jax.vmap(ssm_mats), *[w[n] for n in S5])
    tb, eb, ob = tmat.astype(BF16), emat.astype(BF16), omat.astype(BF16)

    (g_in,) = exchange(shards[0][:1], False, "gather_w_in0")
    have = _whole_weights(dict(w_in=g_in))
    late = shards[0][1:]
    xs = x[0]
    saved, lp, lm = [], [], []
    for l in range(nl):
        p = {n: have[n] for n in have if n != "w_in"}
        for n in ("norm1", "q_gain", "k_gain", "d_skip", "norm2"):
            p[n] = w[n][l].reshape(1, -1)
        p["sink"] = sink[l]
        mats = (tb[l], eb[l], ob[l], amat[l])
        xs, sv, p, got = layer_fwd(xs, have["w_in"], p, mats, l, late, shards[l + 1][:3] if l + 1 < nl else None)
        if l + 1 < nl:
            have = _whole_weights(dict(zip(BIG[:3], got)))
            late = shards[l + 1][3:]
        saved.append(sv)
        lp.append(p)
        lm.append(mats)
    dx, loss_part = loss_grad(xs, loss_target[0], "loss")
    loss = lax.psum(loss_part[0, 0], ("x", "y", "c"))

    grads, dmats, parts = [None] * nl, [None] * nl, [dict() for _ in range(nl)]
    above = None
    for l in reversed(range(nl)):
        dx, grads[l], dmats[l], mine, theirs, above = layer_bwd(dx, lp[l], lm[l], saved[l], l, above)
        parts[l].update(mine)
        if l + 1 < nl:
            parts[l + 1].update(theirs)
    (parts[0]["w_in"],) = exchange([above], True, "exchange_g_in0")

    gs = {n: jnp.stack([grads[l][n].reshape(w[n].shape[1:]) for l in range(nl)])
          for n in ("norm1", "q_gain", "k_gain", "sink", "d_skip", "norm2")}
    ds5 = mats_vjp(tuple(jnp.stack([dmats[l][i] for l in range(nl)]) for i in range(4)))
    gs.update(zip(S5, ds5))
    layout, rows = _small_layout(w)
    (mine,) = exchange([_pack_small(gs, layout, rows).reshape(N_DEV, rows // N_DEV, 128)], True, "scatter_small_grads")
    (small_sum,) = exchange([sum_parts(mine, "sum_small_grads")], False, "gather_small_grads")

    out_g, out_d, out_m, out_v = {}, {}, {}, {}
    for n in BIG:
        c = w[n].shape[-1]
        r = int(np.prod(w[n].shape[:-1]))
        res = adamw_sum([parts[l][n] for l in range(nl)], w[n].reshape(r, c), m[n].reshape(r, c), v[n].reshape(r, c),
                        ADAM_ROWS[c], f"adamw_{n}")
        out_g[n], out_d[n], out_m[n], out_v[n] = (t.reshape(w[n].shape) for t in res)
    res = adamw_sum([small_sum.reshape(1, rows, 128)], _pack_small(w, layout, rows), _pack_small(m, layout, rows),
                    _pack_small(v, layout, rows), rows // N_DEV, "adamw_small")
    for dst, packed in zip((out_g, out_d, out_m, out_v), res):
        dst.update(_unpack_small(packed, w, layout))

    return (loss, dx[None], *[out_g[n] for n in WEIGHTS], *[out_d[n] for n in WEIGHTS],
            *[out_m[n] for n in WEIGHTS], *[out_v[n] for n in WEIGHTS])
```

```python
import numpy as np
import jax
import jax.numpy as jnp
from jax import lax
from jax.experimental import pallas as pl
from jax.experimental.pallas import tpu as pltpu

F32, BF16 = jnp.float32, jnp.bfloat16
EPS = 1e-6
D_MODEL = 1024
ATT_HEADS, KV_HEADS, GQA_GROUP, HEAD_DIM = 8, 2, 4, 64
ATT_W, KV_W, SSM_W, IN_W = 512, 128, 512, 1280
V_END = 768
U0, Q0, K0, V0 = 0, 512, 1024, 1152
BLK = 128
SSM_G, SSM_H, SSM_P = 32, 16, 64
CH = 16
GW = CH * SSM_H
SEGS = 8
N_DEV = 8
NEG = float(np.finfo(np.float32).min)
SLOPES = tuple(2.0 ** (-8.0 * (h + 1) / ATT_HEADS) for h in range(ATT_HEADS))
VMEM_LIMIT = 56 * 1024 * 1024
TM = 512
TM_FFN = 1024

ADAM_LR, ADAM_B1, ADAM_B2, ADAM_EPS, ADAM_WD, ADAM_STEP = 0.001, 0.9, 0.999, 1e-08, 0.01, 10

SMALL = ("norm1", "q_gain", "k_gain", "sink", "lam_re", "lam_im", "log_dt", "b_re", "b_im",
         "c_re", "c_im", "d_skip", "norm2")
S5 = ("lam_re", "lam_im", "log_dt", "b_re", "b_im", "c_re", "c_im")
BIG = ("w_in", "w_glu", "w_out", "w_ff1", "w_ff2")
WEIGHTS = ("norm1", "w_in", "q_gain", "k_gain", "sink", "lam_re", "lam_im", "log_dt", "b_re", "b_im",
           "c_re", "c_im", "d_skip", "w_glu", "w_out", "norm2", "w_ff1", "w_ff2")
ADAM_ROWS = {160: 256, 128: 512, 512: 128, 1024: 64}


def _dot(a, b):
    return jnp.dot(a, b, preferred_element_type=F32)


def _dot_nt(a, b):
    return lax.dot_general(a, b, (((1,), (1,)), ((), ())), preferred_element_type=F32)


def _dot_tn(a, b):
    return lax.dot_general(a, b, (((0,), (0,)), ((), ())), preferred_element_type=F32)


def _rms(x):
    return lax.rsqrt(jnp.mean(x * x, axis=-1, keepdims=True) + EPS)


def _rms_bwd(xhat, r, dxhat):
    return r * (dxhat - xhat * jnp.mean(dxhat * xhat, axis=-1, keepdims=True))


def _sigmoid(x):
    return 1.0 / (1.0 + jnp.exp(-x))


_GC = 0.7978845608028654
_GA = 0.044715


def _gelu(x):
    return 0.5 * x * (1.0 + jnp.tanh(_GC * (x + _GA * x * x * x)))


def _gelu_grad(x):
    t = jnp.tanh(_GC * (x + _GA * x * x * x))
    return 0.5 * (1.0 + t) + 0.5 * x * (1.0 - t * t) * _GC * (1.0 + 3.0 * _GA * x * x)


def _row(i):
    return (i, 0)


def _fixed(i):
    return (0, 0)


def _me_and_peers():
    x, y, c = lax.axis_index("x"), lax.axis_index("y"), lax.axis_index("c")
    me = 4 * x + 2 * y + c
    peers = []
    for k in range(1, N_DEV):
        px = jnp.bitwise_xor(x, (k >> 2) & 1)
        py = jnp.bitwise_xor(y, (k >> 1) & 1)
        pc = jnp.bitwise_xor(c, k & 1)
        peers.append(((px, py, pc), 4 * px + 2 * py + pc))
    return me, peers


def _xch_copies(ins, outs, send_sems, recv_sems, loc_sems, scatter):
    me, peers = _me_and_peers()
    local, sends, recvs = [], [], []
    for a in range(len(ins)):
        local.append(pltpu.make_async_copy(ins[a].at[me] if scatter else ins[a], outs[a].at[me], loc_sems.at[a]))
    for k, (dev, idx) in enumerate(peers):
        for a in range(len(ins)):
            src = ins[a].at[idx] if scatter else ins[a]
            for dst, group in ((outs[a].at[me], sends), (outs[a].at[idx], recvs)):
                group.append(pltpu.make_async_remote_copy(
                    src_ref=src, dst_ref=dst, send_sem=send_sems.at[a, k], recv_sem=recv_sems.at[a, k],
                    device_id=dev, device_id_type=pl.DeviceIdType.MESH))
    return local, sends, recvs


def _xch_start(copies):
    local, sends, _ = copies
    for cp in local + sends:
        cp.start()


def _xch_wait(copies):
    local, sends, recvs = copies
    for cp in recvs:
        cp.wait_recv()
    for cp in sends:
        cp.wait_send()
    for cp in local:
        cp.wait()


def _xch_shapes(arrays, scatter):
    return [jax.ShapeDtypeStruct(a.shape if scatter else (N_DEV,) + a.shape, a.dtype) for a in arrays]


def _xch_sems(n):
    return [pltpu.SemaphoreType.DMA((n, N_DEV - 1)), pltpu.SemaphoreType.DMA((n, N_DEV - 1)),
            pltpu.SemaphoreType.DMA((n,))]


_ANY = pl.BlockSpec(memory_space=pl.ANY)


def exchange(arrays, scatter, name):
    n = len(arrays)

    def body(*refs):
        copies = _xch_copies(refs[:n], refs[n:2 * n], *refs[2 * n:], scatter)
        _xch_start(copies)
        _xch_wait(copies)

    return pl.pallas_call(
        body, name=name, in_specs=[_ANY] * n, out_specs=[_ANY] * n, out_shape=_xch_shapes(arrays, scatter),
        scratch_shapes=_xch_sems(n), compiler_params=pltpu.CompilerParams(has_side_effects=True),
    )(*arrays)


def _call(body, name, grid, in_specs, out_specs, out_shape, sem, inputs, scratch=(), xch=None):
    params = pltpu.CompilerParams(dimension_semantics=sem, vmem_limit_bytes=VMEM_LIMIT)
    if xch is None:
        out = pl.pallas_call(body, name=name, grid=grid, in_specs=in_specs, out_specs=out_specs, out_shape=out_shape,
                             scratch_shapes=list(scratch), compiler_params=params)(*inputs)
        return list(out), None
    arrays, scatter = xch
    n, nin, nout, nsc = len(arrays), len(in_specs), len(out_specs), len(scratch)

    def wrapped(*refs):
        ins, refs = refs[:nin], refs[nin:]
        xin, refs = refs[:n], refs[n:]
        outs, refs = refs[:nout], refs[nout:]
        xout, refs = refs[:n], refs[n:]
        sc, sems = refs[:nsc], refs[nsc:]
        first = last = None
        for ax, size in enumerate(grid):
            f, e = pl.program_id(ax) == 0, pl.program_id(ax) == size - 1
            first = f if first is None else jnp.logical_and(first, f)
            last = e if last is None else jnp.logical_and(last, e)

        @pl.when(first)
        def _():
            _xch_start(_xch_copies(xin, xout, *sems, scatter))

        body(*ins, *outs, *sc)

        @pl.when(last)
        def _():
            _xch_wait(_xch_copies(xin, xout, *sems, scatter))

    out = pl.pallas_call(
        wrapped, name=name, grid=grid, in_specs=list(in_specs) + [_ANY] * n, out_specs=list(out_specs) + [_ANY] * n,
        out_shape=list(out_shape) + _xch_shapes(arrays, scatter), scratch_shapes=list(scratch) + _xch_sems(n),
        compiler_params=params)(*inputs, *arrays)
    return list(out[:nout]), list(out[nout:])


def norm_matmul(x, gain, w, name):
    s, d = x.shape
    n = w.shape[1]

    def body(x_ref, g_ref, w_ref, z_ref):
        xv = x_ref[...]
        h = (xv * _rms(xv) * g_ref[...]).astype(BF16)
        z_ref[...] = _dot(h, w_ref[...])

    (z,), _ = _call(body, name, (s // TM,),
                    [pl.BlockSpec((TM, d), _row), pl.BlockSpec((1, d), _fixed), pl.BlockSpec((d, n), _fixed)],
                    [pl.BlockSpec((TM, n), _row)], [jax.ShapeDtypeStruct((s, n), F32)], ("parallel",), (x, gain, w))
    return z


def in_bwd(x, gain, w, du_a, du_b, dq, dk, dv, dres, name, xch=None):
    s, d = x.shape
    n = w.shape[1]

    def body(x_ref, g_ref, w_ref, dua_ref, dub_ref, dq_ref, dk_ref, dv_ref, dres_ref, dx_ref, dw_ref, dg_ref):
        @pl.when(pl.program_id(0) == 0)
        def _():
            dw_ref[...] = jnp.zeros_like(dw_ref)
            dg_ref[...] = jnp.zeros_like(dg_ref)

        xv = x_ref[...]
        r = _rms(xv)
        xhat = xv * r
        g = g_ref[...]
        h = (xhat * g).astype(BF16)
        dz = jnp.concatenate([(dua_ref[...] + _lanes4(dub_ref)).astype(BF16), dq_ref[...].astype(BF16),
                              dk_ref[...].astype(BF16), dv_ref[...].astype(BF16)], axis=1)
        dh = _dot_nt(dz, w_ref[...])
        dw_ref[...] += _dot_tn(h, dz)
        dg_ref[...] += jnp.sum(dh * xhat, axis=0, keepdims=True)
        dx_ref[...] = dres_ref[...] + _rms_bwd(xhat, r, dh * g)

    return _call(
        body, name, (s // TM,),
        [pl.BlockSpec((TM, d), _row), pl.BlockSpec((1, d), _fixed), pl.BlockSpec((d, n), _fixed),
         pl.BlockSpec((TM, SSM_W), _row), pl.BlockSpec((4, TM, 128), lambda i: (0, i, 0)), pl.BlockSpec((TM, ATT_W), _row),
         pl.BlockSpec((TM, KV_W), _row), pl.BlockSpec((TM, KV_W), _row), pl.BlockSpec((TM, d), _row)],
        [pl.BlockSpec((TM, d), _row), pl.BlockSpec((d, n), _fixed), pl.BlockSpec((1, d), _fixed)],
        [jax.ShapeDtypeStruct((s, d), F32), jax.ShapeDtypeStruct((d, n), F32), jax.ShapeDtypeStruct((1, d), F32)],
        ("arbitrary",), (x, gain, w, du_a, du_b, dq, dk, dv, dres), xch=xch)


def _band_specs(nb):
    def w0(i):
        return jnp.clip(i - 1, 0, nb - 3)

    specs = [pl.BlockSpec((BLK, ATT_W), lambda i: (i, Q0 // ATT_W))]
    for col in (K0 // KV_W, V0 // KV_W):
        specs += [pl.BlockSpec((BLK, KV_W), lambda i, c=col, o=o: (w0(i) + o, c)) for o in range(3)]
    return specs


def _band_mask(i, nb):
    w0 = jnp.clip(i - 1, 0, nb - 3)
    qi = lax.broadcasted_iota(jnp.int32, (GQA_GROUP * BLK, 3 * BLK), 0) % BLK
    ci = lax.broadcasted_iota(jnp.int32, (GQA_GROUP * BLK, 3 * BLK), 1)
    dist = jnp.abs((i - w0) * BLK + qi - ci)
    return dist <= BLK, dist.astype(F32), w0


def _stack_heads(x, j):
    return jnp.concatenate([x[:, h * HEAD_DIM:(h + 1) * HEAD_DIM] for h in range(j * GQA_GROUP, (j + 1) * GQA_GROUP)],
                           axis=0)


def _per_head(vals):
    head = lax.broadcasted_iota(jnp.int32, (GQA_GROUP * BLK, 1), 0) // BLK
    out = jnp.full((GQA_GROUP * BLK, 1), vals[GQA_GROUP - 1], F32)
    for g in range(GQA_GROUP - 2, -1, -1):
        out = jnp.where(head == g, vals[g], out)
    return out


def attn_fwd(z, qg, kg, sink, name, xch=None):
    s = z.shape[0]
    nb = s // BLK

    def body(sink_ref, q_ref, k0, k1, k2, v0, v1, v2, qg_ref, kg_ref, o_ref, lse_ref):
        valid, distf, _ = _band_mask(pl.program_id(0), nb)
        k3 = jnp.concatenate([k0[...], k1[...], k2[...]], axis=0)
        v3 = jnp.concatenate([v0[...], v1[...], v2[...]], axis=0).astype(BF16)
        q = q_ref[...]
        for j in range(KV_HEADS):
            heads = range(j * GQA_GROUP, (j + 1) * GQA_GROUP)
            kj = k3[:, j * HEAD_DIM:(j + 1) * HEAD_DIM]
            knj = (kj * _rms(kj) * kg_ref[...]).astype(BF16)
            vj = v3[:, j * HEAD_DIM:(j + 1) * HEAD_DIM]
            q4 = _stack_heads(q, j)
            qn = (q4 * _rms(q4) * qg_ref[...]).astype(BF16)
            sc = _dot_nt(qn, knj) * 0.125 - _per_head([SLOPES[h] for h in heads]) * distf
            sc = jnp.where(valid, sc, NEG)
            sk = _per_head([sink_ref[h] for h in heads])
            m = jnp.maximum(jnp.max(sc, axis=-1, keepdims=True), sk)
            p = jnp.exp(sc - m)
            den = jnp.sum(p, axis=-1, keepdims=True) + jnp.exp(sk - m)
            o4 = _dot((p / den).astype(BF16), vj)
            lse4 = m + jnp.log(den)
            for g, h in enumerate(heads):
                o_ref[:, h * HEAD_DIM:(h + 1) * HEAD_DIM] = o4[g * BLK:(g + 1) * BLK]
                lse_ref[:, h:h + 1] = lse4[g * BLK:(g + 1) * BLK]

    return _call(
        body, name, (nb,),
        [pl.BlockSpec(memory_space=pltpu.SMEM)] + _band_specs(nb)
        + [pl.BlockSpec((1, HEAD_DIM), _fixed), pl.BlockSpec((1, HEAD_DIM), _fixed)],
        [pl.BlockSpec((BLK, ATT_W), _row), pl.BlockSpec((BLK, ATT_HEADS), _row)],
        [jax.ShapeDtypeStruct((s, ATT_W), F32), jax.ShapeDtypeStruct((s, ATT_HEADS), F32)],
        ("arbitrary",), (sink, z, z, z, z, z, z, z, qg, kg), xch=xch)


def attn_bwd(z, att, datt, lse, qg, kg, sink, name, xch=None):
    s = z.shape[0]
    nb = s // BLK

    def body(sink_ref, q_ref, k0, k1, k2, v0, v1, v2, o_ref, do_ref, lse_ref, qg_ref, kg_ref,
             dq_ref, dk_ref, dv_ref, dqg_ref, dkg_ref, dsk_ref):
        i = pl.program_id(0)

        @pl.when(i == 0)
        def _():
            dk_ref[...] = jnp.zeros_like(dk_ref)
            dv_ref[...] = jnp.zeros_like(dv_ref)
            dqg_ref[...] = jnp.zeros_like(dqg_ref)
            dkg_ref[...] = jnp.zeros_like(dkg_ref)
            dsk_ref[...] = jnp.zeros_like(dsk_ref)

        valid, distf, w0 = _band_mask(i, nb)
        k3 = jnp.concatenate([k0[...], k1[...], k2[...]], axis=0)
        v3 = jnp.concatenate([v0[...], v1[...], v2[...]], axis=0).astype(BF16)
        q = q_ref[...]
        o = o_ref[...]
        do = do_ref[...]
        lse = lse_ref[...]
        qgv = qg_ref[...]
        kgv = kg_ref[...]
        rows = pl.ds(pl.multiple_of(w0 * BLK, BLK), 3 * BLK)
        dqg = jnp.zeros((1, HEAD_DIM), F32)
        dkg = jnp.zeros((1, HEAD_DIM), F32)
        for j in range(KV_HEADS):
            heads = range(j * GQA_GROUP, (j + 1) * GQA_GROUP)
            cols = slice(j * HEAD_DIM, (j + 1) * HEAD_DIM)
            kj = k3[:, cols]
            rk = _rms(kj)
            khat = kj * rk
            knj = (khat * kgv).astype(BF16)
            vj = v3[:, cols]
            q4 = _stack_heads(q, j)
            rq = _rms(q4)
            qhat = q4 * rq
            qn = (qhat * qgv).astype(BF16)
            sc = _dot_nt(qn, knj) * 0.125 - _per_head([SLOPES[h] for h in heads]) * distf
            sc = jnp.where(valid, sc, NEG)
            lse4 = jnp.concatenate([lse[:, h:h + 1] for h in heads], axis=0)
            p = jnp.exp(sc - lse4)
            do4 = _stack_heads(do, j)
            delta = jnp.sum(do4 * _stack_heads(o, j), axis=-1, keepdims=True)
            dob = do4.astype(BF16)
            ds = p * (_dot_nt(dob, vj) - delta)
            sunk = jnp.exp(_per_head([sink_ref[h] for h in heads]) - lse4) * delta
            dsb = (ds * 0.125).astype(BF16)
            dvj = _dot_tn(p.astype(BF16), dob)
            dqn = _dot(dsb, knj)
            dkn = _dot_tn(dsb, qn)
            dqg = dqg + jnp.sum(dqn * qhat, axis=0, keepdims=True)
            dq4 = _rms_bwd(qhat, rq, dqn * qgv)
            for g, h in enumerate(heads):
                dq_ref[:, h * HEAD_DIM:(h + 1) * HEAD_DIM] = dq4[g * BLK:(g + 1) * BLK]
                dsk_ref[:, h:h + 1] += -jnp.sum(sunk[g * BLK:(g + 1) * BLK], axis=0, keepdims=True)
            dkg = dkg + jnp.sum(dkn * khat, axis=0, keepdims=True)
            dk_ref[rows, cols] += _rms_bwd(khat, rk, dkn * kgv)
            dv_ref[rows, cols] += dvj
        dqg_ref[...] += dqg
        dkg_ref[...] += dkg

    return _call(
        body, name, (nb,),
        [pl.BlockSpec(memory_space=pltpu.SMEM)] + _band_specs(nb)
        + [pl.BlockSpec((BLK, ATT_W), _row), pl.BlockSpec((BLK, ATT_W), _row), pl.BlockSpec((BLK, ATT_HEADS), _row),
           pl.BlockSpec((1, HEAD_DIM), _fixed), pl.BlockSpec((1, HEAD_DIM), _fixed)],
        [pl.BlockSpec((BLK, ATT_W), _row), pl.BlockSpec((s, KV_W), _fixed), pl.BlockSpec((s, KV_W), _fixed),
         pl.BlockSpec((1, HEAD_DIM), _fixed), pl.BlockSpec((1, HEAD_DIM), _fixed), pl.BlockSpec((1, ATT_HEADS), _fixed)],
        [jax.ShapeDtypeStruct((s, ATT_W), F32), jax.ShapeDtypeStruct((s, KV_W), F32), jax.ShapeDtypeStruct((s, KV_W), F32),
         jax.ShapeDtypeStruct((1, HEAD_DIM), F32), jax.ShapeDtypeStruct((1, HEAD_DIM), F32),
         jax.ShapeDtypeStruct((1, ATT_HEADS), F32)],
        ("arbitrary",), (sink, z, z, z, z, z, z, z, att, datt, lse, qg, kg), xch=xch)


def _group_steps(nc):
    nstep = nc // SEGS
    return nstep, min(32, nstep)


def _lanes4(ref):
    return jnp.concatenate([ref[q] for q in range(4)], axis=1)


def _pair_split(x4):
    return [jnp.concatenate([x4[:, q * 128 + r * SSM_P:q * 128 + (r + 1) * SSM_P] for q in range(4)], axis=1)
            for r in range(2)]


def _pair_merge(a0, a1):
    return [jnp.concatenate([a[:, q * SSM_P:(q + 1) * SSM_P] for a in (a0, a1)], axis=1) for q in range(4)]


def to_groups(src, col, name):
    s, w = src.shape
    nc = s // CH
    nstep, sb = _group_steps(nc)

    def body(u0, u1, u2, u3, o_ref):
        slot = lax.broadcasted_iota(jnp.int32, (sb, 128), 1) // SSM_H
        for seg in range(SEGS):
            for vc, u_ref in enumerate((u0, u1, u2, u3)):
                for sh in range(2):
                    accs = [None] * 8
                    for sl in range(8):
                        piece = u_ref[seg, pl.ds(sh * 8 + sl, sb, stride=CH), :]
                        for gl in range(8):
                            shift = ((sl - gl) * SSM_H) % 128
                            r = pltpu.roll(piece, shift, 1) if shift else piece
                            accs[gl] = r if sl == 0 else jnp.where(slot == sl, r, accs[gl])
                    for gl in range(8):
                        o_ref[(vc * 8 + gl) * 2 + sh, pl.ds(seg, sb, stride=SEGS), :] = accs[gl]

    src3 = src.reshape(SEGS, s // SEGS, w)
    (out,), _ = _call(body, name, (nstep // sb,),
                      [pl.BlockSpec((SEGS, sb * CH, 128), lambda i, c=col + vc: (0, i, c)) for vc in range(4)],
                      [pl.BlockSpec((2 * SSM_G, sb * SEGS, 128), lambda i: (0, i, 0))],
                      [jax.ShapeDtypeStruct((2 * SSM_G, nc, 128), F32)], ("parallel",), (src3,) * 4)
    return out


def from_groups(yc, name):
    nc = yc.shape[1]
    s = nc * CH
    nstep, sb = _group_steps(nc)

    def body(y_ref, o_ref):
        slot = lax.broadcasted_iota(jnp.int32, (sb, 128), 1) // SSM_H
        for seg in range(SEGS):
            for vc in range(4):
                for sh in range(2):
                    rows = [None] * 8
                    for gl in range(8):
                        piece = y_ref[(vc * 8 + gl) * 2 + sh, pl.ds(seg, sb, stride=SEGS), :]
                        for sl in range(8):
                            shift = ((gl - sl) * SSM_H) % 128
                            r = pltpu.roll(piece, shift, 1) if shift else piece
                            rows[sl] = r if gl == 0 else jnp.where(slot == gl, r, rows[sl])
                    for sl in range(8):
                        o_ref[vc, seg, pl.ds(sh * 8 + sl, sb, stride=CH), :] = rows[sl]

    (out,), _ = _call(body, name, (nstep // sb,),
                      [pl.BlockSpec((2 * SSM_G, sb * SEGS, 128), lambda i: (0, i, 0))],
                      [pl.BlockSpec((4, SEGS, sb * CH, 128), lambda i: (0, 0, i, 0))],
                      [jax.ShapeDtypeStruct((4, SEGS, s // SEGS, 128), F32)], ("parallel",), (yc,))
    return out.reshape(4, s, 128)


def _pair3(i):
    return (i, 0, 0)


def _state_blk(i):
    return (0, 0, i)


def ssm_in(ug, e, name):
    nc = ug.shape[1]

    def body(u_ref, e_ref, s_ref):
        u = _lanes4(u_ref).astype(BF16)
        for q, blk in enumerate(_pair_merge(_dot(u[:, :GW], e_ref[0]), _dot(u[:, GW:], e_ref[1]))):
            s_ref[q] = blk

    (out,), _ = _call(body, name, (SSM_G // 2,),
                      [pl.BlockSpec((4, nc, 128), _pair3), pl.BlockSpec((2, GW, 4 * SSM_P), _pair3)],
                      [pl.BlockSpec((4, nc, 128), _state_blk)],
                      [jax.ShapeDtypeStruct((4, nc, SSM_G * SSM_P), F32)], ("parallel",), (ug, e))
    return out


def chunk_scan(s4, a4, flip, name, xp4=None):
    _, nc, gp = s4.shape
    nstep = nc // SEGS
    assert nstep & (nstep - 1) == 0
    ct = 512
    with_da = xp4 is not None

    def body(*refs):
        if with_da:
            s_ref, a_ref, xp_ref, o_ref, da_ref = refs
        else:
            s_ref, a_ref, o_ref = refs
        rows = lax.broadcasted_iota(jnp.int32, (SEGS, ct), 0)
        zero = jnp.zeros((SEGS, ct), F32)
        for pair in range(2):
            asc = (pair == 0) != flip
            ir, ii = 2 * pair, 2 * pair + 1
            ar1 = a_ref[ir]
            ai1 = a_ref[ii]
            ar = jnp.broadcast_to(ar1, (SEGS, ct))
            ai = jnp.broadcast_to(ai1, (SEGS, ct))

            def tile(t):
                tt = t if asc else nstep - 1 - t
                return pl.ds(pl.multiple_of(tt * SEGS, SEGS), SEGS)

            def local(t, c):
                xr, xi = c
                sl = tile(t)
                return (ar * xr - ai * xi + s_ref[ir, sl, :], ar * xi + ai * xr + s_ref[ii, sl, :])

            er, ei = lax.fori_loop(0, nstep, local, (zero, zero))
            pr, pi = ar1, ai1
            for _ in range(nstep.bit_length() - 1):
                pr, pi = pr * pr - pi * pi, 2.0 * pr * pi
            cr = jnp.zeros((1, ct), F32)
            ci = jnp.zeros((1, ct), F32)
            xin_r, xin_i = zero, zero
            for k in range(SEGS):
                sg = k if asc else SEGS - 1 - k
                here = rows == sg
                xin_r = jnp.where(here, cr, xin_r)
                xin_i = jnp.where(here, ci, xin_i)
                lr = jnp.sum(jnp.where(here, er, 0.0), axis=0, keepdims=True)
                li = jnp.sum(jnp.where(here, ei, 0.0), axis=0, keepdims=True)
                cr, ci = pr * cr - pi * ci + lr, pr * ci + pi * cr + li

            def final(t, c):
                xr, xi, acr, aci = c
                sl = tile(t)
                o_ref[ir, sl, :] = xr
                o_ref[ii, sl, :] = xi
                if with_da:
                    br = xp_ref[ir, sl, :]
                    bi = xp_ref[ii, sl, :]
                    acr = acr + br * xr + bi * xi
                    aci = aci + br * xi - bi * xr
                return (ar * xr - ai * xi + s_ref[ir, sl, :], ar * xi + ai * xr + s_ref[ii, sl, :], acr, aci)

            _, _, acr, aci = lax.fori_loop(0, nstep, final, (xin_r, xin_i, zero, zero))
            if with_da:
                da_ref[ir] = jnp.sum(acr, axis=0, keepdims=True)
                da_ref[ii] = jnp.sum(aci, axis=0, keepdims=True)

    blk = pl.BlockSpec((4, nc, ct), _state_blk)
    ablk = pl.BlockSpec((4, 1, ct), _state_blk)
    sds = jax.ShapeDtypeStruct((4, nc, gp), F32)
    if with_da:
        out, _ = _call(body, name, (gp // ct,), [blk, ablk, blk], [blk, ablk],
                       [sds, jax.ShapeDtypeStruct((4, 1, gp), F32)], ("parallel",), (s4, a4, xp4))
        return out
    (out,), _ = _call(body, name, (gp // ct,), [blk, ablk], [blk], [sds], ("parallel",), (s4, a4))
    return out


def _state_cat(ref):
    return _lanes4(ref).astype(BF16)


def ssm_out(ug, t, xp4, o, name):
    nc = ug.shape[1]

    def body(u_ref, t_ref, xp_ref, o_ref, y_ref):
        xs = _pair_split(_lanes4(xp_ref))
        u = _lanes4(u_ref).astype(BF16)
        for r in range(2):
            y = _dot(u[:, r * GW:(r + 1) * GW], t_ref[r]) + _dot(xs[r].astype(BF16), o_ref[r])
            y_ref[2 * r] = y[:, :128]
            y_ref[2 * r + 1] = y[:, 128:]

    (out,), _ = _call(body, name, (SSM_G // 2,),
                      [pl.BlockSpec((4, nc, 128), _pair3), pl.BlockSpec((2, GW, GW), _pair3),
                       pl.BlockSpec((4, nc, 128), _state_blk), pl.BlockSpec((2, 4 * SSM_P, GW), _pair3)],
                      [pl.BlockSpec((4, nc, 128), _pair3)],
                      [jax.ShapeDtypeStruct((2 * SSM_G, nc, 128), F32)], ("parallel",), (ug, t, xp4, o))
    return out


def ssm_out_bwd(dyg, xp4, o, name):
    nc = dyg.shape[1]

    def body(dy_ref, xp_ref, o_ref, do_ref, dxp_ref):
        xs = _pair_split(_lanes4(xp_ref))
        dy = _lanes4(dy_ref).astype(BF16)
        dxs = []
        for r in range(2):
            dyr = dy[:, r * GW:(r + 1) * GW]
            do_ref[r] = _dot_tn(xs[r].astype(BF16), dyr)
            dxs.append(_dot_nt(dyr, o_ref[r]))
        for q, blk in enumerate(_pair_merge(*dxs)):
            dxp_ref[q] = blk

    out, _ = _call(body, name, (SSM_G // 2,),
                   [pl.BlockSpec((4, nc, 128), _pair3), pl.BlockSpec((4, nc, 128), _state_blk),
                    pl.BlockSpec((2, 4 * SSM_P, GW), _pair3)],
                   [pl.BlockSpec((2, 4 * SSM_P, GW), _pair3), pl.BlockSpec((4, nc, 128), _state_blk)],
                   [jax.ShapeDtypeStruct((SSM_G, 4 * SSM_P, GW), F32), jax.ShapeDtypeStruct((4, nc, SSM_G * SSM_P), F32)],
                   ("parallel",), (dyg, xp4, o))
    return out


def ssm_in_bwd(ug, dyg, ds4, t, e, name):
    nc = ug.shape[1]

    def body(u_ref, dy_ref, ds_ref, t_ref, e_ref, dt_ref, de_ref, du_ref):
        dss = _pair_split(_lanes4(ds_ref))
        u = _lanes4(u_ref).astype(BF16)
        dy = _lanes4(dy_ref).astype(BF16)
        for r in range(2):
            cols = slice(r * GW, (r + 1) * GW)
            ds = dss[r].astype(BF16)
            dt_ref[r] = _dot_tn(u[:, cols], dy[:, cols])
            de_ref[r] = _dot_tn(u[:, cols], ds)
            du = _dot_nt(dy[:, cols], t_ref[r]) + _dot_nt(ds, e_ref[r])
            du_ref[2 * r] = du[:, :128]
            du_ref[2 * r + 1] = du[:, 128:]

    out, _ = _call(body, name, (SSM_G // 2,),
                   [pl.BlockSpec((4, nc, 128), _pair3), pl.BlockSpec((4, nc, 128), _pair3),
                    pl.BlockSpec((4, nc, 128), _state_blk), pl.BlockSpec((2, GW, GW), _pair3),
                    pl.BlockSpec((2, GW, 4 * SSM_P), _pair3)],
                   [pl.BlockSpec((2, GW, GW), _pair3), pl.BlockSpec((2, GW, 4 * SSM_P), _pair3),
                    pl.BlockSpec((4, nc, 128), _pair3)],
                   [jax.ShapeDtypeStruct((SSM_G, GW, GW), F32), jax.ShapeDtypeStruct((SSM_G, GW, 4 * SSM_P), F32),
                    jax.ShapeDtypeStruct((2 * SSM_G, nc, 128), F32)], ("parallel",), (ug, dyg, ds4, t, e))
    return out


def ssm_post_fwd(yc, z, dskip, wglu, name):
    s = yc.shape[1]

    def body(y_ref, u_ref, d_ref, w_ref, o_ref, yp_ref, g_ref):
        yp = _lanes4(y_ref) + d_ref[...] * u_ref[...]
        yp_ref[...] = yp
        gv = _dot(_gelu(yp).astype(BF16), w_ref[...])
        g_ref[...] = gv
        o_ref[...] = gv[:, :SSM_W] * _sigmoid(gv[:, SSM_W:])

    out, _ = _call(body, name, (s // TM,),
                   [pl.BlockSpec((4, TM, 128), lambda i: (0, i, 0)), pl.BlockSpec((TM, SSM_W), lambda i: (i, U0 // SSM_W)),
                    pl.BlockSpec((1, SSM_W), _fixed), pl.BlockSpec((SSM_W, 2 * SSM_W), _fixed)],
                   [pl.BlockSpec((TM, SSM_W), _row), pl.BlockSpec((TM, SSM_W), _row), pl.BlockSpec((TM, 2 * SSM_W), _row)],
                   [jax.ShapeDtypeStruct((s, SSM_W), F32), jax.ShapeDtypeStruct((s, SSM_W), F32),
                    jax.ShapeDtypeStruct((s, 2 * SSM_W), F32)], ("parallel",), (yc, z, dskip, wglu))
    return out


def ssm_post_bwd(dssm, gpre, ypre, z, dskip, wglu, name):
    s = dssm.shape[0]

    def body(do_ref, g_ref, yp_ref, u_ref, d_ref, w_ref, dy_ref, du_ref, dw_ref, dd_ref):
        @pl.when(pl.program_id(0) == 0)
        def _():
            dw_ref[...] = jnp.zeros_like(dw_ref)
            dd_ref[...] = jnp.zeros_like(dd_ref)

        gv = g_ref[...]
        val = gv[:, :SSM_W]
        sg = _sigmoid(gv[:, SSM_W:])
        do = do_ref[...]
        dg = jnp.concatenate([do * sg, do * val * sg * (1.0 - sg)], axis=1).astype(BF16)
        yp = yp_ref[...]
        dgl = _dot_nt(dg, w_ref[...])
        dw_ref[...] += _dot_tn(_gelu(yp).astype(BF16), dg)
        dyp = dgl * _gelu_grad(yp)
        dy_ref[...] = dyp
        du_ref[...] = dyp * d_ref[...]
        dd_ref[...] += jnp.sum(dyp * u_ref[...], axis=0, keepdims=True)

    out, _ = _call(body, name, (s // TM,),
                   [pl.BlockSpec((TM, SSM_W), _row), pl.BlockSpec((TM, 2 * SSM_W), _row), pl.BlockSpec((TM, SSM_W), _row),
                    pl.BlockSpec((TM, SSM_W), lambda i: (i, U0 // SSM_W)), pl.BlockSpec((1, SSM_W), _fixed),
                    pl.BlockSpec((SSM_W, 2 * SSM_W), _fixed)],
                   [pl.BlockSpec((TM, SSM_W), _row), pl.BlockSpec((TM, SSM_W), _row),
                    pl.BlockSpec((SSM_W, 2 * SSM_W), _fixed), pl.BlockSpec((1, SSM_W), _fixed)],
                   [jax.ShapeDtypeStruct((s, SSM_W), F32), jax.ShapeDtypeStruct((s, SSM_W), F32),
                    jax.ShapeDtypeStruct((SSM_W, 2 * SSM_W), F32), jax.ShapeDtypeStruct((1, SSM_W), F32)],
                   ("arbitrary",), (dssm, gpre, ypre, z, dskip, wglu))
    return out


def _toeplitz_select():
    row = lax.broadcasted_iota(jnp.int32, (GW, CH * GW), 0)
    col = lax.broadcasted_iota(jnp.int32, (GW, CH * GW), 1)
    j, h2 = row // SSM_H, row % SSM_H
    s, t, h = col // GW, (col % GW) // SSM_H, col % SSM_H
    same = h2 == h
    return jnp.concatenate([same & (t - s == j), same & (s - t == j)], axis=0).astype(F32)


def ssm_mats(lam_re, lam_im, log_dt, b_re, b_im, c_re, c_im):
    g, p, hh = SSM_G, SSM_P, SSM_H
    hp = lax.Precision.HIGHEST
    jj = jnp.arange(CH + 1, dtype=F32)
    dt = jnp.exp(log_dt)[..., None]
    mag = jnp.exp((lam_re * dt)[..., None] * jj)
    ang = (lam_im * dt)[..., None] * jj
    pr, pi = mag * jnp.cos(ang), mag * jnp.sin(ang)
    abr, abi = pr[..., 1], pi[..., 1]
    den = lam_re * lam_re + lam_im * lam_im
    zr = ((abr - 1.0) * lam_re + abi * lam_im) / den
    zi = (abi * lam_re - (abr - 1.0) * lam_im) / den
    bbr = zr[..., None] * b_re[None] - zi[..., None] * b_im[None]
    bbi = zr[..., None] * b_im[None] + zi[..., None] * b_re[None]
    crt, cit = c_re.transpose(0, 1, 3, 2), c_im.transpose(0, 1, 3, 2)
    car = pr[..., None] * crt[..., None, :] - pi[..., None] * cit[..., None, :]
    cai = pr[..., None] * cit[..., None, :] + pi[..., None] * crt[..., None, :]
    lhs = jnp.concatenate([bbr, -bbi], axis=2).transpose(0, 1, 3, 2)
    rhs = jnp.concatenate([car[..., :CH, :], cai[..., :CH, :]], axis=2).reshape(2, g, 2 * p, GW)
    kt = jnp.einsum("dgkp,dgpn->dgkn", lhs, rhs, precision=hp)
    kcat = jnp.concatenate([kt[0], kt[1]], axis=-1).reshape(g * hh, 2 * GW)
    tmat = jnp.dot(kcat, _toeplitz_select(), precision=hp)
    tmat = tmat.reshape(g, hh, CH, GW).transpose(0, 2, 1, 3).reshape(g, GW, GW)

    def e_part(d, pw_r, pw_i):
        pw_r, pw_i = pw_r.transpose(0, 2, 1)[:, :, None, :], pw_i.transpose(0, 2, 1)[:, :, None, :]
        br, bi = bbr[d].transpose(0, 2, 1)[:, None], bbi[d].transpose(0, 2, 1)[:, None]
        return [pw_r * br - pw_i * bi, pw_r * bi + pw_i * br]

    eparts = (e_part(0, pr[0, ..., :CH][..., ::-1], pi[0, ..., :CH][..., ::-1])
              + e_part(1, pr[1, ..., :CH], pi[1, ..., :CH]))
    emat = jnp.stack(eparts, axis=3).reshape(g, GW, 4 * p)

    oparts = [car[0, ..., 1:, :], -cai[0, ..., 1:, :], car[1, ..., 1:, :][..., ::-1, :], -cai[1, ..., 1:, :][..., ::-1, :]]
    omat = jnp.stack([v.reshape(g, p, GW) for v in oparts], axis=1).reshape(g, 4 * p, GW)
    amat = jnp.stack([pr[0, ..., CH], pi[0, ..., CH], pr[1, ..., CH], pi[1, ..., CH]], axis=0).reshape(4, 1, g * p)
    return tmat, emat, omat, amat


def outproj_fwd(x, att, ssm, wo, name):
    s, d = x.shape

    def body(x_ref, a_ref, s_ref, w_ref, o_ref):
        o_ref[...] = (x_ref[...] + _dot(a_ref[...].astype(BF16), w_ref[0:ATT_W, :])
                      + _dot(s_ref[...].astype(BF16), w_ref[ATT_W:, :]))

    (out,), _ = _call(body, name, (s // TM,),
                      [pl.BlockSpec((TM, d), _row), pl.BlockSpec((TM, ATT_W), _row), pl.BlockSpec((TM, SSM_W), _row),
                       pl.BlockSpec((ATT_W + SSM_W, d), _fixed)],
                      [pl.BlockSpec((TM, d), _row)], [jax.ShapeDtypeStruct((s, d), F32)], ("parallel",),
                      (x, att, ssm, wo))
    return out


def outproj_bwd(dx1, att, ssm, wo, name):
    s, d = dx1.shape

    def body(dx_ref, a_ref, s_ref, w_ref, da_ref, ds_ref, dw_ref):
        @pl.when(pl.program_id(0) == 0)
        def _():
            dw_ref[...] = jnp.zeros_like(dw_ref)

        dxb = dx_ref[...].astype(BF16)
        da_ref[...] = _dot_nt(dxb, w_ref[0:ATT_W, :])
        ds_ref[...] = _dot_nt(dxb, w_ref[ATT_W:, :])
        dw_ref[0:ATT_W, :] += _dot_tn(a_ref[...].astype(BF16), dxb)
        dw_ref[ATT_W:, :] += _dot_tn(s_ref[...].astype(BF16), dxb)

    out, _ = _call(body, name, (s // TM,),
                   [pl.BlockSpec((TM, d), _row), pl.BlockSpec((TM, ATT_W), _row), pl.BlockSpec((TM, SSM_W), _row),
                    pl.BlockSpec((ATT_W + SSM_W, d), _fixed)],
                   [pl.BlockSpec((TM, ATT_W), _row), pl.BlockSpec((TM, SSM_W), _row),
                    pl.BlockSpec((ATT_W + SSM_W, d), _fixed)],
                   [jax.ShapeDtypeStruct((s, ATT_W), F32), jax.ShapeDtypeStruct((s, SSM_W), F32),
                    jax.ShapeDtypeStruct((ATT_W + SSM_W, d), F32)], ("arbitrary",), (dx1, att, ssm, wo))
    return out


def ffn_fwd(x1, gain, w1, w2, name, xch=None):
    s, d = x1.shape
    nch, _, fc = w1.shape

    def body(x_ref, g_ref, w1_ref, w2_ref, o_ref, h_ref, a_ref):
        @pl.when(pl.program_id(1) == 0)
        def _():
            xv = x_ref[...]
            h_ref[...] = (xv * _rms(xv) * g_ref[...]).astype(BF16)
            o_ref[...] = xv

        a = _dot(h_ref[...], w1_ref[...])
        a_ref[...] = a.astype(BF16)
        o_ref[...] += _dot(jnp.square(jnp.maximum(a, 0.0)).astype(BF16), w2_ref[...])

    tm = min(TM_FFN, s)
    return _call(
        body, name, (s // tm, nch),
        [pl.BlockSpec((tm, d), lambda i, k: (i, 0)), pl.BlockSpec((1, d), lambda i, k: (0, 0)),
         pl.BlockSpec((None, d, fc), lambda i, k: (k, 0, 0)), pl.BlockSpec((None, fc, d), lambda i, k: (k, 0, 0))],
        [pl.BlockSpec((tm, d), lambda i, k: (i, 0)), pl.BlockSpec((tm, d), lambda i, k: (i, 0)),
         pl.BlockSpec((tm, fc), lambda i, k: (i, k))],
        [jax.ShapeDtypeStruct((s, d), F32), jax.ShapeDtypeStruct((s, d), BF16), jax.ShapeDtypeStruct((s, nch * fc), BF16)],
        ("arbitrary", "arbitrary"), (x1, gain, w1, w2), xch=xch)


def ffn_bwd_tok(dx2, x1, gain, a, w1, w2, name, xch=None):
    s, d = x1.shape
    nch, _, fc = w1.shape

    def body(dx_ref, x_ref, g_ref, a_ref, w1_ref, w2_ref, da_ref, dx1_ref, dg_ref, dxb_ref, dh_ref):
        i = pl.program_id(0)
        k = pl.program_id(1)

        @pl.when(jnp.logical_and(i == 0, k == 0))
        def _():
            dg_ref[...] = jnp.zeros_like(dg_ref)

        @pl.when(k == 0)
        def _():
            dxb_ref[...] = dx_ref[...].astype(BF16)
            dh_ref[...] = jnp.zeros_like(dh_ref)

        dr = _dot_nt(dxb_ref[...], w2_ref[...])
        da = (dr * (2.0 * jnp.maximum(a_ref[...].astype(F32), 0.0))).astype(BF16)
        da_ref[...] = da
        dh_ref[...] += _dot_nt(da, w1_ref[...])

        @pl.when(k == nch - 1)
        def _():
            xv = x_ref[...]
            r = _rms(xv)
            xhat = xv * r
            dh = dh_ref[...]
            dg_ref[...] += jnp.sum(dh * xhat, axis=0, keepdims=True)
            dx1_ref[...] = dx_ref[...] + _rms_bwd(xhat, r, dh * g_ref[...])

    tm = min(TM_FFN, s)
    return _call(
        body, name, (s // tm, nch),
        [pl.BlockSpec((tm, d), lambda i, k: (i, 0)), pl.BlockSpec((tm, d), lambda i, k: (i, 0)),
         pl.BlockSpec((1, d), lambda i, k: (0, 0)), pl.BlockSpec((tm, fc), lambda i, k: (i, k)),
         pl.BlockSpec((None, d, fc), lambda i, k: (k, 0, 0)), pl.BlockSpec((None, fc, d), lambda i, k: (k, 0, 0))],
        [pl.BlockSpec((tm, fc), lambda i, k: (i, k)), pl.BlockSpec((tm, d), lambda i, k: (i, 0)),
         pl.BlockSpec((1, d), lambda i, k: (0, 0)), pl.BlockSpec((tm, d), lambda i, k: (i, 0))],
        [jax.ShapeDtypeStruct((s, nch * fc), BF16), jax.ShapeDtypeStruct((s, d), F32),
         jax.ShapeDtypeStruct((1, d), F32), jax.ShapeDtypeStruct((s, d), BF16)],
        ("arbitrary", "arbitrary"), (dx2, x1, gain, a, w1, w2), scratch=[pltpu.VMEM((tm, d), F32)], xch=xch)


def ffn_bwd_w(h2, da, a, dxb, nch, name, xch=None):
    s, d = h2.shape
    fc = a.shape[1] // nch

    def body(h_ref, da_ref, a_ref, dx_ref, dw1_ref, dw2_ref):
        @pl.when(pl.program_id(1) == 0)
        def _():
            dw1_ref[...] = jnp.zeros_like(dw1_ref)
            dw2_ref[...] = jnp.zeros_like(dw2_ref)

        dw1_ref[...] += _dot_tn(h_ref[...], da_ref[...])
        r = jnp.square(jnp.maximum(a_ref[...].astype(F32), 0.0)).astype(BF16)
        dw2_ref[...] += _dot_tn(r, dx_ref[...])

    tm = min(TM_FFN, s)
    return _call(
        body, name, (nch, s // tm),
        [pl.BlockSpec((tm, d), lambda k, t: (t, 0)), pl.BlockSpec((tm, fc), lambda k, t: (t, k)),
         pl.BlockSpec((tm, fc), lambda k, t: (t, k)), pl.BlockSpec((tm, d), lambda k, t: (t, 0))],
        [pl.BlockSpec((None, d, fc), lambda k, t: (k, 0, 0)), pl.BlockSpec((None, fc, d), lambda k, t: (k, 0, 0))],
        [jax.ShapeDtypeStruct((nch, d, fc), F32), jax.ShapeDtypeStruct((nch, fc, d), F32)],
        ("arbitrary", "arbitrary"), (h2, da, a, dxb), xch=xch)


def loss_grad(xf, tgt, name):
    s, d = xf.shape
    nt = s // TM

    def body(x_ref, t_ref, dx_ref, l_ref, acc_ref):
        i = pl.program_id(0)

        @pl.when(i == 0)
        def _():
            acc_ref[...] = jnp.zeros_like(acc_ref)

        e = x_ref[...] - t_ref[...]
        dx_ref[...] = e * (1.0 / d)
        acc_ref[...] += jnp.sum(e * e, axis=0, keepdims=True)

        @pl.when(i == nt - 1)
        def _():
            l_ref[...] = jnp.sum(acc_ref[...], axis=1, keepdims=True) * (0.5 / d)

    out, _ = _call(body, name, (nt,), [pl.BlockSpec((TM, d), _row), pl.BlockSpec((TM, d), _row)],
                   [pl.BlockSpec((TM, d), _row), pl.BlockSpec((1, 1), _fixed)],
                   [jax.ShapeDtypeStruct((s, d), F32), jax.ShapeDtypeStruct((1, 1), F32)], ("arbitrary",),
                   (xf, tgt), scratch=[pltpu.VMEM((1, d), F32)])
    return out


def adamw_sum(parts, w, m, v, br, name):
    nl = len(parts)
    npart, r, c = parts[0].shape
    nb = r // br
    c1 = 1.0 - ADAM_B1 ** ADAM_STEP
    c2 = 1.0 - ADAM_B2 ** ADAM_STEP

    def body(*refs):
        p_refs = refs[:nl]
        w_ref, m_ref, v_ref, g_ref, d_ref, nm_ref, nv_ref = refs[nl:]
        for l in range(nl):
            @pl.when(pl.program_id(0) == l)
            def _(l=l):
                g = p_refs[l][0]
                for j in range(1, npart):
                    g = g + p_refs[l][j]
                m2 = ADAM_B1 * m_ref[...] + (1.0 - ADAM_B1) * g
                v2 = ADAM_B2 * v_ref[...] + (1.0 - ADAM_B2) * jnp.square(g)
                g_ref[...] = g
                nm_ref[...] = m2
                nv_ref[...] = v2
                d_ref[...] = -ADAM_LR * ((m2 / c1) / (jnp.sqrt(v2 / c2) + ADAM_EPS) + ADAM_WD * w_ref[...])

    blk = pl.BlockSpec((br, c), lambda l, i: (l * nb + i, 0))
    pspecs = [pl.BlockSpec((npart, br, c), lambda l, i, own=own: (0, jnp.where(l == own, i, 0), 0)) for own in range(nl)]
    sds = jax.ShapeDtypeStruct((nl * r, c), F32)
    out, _ = _call(body, name, (nl, nb), pspecs + [blk, blk, blk], [blk, blk, blk, blk], [sds, sds, sds, sds],
                   ("arbitrary", "arbitrary"), (*parts, w, m, v))
    return out


def sum_parts(parts, name):
    n, r, c = parts.shape

    def body(p_ref, o_ref):
        g = p_ref[0]
        for j in range(1, n):
            g = g + p_ref[j]
        o_ref[...] = g

    (out,), _ = _call(body, name, (1,), [pl.BlockSpec((n, r, c), lambda i: (0, 0, 0))], [pl.BlockSpec((r, c), _fixed)],
                      [jax.ShapeDtypeStruct((r, c), F32)], ("arbitrary",), (parts,))
    return out


def layer_fwd(x, w_in, p, mats, l, late, nxt):
    tmat, emat, omat, amat = mats
    z = norm_matmul(x, p["norm1"], w_in, f"in_fwd{l}")
    (att, lse), got = attn_fwd(z, p["q_gain"], p["k_gain"], p["sink"], f"attn_fwd{l}", (late, False))
    p = dict(p, **_whole_weights(dict(zip(BIG[-len(got):], got))))
    ug = to_groups(z, U0 // 128, f"ssm_to_groups{l}")
    s4 = ssm_in(ug, emat, f"ssm_in{l}")
    xp4 = chunk_scan(s4, amat, False, f"ssm_scan{l}")
    yc = from_groups(ssm_out(ug, tmat, xp4, omat, f"ssm_out{l}"), f"ssm_from_groups{l}")
    ssm, ypre, gpre = ssm_post_fwd(yc, z, p["d_skip"], p["w_glu"], f"ssm_post{l}")
    x1 = outproj_fwd(x, att, ssm, p["w_out"], f"out_fwd{l}")
    (x2, h2, a), gathered = ffn_fwd(x1, p["norm2"], p["w_ff1"], p["w_ff2"], f"ffn_fwd{l}",
                                    None if nxt is None else (nxt, False))
    saved = dict(x=x, z=z, att=att, lse=lse, ug=ug, xp4=xp4, ssm=ssm, ypre=ypre, gpre=gpre, x1=x1, h2=h2, a=a)
    return x2, saved, dict(p, w_in=w_in), gathered


def layer_bwd(dx2, p, mats, sv, l, above):
    tmat, emat, omat, amat = mats
    nch = p["w_ff1"].shape[0]
    last = l == 0
    (da, dx1, dnorm2, dxb), got_tok = ffn_bwd_tok(dx2, sv["x1"], p["norm2"], sv["a"], p["w_ff1"], p["w_ff2"],
                                                  f"ffn_bwd{l}", None if above is None else (above["tok"], True))
    (dw1, dw2), got_w = ffn_bwd_w(sv["h2"], da, sv["a"], dxb, nch, f"ffn_bwdw{l}",
                                  None if above is None else (above["w"], True))
    datt, dssm, dwo = outproj_bwd(dx1, sv["att"], sv["ssm"], p["w_out"], f"out_bwd{l}")
    dyc, du_skip, dwglu, ddskip = ssm_post_bwd(dssm, sv["gpre"], sv["ypre"], sv["z"], p["d_skip"], p["w_glu"],
                                               f"ssm_post_bwd{l}")
    dyg = to_groups(dyc, 0, f"ssm_to_groups_bwd{l}")
    domat, dxp4 = ssm_out_bwd(dyg, sv["xp4"], omat, f"ssm_out_bwd{l}")
    aconj = amat * jnp.array([1.0, -1.0, 1.0, -1.0], F32).reshape(4, 1, 1)
    ds4, damat = chunk_scan(dxp4, aconj, True, f"ssm_scan_bwd{l}", xp4=sv["xp4"])
    dtmat, demat, dug = ssm_in_bwd(sv["ug"], dyg, ds4, tmat, emat, f"ssm_in_bwd{l}")
    du_core = from_groups(dug, f"ssm_from_groups_bwd{l}")
    (dq, dk, dv, dqg, dkg, dsink), got_ff = attn_bwd(sv["z"], sv["att"], datt, sv["lse"], p["q_gain"], p["k_gain"],
                                                     p["sink"], f"attn_bwd{l}", ([dw1, dw2] if last else [dw1], True))
    blocks = _grad_blocks(dict(w_glu=dwglu, w_out=dwo))
    (dx, dwin, dnorm1), got_mix = in_bwd(sv["x"], p["norm1"], p["w_in"], du_skip, du_core, dq, dk, dv, dx1, f"in_bwd{l}",
                                         ([blocks["w_glu"], blocks["w_out"]], True) if last else None)
    grads = dict(norm1=dnorm1, q_gain=dqg, k_gain=dkg, sink=dsink, d_skip=ddskip, norm2=dnorm2)
    win_blocks = _grad_blocks(dict(w_in=dwin))["w_in"]
    mine = dict(w_ff1=got_ff[0])
    if last:
        mine.update(w_ff2=got_ff[1], w_glu=got_mix[0], w_out=got_mix[1])
        below = win_blocks
    else:
        below = dict(tok=[win_blocks, blocks["w_glu"], blocks["w_out"]], w=[dw2])
    theirs = {} if above is None else dict(w_in=got_tok[0], w_glu=got_tok[1], w_out=got_tok[2], w_ff2=got_w[0])
    return dx, grads, (dtmat, demat, domat, damat), mine, theirs, below


def _whole_weights(gathered):
    out = {}
    for n, g in gathered.items():
        if n == "w_in":
            w_in = g.transpose(1, 0, 2).reshape(D_MODEL, IN_W)
            out[n] = jnp.concatenate([w_in[:, V_END:], w_in[:, :V_END]], axis=1)
        elif n == "w_glu":
            out[n] = g.transpose(1, 0, 2).reshape(SSM_W, 2 * SSM_W)
        elif n == "w_out":
            out[n] = g.reshape(ATT_W + SSM_W, D_MODEL)
        else:
            out[n] = g
    return out


def _grad_blocks(grads):
    out = {}
    for n, g in grads.items():
        if n == "w_in":
            g = jnp.concatenate([g[:, IN_W - V_END:], g[:, :IN_W - V_END]], axis=1)
            out[n] = g.reshape(D_MODEL, N_DEV, IN_W // N_DEV).transpose(1, 0, 2)
        elif n == "w_glu":
            out[n] = g.reshape(SSM_W, N_DEV, 2 * SSM_W // N_DEV).transpose(1, 0, 2)
        elif n == "w_out":
            out[n] = g.reshape(N_DEV, (ATT_W + SSM_W) // N_DEV, D_MODEL)
        else:
            out[n] = g
    return out


def _small_layout(like):
    layout, row = {}, 0
    for n in SMALL:
        size = int(np.prod(like[n].shape))
        nrow = -(-size // (8 * 128)) * 8
        layout[n] = (row, nrow, size)
        row += nrow
    return layout, -(-row // (8 * N_DEV)) * 8 * N_DEV


def _pack_small(vals, layout, rows):
    pieces, used = [], 0
    for n in SMALL:
        _, nrow, size = layout[n]
        flat = vals[n].reshape(-1).astype(F32)
        pieces.append(jnp.pad(flat, (0, nrow * 128 - size)).reshape(nrow, 128))
        used += nrow
    if rows > used:
        pieces.append(jnp.zeros((rows - used, 128), F32))
    return jnp.concatenate(pieces, axis=0)


def _unpack_small(packed, like, layout):
    out = {}
    for n in SMALL:
        row, nrow, size = layout[n]
        out[n] = packed[row:row + nrow].reshape(-1)[:size].reshape(like[n].shape)
    return out


def kernel(x, norm1, w_in, q_gain, k_gain, sink, lam_re, lam_im, log_dt, b_re, b_im, c_re, c_im, d_skip, w_glu, w_out, norm2, w_ff1, w_ff2, loss_target, m_norm1, m_w_in, m_q_gain, m_k_gain, m_sink, m_lam_re, m_lam_im, m_log_dt, m_b_re, m_b_im, m_c_re, m_c_im, m_d_skip, m_w_glu, m_w_out, m_norm2, m_w_ff1, m_w_ff2, v_norm1, v_w_in, v_q_gain, v_k_gain, v_sink, v_lam_re, v_lam_im, v_log_dt, v_b_re, v_b_im, v_c_re, v_c_im, v_d_skip, v_w_glu, v_w_out, v_norm2, v_w_ff1, v_w_ff2):
    w = dict(norm1=norm1, w_in=w_in, q_gain=q_gain, k_gain=k_gain, sink=sink, lam_re=lam_re, lam_im=lam_im,
             log_dt=log_dt, b_re=b_re, b_im=b_im, c_re=c_re, c_im=c_im, d_skip=d_skip, w_glu=w_glu, w_out=w_out,
             norm2=norm2, w_ff1=w_ff1, w_ff2=w_ff2)
    m = dict(norm1=m_norm1, w_in=m_w_in, q_gain=m_q_gain, k_gain=m_k_gain, sink=m_sink, lam_re=m_lam_re,
             lam_im=m_lam_im, log_dt=m_log_dt, b_re=m_b_re, b_im=m_b_im, c_re=m_c_re, c_im=m_c_im, d_skip=m_d_skip,
             w_glu=m_w_glu, w_out=m_w_out, norm2=m_norm2, w_ff1=m_w_ff1, w_ff2=m_w_ff2)
    v = dict(norm1=v_norm1, w_in=v_w_in, q_gain=v_q_gain, k_gain=v_k_gain, sink=v_sink, lam_re=v_lam_re,
             lam_im=v_lam_im, log_dt=v_log_dt, b_re=v_b_re, b_im=v_b_im, c_re=v_c_re, c_im=v_c_im, d_skip=v_d_skip,
             w_glu=v_w_glu, w_out=v_w_out, norm2=v_norm2, w_ff1=v_w_ff1, w_ff2=v_w_ff2)
    nl = w_in.shape[0]
    shards = [[w[n][l].astype(BF16) for n in BIG] for l in range(nl)]
    (tmat, emat, omat, amat), mats_vjp = jax.vjp(jax.vmap(ssm_mats), *[w[n] for n in S5])
    tb, eb, ob = tmat.astype(BF16), emat.astype(BF16), omat.astype(BF16)

    (g_in,) = exchange(shards[0][:1], False, "gather_w_in0")
    have = _whole_weights(dict(w_in=g_in))
    late = shards[0][1:]
    xs = x[0]
    saved, lp, lm = [], [], []
    for l in range(nl):
        p = {n: have[n] for n in have if n != "w_in"}
        for n in ("norm1", "q_gain", "k_gain", "d_skip", "norm2"):
            p[n] = w[n][l].reshape(1, -1)
        p["sink"] = sink[l]
        mats = (tb[l], eb[l], ob[l], amat[l])
        xs, sv, p, got = layer_fwd(xs, have["w_in"], p, mats, l, late, shards[l + 1][:3] if l + 1 < nl else None)
        if l + 1 < nl:
            have = _whole_weights(dict(zip(BIG[:3], got)))
            late = shards[l + 1][3:]
        saved.append(sv)
        lp.append(p)
        lm.append(mats)
    dx, loss_part = loss_grad(xs, loss_target[0], "loss")
    loss = lax.psum(loss_part[0, 0], ("x", "y", "c"))

    grads, dmats, parts = [None] * nl, [None] * nl, [dict() for _ in range(nl)]
    above = None
    for l in reversed(range(nl)):
        dx, grads[l], dmats[l], mine, theirs, above = layer_bwd(dx, lp[l], lm[l], saved[l], l, above)
        parts[l].update(mine)
        if l + 1 < nl:
            parts[l + 1].update(theirs)
    (parts[0]["w_in"],) = exchange([above], True, "exchange_g_in0")

    gs = {n: jnp.stack([grads[l][n].reshape(w[n].shape[1:]) for l in range(nl)])
          for n in ("norm1", "q_gain", "k_gain", "sink", "d_skip", "norm2")}
    ds5 = mats_vjp(tuple(jnp.stack([dmats[l][i] for l in range(nl)]) for i in range(4)))
    gs.update(zip(S5, ds5))
    layout, rows = _small_layout(w)
    (mine,) = exchange([_pack_small(gs, layout, rows).reshape(N_DEV, rows // N_DEV, 128)], True, "scatter_small_grads")
    (small_sum,) = exchange([sum_parts(mine, "sum_small_grads")], False, "gather_small_grads")

    out_g, out_d, out_m, out_v = {}, {}, {}, {}
    for n in BIG:
        c = w[n].shape[-1]
        r = int(np.prod(w[n].shape[:-1]))
        res = adamw_sum([parts[l][n] for l in range(nl)], w[n].reshape(r, c), m[n].reshape(r, c), v[n].reshape(r, c),
                        ADAM_ROWS[c], f"adamw_{n}")
        out_g[n], out_d[n], out_m[n], out_v[n] = (t.reshape(w[n].shape) for t in res)
    res = adamw_sum([small_sum.reshape(1, rows, 128)], _pack_small(w, layout, rows), _pack_small(m, layout, rows),
                    _pack_small(v, layout, rows), rows // N_DEV, "adamw_small")
    for dst, packed in zip((out_g, out_d, out_m, out_v), res):
        dst.update(_unpack_small(packed, w, layout))

    return (loss, dx[None], *[out_g[n] for n in WEIGHTS], *[out_d[n] for n in WEIGHTS],
            *[out_m[n] for n in WEIGHTS], *[out_v[n] for n in WEIGHTS])
```

```python
import numpy as np
import jax
import jax.numpy as jnp
from jax import lax
from jax.experimental import pallas as pl
from jax.experimental.pallas import tpu as pltpu

F32, BF16 = jnp.float32, jnp.bfloat16
EPS = 1e-6
D_MODEL = 1024
ATT_HEADS, KV_HEADS, GQA_GROUP, HEAD_DIM = 8, 2, 4, 64
ATT_W, KV_W, SSM_W, IN_W = 512, 128, 512, 1280
V_END = 768
U0, Q0, K0, V0 = 0, 512, 1024, 1152
BLK = 128
SCALE = 0.125
SSM_G, SSM_H, SSM_P = 32, 16, 64
CH = 16
GW = CH * SSM_H
SEGS = 8
N_DEV = 8
NEG = float(np.finfo(np.float32).min)
SLOPES = tuple(2.0 ** (-8.0 * (h + 1) / ATT_HEADS) for h in range(ATT_HEADS))
VMEM_LIMIT = 56 * 1024 * 1024
TM = 512
TM_FFN = 1024

ADAM_LR, ADAM_B1, ADAM_B2, ADAM_EPS, ADAM_WD, ADAM_STEP = 0.001, 0.9, 0.999, 1e-08, 0.01, 10

SMALL = ("norm1", "q_gain", "k_gain", "sink", "lam_re", "lam_im", "log_dt", "b_re", "b_im",
         "c_re", "c_im", "d_skip", "norm2")
S5 = ("lam_re", "lam_im", "log_dt", "b_re", "b_im", "c_re", "c_im")
BIG = ("w_in", "w_glu", "w_out", "w_ff1", "w_ff2")
WEIGHTS = ("norm1", "w_in", "q_gain", "k_gain", "sink", "lam_re", "lam_im", "log_dt", "b_re", "b_im",
           "c_re", "c_im", "d_skip", "w_glu", "w_out", "norm2", "w_ff1", "w_ff2")
ADAM_ROWS = {160: 256, 128: 512, 512: 128, 1024: 64}


def _dot(a, b):
    return jnp.dot(a, b, preferred_element_type=F32)


def _dot_nt(a, b):
    return lax.dot_general(a, b, (((1,), (1,)), ((), ())), preferred_element_type=F32)


def _dot_tn(a, b):
    return lax.dot_general(a, b, (((0,), (0,)), ((), ())), preferred_element_type=F32)


def _rms(x):
    return lax.rsqrt(jnp.mean(x * x, axis=-1, keepdims=True) + EPS)


def _rms_bwd(xhat, r, dxhat):
    return r * (dxhat - xhat * jnp.mean(dxhat * xhat, axis=-1, keepdims=True))


def _sigmoid(x):
    return 1.0 / (1.0 + jnp.exp(-x))


_GC = 0.7978845608028654
_GA = 0.044715


def _gelu(x):
    return 0.5 * x * (1.0 + jnp.tanh(_GC * (x + _GA * x * x * x)))


def _gelu_grad(x):
    t = jnp.tanh(_GC * (x + _GA * x * x * x))
    return 0.5 * (1.0 + t) + 0.5 * x * (1.0 - t * t) * _GC * (1.0 + 3.0 * _GA * x * x)


def _row(i):
    return (i, 0)


def _fixed(i):
    return (0, 0)


def _me_and_peers():
    x, y, c = lax.axis_index("x"), lax.axis_index("y"), lax.axis_index("c")
    me = 4 * x + 2 * y + c
    peers = []
    for k in range(1, N_DEV):
        px = jnp.bitwise_xor(x, (k >> 2) & 1)
        py = jnp.bitwise_xor(y, (k >> 1) & 1)
        pc = jnp.bitwise_xor(c, k & 1)
        peers.append(((px, py, pc), 4 * px + 2 * py + pc))
    return me, peers


def _xch_copies(ins, outs, send_sems, recv_sems, loc_sems, scatter):
    me, peers = _me_and_peers()
    local, sends, recvs = [], [], []
    for a in range(len(ins)):
        local.append(pltpu.make_async_copy(ins[a].at[me] if scatter else ins[a], outs[a].at[me], loc_sems.at[a]))
    for k, (dev, idx) in enumerate(peers):
        for a in range(len(ins)):
            src = ins[a].at[idx] if scatter else ins[a]
            for dst, group in ((outs[a].at[me], sends), (outs[a].at[idx], recvs)):
                group.append(pltpu.make_async_remote_copy(
                    src_ref=src, dst_ref=dst, send_sem=send_sems.at[a, k], recv_sem=recv_sems.at[a, k],
                    device_id=dev, device_id_type=pl.DeviceIdType.MESH))
    return local, sends, recvs


def _xch_start(copies):
    local, sends, _ = copies
    for cp in local + sends:
        cp.start()


def _xch_wait(copies):
    local, sends, recvs = copies
    for cp in recvs:
        cp.wait_recv()
    for cp in sends:
        cp.wait_send()
    for cp in local:
        cp.wait()


def _xch_shapes(arrays, scatter):
    return [jax.ShapeDtypeStruct(a.shape if scatter else (N_DEV,) + a.shape, a.dtype) for a in arrays]


def _xch_sems(n):
    return [pltpu.SemaphoreType.DMA((n, N_DEV - 1)), pltpu.SemaphoreType.DMA((n, N_DEV - 1)),
            pltpu.SemaphoreType.DMA((n,))]


_ANY = pl.BlockSpec(memory_space=pl.ANY)


def exchange(arrays, scatter, name):
    n = len(arrays)

    def body(*refs):
        copies = _xch_copies(refs[:n], refs[n:2 * n], *refs[2 * n:], scatter)
        _xch_start(copies)
        _xch_wait(copies)

    return pl.pallas_call(
        body, name=name, in_specs=[_ANY] * n, out_specs=[_ANY] * n, out_shape=_xch_shapes(arrays, scatter),
        scratch_shapes=_xch_sems(n), compiler_params=pltpu.CompilerParams(has_side_effects=True),
    )(*arrays)


def _call(body, name, grid, in_specs, out_specs, out_shape, sem, inputs, scratch=(), xch=None):
    params = pltpu.CompilerParams(dimension_semantics=sem, vmem_limit_bytes=VMEM_LIMIT)
    if xch is None:
        out = pl.pallas_call(body, name=name, grid=grid, in_specs=in_specs, out_specs=out_specs, out_shape=out_shape,
                             scratch_shapes=list(scratch), compiler_params=params)(*inputs)
        return list(out), None
    arrays, scatter = xch
    n, nin, nout, nsc = len(arrays), len(in_specs), len(out_specs), len(scratch)

    def wrapped(*refs):
        ins, refs = refs[:nin], refs[nin:]
        xin, refs = refs[:n], refs[n:]
        outs, refs = refs[:nout], refs[nout:]
        xout, refs = refs[:n], refs[n:]
        sc, sems = refs[:nsc], refs[nsc:]
        first = last = None
        for ax, size in enumerate(grid):
            f, e = pl.program_id(ax) == 0, pl.program_id(ax) == size - 1
            first = f if first is None else jnp.logical_and(first, f)
            last = e if last is None else jnp.logical_and(last, e)

        @pl.when(first)
        def _():
            _xch_start(_xch_copies(xin, xout, *sems, scatter))

        body(*ins, *outs, *sc)

        @pl.when(last)
        def _():
            _xch_wait(_xch_copies(xin, xout, *sems, scatter))

    out = pl.pallas_call(
        wrapped, name=name, grid=grid, in_specs=list(in_specs) + [_ANY] * n, out_specs=list(out_specs) + [_ANY] * n,
        out_shape=list(out_shape) + _xch_shapes(arrays, scatter), scratch_shapes=list(scratch) + _xch_sems(n),
        compiler_params=params)(*inputs, *arrays)
    return list(out[:nout]), list(out[nout:])


def norm_matmul(x, gain, w, name):
    s, d = x.shape
    n = w.shape[1]

    def body(x_ref, g_ref, w_ref, z_ref):
        xv = x_ref[...]
        h = (xv * _rms(xv) * g_ref[...]).astype(BF16)
        z_ref[...] = _dot(h, w_ref[...])

    (z,), _ = _call(body, name, (s // TM,),
                    [pl.BlockSpec((TM, d), _row), pl.BlockSpec((1, d), _fixed), pl.BlockSpec((d, n), _fixed)],
                    [pl.BlockSpec((TM, n), _row)], [jax.ShapeDtypeStruct((s, n), F32)], ("parallel",), (x, gain, w))
    return z


def in_bwd(x, gain, w, du_a, du_b, dq, dk, dv, dres, name, xch=None):
    s, d = x.shape
    n = w.shape[1]

    def body(x_ref, g_ref, w_ref, dua_ref, dub_ref, dq_ref, dk_ref, dv_ref, dres_ref, dx_ref, dw_ref, dg_ref):
        @pl.when(pl.program_id(0) == 0)
        def _():
            dw_ref[...] = jnp.zeros_like(dw_ref)
            dg_ref[...] = jnp.zeros_like(dg_ref)

        xv = x_ref[...]
        r = _rms(xv)
        xhat = xv * r
        g = g_ref[...]
        h = (xhat * g).astype(BF16)
        dz = jnp.concatenate([(dua_ref[...] + _lanes4(dub_ref)).astype(BF16), dq_ref[...].astype(BF16),
                              dk_ref[...].astype(BF16), dv_ref[...].astype(BF16)], axis=1)
        dh = _dot_nt(dz, w_ref[...])
        dw_ref[...] += _dot_tn(h, dz)
        dg_ref[...] += jnp.sum(dh * xhat, axis=0, keepdims=True)
        dx_ref[...] = dres_ref[...] + _rms_bwd(xhat, r, dh * g)

    return _call(
        body, name, (s // TM,),
        [pl.BlockSpec((TM, d), _row), pl.BlockSpec((1, d), _fixed), pl.BlockSpec((d, n), _fixed),
         pl.BlockSpec((TM, SSM_W), _row), pl.BlockSpec((4, TM, 128), lambda i: (0, i, 0)), pl.BlockSpec((TM, ATT_W), _row),
         pl.BlockSpec((TM, KV_W), _row), pl.BlockSpec((TM, KV_W), _row), pl.BlockSpec((TM, d), _row)],
        [pl.BlockSpec((TM, d), _row), pl.BlockSpec((d, n), _fixed), pl.BlockSpec((1, d), _fixed)],
        [jax.ShapeDtypeStruct((s, d), F32), jax.ShapeDtypeStruct((d, n), F32), jax.ShapeDtypeStruct((1, d), F32)],
        ("arbitrary",), (x, gain, w, du_a, du_b, dq, dk, dv, dres), xch=xch)


def _band_specs(nb):
    def w0(i):
        return jnp.clip(i - 1, 0, nb - 3)

    specs = [pl.BlockSpec((None, KV_HEADS, GQA_GROUP * BLK, 3 * BLK), lambda i: (i - w0(i), 0, 0, 0)),
             pl.BlockSpec((BLK, ATT_W), lambda i: (i, Q0 // ATT_W))]
    for col in (K0 // KV_W, V0 // KV_W):
        specs += [pl.BlockSpec((BLK, KV_W), lambda i, c=col, o=o: (w0(i) + o, c)) for o in range(3)]
    return specs


def band_bias():
    off = jnp.arange(3).reshape(3, 1, 1, 1)
    row = jnp.arange(GQA_GROUP * BLK).reshape(1, 1, -1, 1)
    dist = jnp.abs(off * BLK + row % BLK - jnp.arange(3 * BLK).reshape(1, 1, 1, -1))
    slope = jnp.asarray(SLOPES, F32).reshape(1, KV_HEADS, GQA_GROUP, 1)
    slope = jnp.repeat(slope, BLK, axis=2)
    return jnp.where(dist <= BLK, -slope * dist.astype(F32), NEG)


def _stack_heads(x, j):
    return jnp.concatenate([x[:, h * HEAD_DIM:(h + 1) * HEAD_DIM] for h in range(j * GQA_GROUP, (j + 1) * GQA_GROUP)],
                           axis=0)


def _per_head(vals):
    head = lax.broadcasted_iota(jnp.int32, (GQA_GROUP * BLK, 1), 0) // BLK
    out = jnp.full((GQA_GROUP * BLK, 1), vals[GQA_GROUP - 1], F32)
    for g in range(GQA_GROUP - 2, -1, -1):
        out = jnp.where(head == g, vals[g], out)
    return out


def attn_fwd(z, bias, qg, kg, sink, name, xch=None):
    s = z.shape[0]
    nb = s // BLK

    def body(sink_ref, b_ref, q_ref, k0, k1, k2, v0, v1, v2, qg_ref, kg_ref, o_ref, lse_ref):
        k3 = jnp.concatenate([k0[...], k1[...], k2[...]], axis=0)
        v3 = jnp.concatenate([v0[...], v1[...], v2[...]], axis=0).astype(BF16)
        q = q_ref[...]
        for j in range(KV_HEADS):
            heads = range(j * GQA_GROUP, (j + 1) * GQA_GROUP)
            kj = k3[:, j * HEAD_DIM:(j + 1) * HEAD_DIM]
            knj = (kj * _rms(kj) * kg_ref[...]).astype(BF16)
            vj = v3[:, j * HEAD_DIM:(j + 1) * HEAD_DIM]
            q4 = _stack_heads(q, j)
            qs = (q4 * _rms(q4) * (qg_ref[...] * SCALE)).astype(BF16)
            sc = _dot_nt(qs, knj) + b_ref[j]
            sk = _per_head([sink_ref[h] for h in heads])
            m = jnp.maximum(jnp.max(sc, axis=-1, keepdims=True), sk)
            p = jnp.exp(sc - m)
            den = jnp.sum(p, axis=-1, keepdims=True) + jnp.exp(sk - m)
            o4 = _dot(p.astype(BF16), vj) * (1.0 / den)
            lse4 = m + jnp.log(den)
            for g, h in enumerate(heads):
                o_ref[:, h * HEAD_DIM:(h + 1) * HEAD_DIM] = o4[g * BLK:(g + 1) * BLK]
                lse_ref[:, h:h + 1] = lse4[g * BLK:(g + 1) * BLK]

    return _call(
        body, name, (nb,),
        [pl.BlockSpec(memory_space=pltpu.SMEM)] + _band_specs(nb)
        + [pl.BlockSpec((1, HEAD_DIM), _fixed), pl.BlockSpec((1, HEAD_DIM), _fixed)],
        [pl.BlockSpec((BLK, ATT_W), _row), pl.BlockSpec((BLK, ATT_HEADS), _row)],
        [jax.ShapeDtypeStruct((s, ATT_W), F32), jax.ShapeDtypeStruct((s, ATT_HEADS), F32)],
        ("arbitrary",), (sink, bias, z, z, z, z, z, z, z, qg, kg), xch=xch)


def attn_bwd(z, bias, att, datt, lse, qg, kg, sink, name, xch=None):
    s = z.shape[0]
    nb = s // BLK

    def body(sink_ref, b_ref, q_ref, k0, k1, k2, v0, v1, v2, o_ref, do_ref, lse_ref, qg_ref, kg_ref,
             dq_ref, dk_ref, dv_ref, dqg_ref, dkg_ref, dsk_ref):
        i = pl.program_id(0)

        @pl.when(i == 0)
        def _():
            dk_ref[...] = jnp.zeros_like(dk_ref)
            dv_ref[...] = jnp.zeros_like(dv_ref)
            dqg_ref[...] = jnp.zeros_like(dqg_ref)
            dkg_ref[...] = jnp.zeros_like(dkg_ref)
            dsk_ref[...] = jnp.zeros_like(dsk_ref)

        w0 = jnp.clip(i - 1, 0, nb - 3)
        k3 = jnp.concatenate([k0[...], k1[...], k2[...]], axis=0)
        v3 = jnp.concatenate([v0[...], v1[...], v2[...]], axis=0).astype(BF16)
        q = q_ref[...]
        o = o_ref[...]
        do = do_ref[...]
        lse = lse_ref[...]
        qgv = qg_ref[...]
        kgv = kg_ref[...]
        rows = pl.ds(pl.multiple_of(w0 * BLK, BLK), 3 * BLK)
        dqg = jnp.zeros((1, HEAD_DIM), F32)
        dkg = jnp.zeros((1, HEAD_DIM), F32)
        for j in range(KV_HEADS):
            heads = range(j * GQA_GROUP, (j + 1) * GQA_GROUP)
            cols = slice(j * HEAD_DIM, (j + 1) * HEAD_DIM)
            kj = k3[:, cols]
            rk = _rms(kj)
            khat = kj * rk
            knj = (khat * kgv).astype(BF16)
            vj = v3[:, cols]
            q4 = _stack_heads(q, j)
            rq = _rms(q4)
            qhat = q4 * rq
            qs = (qhat * (qgv * SCALE)).astype(BF16)
            sc = _dot_nt(qs, knj) + b_ref[j]
            lse4 = jnp.concatenate([lse[:, h:h + 1] for h in heads], axis=0)
            p = jnp.exp(sc - lse4)
            do4 = _stack_heads(do, j)
            delta = jnp.sum(do4 * _stack_heads(o, j), axis=-1, keepdims=True)
            dob = do4.astype(BF16)
            ds = p * (_dot_nt(dob, vj) - delta)
            sunk = jnp.exp(_per_head([sink_ref[h] for h in heads]) - lse4) * delta
            dsb = ds.astype(BF16)
            dvj = _dot_tn(p.astype(BF16), dob)
            dqn = _dot(dsb, knj) * SCALE
            dkn = _dot_tn(dsb, qs)
            dqg = dqg + jnp.sum(dqn * qhat, axis=0, keepdims=True)
            dq4 = _rms_bwd(qhat, rq, dqn * qgv)
            for g, h in enumerate(heads):
                dq_ref[:, h * HEAD_DIM:(h + 1) * HEAD_DIM] = dq4[g * BLK:(g + 1) * BLK]
                dsk_ref[:, h:h + 1] += -jnp.sum(sunk[g * BLK:(g + 1) * BLK], axis=0, keepdims=True)
            dkg = dkg + jnp.sum(dkn * khat, axis=0, keepdims=True)
            dk_ref[rows, cols] += _rms_bwd(khat, rk, dkn * kgv)
            dv_ref[rows, cols] += dvj
        dqg_ref[...] += dqg
        dkg_ref[...] += dkg

    return _call(
        body, name, (nb,),
        [pl.BlockSpec(memory_space=pltpu.SMEM)] + _band_specs(nb)
        + [pl.BlockSpec((BLK, ATT_W), _row), pl.BlockSpec((BLK, ATT_W), _row), pl.BlockSpec((BLK, ATT_HEADS), _row),
           pl.BlockSpec((1, HEAD_DIM), _fixed), pl.BlockSpec((1, HEAD_DIM), _fixed)],
        [pl.BlockSpec((BLK, ATT_W), _row), pl.BlockSpec((s, KV_W), _fixed), pl.BlockSpec((s, KV_W), _fixed),
         pl.BlockSpec((1, HEAD_DIM), _fixed), pl.BlockSpec((1, HEAD_DIM), _fixed), pl.BlockSpec((1, ATT_HEADS), _fixed)],
        [jax.ShapeDtypeStruct((s, ATT_W), F32), jax.ShapeDtypeStruct((s, KV_W), F32), jax.ShapeDtypeStruct((s, KV_W), F32),
         jax.ShapeDtypeStruct((1, HEAD_DIM), F32), jax.ShapeDtypeStruct((1, HEAD_DIM), F32),
         jax.ShapeDtypeStruct((1, ATT_HEADS), F32)],
        ("arbitrary",), (sink, bias, z, z, z, z, z, z, z, att, datt, lse, qg, kg), xch=xch)


def _group_steps(nc):
    nstep = nc // SEGS
    return nstep, min(32, nstep)


def _lanes4(ref):
    return jnp.concatenate([ref[q] for q in range(4)], axis=1)


def _pair_split(x4):
    return [jnp.concatenate([x4[:, q * 128 + r * SSM_P:q * 128 + (r + 1) * SSM_P] for q in range(4)], axis=1)
            for r in range(2)]


def _pair_merge(a0, a1):
    return [jnp.concatenate([a[:, q * SSM_P:(q + 1) * SSM_P] for a in (a0, a1)], axis=1) for q in range(4)]


def to_groups(src, col, name):
    s, w = src.shape
    nc = s // CH
    nstep, sb = _group_steps(nc)

    def body(u0, u1, u2, u3, o_ref):
        slot = lax.broadcasted_iota(jnp.int32, (sb, 128), 1) // SSM_H
        for seg in range(SEGS):
            for vc, u_ref in enumerate((u0, u1, u2, u3)):
                for sh in range(2):
                    accs = [None] * 8
                    for sl in range(8):
                        piece = u_ref[seg, pl.ds(sh * 8 + sl, sb, stride=CH), :]
                        for gl in range(8):
                            shift = ((sl - gl) * SSM_H) % 128
                            r = pltpu.roll(piece, shift, 1) if shift else piece
                            accs[gl] = r if sl == 0 else jnp.where(slot == sl, r, accs[gl])
                    for gl in range(8):
                        o_ref[(vc * 8 + gl) * 2 + sh, pl.ds(seg, sb, stride=SEGS), :] = accs[gl]

    src3 = src.reshape(SEGS, s // SEGS, w)
    (out,), _ = _call(body, name, (nstep // sb,),
                      [pl.BlockSpec((SEGS, sb * CH, 128), lambda i, c=col + vc: (0, i, c)) for vc in range(4)],
                      [pl.BlockSpec((2 * SSM_G, sb * SEGS, 128), lambda i: (0, i, 0))],
                      [jax.ShapeDtypeStruct((2 * SSM_G, nc, 128), F32)], ("parallel",), (src3,) * 4)
    return out


def from_groups(yc, name):
    nc = yc.shape[1]
    s = nc * CH
    nstep, sb = _group_steps(nc)

    def body(y_ref, o_ref):
        slot = lax.broadcasted_iota(jnp.int32, (sb, 128), 1) // SSM_H
        for seg in range(SEGS):
            for vc in range(4):
                for sh in range(2):
                    rows = [None] * 8
                    for gl in range(8):
                        piece = y_ref[(vc * 8 + gl) * 2 + sh, pl.ds(seg, sb, stride=SEGS), :]
                        for sl in range(8):
                            shift = ((gl - sl) * SSM_H) % 128
                            r = pltpu.roll(piece, shift, 1) if shift else piece
                            rows[sl] = r if gl == 0 else jnp.where(slot == gl, r, rows[sl])
                    for sl in range(8):
                        o_ref[vc, seg, pl.ds(sh * 8 + sl, sb, stride=CH), :] = rows[sl]

    (out,), _ = _call(body, name, (nstep // sb,),
                      [pl.BlockSpec((2 * SSM_G, sb * SEGS, 128), lambda i: (0, i, 0))],
                      [pl.BlockSpec((4, SEGS, sb * CH, 128), lambda i: (0, 0, i, 0))],
                      [jax.ShapeDtypeStruct((4, SEGS, s // SEGS, 128), F32)], ("parallel",), (yc,))
    return out.reshape(4, s, 128)


def _pair3(i):
    return (i, 0, 0)


def _op_spec(l, rows, cols):
    return pl.BlockSpec((None, 2, rows, cols), lambda i: (l, i, 0, 0))


def _state_blk(i):
    return (0, 0, i)


def ssm_in(ug, e, l, name):
    nc = ug.shape[1]

    def body(u_ref, e_ref, s_ref):
        u = _lanes4(u_ref).astype(BF16)
        for q, blk in enumerate(_pair_merge(_dot(u[:, :GW], e_ref[0].astype(BF16)), _dot(u[:, GW:], e_ref[1].astype(BF16)))):
            s_ref[q] = blk

    (out,), _ = _call(body, name, (SSM_G // 2,),
                      [pl.BlockSpec((4, nc, 128), _pair3), _op_spec(l, GW, 4 * SSM_P)],
                      [pl.BlockSpec((4, nc, 128), _state_blk)],
                      [jax.ShapeDtypeStruct((4, nc, SSM_G * SSM_P), F32)], ("parallel",), (ug, e))
    return out


def chunk_scan(s4, a4, flip, name, xp4=None):
    _, nc, gp = s4.shape
    nstep = nc // SEGS
    assert nstep & (nstep - 1) == 0
    ct = 512
    with_da = xp4 is not None

    def body(*refs):
        if with_da:
            s_ref, a_ref, xp_ref, o_ref, da_ref = refs
        else:
            s_ref, a_ref, o_ref = refs
        rows = lax.broadcasted_iota(jnp.int32, (SEGS, ct), 0)
        zero = jnp.zeros((SEGS, ct), F32)
        for pair in range(2):
            asc = (pair == 0) != flip
            ir, ii = 2 * pair, 2 * pair + 1
            ar1 = a_ref[ir]
            ai1 = a_ref[ii]
            ar = jnp.broadcast_to(ar1, (SEGS, ct))
            ai = jnp.broadcast_to(ai1, (SEGS, ct))

            def tile(t):
                tt = t if asc else nstep - 1 - t
                return pl.ds(pl.multiple_of(tt * SEGS, SEGS), SEGS)

            def local(t, c):
                xr, xi = c
                sl = tile(t)
                return (ar * xr - ai * xi + s_ref[ir, sl, :], ar * xi + ai * xr + s_ref[ii, sl, :])

            er, ei = lax.fori_loop(0, nstep, local, (zero, zero))
            pr, pi = ar1, ai1
            for _ in range(nstep.bit_length() - 1):
                pr, pi = pr * pr - pi * pi, 2.0 * pr * pi
            cr = jnp.zeros((1, ct), F32)
            ci = jnp.zeros((1, ct), F32)
            xin_r, xin_i = zero, zero
            for k in range(SEGS):
                sg = k if asc else SEGS - 1 - k
                here = rows == sg
                xin_r = jnp.where(here, cr, xin_r)
                xin_i = jnp.where(here, ci, xin_i)
                lr = jnp.sum(jnp.where(here, er, 0.0), axis=0, keepdims=True)
                li = jnp.sum(jnp.where(here, ei, 0.0), axis=0, keepdims=True)
                cr, ci = pr * cr - pi * ci + lr, pr * ci + pi * cr + li

            def final(t, c):
                xr, xi, acr, aci = c
                sl = tile(t)
                o_ref[ir, sl, :] = xr
                o_ref[ii, sl, :] = xi
                if with_da:
                    br = xp_ref[ir, sl, :]
                    bi = xp_ref[ii, sl, :]
                    acr = acr + br * xr + bi * xi
                    aci = aci + br * xi - bi * xr
                return (ar * xr - ai * xi + s_ref[ir, sl, :], ar * xi + ai * xr + s_ref[ii, sl, :], acr, aci)

            _, _, acr, aci = lax.fori_loop(0, nstep, final, (xin_r, xin_i, zero, zero))
            if with_da:
                da_ref[ir] = jnp.sum(acr, axis=0, keepdims=True)
                da_ref[ii] = jnp.sum(aci, axis=0, keepdims=True)

    blk = pl.BlockSpec((4, nc, ct), _state_blk)
    ablk = pl.BlockSpec((4, 1, ct), _state_blk)
    sds = jax.ShapeDtypeStruct((4, nc, gp), F32)
    if with_da:
        out, _ = _call(body, name, (gp // ct,), [blk, ablk, blk], [blk, ablk],
                       [sds, jax.ShapeDtypeStruct((4, 1, gp), F32)], ("parallel",), (s4, a4, xp4))
        return out
    (out,), _ = _call(body, name, (gp // ct,), [blk, ablk], [blk], [sds], ("parallel",), (s4, a4))
    return out


def _state_cat(ref):
    return _lanes4(ref).astype(BF16)


def ssm_out(ug, t, xp4, o, l, name):
    nc = ug.shape[1]

    def body(u_ref, t_ref, xp_ref, o_ref, y_ref):
        xs = _pair_split(_lanes4(xp_ref))
        u = _lanes4(u_ref).astype(BF16)
        for r in range(2):
            y = _dot(u[:, r * GW:(r + 1) * GW], t_ref[r].astype(BF16)) + _dot(xs[r].astype(BF16), o_ref[r].astype(BF16))
            y_ref[2 * r] = y[:, :128]
            y_ref[2 * r + 1] = y[:, 128:]

    (out,), _ = _call(body, name, (SSM_G // 2,),
                      [pl.BlockSpec((4, nc, 128), _pair3), _op_spec(l, GW, GW),
                       pl.BlockSpec((4, nc, 128), _state_blk), _op_spec(l, 4 * SSM_P, GW)],
                      [pl.BlockSpec((4, nc, 128), _pair3)],
                      [jax.ShapeDtypeStruct((2 * SSM_G, nc, 128), F32)], ("parallel",), (ug, t, xp4, o))
    return out


def ssm_out_bwd(dyg, xp4, o, l, name):
    nc = dyg.shape[1]

    def body(dy_ref, xp_ref, o_ref, do_ref, dxp_ref):
        xs = _pair_split(_lanes4(xp_ref))
        dy = _lanes4(dy_ref).astype(BF16)
        dxs = []
        for r in range(2):
            dyr = dy[:, r * GW:(r + 1) * GW]
            do_ref[r] = _dot_tn(xs[r].astype(BF16), dyr)
            dxs.append(_dot_nt(dyr, o_ref[r].astype(BF16)))
        for q, blk in enumerate(_pair_merge(*dxs)):
            dxp_ref[q] = blk

    out, _ = _call(body, name, (SSM_G // 2,),
                   [pl.BlockSpec((4, nc, 128), _pair3), pl.BlockSpec((4, nc, 128), _state_blk),
                    _op_spec(l, 4 * SSM_P, GW)],
                   [pl.BlockSpec((2, 4 * SSM_P, GW), _pair3), pl.BlockSpec((4, nc, 128), _state_blk)],
                   [jax.ShapeDtypeStruct((SSM_G, 4 * SSM_P, GW), F32), jax.ShapeDtypeStruct((4, nc, SSM_G * SSM_P), F32)],
                   ("parallel",), (dyg, xp4, o))
    return out


def ssm_in_bwd(ug, dyg, ds4, t, e, l, name):
    nc = ug.shape[1]

    def body(u_ref, dy_ref, ds_ref, t_ref, e_ref, dt_ref, de_ref, du_ref):
        dss = _pair_split(_lanes4(ds_ref))
        u = _lanes4(u_ref).astype(BF16)
        dy = _lanes4(dy_ref).astype(BF16)
        for r in range(2):
            cols = slice(r * GW, (r + 1) * GW)
            ds = dss[r].astype(BF16)
            dt_ref[r] = _dot_tn(u[:, cols], dy[:, cols])
            de_ref[r] = _dot_tn(u[:, cols], ds)
            du = _dot_nt(dy[:, cols], t_ref[r].astype(BF16)) + _dot_nt(ds, e_ref[r].astype(BF16))
            du_ref[2 * r] = du[:, :128]
            du_ref[2 * r + 1] = du[:, 128:]

    out, _ = _call(body, name, (SSM_G // 2,),
                   [pl.BlockSpec((4, nc, 128), _pair3), pl.BlockSpec((4, nc, 128), _pair3),
                    pl.BlockSpec((4, nc, 128), _state_blk), _op_spec(l, GW, GW), _op_spec(l, GW, 4 * SSM_P)],
                   [pl.BlockSpec((2, GW, GW), _pair3), pl.BlockSpec((2, GW, 4 * SSM_P), _pair3),
                    pl.BlockSpec((4, nc, 128), _pair3)],
                   [jax.ShapeDtypeStruct((SSM_G, GW, GW), F32), jax.ShapeDtypeStruct((SSM_G, GW, 4 * SSM_P), F32),
                    jax.ShapeDtypeStruct((2 * SSM_G, nc, 128), F32)], ("parallel",), (ug, dyg, ds4, t, e))
    return out


def ssm_post_fwd(yc, z, dskip, wglu, name):
    s = yc.shape[1]

    def body(y_ref, u_ref, d_ref, w_ref, o_ref, yp_ref, g_ref):
        yp = _lanes4(y_ref) + d_ref[...] * u_ref[...]
        yp_ref[...] = yp
        gv = _dot(_gelu(yp).astype(BF16), w_ref[...])
        g_ref[...] = gv
        o_ref[...] = gv[:, :SSM_W] * _sigmoid(gv[:, SSM_W:])

    out, _ = _call(body, name, (s // TM,),
                   [pl.BlockSpec((4, TM, 128), lambda i: (0, i, 0)), pl.BlockSpec((TM, SSM_W), lambda i: (i, U0 // SSM_W)),
                    pl.BlockSpec((1, SSM_W), _fixed), pl.BlockSpec((SSM_W, 2 * SSM_W), _fixed)],
                   [pl.BlockSpec((TM, SSM_W), _row), pl.BlockSpec((TM, SSM_W), _row), pl.BlockSpec((TM, 2 * SSM_W), _row)],
                   [jax.ShapeDtypeStruct((s, SSM_W), F32), jax.ShapeDtypeStruct((s, SSM_W), F32),
                    jax.ShapeDtypeStruct((s, 2 * SSM_W), F32)], ("parallel",), (yc, z, dskip, wglu))
    return out


def ssm_post_bwd(dssm, gpre, ypre, z, dskip, wglu, name):
    s = dssm.shape[0]

    def body(do_ref, g_ref, yp_ref, u_ref, d_ref, w_ref, dy_ref, du_ref, dw_ref, dd_ref):
        @pl.when(pl.program_id(0) == 0)
        def _():
            dw_ref[...] = jnp.zeros_like(dw_ref)
            dd_ref[...] = jnp.zeros_like(dd_ref)

        gv = g_ref[...]
        val = gv[:, :SSM_W]
        sg = _sigmoid(gv[:, SSM_W:])
        do = do_ref[...]
        dg = jnp.concatenate([do * sg, do * val * sg * (1.0 - sg)], axis=1).astype(BF16)
        yp = yp_ref[...]
        dgl = _dot_nt(dg, w_ref[...])
        dw_ref[...] += _dot_tn(_gelu(yp).astype(BF16), dg)
        dyp = dgl * _gelu_grad(yp)
        dy_ref[...] = dyp
        du_ref[...] = dyp * d_ref[...]
        dd_ref[...] += jnp.sum(dyp * u_ref[...], axis=0, keepdims=True)

    out, _ = _call(body, name, (s // TM,),
                   [pl.BlockSpec((TM, SSM_W), _row), pl.BlockSpec((TM, 2 * SSM_W), _row), pl.BlockSpec((TM, SSM_W), _row),
                    pl.BlockSpec((TM, SSM_W), lambda i: (i, U0 // SSM_W)), pl.BlockSpec((1, SSM_W), _fixed),
                    pl.BlockSpec((SSM_W, 2 * SSM_W), _fixed)],
                   [pl.BlockSpec((TM, SSM_W), _row), pl.BlockSpec((TM, SSM_W), _row),
                    pl.BlockSpec((SSM_W, 2 * SSM_W), _fixed), pl.BlockSpec((1, SSM_W), _fixed)],
                   [jax.ShapeDtypeStruct((s, SSM_W), F32), jax.ShapeDtypeStruct((s, SSM_W), F32),
                    jax.ShapeDtypeStruct((SSM_W, 2 * SSM_W), F32), jax.ShapeDtypeStruct((1, SSM_W), F32)],
                   ("arbitrary",), (dssm, gpre, ypre, z, dskip, wglu))
    return out


def _toeplitz_select():
    row = lax.broadcasted_iota(jnp.int32, (GW, CH * GW), 0)
    col = lax.broadcasted_iota(jnp.int32, (GW, CH * GW), 1)
    j, h2 = row // SSM_H, row % SSM_H
    s, t, h = col // GW, (col % GW) // SSM_H, col % SSM_H
    same = h2 == h
    return jnp.concatenate([same & (t - s == j), same & (s - t == j)], axis=0).astype(F32)


def ssm_mats(lam_re, lam_im, log_dt, b_re, b_im, c_re, c_im):
    g, p, hh = SSM_G, SSM_P, SSM_H
    hp = lax.Precision.HIGHEST
    jj = jnp.arange(CH + 1, dtype=F32)
    dt = jnp.exp(log_dt)[..., None]
    mag = jnp.exp((lam_re * dt)[..., None] * jj)
    ang = (lam_im * dt)[..., None] * jj
    pr, pi = mag * jnp.cos(ang), mag * jnp.sin(ang)
    abr, abi = pr[..., 1], pi[..., 1]
    den = lam_re * lam_re + lam_im * lam_im
    zr = ((abr - 1.0) * lam_re + abi * lam_im) / den
    zi = (abi * lam_re - (abr - 1.0) * lam_im) / den
    bbr = zr[..., None] * b_re[None] - zi[..., None] * b_im[None]
    bbi = zr[..., None] * b_im[None] + zi[..., None] * b_re[None]
    crt, cit = c_re.transpose(0, 1, 3, 2), c_im.transpose(0, 1, 3, 2)
    car = pr[..., None] * crt[..., None, :] - pi[..., None] * cit[..., None, :]
    cai = pr[..., None] * cit[..., None, :] + pi[..., None] * crt[..., None, :]
    lhs = jnp.concatenate([bbr, -bbi], axis=2).transpose(0, 1, 3, 2)
    rhs = jnp.concatenate([car[..., :CH, :], cai[..., :CH, :]], axis=2).reshape(2, g, 2 * p, GW)
    kt = jnp.einsum("dgkp,dgpn->dgkn", lhs, rhs, precision=hp)
    kcat = jnp.concatenate([kt[0], kt[1]], axis=-1).reshape(g * hh, 2 * GW)
    tmat = jnp.dot(kcat, _toeplitz_select(), precision=hp)
    tmat = tmat.reshape(g, hh, CH, GW).transpose(0, 2, 1, 3).reshape(g, GW, GW)

    def e_part(d, pw_r, pw_i):
        pw_r, pw_i = pw_r.transpose(0, 2, 1)[:, :, None, :], pw_i.transpose(0, 2, 1)[:, :, None, :]
        br, bi = bbr[d].transpose(0, 2, 1)[:, None], bbi[d].transpose(0, 2, 1)[:, None]
        return [pw_r * br - pw_i * bi, pw_r * bi + pw_i * br]

    eparts = (e_part(0, pr[0, ..., :CH][..., ::-1], pi[0, ..., :CH][..., ::-1])
              + e_part(1, pr[1, ..., :CH], pi[1, ..., :CH]))
    emat = jnp.stack(eparts, axis=3).reshape(g, GW, 4 * p)

    oparts = [car[0, ..., 1:, :], -cai[0, ..., 1:, :], car[1, ..., 1:, :][..., ::-1, :], -cai[1, ..., 1:, :][..., ::-1, :]]
    omat = jnp.stack([v.reshape(g, p, GW) for v in oparts], axis=1).reshape(g, 4 * p, GW)
    amat = jnp.stack([pr[0, ..., CH], pi[0, ..., CH], pr[1, ..., CH], pi[1, ..., CH]], axis=0).reshape(4, 1, g * p)
    return tmat, emat, omat, amat


def outproj_fwd(x, att, ssm, wo, name):
    s, d = x.shape

    def body(x_ref, a_ref, s_ref, w_ref, o_ref):
        o_ref[...] = (x_ref[...] + _dot(a_ref[...].astype(BF16), w_ref[0:ATT_W, :])
                      + _dot(s_ref[...].astype(BF16), w_ref[ATT_W:, :]))

    (out,), _ = _call(body, name, (s // TM,),
                      [pl.BlockSpec((TM, d), _row), pl.BlockSpec((TM, ATT_W), _row), pl.BlockSpec((TM, SSM_W), _row),
                       pl.BlockSpec((ATT_W + SSM_W, d), _fixed)],
                      [pl.BlockSpec((TM, d), _row)], [jax.ShapeDtypeStruct((s, d), F32)], ("parallel",),
                      (x, att, ssm, wo))
    return out


def outproj_bwd(dx1, att, ssm, wo, name):
    s, d = dx1.shape

    def body(dx_ref, a_ref, s_ref, w_ref, da_ref, ds_ref, dw_ref):
        @pl.when(pl.program_id(0) == 0)
        def _():
            dw_ref[...] = jnp.zeros_like(dw_ref)

        dxb = dx_ref[...].astype(BF16)
        da_ref[...] = _dot_nt(dxb, w_ref[0:ATT_W, :])
        ds_ref[...] = _dot_nt(dxb, w_ref[ATT_W:, :])
        dw_ref[0:ATT_W, :] += _dot_tn(a_ref[...].astype(BF16), dxb)
        dw_ref[ATT_W:, :] += _dot_tn(s_ref[...].astype(BF16), dxb)

    out, _ = _call(body, name, (s // TM,),
                   [pl.BlockSpec((TM, d), _row), pl.BlockSpec((TM, ATT_W), _row), pl.BlockSpec((TM, SSM_W), _row),
                    pl.BlockSpec((ATT_W + SSM_W, d), _fixed)],
                   [pl.BlockSpec((TM, ATT_W), _row), pl.BlockSpec((TM, SSM_W), _row),
                    pl.BlockSpec((ATT_W + SSM_W, d), _fixed)],
                   [jax.ShapeDtypeStruct((s, ATT_W), F32), jax.ShapeDtypeStruct((s, SSM_W), F32),
                    jax.ShapeDtypeStruct((ATT_W + SSM_W, d), F32)], ("arbitrary",), (dx1, att, ssm, wo))
    return out


def ffn_fwd(x1, gain, w1, w2, name, xch=None):
    s, d = x1.shape
    nch, _, fc = w1.shape

    def body(x_ref, g_ref, w1_ref, w2_ref, o_ref, h_ref, a_ref):
        @pl.when(pl.program_id(1) == 0)
        def _():
            xv = x_ref[...]
            h_ref[...] = (xv * _rms(xv) * g_ref[...]).astype(BF16)
            o_ref[...] = xv

        a = _dot(h_ref[...], w1_ref[...])
        a_ref[...] = a.astype(BF16)
        o_ref[...] += _dot(jnp.square(jnp.maximum(a, 0.0)).astype(BF16), w2_ref[...])

    tm = min(TM_FFN, s)
    return _call(
        body, name, (s // tm, nch),
        [pl.BlockSpec((tm, d), lambda i, k: (i, 0)), pl.BlockSpec((1, d), lambda i, k: (0, 0)),
         pl.BlockSpec((None, d, fc), lambda i, k: (k, 0, 0)), pl.BlockSpec((None, fc, d), lambda i, k: (k, 0, 0))],
        [pl.BlockSpec((tm, d), lambda i, k: (i, 0)), pl.BlockSpec((tm, d), lambda i, k: (i, 0)),
         pl.BlockSpec((tm, fc), lambda i, k: (i, k))],
        [jax.ShapeDtypeStruct((s, d), F32), jax.ShapeDtypeStruct((s, d), BF16), jax.ShapeDtypeStruct((s, nch * fc), BF16)],
        ("arbitrary", "arbitrary"), (x1, gain, w1, w2), xch=xch)


def ffn_bwd_tok(dx2, x1, gain, a, w1, w2, name, xch=None):
    s, d = x1.shape
    nch, _, fc = w1.shape

    def body(dx_ref, x_ref, g_ref, a_ref, w1_ref, w2_ref, da_ref, dx1_ref, dg_ref, dxb_ref, dh_ref):
        i = pl.program_id(0)
        k = pl.program_id(1)

        @pl.when(jnp.logical_and(i == 0, k == 0))
        def _():
            dg_ref[...] = jnp.zeros_like(dg_ref)

        @pl.when(k == 0)
        def _():
            dxb_ref[...] = dx_ref[...].astype(BF16)
            dh_ref[...] = jnp.zeros_like(dh_ref)

        dr = _dot_nt(dxb_ref[...], w2_ref[...])
        da = (dr * (2.0 * jnp.maximum(a_ref[...].astype(F32), 0.0))).astype(BF16)
        da_ref[...] = da
        dh_ref[...] += _dot_nt(da, w1_ref[...])

        @pl.when(k == nch - 1)
        def _():
            xv = x_ref[...]
            r = _rms(xv)
            xhat = xv * r
            dh = dh_ref[...]
            dg_ref[...] += jnp.sum(dh * xhat, axis=0, keepdims=True)
            dx1_ref[...] = dx_ref[...] + _rms_bwd(xhat, r, dh * g_ref[...])

    tm = min(TM_FFN, s)
    return _call(
        body, name, (s // tm, nch),
        [pl.BlockSpec((tm, d), lambda i, k: (i, 0)), pl.BlockSpec((tm, d), lambda i, k: (i, 0)),
         pl.BlockSpec((1, d), lambda i, k: (0, 0)), pl.BlockSpec((tm, fc), lambda i, k: (i, k)),
         pl.BlockSpec((None, d, fc), lambda i, k: (k, 0, 0)), pl.BlockSpec((None, fc, d), lambda i, k: (k, 0, 0))],
        [pl.BlockSpec((tm, fc), lambda i, k: (i, k)), pl.BlockSpec((tm, d), lambda i, k: (i, 0)),
         pl.BlockSpec((1, d), lambda i, k: (0, 0)), pl.BlockSpec((tm, d), lambda i, k: (i, 0))],
        [jax.ShapeDtypeStruct((s, nch * fc), BF16), jax.ShapeDtypeStruct((s, d), F32),
         jax.ShapeDtypeStruct((1, d), F32), jax.ShapeDtypeStruct((s, d), BF16)],
        ("arbitrary", "arbitrary"), (dx2, x1, gain, a, w1, w2), scratch=[pltpu.VMEM((tm, d), F32)], xch=xch)


def ffn_bwd_w(h2, da, a, dxb, nch, name, xch=None):
    s, d = h2.shape
    fc = a.shape[1] // nch

    def body(h_ref, da_ref, a_ref, dx_ref, dw1_ref, dw2_ref):
        @pl.when(pl.program_id(1) == 0)
        def _():
            dw1_ref[...] = jnp.zeros_like(dw1_ref)
            dw2_ref[...] = jnp.zeros_like(dw2_ref)

        dw1_ref[...] += _dot_tn(h_ref[...], da_ref[...])
        r = jnp.square(jnp.maximum(a_ref[...].astype(F32), 0.0)).astype(BF16)
        dw2_ref[...] += _dot_tn(r, dx_ref[...])

    tm = min(TM_FFN, s)
    return _call(
        body, name, (nch, s // tm),
        [pl.BlockSpec((tm, d), lambda k, t: (t, 0)), pl.BlockSpec((tm, fc), lambda k, t: (t, k)),
         pl.BlockSpec((tm, fc), lambda k, t: (t, k)), pl.BlockSpec((tm, d), lambda k, t: (t, 0))],
        [pl.BlockSpec((None, d, fc), lambda k, t: (k, 0, 0)), pl.BlockSpec((None, fc, d), lambda k, t: (k, 0, 0))],
        [jax.ShapeDtypeStruct((nch, d, fc), F32), jax.ShapeDtypeStruct((nch, fc, d), F32)],
        ("arbitrary", "arbitrary"), (h2, da, a, dxb), xch=xch)


def loss_grad(xf, tgt, name):
    s, d = xf.shape
    nt = s // TM

    def body(x_ref, t_ref, dx_ref, l_ref, acc_ref):
        i = pl.program_id(0)

        @pl.when(i == 0)
        def _():
            acc_ref[...] = jnp.zeros_like(acc_ref)

        e = x_ref[...] - t_ref[...]
        dx_ref[...] = e * (1.0 / d)
        acc_ref[...] += jnp.sum(e * e, axis=0, keepdims=True)

        @pl.when(i == nt - 1)
        def _():
            l_ref[...] = jnp.sum(acc_ref[...], axis=1, keepdims=True) * (0.5 / d)

    out, _ = _call(body, name, (nt,), [pl.BlockSpec((TM, d), _row), pl.BlockSpec((TM, d), _row)],
                   [pl.BlockSpec((TM, d), _row), pl.BlockSpec((1, 1), _fixed)],
                   [jax.ShapeDtypeStruct((s, d), F32), jax.ShapeDtypeStruct((1, 1), F32)], ("arbitrary",),
                   (xf, tgt), scratch=[pltpu.VMEM((1, d), F32)])
    return out


def adamw_sum(parts, w, m, v, br, name):
    nl = len(parts)
    npart, r, c = parts[0].shape
    nb = r // br
    c1 = 1.0 - ADAM_B1 ** ADAM_STEP
    c2 = 1.0 - ADAM_B2 ** ADAM_STEP

    def body(*refs):
        p_refs = refs[:nl]
        w_ref, m_ref, v_ref, g_ref, d_ref, nm_ref, nv_ref = refs[nl:]
        for l in range(nl):
            @pl.when(pl.program_id(0) == l)
            def _(l=l):
                g = p_refs[l][0]
                for j in range(1, npart):
                    g = g + p_refs[l][j]
                m2 = ADAM_B1 * m_ref[...] + (1.0 - ADAM_B1) * g
                v2 = ADAM_B2 * v_ref[...] + (1.0 - ADAM_B2) * jnp.square(g)
                g_ref[...] = g
                nm_ref[...] = m2
                nv_ref[...] = v2
                d_ref[...] = -ADAM_LR * ((m2 / c1) / (jnp.sqrt(v2 / c2) + ADAM_EPS) + ADAM_WD * w_ref[...])

    blk = pl.BlockSpec((br, c), lambda l, i: (l * nb + i, 0))
    pspecs = [pl.BlockSpec((npart, br, c), lambda l, i, own=own: (0, jnp.where(l == own, i, 0), 0)) for own in range(nl)]
    sds = jax.ShapeDtypeStruct((nl * r, c), F32)
    out, _ = _call(body, name, (nl, nb), pspecs + [blk, blk, blk], [blk, blk, blk, blk], [sds, sds, sds, sds],
                   ("arbitrary", "arbitrary"), (*parts, w, m, v))
    return out


def sum_parts(parts, name):
    n, r, c = parts.shape

    def body(p_ref, o_ref):
        g = p_ref[0]
        for j in range(1, n):
            g = g + p_ref[j]
        o_ref[...] = g

    (out,), _ = _call(body, name, (1,), [pl.BlockSpec((n, r, c), lambda i: (0, 0, 0))], [pl.BlockSpec((r, c), _fixed)],
                      [jax.ShapeDtypeStruct((r, c), F32)], ("arbitrary",), (parts,))
    return out


def layer_fwd(x, w_in, p, mats, l, late, nxt):
    tmat, emat, omat, amat = mats
    z = norm_matmul(x, p["norm1"], w_in, f"in_fwd{l}")
    (att, lse), got = attn_fwd(z, p["bias"], p["q_gain"], p["k_gain"], p["sink"], f"attn_fwd{l}", (late, False))
    p = dict(p, **_whole_weights(dict(zip(BIG[-len(got):], got))))
    ug = to_groups(z, U0 // 128, f"ssm_to_groups{l}")
    s4 = ssm_in(ug, emat, l, f"ssm_in{l}")
    xp4 = chunk_scan(s4, amat, False, f"ssm_scan{l}")
    yc = from_groups(ssm_out(ug, tmat, xp4, omat, l, f"ssm_out{l}"), f"ssm_from_groups{l}")
    ssm, ypre, gpre = ssm_post_fwd(yc, z, p["d_skip"], p["w_glu"], f"ssm_post{l}")
    x1 = outproj_fwd(x, att, ssm, p["w_out"], f"out_fwd{l}")
    (x2, h2, a), gathered = ffn_fwd(x1, p["norm2"], p["w_ff1"], p["w_ff2"], f"ffn_fwd{l}",
                                    None if nxt is None else (nxt, False))
    saved = dict(x=x, z=z, att=att, lse=lse, ug=ug, xp4=xp4, ssm=ssm, ypre=ypre, gpre=gpre, x1=x1, h2=h2, a=a)
    return x2, saved, dict(p, w_in=w_in), gathered


def layer_bwd(dx2, p, mats, sv, l, above):
    tmat, emat, omat, amat = mats
    nch = p["w_ff1"].shape[0]
    last = l == 0
    (da, dx1, dnorm2, dxb), got_tok = ffn_bwd_tok(dx2, sv["x1"], p["norm2"], sv["a"], p["w_ff1"], p["w_ff2"],
                                                  f"ffn_bwd{l}", None if above is None else (above["tok"], True))
    (dw1, dw2), got_w = ffn_bwd_w(sv["h2"], da, sv["a"], dxb, nch, f"ffn_bwdw{l}",
                                  None if above is None else (above["w"], True))
    datt, dssm, dwo = outproj_bwd(dx1, sv["att"], sv["ssm"], p["w_out"], f"out_bwd{l}")
    dyc, du_skip, dwglu, ddskip = ssm_post_bwd(dssm, sv["gpre"], sv["ypre"], sv["z"], p["d_skip"], p["w_glu"],
                                               f"ssm_post_bwd{l}")
    dyg = to_groups(dyc, 0, f"ssm_to_groups_bwd{l}")
    domat, dxp4 = ssm_out_bwd(dyg, sv["xp4"], omat, l, f"ssm_out_bwd{l}")
    aconj = amat * jnp.array([1.0, -1.0, 1.0, -1.0], F32).reshape(4, 1, 1)
    ds4, damat = chunk_scan(dxp4, aconj, True, f"ssm_scan_bwd{l}", xp4=sv["xp4"])
    dtmat, demat, dug = ssm_in_bwd(sv["ug"], dyg, ds4, tmat, emat, l, f"ssm_in_bwd{l}")
    du_core = from_groups(dug, f"ssm_from_groups_bwd{l}")
    (dq, dk, dv, dqg, dkg, dsink), got_ff = attn_bwd(sv["z"], p["bias"], sv["att"], datt, sv["lse"], p["q_gain"], p["k_gain"],
                                                     p["sink"], f"attn_bwd{l}", ([dw1, dw2] if last else [dw1], True))
    blocks = _grad_blocks(dict(w_glu=dwglu, w_out=dwo))
    (dx, dwin, dnorm1), got_mix = in_bwd(sv["x"], p["norm1"], p["w_in"], du_skip, du_core, dq, dk, dv, dx1, f"in_bwd{l}",
                                         ([blocks["w_glu"], blocks["w_out"]], True) if last else None)
    grads = dict(norm1=dnorm1, q_gain=dqg, k_gain=dkg, sink=dsink, d_skip=ddskip, norm2=dnorm2)
    win_blocks = _grad_blocks(dict(w_in=dwin))["w_in"]
    mine = dict(w_ff1=got_ff[0])
    if last:
        mine.update(w_ff2=got_ff[1], w_glu=got_mix[0], w_out=got_mix[1])
        below = win_blocks
    else:
        below = dict(tok=[win_blocks, blocks["w_glu"], blocks["w_out"]], w=[dw2])
    theirs = {} if above is None else dict(w_in=got_tok[0], w_glu=got_tok[1], w_out=got_tok[2], w_ff2=got_w[0])
    return dx, grads, (dtmat, demat, domat, damat), mine, theirs, below


def _whole_weights(gathered):
    out = {}
    for n, g in gathered.items():
        if n == "w_in":
            w_in = g.transpose(1, 0, 2).reshape(D_MODEL, IN_W)
            out[n] = jnp.concatenate([w_in[:, V_END:], w_in[:, :V_END]], axis=1)
        elif n == "w_glu":
            out[n] = g.transpose(1, 0, 2).reshape(SSM_W, 2 * SSM_W)
        elif n == "w_out":
            out[n] = g.reshape(ATT_W + SSM_W, D_MODEL)
        else:
            out[n] = g
    return out


def _grad_blocks(grads):
    out = {}
    for n, g in grads.items():
        if n == "w_in":
            g = jnp.concatenate([g[:, IN_W - V_END:], g[:, :IN_W - V_END]], axis=1)
            out[n] = g.reshape(D_MODEL, N_DEV, IN_W // N_DEV).transpose(1, 0, 2)
        elif n == "w_glu":
            out[n] = g.reshape(SSM_W, N_DEV, 2 * SSM_W // N_DEV).transpose(1, 0, 2)
        elif n == "w_out":
            out[n] = g.reshape(N_DEV, (ATT_W + SSM_W) // N_DEV, D_MODEL)
        else:
            out[n] = g
    return out


def _small_layout(like):
    layout, row = {}, 0
    for n in SMALL:
        size = int(np.prod(like[n].shape))
        nrow = -(-size // (8 * 128)) * 8
        layout[n] = (row, nrow, size)
        row += nrow
    return layout, -(-row // (8 * N_DEV)) * 8 * N_DEV


def _pack_small(vals, layout, rows):
    pieces, used = [], 0
    for n in SMALL:
        _, nrow, size = layout[n]
        flat = vals[n].reshape(-1).astype(F32)
        pieces.append(jnp.pad(flat, (0, nrow * 128 - size)).reshape(nrow, 128))
        used += nrow
    if rows > used:
        pieces.append(jnp.zeros((rows - used, 128), F32))
    return jnp.concatenate(pieces, axis=0)


def _unpack_small(packed, like, layout):
    out = {}
    for n in SMALL:
        row, nrow, size = layout[n]
        out[n] = packed[row:row + nrow].reshape(-1)[:size].reshape(like[n].shape)
    return out


def kernel(x, norm1, w_in, q_gain, k_gain, sink, lam_re, lam_im, log_dt, b_re, b_im, c_re, c_im, d_skip, w_glu, w_out, norm2, w_ff1, w_ff2, loss_target, m_norm1, m_w_in, m_q_gain, m_k_gain, m_sink, m_lam_re, m_lam_im, m_log_dt, m_b_re, m_b_im, m_c_re, m_c_im, m_d_skip, m_w_glu, m_w_out, m_norm2, m_w_ff1, m_w_ff2, v_norm1, v_w_in, v_q_gain, v_k_gain, v_sink, v_lam_re, v_lam_im, v_log_dt, v_b_re, v_b_im, v_c_re, v_c_im, v_d_skip, v_w_glu, v_w_out, v_norm2, v_w_ff1, v_w_ff2):
    w = dict(norm1=norm1, w_in=w_in, q_gain=q_gain, k_gain=k_gain, sink=sink, lam_re=lam_re, lam_im=lam_im,
             log_dt=log_dt, b_re=b_re, b_im=b_im, c_re=c_re, c_im=c_im, d_skip=d_skip, w_glu=w_glu, w_out=w_out,
             norm2=norm2, w_ff1=w_ff1, w_ff2=w_ff2)
    m = dict(norm1=m_norm1, w_in=m_w_in, q_gain=m_q_gain, k_gain=m_k_gain, sink=m_sink, lam_re=m_lam_re,
             lam_im=m_lam_im, log_dt=m_log_dt, b_re=m_b_re, b_im=m_b_im, c_re=m_c_re, c_im=m_c_im, d_skip=m_d_skip,
             w_glu=m_w_glu, w_out=m_w_out, norm2=m_norm2, w_ff1=m_w_ff1, w_ff2=m_w_ff2)
    v = dict(norm1=v_norm1, w_in=v_w_in, q_gain=v_q_gain, k_gain=v_k_gain, sink=v_sink, lam_re=v_lam_re,
             lam_im=v_lam_im, log_dt=v_log_dt, b_re=v_b_re, b_im=v_b_im, c_re=v_c_re, c_im=v_c_im, d_skip=v_d_skip,
             w_glu=v_w_glu, w_out=v_w_out, norm2=v_norm2, w_ff1=v_w_ff1, w_ff2=v_w_ff2)
    nl = w_in.shape[0]
    shards = [[w[n][l].astype(BF16) for n in BIG] for l in range(nl)]
    (tmat, emat, omat, amat), mats_vjp = jax.vjp(jax.vmap(ssm_mats), *[w[n] for n in S5])

    (g_in,) = exchange(shards[0][:1], False, "gather_w_in0")
    have = _whole_weights(dict(w_in=g_in))
    late = shards[0][1:]
    xs = x[0]
    bias = band_bias()
    saved, lp, lm = [], [], []
    for l in range(nl):
        p = {n: have[n] for n in have if n != "w_in"}
        for n in ("norm1", "q_gain", "k_gain", "d_skip", "norm2"):
            p[n] = w[n][l].reshape(1, -1)
        p["sink"] = sink[l]
        p["bias"] = bias
        mats = (tmat, emat, omat, amat[l])
        xs, sv, p, got = layer_fwd(xs, have["w_in"], p, mats, l, late, shards[l + 1][:3] if l + 1 < nl else None)
        if l + 1 < nl:
            have = _whole_weights(dict(zip(BIG[:3], got)))
            late = shards[l + 1][3:]
        saved.append(sv)
        lp.append(p)
        lm.append(mats)
    dx, loss_part = loss_grad(xs, loss_target[0], "loss")
    loss = lax.psum(loss_part[0, 0], ("x", "y", "c"))

    grads, dmats, parts = [None] * nl, [None] * nl, [dict() for _ in range(nl)]
    above = None
    for l in reversed(range(nl)):
        dx, grads[l], dmats[l], mine, theirs, above = layer_bwd(dx, lp[l], lm[l], saved[l], l, above)
        parts[l].update(mine)
        if l + 1 < nl:
            parts[l + 1].update(theirs)
    (parts[0]["w_in"],) = exchange([above], True, "exchange_g_in0")

    gs = {n: jnp.stack([grads[l][n].reshape(w[n].shape[1:]) for l in range(nl)])
          for n in ("norm1", "q_gain", "k_gain", "sink", "d_skip", "norm2")}
    ds5 = mats_vjp(tuple(jnp.stack([dmats[l][i] for l in range(nl)]) for i in range(4)))
    gs.update(zip(S5, ds5))
    layout, rows = _small_layout(w)
    (mine,) = exchange([_pack_small(gs, layout, rows).reshape(N_DEV, rows // N_DEV, 128)], True, "scatter_small_grads")
    (small_sum,) = exchange([sum_parts(mine, "sum_small_grads")], False, "gather_small_grads")

    out_g, out_d, out_m, out_v = {}, {}, {}, {}
    for n in BIG:
        c = w[n].shape[-1]
        r = int(np.prod(w[n].shape[:-1]))
        res = adamw_sum([parts[l][n] for l in range(nl)], w[n].reshape(r, c), m[n].reshape(r, c), v[n].reshape(r, c),
                        ADAM_ROWS[c], f"adamw_{n}")
        out_g[n], out_d[n], out_m[n], out_v[n] = (t.reshape(w[n].shape) for t in res)
    res = adamw_sum([small_sum.reshape(1, rows, 128)], _pack_small(w, layout, rows), _pack_small(m, layout, rows),
                    _pack_small(v, layout, rows), rows // N_DEV, "adamw_small")
    for dst, packed in zip((out_g, out_d, out_m, out_v), res):
        dst.update(_unpack_small(packed, w, layout))

    return (loss, dx[None], *[out_g[n] for n in WEIGHTS], *[out_d[n] for n in WEIGHTS],
            *[out_m[n] for n in WEIGHTS], *[out_v[n] for n in WEIGHTS])
```

```python
import numpy as np
import jax
import jax.numpy as jnp
from jax import lax
from jax.experimental import pallas as pl
from jax.experimental.pallas import tpu as pltpu

F32, BF16 = jnp.float32, jnp.bfloat16
EPS = 1e-6
D_MODEL = 1024
ATT_HEADS, KV_HEADS, GQA_GROUP, HEAD_DIM = 8, 2, 4, 64
ATT_W, KV_W, SSM_W, IN_W = 512, 128, 512, 1280
V_END = 768
U0, Q0, K0, V0 = 0, 512, 1024, 1152
BLK = 128
SCALE = 0.125
SSM_G, SSM_H, SSM_P = 32, 16, 64
CH = 16
GW = CH * SSM_H
SEGS = 8
N_DEV = 8
NEG = float(np.finfo(np.float32).min)
SLOPES = tuple(2.0 ** (-8.0 * (h + 1) / ATT_HEADS) for h in range(ATT_HEADS))
VMEM_LIMIT = 56 * 1024 * 1024
TM = 512
TM_FFN = 1024

ADAM_LR, ADAM_B1, ADAM_B2, ADAM_EPS, ADAM_WD, ADAM_STEP = 0.001, 0.9, 0.999, 1e-08, 0.01, 10

SMALL = ("norm1", "q_gain", "k_gain", "sink", "lam_re", "lam_im", "log_dt", "b_re", "b_im",
         "c_re", "c_im", "d_skip", "norm2")
S5 = ("lam_re", "lam_im", "log_dt", "b_re", "b_im", "c_re", "c_im")
BIG = ("w_in", "w_glu", "w_out", "w_ff1", "w_ff2")
WEIGHTS = ("norm1", "w_in", "q_gain", "k_gain", "sink", "lam_re", "lam_im", "log_dt", "b_re", "b_im",
           "c_re", "c_im", "d_skip", "w_glu", "w_out", "norm2", "w_ff1", "w_ff2")
ADAM_ROWS = {160: 256, 128: 512, 512: 128, 1024: 64}


def _dot(a, b):
    return jnp.dot(a, b, preferred_element_type=F32)


def _dot_nt(a, b):
    return lax.dot_general(a, b, (((1,), (1,)), ((), ())), preferred_element_type=F32)


def _dot_tn(a, b):
    return lax.dot_general(a, b, (((0,), (0,)), ((), ())), preferred_element_type=F32)


def _rms(x):
    return lax.rsqrt(jnp.mean(x * x, axis=-1, keepdims=True) + EPS)


def _rms_bwd(xhat, r, dxhat):
    return r * (dxhat - xhat * jnp.mean(dxhat * xhat, axis=-1, keepdims=True))


def _sigmoid(x):
    return 1.0 / (1.0 + jnp.exp(-x))


_GC = 0.7978845608028654
_GA = 0.044715


def _gelu(x):
    return 0.5 * x * (1.0 + jnp.tanh(_GC * (x + _GA * x * x * x)))


def _gelu_grad(x):
    t = jnp.tanh(_GC * (x + _GA * x * x * x))
    return 0.5 * (1.0 + t) + 0.5 * x * (1.0 - t * t) * _GC * (1.0 + 3.0 * _GA * x * x)


def _row(i):
    return (i, 0)


def _fixed(i):
    return (0, 0)


def _me_and_peers():
    x, y, c = lax.axis_index("x"), lax.axis_index("y"), lax.axis_index("c")
    me = 4 * x + 2 * y + c
    peers = []
    for k in range(1, N_DEV):
        px = jnp.bitwise_xor(x, (k >> 2) & 1)
        py = jnp.bitwise_xor(y, (k >> 1) & 1)
        pc = jnp.bitwise_xor(c, k & 1)
        peers.append(((px, py, pc), 4 * px + 2 * py + pc))
    return me, peers


def _xch_copies(ins, outs, send_sems, recv_sems, loc_sems, scatter):
    me, peers = _me_and_peers()
    local, sends, recvs = [], [], []
    for a in range(len(ins)):
        local.append(pltpu.make_async_copy(ins[a].at[me] if scatter else ins[a], outs[a].at[me], loc_sems.at[a]))
    for k, (dev, idx) in enumerate(peers):
        for a in range(len(ins)):
            src = ins[a].at[idx] if scatter else ins[a]
            for dst, group in ((outs[a].at[me], sends), (outs[a].at[idx], recvs)):
                group.append(pltpu.make_async_remote_copy(
                    src_ref=src, dst_ref=dst, send_sem=send_sems.at[a, k], recv_sem=recv_sems.at[a, k],
                    device_id=dev, device_id_type=pl.DeviceIdType.MESH))
    return local, sends, recvs


def _xch_start(copies):
    local, sends, _ = copies
    for cp in local + sends:
        cp.start()


def _xch_wait(copies):
    local, sends, recvs = copies
    for cp in recvs:
        cp.wait_recv()
    for cp in sends:
        cp.wait_send()
    for cp in local:
        cp.wait()


def _xch_shapes(arrays, scatter):
    return [jax.ShapeDtypeStruct(a.shape if scatter else (N_DEV,) + a.shape, a.dtype) for a in arrays]


def _xch_sems(n):
    return [pltpu.SemaphoreType.DMA((n, N_DEV - 1)), pltpu.SemaphoreType.DMA((n, N_DEV - 1)),
            pltpu.SemaphoreType.DMA((n,))]


_ANY = pl.BlockSpec(memory_space=pl.ANY)


def exchange(arrays, scatter, name):
    n = len(arrays)

    def body(*refs):
        copies = _xch_copies(refs[:n], refs[n:2 * n], *refs[2 * n:], scatter)
        _xch_start(copies)
        _xch_wait(copies)

    return pl.pallas_call(
        body, name=name, in_specs=[_ANY] * n, out_specs=[_ANY] * n, out_shape=_xch_shapes(arrays, scatter),
        scratch_shapes=_xch_sems(n), compiler_params=pltpu.CompilerParams(has_side_effects=True),
    )(*arrays)


def _call(body, name, grid, in_specs, out_specs, out_shape, sem, inputs, scratch=(), xch=None):
    params = pltpu.CompilerParams(dimension_semantics=sem, vmem_limit_bytes=VMEM_LIMIT)
    if xch is None:
        out = pl.pallas_call(body, name=name, grid=grid, in_specs=in_specs, out_specs=out_specs, out_shape=out_shape,
                             scratch_shapes=list(scratch), compiler_params=params)(*inputs)
        return list(out), None
    arrays, scatter = xch
    n, nin, nout, nsc = len(arrays), len(in_specs), len(out_specs), len(scratch)

    def wrapped(*refs):
        ins, refs = refs[:nin], refs[nin:]
        xin, refs = refs[:n], refs[n:]
        outs, refs = refs[:nout], refs[nout:]
        xout, refs = refs[:n], refs[n:]
        sc, sems = refs[:nsc], refs[nsc:]
        first = last = None
        for ax, size in enumerate(grid):
            f, e = pl.program_id(ax) == 0, pl.program_id(ax) == size - 1
            first = f if first is None else jnp.logical_and(first, f)
            last = e if last is None else jnp.logical_and(last, e)

        @pl.when(first)
        def _():
            _xch_start(_xch_copies(xin, xout, *sems, scatter))

        body(*ins, *outs, *sc)

        @pl.when(last)
        def _():
            _xch_wait(_xch_copies(xin, xout, *sems, scatter))

    out = pl.pallas_call(
        wrapped, name=name, grid=grid, in_specs=list(in_specs) + [_ANY] * n, out_specs=list(out_specs) + [_ANY] * n,
        out_shape=list(out_shape) + _xch_shapes(arrays, scatter), scratch_shapes=list(scratch) + _xch_sems(n),
        compiler_params=params)(*inputs, *arrays)
    return list(out[:nout]), list(out[nout:])


def norm_matmul(x, gain, w, name):
    s, d = x.shape
    n = w.shape[1]

    def body(x_ref, g_ref, w_ref, z_ref):
        xv = x_ref[...]
        h = (xv * _rms(xv) * g_ref[...]).astype(BF16)
        z_ref[...] = _dot(h, w_ref[...])

    (z,), _ = _call(body, name, (s // TM,),
                    [pl.BlockSpec((TM, d), _row), pl.BlockSpec((1, d), _fixed), pl.BlockSpec((d, n), _fixed)],
                    [pl.BlockSpec((TM, n), _row)], [jax.ShapeDtypeStruct((s, n), F32)], ("parallel",), (x, gain, w))
    return z


def in_bwd(x, gain, w, du_a, du_b, dq, dk, dv, dres, name, xch=None):
    s, d = x.shape
    n = w.shape[1]

    def body(x_ref, g_ref, w_ref, dua_ref, dub_ref, dq_ref, dk_ref, dv_ref, dres_ref, dx_ref, dw_ref, dg_ref):
        @pl.when(pl.program_id(0) == 0)
        def _():
            dw_ref[...] = jnp.zeros_like(dw_ref)
            dg_ref[...] = jnp.zeros_like(dg_ref)

        xv = x_ref[...]
        r = _rms(xv)
        xhat = xv * r
        g = g_ref[...]
        h = (xhat * g).astype(BF16)
        dz = jnp.concatenate([(dua_ref[...] + _lanes4(dub_ref)).astype(BF16), dq_ref[...].astype(BF16),
                              dk_ref[...].astype(BF16), dv_ref[...].astype(BF16)], axis=1)
        dh = _dot_nt(dz, w_ref[...])
        dw_ref[...] += _dot_tn(h, dz)
        dg_ref[...] += jnp.sum(dh * xhat, axis=0, keepdims=True)
        dx_ref[...] = dres_ref[...] + _rms_bwd(xhat, r, dh * g)

    return _call(
        body, name, (s // TM,),
        [pl.BlockSpec((TM, d), _row), pl.BlockSpec((1, d), _fixed), pl.BlockSpec((d, n), _fixed),
         pl.BlockSpec((TM, SSM_W), _row), pl.BlockSpec((4, TM, 128), lambda i: (0, i, 0)), pl.BlockSpec((TM, ATT_W), _row),
         pl.BlockSpec((TM, KV_W), _row), pl.BlockSpec((TM, KV_W), _row), pl.BlockSpec((TM, d), _row)],
        [pl.BlockSpec((TM, d), _row), pl.BlockSpec((d, n), _fixed), pl.BlockSpec((1, d), _fixed)],
        [jax.ShapeDtypeStruct((s, d), F32), jax.ShapeDtypeStruct((d, n), F32), jax.ShapeDtypeStruct((1, d), F32)],
        ("arbitrary",), (x, gain, w, du_a, du_b, dq, dk, dv, dres), xch=xch)


def _band_specs(nb):
    def w0(i):
        return jnp.clip(i - 1, 0, nb - 3)

    specs = [pl.BlockSpec((None, KV_HEADS, GQA_GROUP * BLK, 3 * BLK), lambda i: (i - w0(i), 0, 0, 0)),
             pl.BlockSpec((BLK, ATT_W), lambda i: (i, Q0 // ATT_W))]
    for col in (K0 // KV_W, V0 // KV_W):
        specs += [pl.BlockSpec((BLK, KV_W), lambda i, c=col, o=o: (w0(i) + o, c)) for o in range(3)]
    return specs


def band_bias():
    off = jnp.arange(3).reshape(3, 1, 1, 1)
    row = jnp.arange(GQA_GROUP * BLK).reshape(1, 1, -1, 1)
    dist = jnp.abs(off * BLK + row % BLK - jnp.arange(3 * BLK).reshape(1, 1, 1, -1))
    slope = jnp.asarray(SLOPES, F32).reshape(1, KV_HEADS, GQA_GROUP, 1)
    slope = jnp.repeat(slope, BLK, axis=2)
    return jnp.where(dist <= BLK, -slope * dist.astype(F32), NEG)


def _stack_heads(x, j):
    return jnp.concatenate([x[:, h * HEAD_DIM:(h + 1) * HEAD_DIM] for h in range(j * GQA_GROUP, (j + 1) * GQA_GROUP)],
                           axis=0)


def _per_head(vals):
    head = lax.broadcasted_iota(jnp.int32, (GQA_GROUP * BLK, 1), 0) // BLK
    out = jnp.full((GQA_GROUP * BLK, 1), vals[GQA_GROUP - 1], F32)
    for g in range(GQA_GROUP - 2, -1, -1):
        out = jnp.where(head == g, vals[g], out)
    return out


def attn_fwd(z, bias, qg, kg, sink, name, xch=None):
    s = z.shape[0]
    nb = s // BLK

    def body(sink_ref, b_ref, q_ref, k0, k1, k2, v0, v1, v2, qg_ref, kg_ref, o_ref, lse_ref):
        k3 = jnp.concatenate([k0[...], k1[...], k2[...]], axis=0)
        v3 = jnp.concatenate([v0[...], v1[...], v2[...]], axis=0).astype(BF16)
        q = q_ref[...]
        for j in range(KV_HEADS):
            heads = range(j * GQA_GROUP, (j + 1) * GQA_GROUP)
            kj = k3[:, j * HEAD_DIM:(j + 1) * HEAD_DIM]
            knj = (kj * _rms(kj) * kg_ref[...]).astype(BF16)
            vj = v3[:, j * HEAD_DIM:(j + 1) * HEAD_DIM]
            q4 = _stack_heads(q, j)
            qs = (q4 * _rms(q4) * (qg_ref[...] * SCALE)).astype(BF16)
            sc = _dot_nt(qs, knj) + b_ref[j]
            sk = _per_head([sink_ref[h] for h in heads])
            m = jnp.maximum(jnp.max(sc, axis=-1, keepdims=True), sk)
            p = jnp.exp(sc - m)
            den = jnp.sum(p, axis=-1, keepdims=True) + jnp.exp(sk - m)
            o4 = _dot(p.astype(BF16), vj) * (1.0 / den)
            lse4 = m + jnp.log(den)
            for g, h in enumerate(heads):
                o_ref[:, h * HEAD_DIM:(h + 1) * HEAD_DIM] = o4[g * BLK:(g + 1) * BLK]
                lse_ref[:, h:h + 1] = lse4[g * BLK:(g + 1) * BLK]

    return _call(
        body, name, (nb,),
        [pl.BlockSpec(memory_space=pltpu.SMEM)] + _band_specs(nb)
        + [pl.BlockSpec((1, HEAD_DIM), _fixed), pl.BlockSpec((1, HEAD_DIM), _fixed)],
        [pl.BlockSpec((BLK, ATT_W), _row), pl.BlockSpec((BLK, ATT_HEADS), _row)],
        [jax.ShapeDtypeStruct((s, ATT_W), F32), jax.ShapeDtypeStruct((s, ATT_HEADS), F32)],
        ("arbitrary",), (sink, bias, z, z, z, z, z, z, z, qg, kg), xch=xch)


def attn_bwd(z, bias, att, datt, lse, qg, kg, sink, name, xch=None):
    s = z.shape[0]
    nb = s // BLK

    def body(sink_ref, b_ref, q_ref, k0, k1, k2, v0, v1, v2, o_ref, do_ref, lse_ref, qg_ref, kg_ref,
             dq_ref, dk_ref, dv_ref, dqg_ref, dkg_ref, dsk_ref):
        i = pl.program_id(0)

        @pl.when(i == 0)
        def _():
            dk_ref[...] = jnp.zeros_like(dk_ref)
            dv_ref[...] = jnp.zeros_like(dv_ref)
            dqg_ref[...] = jnp.zeros_like(dqg_ref)
            dkg_ref[...] = jnp.zeros_like(dkg_ref)
            dsk_ref[...] = jnp.zeros_like(dsk_ref)

        w0 = jnp.clip(i - 1, 0, nb - 3)
        k3 = jnp.concatenate([k0[...], k1[...], k2[...]], axis=0)
        v3 = jnp.concatenate([v0[...], v1[...], v2[...]], axis=0).astype(BF16)
        q = q_ref[...]
        o = o_ref[...]
        do = do_ref[...]
        lse = lse_ref[...]
        qgv = qg_ref[...]
        kgv = kg_ref[...]
        rows = pl.ds(pl.multiple_of(w0 * BLK, BLK), 3 * BLK)
        dqg = jnp.zeros((1, HEAD_DIM), F32)
        dkg = jnp.zeros((1, HEAD_DIM), F32)
        for j in range(KV_HEADS):
            heads = range(j * GQA_GROUP, (j + 1) * GQA_GROUP)
            cols = slice(j * HEAD_DIM, (j + 1) * HEAD_DIM)
            kj = k3[:, cols]
            rk = _rms(kj)
            khat = kj * rk
            knj = (khat * kgv).astype(BF16)
            vj = v3[:, cols]
            q4 = _stack_heads(q, j)
            rq = _rms(q4)
            qhat = q4 * rq
            qs = (qhat * (qgv * SCALE)).astype(BF16)
            sc = _dot_nt(qs, knj) + b_ref[j]
            lse4 = jnp.concatenate([lse[:, h:h + 1] for h in heads], axis=0)
            p = jnp.exp(sc - lse4)
            do4 = _stack_heads(do, j)
            delta = jnp.sum(do4 * _stack_heads(o, j), axis=-1, keepdims=True)
            dob = do4.astype(BF16)
            ds = p * (_dot_nt(dob, vj) - delta)
            sunk = jnp.exp(_per_head([sink_ref[h] for h in heads]) - lse4) * delta
            dsb = ds.astype(BF16)
            dvj = _dot_tn(p.astype(BF16), dob)
            dqn = _dot(dsb, knj) * SCALE
            dkn = _dot_tn(dsb, qs)
            dqg = dqg + jnp.sum(dqn * qhat, axis=0, keepdims=True)
            dq4 = _rms_bwd(qhat, rq, dqn * qgv)
            for g, h in enumerate(heads):
                dq_ref[:, h * HEAD_DIM:(h + 1) * HEAD_DIM] = dq4[g * BLK:(g + 1) * BLK]
                dsk_ref[:, h:h + 1] += -jnp.sum(sunk[g * BLK:(g + 1) * BLK], axis=0, keepdims=True)
            dkg = dkg + jnp.sum(dkn * khat, axis=0, keepdims=True)
            dk_ref[rows, cols] += _rms_bwd(khat, rk, dkn * kgv)
            dv_ref[rows, cols] += dvj
        dqg_ref[...] += dqg
        dkg_ref[...] += dkg

    return _call(
        body, name, (nb,),
        [pl.BlockSpec(memory_space=pltpu.SMEM)] + _band_specs(nb)
        + [pl.BlockSpec((BLK, ATT_W), _row), pl.BlockSpec((BLK, ATT_W), _row), pl.BlockSpec((BLK, ATT_HEADS), _row),
           pl.BlockSpec((1, HEAD_DIM), _fixed), pl.BlockSpec((1, HEAD_DIM), _fixed)],
        [pl.BlockSpec((BLK, ATT_W), _row), pl.BlockSpec((s, KV_W), _fixed), pl.BlockSpec((s, KV_W), _fixed),
         pl.BlockSpec((1, HEAD_DIM), _fixed), pl.BlockSpec((1, HEAD_DIM), _fixed), pl.BlockSpec((1, ATT_HEADS), _fixed)],
        [jax.ShapeDtypeStruct((s, ATT_W), F32), jax.ShapeDtypeStruct((s, KV_W), F32), jax.ShapeDtypeStruct((s, KV_W), F32),
         jax.ShapeDtypeStruct((1, HEAD_DIM), F32), jax.ShapeDtypeStruct((1, HEAD_DIM), F32),
         jax.ShapeDtypeStruct((1, ATT_HEADS), F32)],
        ("arbitrary",), (sink, bias, z, z, z, z, z, z, z, att, datt, lse, qg, kg), xch=xch)


def _group_steps(nc):
    nstep = nc // SEGS
    return nstep, min(32, nstep)


def _lanes4(ref):
    return jnp.concatenate([ref[q] for q in range(4)], axis=1)


def _pair_split(x4):
    return [jnp.concatenate([x4[:, q * 128 + r * SSM_P:q * 128 + (r + 1) * SSM_P] for q in range(4)], axis=1)
            for r in range(2)]


def _pair_merge(a0, a1):
    return [jnp.concatenate([a[:, q * SSM_P:(q + 1) * SSM_P] for a in (a0, a1)], axis=1) for q in range(4)]


def _block_transpose(vals, slot):
    vals = list(vals)
    for k in (4, 2, 1):
        low = (slot & k) == 0
        for i in range(8):
            if i & k:
                continue
            a, b = vals[i], vals[i + k]
            vals[i] = jnp.where(low, a, pltpu.roll(b, k * SSM_H, 1))
            vals[i + k] = jnp.where(low, pltpu.roll(a, 128 - k * SSM_H, 1), b)
    return vals


def to_groups(src, col, name):
    s, w = src.shape
    nc = s // CH
    nstep, sb = _group_steps(nc)

    def body(u0, u1, u2, u3, o_ref):
        slot = lax.broadcasted_iota(jnp.int32, (sb, 128), 1) // SSM_H
        for seg in range(SEGS):
            for vc, u_ref in enumerate((u0, u1, u2, u3)):
                for sh in range(2):
                    pieces = [u_ref[seg, pl.ds(sh * 8 + sl, sb, stride=CH), :] for sl in range(8)]
                    for gl, blk in enumerate(_block_transpose(pieces, slot)):
                        o_ref[(vc * 8 + gl) * 2 + sh, pl.ds(seg, sb, stride=SEGS), :] = blk

    src3 = src.reshape(SEGS, s // SEGS, w)
    (out,), _ = _call(body, name, (nstep // sb,),
                      [pl.BlockSpec((SEGS, sb * CH, 128), lambda i, c=col + vc: (0, i, c)) for vc in range(4)],
                      [pl.BlockSpec((2 * SSM_G, sb * SEGS, 128), lambda i: (0, i, 0))],
                      [jax.ShapeDtypeStruct((2 * SSM_G, nc, 128), F32)], ("parallel",), (src3,) * 4)
    return out


def from_groups(yc, name):
    nc = yc.shape[1]
    s = nc * CH
    nstep, sb = _group_steps(nc)

    def body(y_ref, o_ref):
        slot = lax.broadcasted_iota(jnp.int32, (sb, 128), 1) // SSM_H
        for seg in range(SEGS):
            for vc in range(4):
                for sh in range(2):
                    pieces = [y_ref[(vc * 8 + gl) * 2 + sh, pl.ds(seg, sb, stride=SEGS), :] for gl in range(8)]
                    for sl, blk in enumerate(_block_transpose(pieces, slot)):
                        o_ref[vc, seg, pl.ds(sh * 8 + sl, sb, stride=CH), :] = blk

    (out,), _ = _call(body, name, (nstep // sb,),
                      [pl.BlockSpec((2 * SSM_G, sb * SEGS, 128), lambda i: (0, i, 0))],
                      [pl.BlockSpec((4, SEGS, sb * CH, 128), lambda i: (0, 0, i, 0))],
                      [jax.ShapeDtypeStruct((4, SEGS, s // SEGS, 128), F32)], ("parallel",), (yc,))
    return out.reshape(4, s, 128)


def _pair3(i):
    return (i, 0, 0)


def _op_spec(l, rows, cols):
    return pl.BlockSpec((None, 2, rows, cols), lambda i: (l, i, 0, 0))


def _state_blk(i):
    return (0, 0, i)


def ssm_in(ug, e, l, name):
    nc = ug.shape[1]

    def body(u_ref, e_ref, s_ref):
        u = _lanes4(u_ref).astype(BF16)
        for q, blk in enumerate(_pair_merge(_dot(u[:, :GW], e_ref[0].astype(BF16)), _dot(u[:, GW:], e_ref[1].astype(BF16)))):
            s_ref[q] = blk

    (out,), _ = _call(body, name, (SSM_G // 2,),
                      [pl.BlockSpec((4, nc, 128), _pair3), _op_spec(l, GW, 4 * SSM_P)],
                      [pl.BlockSpec((4, nc, 128), _state_blk)],
                      [jax.ShapeDtypeStruct((4, nc, SSM_G * SSM_P), F32)], ("parallel",), (ug, e))
    return out


def chunk_scan(s4, a4, flip, name, xp4=None):
    _, nc, gp = s4.shape
    nstep = nc // SEGS
    assert nstep & (nstep - 1) == 0
    ct = 512
    with_da = xp4 is not None

    def body(*refs):
        if with_da:
            s_ref, a_ref, xp_ref, o_ref, da_ref = refs
        else:
            s_ref, a_ref, o_ref = refs
        rows = lax.broadcasted_iota(jnp.int32, (SEGS, ct), 0)
        zero = jnp.zeros((SEGS, ct), F32)
        for pair in range(2):
            asc = (pair == 0) != flip
            ir, ii = 2 * pair, 2 * pair + 1
            ar1 = a_ref[ir]
            ai1 = a_ref[ii]
            ar = jnp.broadcast_to(ar1, (SEGS, ct))
            ai = jnp.broadcast_to(ai1, (SEGS, ct))

            def tile(t):
                tt = t if asc else nstep - 1 - t
                return pl.ds(pl.multiple_of(tt * SEGS, SEGS), SEGS)

            def local(t, c):
                xr, xi = c
                sl = tile(t)
                return (ar * xr - ai * xi + s_ref[ir, sl, :], ar * xi + ai * xr + s_ref[ii, sl, :])

            er, ei = lax.fori_loop(0, nstep, local, (zero, zero))
            pr, pi = ar1, ai1
            for _ in range(nstep.bit_length() - 1):
                pr, pi = pr * pr - pi * pi, 2.0 * pr * pi
            cr = jnp.zeros((1, ct), F32)
            ci = jnp.zeros((1, ct), F32)
            xin_r, xin_i = zero, zero
            for k in range(SEGS):
                sg = k if asc else SEGS - 1 - k
                here = rows == sg
                xin_r = jnp.where(here, cr, xin_r)
                xin_i = jnp.where(here, ci, xin_i)
                lr = jnp.sum(jnp.where(here, er, 0.0), axis=0, keepdims=True)
                li = jnp.sum(jnp.where(here, ei, 0.0), axis=0, keepdims=True)
                cr, ci = pr * cr - pi * ci + lr, pr * ci + pi * cr + li

            def final(t, c):
                xr, xi, acr, aci = c
                sl = tile(t)
                o_ref[ir, sl, :] = xr
                o_ref[ii, sl, :] = xi
                if with_da:
                    br = xp_ref[ir, sl, :]
                    bi = xp_ref[ii, sl, :]
                    acr = acr + br * xr + bi * xi
                    aci = aci + br * xi - bi * xr
                return (ar * xr - ai * xi + s_ref[ir, sl, :], ar * xi + ai * xr + s_ref[ii, sl, :], acr, aci)

            _, _, acr, aci = lax.fori_loop(0, nstep, final, (xin_r, xin_i, zero, zero))
            if with_da:
                da_ref[ir] = jnp.sum(acr, axis=0, keepdims=True)
                da_ref[ii] = jnp.sum(aci, axis=0, keepdims=True)

    blk = pl.BlockSpec((4, nc, ct), _state_blk)
    ablk = pl.BlockSpec((4, 1, ct), _state_blk)
    sds = jax.ShapeDtypeStruct((4, nc, gp), F32)
    if with_da:
        out, _ = _call(body, name, (gp // ct,), [blk, ablk, blk], [blk, ablk],
                       [sds, jax.ShapeDtypeStruct((4, 1, gp), F32)], ("parallel",), (s4, a4, xp4))
        return out
    (out,), _ = _call(body, name, (gp // ct,), [blk, ablk], [blk], [sds], ("parallel",), (s4, a4))
    return out


def _state_cat(ref):
    return _lanes4(ref).astype(BF16)


def ssm_out(ug, t, xp4, o, l, name):
    nc = ug.shape[1]

    def body(u_ref, t_ref, xp_ref, o_ref, y_ref):
        xs = _pair_split(_lanes4(xp_ref))
        u = _lanes4(u_ref).astype(BF16)
        for r in range(2):
            y = _dot(u[:, r * GW:(r + 1) * GW], t_ref[r].astype(BF16)) + _dot(xs[r].astype(BF16), o_ref[r].astype(BF16))
            y_ref[2 * r] = y[:, :128]
            y_ref[2 * r + 1] = y[:, 128:]

    (out,), _ = _call(body, name, (SSM_G // 2,),
                      [pl.BlockSpec((4, nc, 128), _pair3), _op_spec(l, GW, GW),
                       pl.BlockSpec((4, nc, 128), _state_blk), _op_spec(l, 4 * SSM_P, GW)],
                      [pl.BlockSpec((4, nc, 128), _pair3)],
                      [jax.ShapeDtypeStruct((2 * SSM_G, nc, 128), F32)], ("parallel",), (ug, t, xp4, o))
    return out


def ssm_out_bwd(dyg, xp4, o, l, name):
    nc = dyg.shape[1]

    def body(dy_ref, xp_ref, o_ref, do_ref, dxp_ref):
        xs = _pair_split(_lanes4(xp_ref))
        dy = _lanes4(dy_ref).astype(BF16)
        dxs = []
        for r in range(2):
            dyr = dy[:, r * GW:(r + 1) * GW]
            do_ref[r] = _dot_tn(xs[r].astype(BF16), dyr)
            dxs.append(_dot_nt(dyr, o_ref[r].astype(BF16)))
        for q, blk in enumerate(_pair_merge(*dxs)):
            dxp_ref[q] = blk

    out, _ = _call(body, name, (SSM_G // 2,),
                   [pl.BlockSpec((4, nc, 128), _pair3), pl.BlockSpec((4, nc, 128), _state_blk),
                    _op_spec(l, 4 * SSM_P, GW)],
                   [pl.BlockSpec((2, 4 * SSM_P, GW), _pair3), pl.BlockSpec((4, nc, 128), _state_blk)],
                   [jax.ShapeDtypeStruct((SSM_G, 4 * SSM_P, GW), F32), jax.ShapeDtypeStruct((4, nc, SSM_G * SSM_P), F32)],
                   ("parallel",), (dyg, xp4, o))
    return out


def ssm_in_bwd(ug, dyg, ds4, t, e, l, name):
    nc = ug.shape[1]

    def body(u_ref, dy_ref, ds_ref, t_ref, e_ref, dt_ref, de_ref, du_ref):
        dss = _pair_split(_lanes4(ds_ref))
        u = _lanes4(u_ref).astype(BF16)
        dy = _lanes4(dy_ref).astype(BF16)
        for r in range(2):
            cols = slice(r * GW, (r + 1) * GW)
            ds = dss[r].astype(BF16)
            dt_ref[r] = _dot_tn(u[:, cols], dy[:, cols])
            de_ref[r] = _dot_tn(u[:, cols], ds)
            du = _dot_nt(dy[:, cols], t_ref[r].astype(BF16)) + _dot_nt(ds, e_ref[r].astype(BF16))
            du_ref[2 * r] = du[:, :128]
            du_ref[2 * r + 1] = du[:, 128:]

    out, _ = _call(body, name, (SSM_G // 2,),
                   [pl.BlockSpec((4, nc, 128), _pair3), pl.BlockSpec((4, nc, 128), _pair3),
                    pl.BlockSpec((4, nc, 128), _state_blk), _op_spec(l, GW, GW), _op_spec(l, GW, 4 * SSM_P)],
                   [pl.BlockSpec((2, GW, GW), _pair3), pl.BlockSpec((2, GW, 4 * SSM_P), _pair3),
                    pl.BlockSpec((4, nc, 128), _pair3)],
                   [jax.ShapeDtypeStruct((SSM_G, GW, GW), F32), jax.ShapeDtypeStruct((SSM_G, GW, 4 * SSM_P), F32),
                    jax.ShapeDtypeStruct((2 * SSM_G, nc, 128), F32)], ("parallel",), (ug, dyg, ds4, t, e))
    return out


def ssm_post_fwd(yc, z, dskip, wglu, name):
    s = yc.shape[1]

    def body(y_ref, u_ref, d_ref, w_ref, o_ref, yp_ref, g_ref):
        yp = _lanes4(y_ref) + d_ref[...] * u_ref[...]
        yp_ref[...] = yp
        gv = _dot(_gelu(yp).astype(BF16), w_ref[...])
        g_ref[...] = gv
        o_ref[...] = gv[:, :SSM_W] * _sigmoid(gv[:, SSM_W:])

    out, _ = _call(body, name, (s // TM,),
                   [pl.BlockSpec((4, TM, 128), lambda i: (0, i, 0)), pl.BlockSpec((TM, SSM_W), lambda i: (i, U0 // SSM_W)),
                    pl.BlockSpec((1, SSM_W), _fixed), pl.BlockSpec((SSM_W, 2 * SSM_W), _fixed)],
                   [pl.BlockSpec((TM, SSM_W), _row), pl.BlockSpec((TM, SSM_W), _row), pl.BlockSpec((TM, 2 * SSM_W), _row)],
                   [jax.ShapeDtypeStruct((s, SSM_W), F32), jax.ShapeDtypeStruct((s, SSM_W), F32),
                    jax.ShapeDtypeStruct((s, 2 * SSM_W), F32)], ("parallel",), (yc, z, dskip, wglu))
    return out


def ssm_post_bwd(dssm, gpre, ypre, z, dskip, wglu, name):
    s = dssm.shape[0]

    def body(do_ref, g_ref, yp_ref, u_ref, d_ref, w_ref, dy_ref, du_ref, dw_ref, dd_ref):
        @pl.when(pl.program_id(0) == 0)
        def _():
            dw_ref[...] = jnp.zeros_like(dw_ref)
            dd_ref[...] = jnp.zeros_like(dd_ref)

        gv = g_ref[...]
        val = gv[:, :SSM_W]
        sg = _sigmoid(gv[:, SSM_W:])
        do = do_ref[...]
        dg = jnp.concatenate([do * sg, do * val * sg * (1.0 - sg)], axis=1).astype(BF16)
        yp = yp_ref[...]
        dgl = _dot_nt(dg, w_ref[...])
        dw_ref[...] += _dot_tn(_gelu(yp).astype(BF16), dg)
        dyp = dgl * _gelu_grad(yp)
        dy_ref[...] = dyp
        du_ref[...] = dyp * d_ref[...]
        dd_ref[...] += jnp.sum(dyp * u_ref[...], axis=0, keepdims=True)

    out, _ = _call(body, name, (s // TM,),
                   [pl.BlockSpec((TM, SSM_W), _row), pl.BlockSpec((TM, 2 * SSM_W), _row), pl.BlockSpec((TM, SSM_W), _row),
                    pl.BlockSpec((TM, SSM_W), lambda i: (i, U0 // SSM_W)), pl.BlockSpec((1, SSM_W), _fixed),
                    pl.BlockSpec((SSM_W, 2 * SSM_W), _fixed)],
                   [pl.BlockSpec((TM, SSM_W), _row), pl.BlockSpec((TM, SSM_W), _row),
                    pl.BlockSpec((SSM_W, 2 * SSM_W), _fixed), pl.BlockSpec((1, SSM_W), _fixed)],
                   [jax.ShapeDtypeStruct((s, SSM_W), F32), jax.ShapeDtypeStruct((s, SSM_W), F32),
                    jax.ShapeDtypeStruct((SSM_W, 2 * SSM_W), F32), jax.ShapeDtypeStruct((1, SSM_W), F32)],
                   ("arbitrary",), (dssm, gpre, ypre, z, dskip, wglu))
    return out


def _toeplitz_select():
    row = lax.broadcasted_iota(jnp.int32, (GW, CH * GW), 0)
    col = lax.broadcasted_iota(jnp.int32, (GW, CH * GW), 1)
    j, h2 = row // SSM_H, row % SSM_H
    s, t, h = col // GW, (col % GW) // SSM_H, col % SSM_H
    same = h2 == h
    return jnp.concatenate([same & (t - s == j), same & (s - t == j)], axis=0).astype(F32)


def ssm_mats(lam_re, lam_im, log_dt, b_re, b_im, c_re, c_im):
    g, p, hh = SSM_G, SSM_P, SSM_H
    hp = lax.Precision.HIGHEST
    jj = jnp.arange(CH + 1, dtype=F32)
    dt = jnp.exp(log_dt)[..., None]
    mag = jnp.exp((lam_re * dt)[..., None] * jj)
    ang = (lam_im * dt)[..., None] * jj
    pr, pi = mag * jnp.cos(ang), mag * jnp.sin(ang)
    abr, abi = pr[..., 1], pi[..., 1]
    den = lam_re * lam_re + lam_im * lam_im
    zr = ((abr - 1.0) * lam_re + abi * lam_im) / den
    zi = (abi * lam_re - (abr - 1.0) * lam_im) / den
    bbr = zr[..., None] * b_re[None] - zi[..., None] * b_im[None]
    bbi = zr[..., None] * b_im[None] + zi[..., None] * b_re[None]
    crt, cit = c_re.transpose(0, 1, 3, 2), c_im.transpose(0, 1, 3, 2)
    car = pr[..., None] * crt[..., None, :] - pi[..., None] * cit[..., None, :]
    cai = pr[..., None] * cit[..., None, :] + pi[..., None] * crt[..., None, :]
    lhs = jnp.concatenate([bbr, -bbi], axis=2).transpose(0, 1, 3, 2)
    rhs = jnp.concatenate([car[..., :CH, :], cai[..., :CH, :]], axis=2).reshape(2, g, 2 * p, GW)
    kt = jnp.einsum("dgkp,dgpn->dgkn", lhs, rhs, precision=hp)
    kcat = jnp.concatenate([kt[0], kt[1]], axis=-1).reshape(g * hh, 2 * GW)
    tmat = jnp.dot(kcat, _toeplitz_select(), precision=hp)
    tmat = tmat.reshape(g, hh, CH, GW).transpose(0, 2, 1, 3).reshape(g, GW, GW)

    def e_part(d, pw_r, pw_i):
        pw_r, pw_i = pw_r.transpose(0, 2, 1)[:, :, None, :], pw_i.transpose(0, 2, 1)[:, :, None, :]
        br, bi = bbr[d].transpose(0, 2, 1)[:, None], bbi[d].transpose(0, 2, 1)[:, None]
        return [pw_r * br - pw_i * bi, pw_r * bi + pw_i * br]

    eparts = (e_part(0, pr[0, ..., :CH][..., ::-1], pi[0, ..., :CH][..., ::-1])
              + e_part(1, pr[1, ..., :CH], pi[1, ..., :CH]))
    emat = jnp.concatenate(eparts, axis=3).reshape(g, GW, 4 * p)

    oparts = [car[0, ..., 1:, :], -cai[0, ..., 1:, :], car[1, ..., 1:, :][..., ::-1, :], -cai[1, ..., 1:, :][..., ::-1, :]]
    omat = jnp.stack([v.reshape(g, p, GW) for v in oparts], axis=1).reshape(g, 4 * p, GW)
    amat = jnp.stack([pr[0, ..., CH], pi[0, ..., CH], pr[1, ..., CH], pi[1, ..., CH]], axis=0).reshape(4, 1, g * p)
    return tmat, emat, omat, amat


def outproj_fwd(x, att, ssm, wo, name):
    s, d = x.shape

    def body(x_ref, a_ref, s_ref, w_ref, o_ref):
        o_ref[...] = (x_ref[...] + _dot(a_ref[...].astype(BF16), w_ref[0:ATT_W, :])
                      + _dot(s_ref[...].astype(BF16), w_ref[ATT_W:, :]))

    (out,), _ = _call(body, name, (s // TM,),
                      [pl.BlockSpec((TM, d), _row), pl.BlockSpec((TM, ATT_W), _row), pl.BlockSpec((TM, SSM_W), _row),
                       pl.BlockSpec((ATT_W + SSM_W, d), _fixed)],
                      [pl.BlockSpec((TM, d), _row)], [jax.ShapeDtypeStruct((s, d), F32)], ("parallel",),
                      (x, att, ssm, wo))
    return out


def outproj_bwd(dx1, att, ssm, wo, name):
    s, d = dx1.shape

    def body(dx_ref, a_ref, s_ref, w_ref, da_ref, ds_ref, dw_ref):
        @pl.when(pl.program_id(0) == 0)
        def _():
            dw_ref[...] = jnp.zeros_like(dw_ref)

        dxb = dx_ref[...].astype(BF16)
        da_ref[...] = _dot_nt(dxb, w_ref[0:ATT_W, :])
        ds_ref[...] = _dot_nt(dxb, w_ref[ATT_W:, :])
        dw_ref[0:ATT_W, :] += _dot_tn(a_ref[...].astype(BF16), dxb)
        dw_ref[ATT_W:, :] += _dot_tn(s_ref[...].astype(BF16), dxb)

    out, _ = _call(body, name, (s // TM,),
                   [pl.BlockSpec((TM, d), _row), pl.BlockSpec((TM, ATT_W), _row), pl.BlockSpec((TM, SSM_W), _row),
                    pl.BlockSpec((ATT_W + SSM_W, d), _fixed)],
                   [pl.BlockSpec((TM, ATT_W), _row), pl.BlockSpec((TM, SSM_W), _row),
                    pl.BlockSpec((ATT_W + SSM_W, d), _fixed)],
                   [jax.ShapeDtypeStruct((s, ATT_W), F32), jax.ShapeDtypeStruct((s, SSM_W), F32),
                    jax.ShapeDtypeStruct((ATT_W + SSM_W, d), F32)], ("arbitrary",), (dx1, att, ssm, wo))
    return out


def ffn_fwd(x1, gain, w1, w2, name, xch=None):
    s, d = x1.shape
    nch, _, fc = w1.shape

    def body(x_ref, g_ref, w1_ref, w2_ref, o_ref, h_ref, a_ref):
        @pl.when(pl.program_id(1) == 0)
        def _():
            xv = x_ref[...]
            h_ref[...] = (xv * _rms(xv) * g_ref[...]).astype(BF16)
            o_ref[...] = xv

        a = _dot(h_ref[...], w1_ref[...])
        a_ref[...] = a.astype(BF16)
        o_ref[...] += _dot(jnp.square(jnp.maximum(a, 0.0)).astype(BF16), w2_ref[...])

    tm = min(TM_FFN, s)
    return _call(
        body, name, (s // tm, nch),
        [pl.BlockSpec((tm, d), lambda i, k: (i, 0)), pl.BlockSpec((1, d), lambda i, k: (0, 0)),
         pl.BlockSpec((None, d, fc), lambda i, k: (k, 0, 0)), pl.BlockSpec((None, fc, d), lambda i, k: (k, 0, 0))],
        [pl.BlockSpec((tm, d), lambda i, k: (i, 0)), pl.BlockSpec((tm, d), lambda i, k: (i, 0)),
         pl.BlockSpec((tm, fc), lambda i, k: (i, k))],
        [jax.ShapeDtypeStruct((s, d), F32), jax.ShapeDtypeStruct((s, d), BF16), jax.ShapeDtypeStruct((s, nch * fc), BF16)],
        ("arbitrary", "arbitrary"), (x1, gain, w1, w2), xch=xch)


def ffn_bwd_tok(dx2, x1, gain, a, w1, w2, name, xch=None):
    s, d = x1.shape
    nch, _, fc = w1.shape

    def body(dx_ref, x_ref, g_ref, a_ref, w1_ref, w2_ref, da_ref, dx1_ref, dg_ref, dxb_ref, dh_ref):
        i = pl.program_id(0)
        k = pl.program_id(1)

        @pl.when(jnp.logical_and(i == 0, k == 0))
        def _():
            dg_ref[...] = jnp.zeros_like(dg_ref)

        @pl.when(k == 0)
        def _():
            dxb_ref[...] = dx_ref[...].astype(BF16)
            dh_ref[...] = jnp.zeros_like(dh_ref)

        dr = _dot_nt(dxb_ref[...], w2_ref[...])
        da = (dr * (2.0 * jnp.maximum(a_ref[...].astype(F32), 0.0))).astype(BF16)
        da_ref[...] = da
        dh_ref[...] += _dot_nt(da, w1_ref[...])

        @pl.when(k == nch - 1)
        def _():
            xv = x_ref[...]
            r = _rms(xv)
            xhat = xv * r
            dh = dh_ref[...]
            dg_ref[...] += jnp.sum(dh * xhat, axis=0, keepdims=True)
            dx1_ref[...] = dx_ref[...] + _rms_bwd(xhat, r, dh * g_ref[...])

    tm = min(TM_FFN, s)
    return _call(
        body, name, (s // tm, nch),
        [pl.BlockSpec((tm, d), lambda i, k: (i, 0)), pl.BlockSpec((tm, d), lambda i, k: (i, 0)),
         pl.BlockSpec((1, d), lambda i, k: (0, 0)), pl.BlockSpec((tm, fc), lambda i, k: (i, k)),
         pl.BlockSpec((None, d, fc), lambda i, k: (k, 0, 0)), pl.BlockSpec((None, fc, d), lambda i, k: (k, 0, 0))],
        [pl.BlockSpec((tm, fc), lambda i, k: (i, k)), pl.BlockSpec((tm, d), lambda i, k: (i, 0)),
         pl.BlockSpec((1, d), lambda i, k: (0, 0)), pl.BlockSpec((tm, d), lambda i, k: (i, 0))],
        [jax.ShapeDtypeStruct((s, nch * fc), BF16), jax.ShapeDtypeStruct((s, d), F32),
         jax.ShapeDtypeStruct((1, d), F32), jax.ShapeDtypeStruct((s, d), BF16)],
        ("arbitrary", "arbitrary"), (dx2, x1, gain, a, w1, w2), scratch=[pltpu.VMEM((tm, d), F32)], xch=xch)


def ffn_bwd_w(h2, da, a, dxb, nch, name, xch=None):
    s, d = h2.shape
    fc = a.shape[1] // nch

    def body(h_ref, da_ref, a_ref, dx_ref, dw1_ref, dw2_ref):
        @pl.when(pl.program_id(1) == 0)
        def _():
            dw1_ref[...] = jnp.zeros_like(dw1_ref)
            dw2_ref[...] = jnp.zeros_like(dw2_ref)

        dw1_ref[...] += _dot_tn(h_ref[...], da_ref[...])
        r = jnp.square(jnp.maximum(a_ref[...].astype(F32), 0.0)).astype(BF16)
        dw2_ref[...] += _dot_tn(r, dx_ref[...])

    tm = min(TM_FFN, s)
    return _call(
        body, name, (nch, s // tm),
        [pl.BlockSpec((tm, d), lambda k, t: (t, 0)), pl.BlockSpec((tm, fc), lambda k, t: (t, k)),
         pl.BlockSpec((tm, fc), lambda k, t: (t, k)), pl.BlockSpec((tm, d), lambda k, t: (t, 0))],
        [pl.BlockSpec((None, d, fc), lambda k, t: (k, 0, 0)), pl.BlockSpec((None, fc, d), lambda k, t: (k, 0, 0))],
        [jax.ShapeDtypeStruct((nch, d, fc), F32), jax.ShapeDtypeStruct((nch, fc, d), F32)],
        ("arbitrary", "arbitrary"), (h2, da, a, dxb), xch=xch)


def loss_grad(xf, tgt, name):
    s, d = xf.shape
    nt = s // TM

    def body(x_ref, t_ref, dx_ref, l_ref, acc_ref):
        i = pl.program_id(0)

        @pl.when(i == 0)
        def _():
            acc_ref[...] = jnp.zeros_like(acc_ref)

        e = x_ref[...] - t_ref[...]
        dx_ref[...] = e * (1.0 / d)
        acc_ref[...] += jnp.sum(e * e, axis=0, keepdims=True)

        @pl.when(i == nt - 1)
        def _():
            l_ref[...] = jnp.sum(acc_ref[...], axis=1, keepdims=True) * (0.5 / d)

    out, _ = _call(body, name, (nt,), [pl.BlockSpec((TM, d), _row), pl.BlockSpec((TM, d), _row)],
                   [pl.BlockSpec((TM, d), _row), pl.BlockSpec((1, 1), _fixed)],
                   [jax.ShapeDtypeStruct((s, d), F32), jax.ShapeDtypeStruct((1, 1), F32)], ("arbitrary",),
                   (xf, tgt), scratch=[pltpu.VMEM((1, d), F32)])
    return out


def adamw_sum(parts, w, m, v, br, name):
    nl = len(parts)
    npart, r, c = parts[0].shape
    nb = r // br
    c1 = 1.0 - ADAM_B1 ** ADAM_STEP
    c2 = 1.0 - ADAM_B2 ** ADAM_STEP

    def body(*refs):
        p_refs = refs[:nl]
        w_ref, m_ref, v_ref, g_ref, d_ref, nm_ref, nv_ref = refs[nl:]
        for l in range(nl):
            @pl.when(pl.program_id(0) == l)
            def _(l=l):
                g = p_refs[l][0]
                for j in range(1, npart):
                    g = g + p_refs[l][j]
                m2 = ADAM_B1 * m_ref[...] + (1.0 - ADAM_B1) * g
                v2 = ADAM_B2 * v_ref[...] + (1.0 - ADAM_B2) * jnp.square(g)
                g_ref[...] = g
                nm_ref[...] = m2
                nv_ref[...] = v2
                d_ref[...] = -ADAM_LR * ((m2 / c1) / (jnp.sqrt(v2 / c2) + ADAM_EPS) + ADAM_WD * w_ref[...])

    blk = pl.BlockSpec((br, c), lambda l, i: (l * nb + i, 0))
    pspecs = [pl.BlockSpec((npart, br, c), lambda l, i, own=own: (0, jnp.where(l == own, i, 0), 0)) for own in range(nl)]
    sds = jax.ShapeDtypeStruct((nl * r, c), F32)
    out, _ = _call(body, name, (nl, nb), pspecs + [blk, blk, blk], [blk, blk, blk, blk], [sds, sds, sds, sds],
                   ("arbitrary", "arbitrary"), (*parts, w, m, v))
    return out


def sum_parts(parts, name):
    n, r, c = parts.shape

    def body(p_ref, o_ref):
        g = p_ref[0]
        for j in range(1, n):
            g = g + p_ref[j]
        o_ref[...] = g

    (out,), _ = _call(body, name, (1,), [pl.BlockSpec((n, r, c), lambda i: (0, 0, 0))], [pl.BlockSpec((r, c), _fixed)],
                      [jax.ShapeDtypeStruct((r, c), F32)], ("arbitrary",), (parts,))
    return out


def layer_fwd(x, w_in, p, mats, l, late, nxt):
    tmat, emat, omat, amat = mats
    z = norm_matmul(x, p["norm1"], w_in, f"in_fwd{l}")
    (att, lse), got = attn_fwd(z, p["bias"], p["q_gain"], p["k_gain"], p["sink"], f"attn_fwd{l}", (late, False))
    p = dict(p, **_whole_weights(dict(zip(BIG[-len(got):], got))))
    ug = to_groups(z, U0 // 128, f"ssm_to_groups{l}")
    s4 = ssm_in(ug, emat, l, f"ssm_in{l}")
    xp4 = chunk_scan(s4, amat, False, f"ssm_scan{l}")
    yc = from_groups(ssm_out(ug, tmat, xp4, omat, l, f"ssm_out{l}"), f"ssm_from_groups{l}")
    ssm, ypre, gpre = ssm_post_fwd(yc, z, p["d_skip"], p["w_glu"], f"ssm_post{l}")
    x1 = outproj_fwd(x, att, ssm, p["w_out"], f"out_fwd{l}")
    (x2, h2, a), gathered = ffn_fwd(x1, p["norm2"], p["w_ff1"], p["w_ff2"], f"ffn_fwd{l}",
                                    None if nxt is None else (nxt, False))
    saved = dict(x=x, z=z, att=att, lse=lse, ug=ug, xp4=xp4, ssm=ssm, ypre=ypre, gpre=gpre, x1=x1, h2=h2, a=a)
    return x2, saved, dict(p, w_in=w_in), gathered


def layer_bwd(dx2, p, mats, sv, l, above):
    tmat, emat, omat, amat = mats
    nch = p["w_ff1"].shape[0]
    last = l == 0
    (da, dx1, dnorm2, dxb), got_tok = ffn_bwd_tok(dx2, sv["x1"], p["norm2"], sv["a"], p["w_ff1"], p["w_ff2"],
                                                  f"ffn_bwd{l}", None if above is None else (above["tok"], True))
    (dw1, dw2), got_w = ffn_bwd_w(sv["h2"], da, sv["a"], dxb, nch, f"ffn_bwdw{l}",
                                  None if above is None else (above["w"], True))
    datt, dssm, dwo = outproj_bwd(dx1, sv["att"], sv["ssm"], p["w_out"], f"out_bwd{l}")
    dyc, du_skip, dwglu, ddskip = ssm_post_bwd(dssm, sv["gpre"], sv["ypre"], sv["z"], p["d_skip"], p["w_glu"],
                                               f"ssm_post_bwd{l}")
    dyg = to_groups(dyc, 0, f"ssm_to_groups_bwd{l}")
    domat, dxp4 = ssm_out_bwd(dyg, sv["xp4"], omat, l, f"ssm_out_bwd{l}")
    aconj = amat * jnp.array([1.0, -1.0, 1.0, -1.0], F32).reshape(4, 1, 1)
    ds4, damat = chunk_scan(dxp4, aconj, True, f"ssm_scan_bwd{l}", xp4=sv["xp4"])
    dtmat, demat, dug = ssm_in_bwd(sv["ug"], dyg, ds4, tmat, emat, l, f"ssm_in_bwd{l}")
    du_core = from_groups(dug, f"ssm_from_groups_bwd{l}")
    (dq, dk, dv, dqg, dkg, dsink), got_ff = attn_bwd(sv["z"], p["bias"], sv["att"], datt, sv["lse"], p["q_gain"], p["k_gain"],
                                                     p["sink"], f"attn_bwd{l}", ([dw1, dw2] if last else [dw1], True))
    blocks = _grad_blocks(dict(w_glu=dwglu, w_out=dwo))
    (dx, dwin, dnorm1), got_mix = in_bwd(sv["x"], p["norm1"], p["w_in"], du_skip, du_core, dq, dk, dv, dx1, f"in_bwd{l}",
                                         ([blocks["w_glu"], blocks["w_out"]], True) if last else None)
    grads = dict(norm1=dnorm1, q_gain=dqg, k_gain=dkg, sink=dsink, d_skip=ddskip, norm2=dnorm2)
    win_blocks = _grad_blocks(dict(w_in=dwin))["w_in"]
    mine = dict(w_ff1=got_ff[0])
    if last:
        mine.update(w_ff2=got_ff[1], w_glu=got_mix[0], w_out=got_mix[1])
        below = win_blocks
    else:
        below = dict(tok=[win_blocks, blocks["w_glu"], blocks["w_out"]], w=[dw2])
    theirs = {} if above is None else dict(w_in=got_tok[0], w_glu=got_tok[1], w_out=got_tok[2], w_ff2=got_w[0])
    return dx, grads, (dtmat, demat, domat, damat), mine, theirs, below


def _whole_weights(gathered):
    out = {}
    for n, g in gathered.items():
        if n == "w_in":
            w_in = g.transpose(1, 0, 2).reshape(D_MODEL, IN_W)
            out[n] = jnp.concatenate([w_in[:, V_END:], w_in[:, :V_END]], axis=1)
        elif n == "w_glu":
            out[n] = g.transpose(1, 0, 2).reshape(SSM_W, 2 * SSM_W)
        elif n == "w_out":
            out[n] = g.reshape(ATT_W + SSM_W, D_MODEL)
        else:
            out[n] = g
    return out


def _grad_blocks(grads):
    out = {}
    for n, g in grads.items():
        if n == "w_in":
            g = jnp.concatenate([g[:, IN_W - V_END:], g[:, :IN_W - V_END]], axis=1)
            out[n] = g.reshape(D_MODEL, N_DEV, IN_W // N_DEV).transpose(1, 0, 2)
        elif n == "w_glu":
            out[n] = g.reshape(SSM_W, N_DEV, 2 * SSM_W // N_DEV).transpose(1, 0, 2)
        elif n == "w_out":
            out[n] = g.reshape(N_DEV, (ATT_W + SSM_W) // N_DEV, D_MODEL)
        else:
            out[n] = g
    return out


def _small_layout(like):
    layout, row = {}, 0
    for n in SMALL:
        size = int(np.prod(like[n].shape))
        nrow = -(-size // (8 * 128)) * 8
        layout[n] = (row, nrow, size)
        row += nrow
    return layout, -(-row // (8 * N_DEV)) * 8 * N_DEV


def _pack_small(vals, layout, rows):
    pieces, used = [], 0
    for n in SMALL:
        _, nrow, size = layout[n]
        flat = vals[n].reshape(-1).astype(F32)
        pieces.append(jnp.pad(flat, (0, nrow * 128 - size)).reshape(nrow, 128))
        used += nrow
    if rows > used:
        pieces.append(jnp.zeros((rows - used, 128), F32))
    return jnp.concatenate(pieces, axis=0)


def _unpack_small(packed, like, layout):
    out = {}
    for n in SMALL:
        row, nrow, size = layout[n]
        out[n] = packed[row:row + nrow].reshape(-1)[:size].reshape(like[n].shape)
    return out


def kernel(x, norm1, w_in, q_gain, k_gain, sink, lam_re, lam_im, log_dt, b_re, b_im, c_re, c_im, d_skip, w_glu, w_out, norm2, w_ff1, w_ff2, loss_target, m_norm1, m_w_in, m_q_gain, m_k_gain, m_sink, m_lam_re, m_lam_im, m_log_dt, m_b_re, m_b_im, m_c_re, m_c_im, m_d_skip, m_w_glu, m_w_out, m_norm2, m_w_ff1, m_w_ff2, v_norm1, v_w_in, v_q_gain, v_k_gain, v_sink, v_lam_re, v_lam_im, v_log_dt, v_b_re, v_b_im, v_c_re, v_c_im, v_d_skip, v_w_glu, v_w_out, v_norm2, v_w_ff1, v_w_ff2):
    w = dict(norm1=norm1, w_in=w_in, q_gain=q_gain, k_gain=k_gain, sink=sink, lam_re=lam_re, lam_im=lam_im,
             log_dt=log_dt, b_re=b_re, b_im=b_im, c_re=c_re, c_im=c_im, d_skip=d_skip, w_glu=w_glu, w_out=w_out,
             norm2=norm2, w_ff1=w_ff1, w_ff2=w_ff2)
    m = dict(norm1=m_norm1, w_in=m_w_in, q_gain=m_q_gain, k_gain=m_k_gain, sink=m_sink, lam_re=m_lam_re,
             lam_im=m_lam_im, log_dt=m_log_dt, b_re=m_b_re, b_im=m_b_im, c_re=m_c_re, c_im=m_c_im, d_skip=m_d_skip,
             w_glu=m_w_glu, w_out=m_w_out, norm2=m_norm2, w_ff1=m_w_ff1, w_ff2=m_w_ff2)
    v = dict(norm1=v_norm1, w_in=v_w_in, q_gain=v_q_gain, k_gain=v_k_gain, sink=v_sink, lam_re=v_lam_re,
             lam_im=v_lam_im, log_dt=v_log_dt, b_re=v_b_re, b_im=v_b_im, c_re=v_c_re, c_im=v_c_im, d_skip=v_d_skip,
             w_glu=v_w_glu, w_out=v_w_out, norm2=v_norm2, w_ff1=v_w_ff1, w_ff2=v_w_ff2)
    nl = w_in.shape[0]
    shards = [[w[n][l].astype(BF16) for n in BIG] for l in range(nl)]
    (tmat, emat, omat, amat), mats_vjp = jax.vjp(jax.vmap(ssm_mats), *[w[n] for n in S5])

    (g_in,) = exchange(shards[0][:1], False, "gather_w_in0")
    have = _whole_weights(dict(w_in=g_in))
    late = shards[0][1:]
    xs = x[0]
    bias = band_bias()
    saved, lp, lm = [], [], []
    for l in range(nl):
        p = {n: have[n] for n in have if n != "w_in"}
        for n in ("norm1", "q_gain", "k_gain", "d_skip", "norm2"):
            p[n] = w[n][l].reshape(1, -1)
        p["sink"] = sink[l]
        p["bias"] = bias
        mats = (tmat, emat, omat, amat[l])
        xs, sv, p, got = layer_fwd(xs, have["w_in"], p, mats, l, late, shards[l + 1][:4] if l + 1 < nl else None)
        if l + 1 < nl:
            have = _whole_weights(dict(zip(BIG[:4], got)))
            late = shards[l + 1][4:]
        saved.append(sv)
        lp.append(p)
        lm.append(mats)
    dx, loss_part = loss_grad(xs, loss_target[0], "loss")
    loss = lax.psum(loss_part[0, 0], ("x", "y", "c"))

    grads, dmats, parts = [None] * nl, [None] * nl, [dict() for _ in range(nl)]
    above = None
    for l in reversed(range(nl)):
        dx, grads[l], dmats[l], mine, theirs, above = layer_bwd(dx, lp[l], lm[l], saved[l], l, above)
        parts[l].update(mine)
        if l + 1 < nl:
            parts[l + 1].update(theirs)
    (parts[0]["w_in"],) = exchange([above], True, "exchange_g_in0")

    gs = {n: jnp.stack([grads[l][n].reshape(w[n].shape[1:]) for l in range(nl)])
          for n in ("norm1", "q_gain", "k_gain", "sink", "d_skip", "norm2")}
    ds5 = mats_vjp(tuple(jnp.stack([dmats[l][i] for l in range(nl)]) for i in range(4)))
    gs.update(zip(S5, ds5))
    layout, rows = _small_layout(w)
    (mine,) = exchange([_pack_small(gs, layout, rows).reshape(N_DEV, rows // N_DEV, 128)], True, "scatter_small_grads")
    (small_sum,) = exchange([sum_parts(mine, "sum_small_grads")], False, "gather_small_grads")

    out_g, out_d, out_m, out_v = {}, {}, {}, {}
    for n in BIG:
        c = w[n].shape[-1]
        r = int(np.prod(w[n].shape[:-1]))
        res = adamw_sum([parts[l][n] for l in range(nl)], w[n].reshape(r, c), m[n].reshape(r, c), v[n].reshape(r, c),
                        ADAM_ROWS[c], f"adamw_{n}")
        out_g[n], out_d[n], out_m[n], out_v[n] = (t.reshape(w[n].shape) for t in res)
    res = adamw_sum([small_sum.reshape(1, rows, 128)], _pack_small(w, layout, rows), _pack_small(m, layout, rows),
                    _pack_small(v, layout, rows), rows // N_DEV, "adamw_small")
    for dst, packed in zip((out_g, out_d, out_m, out_v), res):
        dst.update(_unpack_small(packed, w, layout))

    return (loss, dx[None], *[out_g[n] for n in WEIGHTS], *[out_d[n] for n in WEIGHTS],
            *[out_m[n] for n in WEIGHTS], *[out_v[n] for n in WEIGHTS])
```

```python
import numpy as np
import jax
import jax.numpy as jnp
from jax import lax
from jax.experimental import pallas as pl
from jax.experimental.pallas import tpu as pltpu

F32, BF16 = jnp.float32, jnp.bfloat16
EPS = 1e-6
D_MODEL = 1024
ATT_HEADS, KV_HEADS, GQA_GROUP, HEAD_DIM = 8, 2, 4, 64
ATT_W, KV_W, SSM_W, IN_W = 512, 128, 512, 1280
V_END = 768
U0, Q0, K0, V0 = 0, 512, 1024, 1152
BLK = 128
SCALE = 0.125
SSM_G, SSM_H, SSM_P = 32, 16, 64
CH = 16
GW = CH * SSM_H
SEGS = 8
N_DEV = 8
NEG = float(np.finfo(np.float32).min)
SLOPES = tuple(2.0 ** (-8.0 * (h + 1) / ATT_HEADS) for h in range(ATT_HEADS))
VMEM_LIMIT = 56 * 1024 * 1024
TM = 512
TM_FFN = 1024

ADAM_LR, ADAM_B1, ADAM_B2, ADAM_EPS, ADAM_WD, ADAM_STEP = 0.001, 0.9, 0.999, 1e-08, 0.01, 10

SMALL = ("norm1", "q_gain", "k_gain", "sink", "lam_re", "lam_im", "log_dt", "b_re", "b_im",
         "c_re", "c_im", "d_skip", "norm2")
S5 = ("lam_re", "lam_im", "log_dt", "b_re", "b_im", "c_re", "c_im")
BIG = ("w_in", "w_glu", "w_out", "w_ff1", "w_ff2")
WEIGHTS = ("norm1", "w_in", "q_gain", "k_gain", "sink", "lam_re", "lam_im", "log_dt", "b_re", "b_im",
           "c_re", "c_im", "d_skip", "w_glu", "w_out", "norm2", "w_ff1", "w_ff2")
ADAM_ROWS = {160: 256, 128: 512, 512: 128, 1024: 64}


def _dot(a, b):
    return jnp.dot(a, b, preferred_element_type=F32)


def _dot_nt(a, b):
    return lax.dot_general(a, b, (((1,), (1,)), ((), ())), preferred_element_type=F32)


def _dot_tn(a, b):
    return lax.dot_general(a, b, (((0,), (0,)), ((), ())), preferred_element_type=F32)


def _rms(x):
    return lax.rsqrt(jnp.mean(x * x, axis=-1, keepdims=True) + EPS)


def _rms_bwd(xhat, r, dxhat):
    return r * (dxhat - xhat * jnp.mean(dxhat * xhat, axis=-1, keepdims=True))


def _sigmoid(x):
    return 1.0 / (1.0 + jnp.exp(-x))


_GC = 0.7978845608028654
_GA = 0.044715


def _gelu(x):
    return 0.5 * x * (1.0 + jnp.tanh(_GC * (x + _GA * x * x * x)))


def _gelu_grad(x):
    t = jnp.tanh(_GC * (x + _GA * x * x * x))
    return 0.5 * (1.0 + t) + 0.5 * x * (1.0 - t * t) * _GC * (1.0 + 3.0 * _GA * x * x)


def _row(i):
    return (i, 0)


def _fixed(i):
    return (0, 0)


def _me_and_peers():
    x, y, c = lax.axis_index("x"), lax.axis_index("y"), lax.axis_index("c")
    me = 4 * x + 2 * y + c
    peers = []
    for k in range(1, N_DEV):
        px = jnp.bitwise_xor(x, (k >> 2) & 1)
        py = jnp.bitwise_xor(y, (k >> 1) & 1)
        pc = jnp.bitwise_xor(c, k & 1)
        peers.append(((px, py, pc), 4 * px + 2 * py + pc))
    return me, peers


def _xch_copies(ins, outs, send_sems, recv_sems, loc_sems, scatter):
    me, peers = _me_and_peers()
    local, sends, recvs = [], [], []
    for a in range(len(ins)):
        local.append(pltpu.make_async_copy(ins[a].at[me] if scatter else ins[a], outs[a].at[me], loc_sems.at[a]))
    for k, (dev, idx) in enumerate(peers):
        for a in range(len(ins)):
            src = ins[a].at[idx] if scatter else ins[a]
            for dst, group in ((outs[a].at[me], sends), (outs[a].at[idx], recvs)):
                group.append(pltpu.make_async_remote_copy(
                    src_ref=src, dst_ref=dst, send_sem=send_sems.at[a, k], recv_sem=recv_sems.at[a, k],
                    device_id=dev, device_id_type=pl.DeviceIdType.MESH))
    return local, sends, recvs


def _xch_start(copies):
    local, sends, _ = copies
    for cp in local + sends:
        cp.start()


def _xch_wait(copies):
    local, sends, recvs = copies
    for cp in recvs:
        cp.wait_recv()
    for cp in sends:
        cp.wait_send()
    for cp in local:
        cp.wait()


def _xch_shapes(arrays, scatter):
    return [jax.ShapeDtypeStruct(a.shape if scatter else (N_DEV,) + a.shape, a.dtype) for a in arrays]


def _xch_sems(n):
    return [pltpu.SemaphoreType.DMA((n, N_DEV - 1)), pltpu.SemaphoreType.DMA((n, N_DEV - 1)),
            pltpu.SemaphoreType.DMA((n,))]


_ANY = pl.BlockSpec(memory_space=pl.ANY)


def exchange(arrays, scatter, name):
    n = len(arrays)

    def body(*refs):
        copies = _xch_copies(refs[:n], refs[n:2 * n], *refs[2 * n:], scatter)
        _xch_start(copies)
        _xch_wait(copies)

    return pl.pallas_call(
        body, name=name, in_specs=[_ANY] * n, out_specs=[_ANY] * n, out_shape=_xch_shapes(arrays, scatter),
        scratch_shapes=_xch_sems(n), compiler_params=pltpu.CompilerParams(has_side_effects=True),
    )(*arrays)


def _call(body, name, grid, in_specs, out_specs, out_shape, sem, inputs, scratch=(), xch=None):
    params = pltpu.CompilerParams(dimension_semantics=sem, vmem_limit_bytes=VMEM_LIMIT)
    if xch is None:
        out = pl.pallas_call(body, name=name, grid=grid, in_specs=in_specs, out_specs=out_specs, out_shape=out_shape,
                             scratch_shapes=list(scratch), compiler_params=params)(*inputs)
        return list(out), None
    arrays, scatter = xch
    n, nin, nout, nsc = len(arrays), len(in_specs), len(out_specs), len(scratch)

    def wrapped(*refs):
        ins, refs = refs[:nin], refs[nin:]
        xin, refs = refs[:n], refs[n:]
        outs, refs = refs[:nout], refs[nout:]
        xout, refs = refs[:n], refs[n:]
        sc, sems = refs[:nsc], refs[nsc:]
        first = last = None
        for ax, size in enumerate(grid):
            f, e = pl.program_id(ax) == 0, pl.program_id(ax) == size - 1
            first = f if first is None else jnp.logical_and(first, f)
            last = e if last is None else jnp.logical_and(last, e)

        @pl.when(first)
        def _():
            _xch_start(_xch_copies(xin, xout, *sems, scatter))

        body(*ins, *outs, *sc)

        @pl.when(last)
        def _():
            _xch_wait(_xch_copies(xin, xout, *sems, scatter))

    out = pl.pallas_call(
        wrapped, name=name, grid=grid, in_specs=list(in_specs) + [_ANY] * n, out_specs=list(out_specs) + [_ANY] * n,
        out_shape=list(out_shape) + _xch_shapes(arrays, scatter), scratch_shapes=list(scratch) + _xch_sems(n),
        compiler_params=params)(*inputs, *arrays)
    return list(out[:nout]), list(out[nout:])


def norm_matmul(x, gain, w, name):
    s, d = x.shape
    n = w.shape[1]

    def body(x_ref, g_ref, w_ref, z_ref):
        xv = x_ref[...]
        h = (xv * _rms(xv) * g_ref[...]).astype(BF16)
        z_ref[...] = _dot(h, w_ref[...])

    (z,), _ = _call(body, name, (s // TM,),
                    [pl.BlockSpec((TM, d), _row), pl.BlockSpec((1, d), _fixed), pl.BlockSpec((d, n), _fixed)],
                    [pl.BlockSpec((TM, n), _row)], [jax.ShapeDtypeStruct((s, n), F32)], ("parallel",), (x, gain, w))
    return z


def in_bwd(x, gain, w, du_a, du_b, dq, dk, dv, dres, name, xch=None):
    s, d = x.shape
    n = w.shape[1]

    def body(x_ref, g_ref, w_ref, dua_ref, dub_ref, dq_ref, dk_ref, dv_ref, dres_ref, dx_ref, dw_ref, dg_ref):
        @pl.when(pl.program_id(0) == 0)
        def _():
            dw_ref[...] = jnp.zeros_like(dw_ref)
            dg_ref[...] = jnp.zeros_like(dg_ref)

        xv = x_ref[...]
        r = _rms(xv)
        xhat = xv * r
        g = g_ref[...]
        h = (xhat * g).astype(BF16)
        dz = jnp.concatenate([(dua_ref[...] + _lanes4(dub_ref)).astype(BF16), dq_ref[...].astype(BF16),
                              dk_ref[...].astype(BF16), dv_ref[...].astype(BF16)], axis=1)
        dh = _dot_nt(dz, w_ref[...])
        dw_ref[...] += _dot_tn(h, dz)
        dg_ref[...] += jnp.sum(dh * xhat, axis=0, keepdims=True)
        dx_ref[...] = dres_ref[...] + _rms_bwd(xhat, r, dh * g)

    return _call(
        body, name, (s // TM,),
        [pl.BlockSpec((TM, d), _row), pl.BlockSpec((1, d), _fixed), pl.BlockSpec((d, n), _fixed),
         pl.BlockSpec((TM, SSM_W), _row), pl.BlockSpec((4, TM, 128), lambda i: (0, i, 0)), pl.BlockSpec((TM, ATT_W), _row),
         pl.BlockSpec((TM, KV_W), _row), pl.BlockSpec((TM, KV_W), _row), pl.BlockSpec((TM, d), _row)],
        [pl.BlockSpec((TM, d), _row), pl.BlockSpec((d, n), _fixed), pl.BlockSpec((1, d), _fixed)],
        [jax.ShapeDtypeStruct((s, d), F32), jax.ShapeDtypeStruct((d, n), F32), jax.ShapeDtypeStruct((1, d), F32)],
        ("arbitrary",), (x, gain, w, du_a, du_b, dq, dk, dv, dres), xch=xch)


def _band_specs(nb):
    def w0(i):
        return jnp.clip(i - 1, 0, nb - 3)

    specs = [pl.BlockSpec((None, KV_HEADS, GQA_GROUP * BLK, 3 * BLK), lambda i: (i - w0(i), 0, 0, 0)),
             pl.BlockSpec((BLK, ATT_W), lambda i: (i, Q0 // ATT_W))]
    for col in (K0 // KV_W, V0 // KV_W):
        specs += [pl.BlockSpec((BLK, KV_W), lambda i, c=col, o=o: (w0(i) + o, c)) for o in range(3)]
    return specs


def band_bias():
    off = jnp.arange(3).reshape(3, 1, 1, 1)
    row = jnp.arange(GQA_GROUP * BLK).reshape(1, 1, -1, 1)
    dist = jnp.abs(off * BLK + row % BLK - jnp.arange(3 * BLK).reshape(1, 1, 1, -1))
    slope = jnp.asarray(SLOPES, F32).reshape(1, KV_HEADS, GQA_GROUP, 1)
    slope = jnp.repeat(slope, BLK, axis=2)
    return jnp.where(dist <= BLK, -slope * dist.astype(F32), NEG)


def _stack_heads(x, j):
    return jnp.concatenate([x[:, h * HEAD_DIM:(h + 1) * HEAD_DIM] for h in range(j * GQA_GROUP, (j + 1) * GQA_GROUP)],
                           axis=0)


def _per_head(vals):
    head = lax.broadcasted_iota(jnp.int32, (GQA_GROUP * BLK, 1), 0) // BLK
    out = jnp.full((GQA_GROUP * BLK, 1), vals[GQA_GROUP - 1], F32)
    for g in range(GQA_GROUP - 2, -1, -1):
        out = jnp.where(head == g, vals[g], out)
    return out


def attn_fwd(z, bias, qg, kg, sink, name, xch=None):
    s = z.shape[0]
    nb = s // BLK

    def body(sink_ref, b_ref, q_ref, k0, k1, k2, v0, v1, v2, qg_ref, kg_ref, o_ref, lse_ref):
        k3 = jnp.concatenate([k0[...], k1[...], k2[...]], axis=0)
        v3 = jnp.concatenate([v0[...], v1[...], v2[...]], axis=0).astype(BF16)
        q = q_ref[...]
        for j in range(KV_HEADS):
            heads = range(j * GQA_GROUP, (j + 1) * GQA_GROUP)
            kj = k3[:, j * HEAD_DIM:(j + 1) * HEAD_DIM]
            knj = (kj * _rms(kj) * kg_ref[...]).astype(BF16)
            vj = v3[:, j * HEAD_DIM:(j + 1) * HEAD_DIM]
            q4 = _stack_heads(q, j)
            qs = (q4 * _rms(q4) * (qg_ref[...] * SCALE)).astype(BF16)
            sc = _dot_nt(qs, knj) + b_ref[j]
            sk = _per_head([sink_ref[h] for h in heads])
            m = jnp.maximum(jnp.max(sc, axis=-1, keepdims=True), sk)
            p = jnp.exp(sc - m)
            den = jnp.sum(p, axis=-1, keepdims=True) + jnp.exp(sk - m)
            o4 = _dot(p.astype(BF16), vj) * (1.0 / den)
            lse4 = m + jnp.log(den)
            for g, h in enumerate(heads):
                o_ref[:, h * HEAD_DIM:(h + 1) * HEAD_DIM] = o4[g * BLK:(g + 1) * BLK]
                lse_ref[:, h:h + 1] = lse4[g * BLK:(g + 1) * BLK]

    return _call(
        body, name, (nb,),
        [pl.BlockSpec(memory_space=pltpu.SMEM)] + _band_specs(nb)
        + [pl.BlockSpec((1, HEAD_DIM), _fixed), pl.BlockSpec((1, HEAD_DIM), _fixed)],
        [pl.BlockSpec((BLK, ATT_W), _row), pl.BlockSpec((BLK, ATT_HEADS), _row)],
        [jax.ShapeDtypeStruct((s, ATT_W), F32), jax.ShapeDtypeStruct((s, ATT_HEADS), F32)],
        ("arbitrary",), (sink, bias, z, z, z, z, z, z, z, qg, kg), xch=xch)


def attn_bwd(z, bias, att, datt, lse, qg, kg, sink, name, xch=None):
    s = z.shape[0]
    nb = s // BLK

    def body(sink_ref, b_ref, q_ref, k0, k1, k2, v0, v1, v2, o_ref, do_ref, lse_ref, qg_ref, kg_ref,
             dq_ref, dk_ref, dv_ref, dqg_ref, dkg_ref, dsk_ref):
        i = pl.program_id(0)

        @pl.when(i == 0)
        def _():
            dk_ref[...] = jnp.zeros_like(dk_ref)
            dv_ref[...] = jnp.zeros_like(dv_ref)
            dqg_ref[...] = jnp.zeros_like(dqg_ref)
            dkg_ref[...] = jnp.zeros_like(dkg_ref)
            dsk_ref[...] = jnp.zeros_like(dsk_ref)

        w0 = jnp.clip(i - 1, 0, nb - 3)
        k3 = jnp.concatenate([k0[...], k1[...], k2[...]], axis=0)
        v3 = jnp.concatenate([v0[...], v1[...], v2[...]], axis=0).astype(BF16)
        q = q_ref[...]
        o = o_ref[...]
        do = do_ref[...]
        lse = lse_ref[...]
        qgv = qg_ref[...]
        kgv = kg_ref[...]
        rows = pl.ds(pl.multiple_of(w0 * BLK, BLK), 3 * BLK)
        dqg = jnp.zeros((1, HEAD_DIM), F32)
        dkg = jnp.zeros((1, HEAD_DIM), F32)
        for j in range(KV_HEADS):
            heads = range(j * GQA_GROUP, (j + 1) * GQA_GROUP)
            cols = slice(j * HEAD_DIM, (j + 1) * HEAD_DIM)
            kj = k3[:, cols]
            rk = _rms(kj)
            khat = kj * rk
            knj = (khat * kgv).astype(BF16)
            vj = v3[:, cols]
            q4 = _stack_heads(q, j)
            rq = _rms(q4)
            qhat = q4 * rq
            qs = (qhat * (qgv * SCALE)).astype(BF16)
            sc = _dot_nt(qs, knj) + b_ref[j]
            lse4 = jnp.concatenate([lse[:, h:h + 1] for h in heads], axis=0)
            p = jnp.exp(sc - lse4)
            do4 = _stack_heads(do, j)
            delta = jnp.sum(do4 * _stack_heads(o, j), axis=-1, keepdims=True)
            dob = do4.astype(BF16)
            ds = p * (_dot_nt(dob, vj) - delta)
            sunk = jnp.exp(_per_head([sink_ref[h] for h in heads]) - lse4) * delta
            dsb = ds.astype(BF16)
            dvj = _dot_tn(p.astype(BF16), dob)
            dqn = _dot(dsb, knj) * SCALE
            dkn = _dot_tn(dsb, qs)
            dqg = dqg + jnp.sum(dqn * qhat, axis=0, keepdims=True)
            dq4 = _rms_bwd(qhat, rq, dqn * qgv)
            for g, h in enumerate(heads):
                dq_ref[:, h * HEAD_DIM:(h + 1) * HEAD_DIM] = dq4[g * BLK:(g + 1) * BLK]
                dsk_ref[:, h:h + 1] += -jnp.sum(sunk[g * BLK:(g + 1) * BLK], axis=0, keepdims=True)
            dkg = dkg + jnp.sum(dkn * khat, axis=0, keepdims=True)
            dk_ref[rows, cols] += _rms_bwd(khat, rk, dkn * kgv)
            dv_ref[rows, cols] += dvj
        dqg_ref[...] += dqg
        dkg_ref[...] += dkg

    return _call(
        body, name, (nb,),
        [pl.BlockSpec(memory_space=pltpu.SMEM)] + _band_specs(nb)
        + [pl.BlockSpec((BLK, ATT_W), _row), pl.BlockSpec((BLK, ATT_W), _row), pl.BlockSpec((BLK, ATT_HEADS), _row),
           pl.BlockSpec((1, HEAD_DIM), _fixed), pl.BlockSpec((1, HEAD_DIM), _fixed)],
        [pl.BlockSpec((BLK, ATT_W), _row), pl.BlockSpec((s, KV_W), _fixed), pl.BlockSpec((s, KV_W), _fixed),
         pl.BlockSpec((1, HEAD_DIM), _fixed), pl.BlockSpec((1, HEAD_DIM), _fixed), pl.BlockSpec((1, ATT_HEADS), _fixed)],
        [jax.ShapeDtypeStruct((s, ATT_W), F32), jax.ShapeDtypeStruct((s, KV_W), F32), jax.ShapeDtypeStruct((s, KV_W), F32),
         jax.ShapeDtypeStruct((1, HEAD_DIM), F32), jax.ShapeDtypeStruct((1, HEAD_DIM), F32),
         jax.ShapeDtypeStruct((1, ATT_HEADS), F32)],
        ("arbitrary",), (sink, bias, z, z, z, z, z, z, z, att, datt, lse, qg, kg), xch=xch)


def _group_steps(nc):
    nstep = nc // SEGS
    return nstep, min(32, nstep)


def _lanes4(ref):
    return jnp.concatenate([ref[q] for q in range(4)], axis=1)


def _pair_split(x4):
    return [jnp.concatenate([x4[:, q * 128 + r * SSM_P:q * 128 + (r + 1) * SSM_P] for q in range(4)], axis=1)
            for r in range(2)]


def _pair_merge(a0, a1):
    return [jnp.concatenate([a[:, q * SSM_P:(q + 1) * SSM_P] for a in (a0, a1)], axis=1) for q in range(4)]


def _block_transpose(vals, slot):
    vals = list(vals)
    for k in (4, 2, 1):
        low = (slot & k) == 0
        for i in range(8):
            if i & k:
                continue
            a, b = vals[i], vals[i + k]
            vals[i] = jnp.where(low, a, pltpu.roll(b, k * SSM_H, 1))
            vals[i + k] = jnp.where(low, pltpu.roll(a, 128 - k * SSM_H, 1), b)
    return vals


def to_groups(src, col, name):
    s, w = src.shape
    nc = s // CH
    nstep, sb = _group_steps(nc)

    def body(u0, u1, u2, u3, o_ref):
        slot = lax.broadcasted_iota(jnp.int32, (sb, 128), 1) // SSM_H
        for seg in range(SEGS):
            for vc, u_ref in enumerate((u0, u1, u2, u3)):
                for sh in range(2):
                    pieces = [u_ref[seg, pl.ds(sh * 8 + sl, sb, stride=CH), :] for sl in range(8)]
                    for gl, blk in enumerate(_block_transpose(pieces, slot)):
                        o_ref[(vc * 8 + gl) * 2 + sh, pl.ds(seg, sb, stride=SEGS), :] = blk

    src3 = src.reshape(SEGS, s // SEGS, w)
    (out,), _ = _call(body, name, (nstep // sb,),
                      [pl.BlockSpec((SEGS, sb * CH, 128), lambda i, c=col + vc: (0, i, c)) for vc in range(4)],
                      [pl.BlockSpec((2 * SSM_G, sb * SEGS, 128), lambda i: (0, i, 0))],
                      [jax.ShapeDtypeStruct((2 * SSM_G, nc, 128), F32)], ("parallel",), (src3,) * 4)
    return out


def from_groups(yc, name):
    nc = yc.shape[1]
    s = nc * CH
    nstep, sb = _group_steps(nc)

    def body(y_ref, o_ref):
        slot = lax.broadcasted_iota(jnp.int32, (sb, 128), 1) // SSM_H
        for seg in range(SEGS):
            for vc in range(4):
                for sh in range(2):
                    pieces = [y_ref[(vc * 8 + gl) * 2 + sh, pl.ds(seg, sb, stride=SEGS), :] for gl in range(8)]
                    for sl, blk in enumerate(_block_transpose(pieces, slot)):
                        o_ref[vc, seg, pl.ds(sh * 8 + sl, sb, stride=CH), :] = blk

    (out,), _ = _call(body, name, (nstep // sb,),
                      [pl.BlockSpec((2 * SSM_G, sb * SEGS, 128), lambda i: (0, i, 0))],
                      [pl.BlockSpec((4, SEGS, sb * CH, 128), lambda i: (0, 0, i, 0))],
                      [jax.ShapeDtypeStruct((4, SEGS, s // SEGS, 128), F32)], ("parallel",), (yc,))
    return out.reshape(4, s, 128)


def _pair3(i):
    return (i, 0, 0)


def _op_spec(l, rows, cols):
    return pl.BlockSpec((None, 2, rows, cols), lambda i: (l, i, 0, 0))


def _state_blk(i):
    return (0, 0, i)


def ssm_in(ug, e, l, name):
    nc = ug.shape[1]

    def body(u_ref, e_ref, s_ref):
        u = _lanes4(u_ref).astype(BF16)
        for q, blk in enumerate(_pair_merge(_dot(u[:, :GW], e_ref[0].astype(BF16)), _dot(u[:, GW:], e_ref[1].astype(BF16)))):
            s_ref[q] = blk

    (out,), _ = _call(body, name, (SSM_G // 2,),
                      [pl.BlockSpec((4, nc, 128), _pair3), _op_spec(l, GW, 4 * SSM_P)],
                      [pl.BlockSpec((4, nc, 128), _state_blk)],
                      [jax.ShapeDtypeStruct((4, nc, SSM_G * SSM_P), F32)], ("parallel",), (ug, e))
    return out


def chunk_scan(s4, a4, flip, name, xp4=None):
    _, nc, gp = s4.shape
    nstep = nc // SEGS
    assert nstep & (nstep - 1) == 0
    ct = 512
    with_da = xp4 is not None

    def body(*refs):
        if with_da:
            s_ref, a_ref, xp_ref, o_ref, da_ref = refs
        else:
            s_ref, a_ref, o_ref = refs
        rows = lax.broadcasted_iota(jnp.int32, (SEGS, ct), 0)
        zero = jnp.zeros((SEGS, ct), F32)
        for pair in range(2):
            asc = (pair == 0) != flip
            ir, ii = 2 * pair, 2 * pair + 1
            ar1 = a_ref[ir]
            ai1 = a_ref[ii]
            ar = jnp.broadcast_to(ar1, (SEGS, ct))
            ai = jnp.broadcast_to(ai1, (SEGS, ct))

            def tile(t):
                tt = t if asc else nstep - 1 - t
                return pl.ds(pl.multiple_of(tt * SEGS, SEGS), SEGS)

            def local(t, c):
                xr, xi = c
                sl = tile(t)
                return (ar * xr - ai * xi + s_ref[ir, sl, :], ar * xi + ai * xr + s_ref[ii, sl, :])

            er, ei = lax.fori_loop(0, nstep, local, (zero, zero))
            pr, pi = ar1, ai1
            for _ in range(nstep.bit_length() - 1):
                pr, pi = pr * pr - pi * pi, 2.0 * pr * pi
            cr = jnp.zeros((1, ct), F32)
            ci = jnp.zeros((1, ct), F32)
            xin_r, xin_i = zero, zero
            for k in range(SEGS):
                sg = k if asc else SEGS - 1 - k
                here = rows == sg
                xin_r = jnp.where(here, cr, xin_r)
                xin_i = jnp.where(here, ci, xin_i)
                lr = jnp.sum(jnp.where(here, er, 0.0), axis=0, keepdims=True)
                li = jnp.sum(jnp.where(here, ei, 0.0), axis=0, keepdims=True)
                cr, ci = pr * cr - pi * ci + lr, pr * ci + pi * cr + li

            def final(t, c):
                xr, xi, acr, aci = c
                sl = tile(t)
                o_ref[ir, sl, :] = xr
                o_ref[ii, sl, :] = xi
                if with_da:
                    br = xp_ref[ir, sl, :]
                    bi = xp_ref[ii, sl, :]
                    acr = acr + br * xr + bi * xi
                    aci = aci + br * xi - bi * xr
                return (ar * xr - ai * xi + s_ref[ir, sl, :], ar * xi + ai * xr + s_ref[ii, sl, :], acr, aci)

            _, _, acr, aci = lax.fori_loop(0, nstep, final, (xin_r, xin_i, zero, zero))
            if with_da:
                da_ref[ir] = jnp.sum(acr, axis=0, keepdims=True)
                da_ref[ii] = jnp.sum(aci, axis=0, keepdims=True)

    blk = pl.BlockSpec((4, nc, ct), _state_blk)
    ablk = pl.BlockSpec((4, 1, ct), _state_blk)
    sds = jax.ShapeDtypeStruct((4, nc, gp), F32)
    if with_da:
        out, _ = _call(body, name, (gp // ct,), [blk, ablk, blk], [blk, ablk],
                       [sds, jax.ShapeDtypeStruct((4, 1, gp), F32)], ("parallel",), (s4, a4, xp4))
        return out
    (out,), _ = _call(body, name, (gp // ct,), [blk, ablk], [blk], [sds], ("parallel",), (s4, a4))
    return out


def _state_cat(ref):
    return _lanes4(ref).astype(BF16)


def ssm_out(ug, t, xp4, o, l, name):
    nc = ug.shape[1]

    def body(u_ref, t_ref, xp_ref, o_ref, y_ref):
        xs = _pair_split(_lanes4(xp_ref))
        u = _lanes4(u_ref).astype(BF16)
        for r in range(2):
            y = _dot(u[:, r * GW:(r + 1) * GW], t_ref[r].astype(BF16)) + _dot(xs[r].astype(BF16), o_ref[r].astype(BF16))
            y_ref[2 * r] = y[:, :128]
            y_ref[2 * r + 1] = y[:, 128:]

    (out,), _ = _call(body, name, (SSM_G // 2,),
                      [pl.BlockSpec((4, nc, 128), _pair3), _op_spec(l, GW, GW),
                       pl.BlockSpec((4, nc, 128), _state_blk), _op_spec(l, 4 * SSM_P, GW)],
                      [pl.BlockSpec((4, nc, 128), _pair3)],
                      [jax.ShapeDtypeStruct((2 * SSM_G, nc, 128), F32)], ("parallel",), (ug, t, xp4, o))
    return out


def ssm_out_bwd(dyg, xp4, o, l, name):
    nc = dyg.shape[1]

    def body(dy_ref, xp_ref, o_ref, do_ref, dxp_ref):
        xs = _pair_split(_lanes4(xp_ref))
        dy = _lanes4(dy_ref).astype(BF16)
        dxs = []
        for r in range(2):
            dyr = dy[:, r * GW:(r + 1) * GW]
            do_ref[r] = _dot_tn(xs[r].astype(BF16), dyr)
            dxs.append(_dot_nt(dyr, o_ref[r].astype(BF16)))
        for q, blk in enumerate(_pair_merge(*dxs)):
            dxp_ref[q] = blk

    out, _ = _call(body, name, (SSM_G // 2,),
                   [pl.BlockSpec((4, nc, 128), _pair3), pl.BlockSpec((4, nc, 128), _state_blk),
                    _op_spec(l, 4 * SSM_P, GW)],
                   [pl.BlockSpec((2, 4 * SSM_P, GW), _pair3), pl.BlockSpec((4, nc, 128), _state_blk)],
                   [jax.ShapeDtypeStruct((SSM_G, 4 * SSM_P, GW), F32), jax.ShapeDtypeStruct((4, nc, SSM_G * SSM_P), F32)],
                   ("parallel",), (dyg, xp4, o))
    return out


def ssm_in_bwd(ug, dyg, ds4, t, e, l, name):
    nc = ug.shape[1]

    def body(u_ref, dy_ref, ds_ref, t_ref, e_ref, dt_ref, de_ref, du_ref):
        dss = _pair_split(_lanes4(ds_ref))
        u = _lanes4(u_ref).astype(BF16)
        dy = _lanes4(dy_ref).astype(BF16)
        for r in range(2):
            cols = slice(r * GW, (r + 1) * GW)
            ds = dss[r].astype(BF16)
            dt_ref[r] = _dot_tn(u[:, cols], dy[:, cols])
            de_ref[r] = _dot_tn(u[:, cols], ds)
            du = _dot_nt(dy[:, cols], t_ref[r].astype(BF16)) + _dot_nt(ds, e_ref[r].astype(BF16))
            du_ref[2 * r] = du[:, :128]
            du_ref[2 * r + 1] = du[:, 128:]

    out, _ = _call(body, name, (SSM_G // 2,),
                   [pl.BlockSpec((4, nc, 128), _pair3), pl.BlockSpec((4, nc, 128), _pair3),
                    pl.BlockSpec((4, nc, 128), _state_blk), _op_spec(l, GW, GW), _op_spec(l, GW, 4 * SSM_P)],
                   [pl.BlockSpec((2, GW, GW), _pair3), pl.BlockSpec((2, GW, 4 * SSM_P), _pair3),
                    pl.BlockSpec((4, nc, 128), _pair3)],
                   [jax.ShapeDtypeStruct((SSM_G, GW, GW), F32), jax.ShapeDtypeStruct((SSM_G, GW, 4 * SSM_P), F32),
                    jax.ShapeDtypeStruct((2 * SSM_G, nc, 128), F32)], ("parallel",), (ug, dyg, ds4, t, e))
    return out


def ssm_post_fwd(yc, z, dskip, wglu, name):
    s = yc.shape[1]

    def body(y_ref, u_ref, d_ref, w_ref, o_ref, yp_ref, g_ref):
        yp = _lanes4(y_ref) + d_ref[...] * u_ref[...]
        yp_ref[...] = yp
        gv = _dot(_gelu(yp).astype(BF16), w_ref[...])
        g_ref[...] = gv
        o_ref[...] = gv[:, :SSM_W] * _sigmoid(gv[:, SSM_W:])

    out, _ = _call(body, name, (s // TM,),
                   [pl.BlockSpec((4, TM, 128), lambda i: (0, i, 0)), pl.BlockSpec((TM, SSM_W), lambda i: (i, U0 // SSM_W)),
                    pl.BlockSpec((1, SSM_W), _fixed), pl.BlockSpec((SSM_W, 2 * SSM_W), _fixed)],
                   [pl.BlockSpec((TM, SSM_W), _row), pl.BlockSpec((TM, SSM_W), _row), pl.BlockSpec((TM, 2 * SSM_W), _row)],
                   [jax.ShapeDtypeStruct((s, SSM_W), F32), jax.ShapeDtypeStruct((s, SSM_W), F32),
                    jax.ShapeDtypeStruct((s, 2 * SSM_W), F32)], ("parallel",), (yc, z, dskip, wglu))
    return out


def ssm_post_bwd(dssm, gpre, ypre, z, dskip, wglu, name):
    s = dssm.shape[0]

    def body(do_ref, g_ref, yp_ref, u_ref, d_ref, w_ref, dy_ref, du_ref, dw_ref, dd_ref):
        @pl.when(pl.program_id(0) == 0)
        def _():
            dw_ref[...] = jnp.zeros_like(dw_ref)
            dd_ref[...] = jnp.zeros_like(dd_ref)

        gv = g_ref[...]
        val = gv[:, :SSM_W]
        sg = _sigmoid(gv[:, SSM_W:])
        do = do_ref[...]
        dg = jnp.concatenate([do * sg, do * val * sg * (1.0 - sg)], axis=1).astype(BF16)
        yp = yp_ref[...]
        dgl = _dot_nt(dg, w_ref[...])
        dw_ref[...] += _dot_tn(_gelu(yp).astype(BF16), dg)
        dyp = dgl * _gelu_grad(yp)
        dy_ref[...] = dyp
        du_ref[...] = dyp * d_ref[...]
        dd_ref[...] += jnp.sum(dyp * u_ref[...], axis=0, keepdims=True)

    out, _ = _call(body, name, (s // TM,),
                   [pl.BlockSpec((TM, SSM_W), _row), pl.BlockSpec((TM, 2 * SSM_W), _row), pl.BlockSpec((TM, SSM_W), _row),
                    pl.BlockSpec((TM, SSM_W), lambda i: (i, U0 // SSM_W)), pl.BlockSpec((1, SSM_W), _fixed),
                    pl.BlockSpec((SSM_W, 2 * SSM_W), _fixed)],
                   [pl.BlockSpec((TM, SSM_W), _row), pl.BlockSpec((TM, SSM_W), _row),
                    pl.BlockSpec((SSM_W, 2 * SSM_W), _fixed), pl.BlockSpec((1, SSM_W), _fixed)],
                   [jax.ShapeDtypeStruct((s, SSM_W), F32), jax.ShapeDtypeStruct((s, SSM_W), F32),
                    jax.ShapeDtypeStruct((SSM_W, 2 * SSM_W), F32), jax.ShapeDtypeStruct((1, SSM_W), F32)],
                   ("arbitrary",), (dssm, gpre, ypre, z, dskip, wglu))
    return out


def _toeplitz_select():
    row = lax.broadcasted_iota(jnp.int32, (GW, CH * GW), 0)
    col = lax.broadcasted_iota(jnp.int32, (GW, CH * GW), 1)
    j, h2 = row // SSM_H, row % SSM_H
    s, t, h = col // GW, (col % GW) // SSM_H, col % SSM_H
    same = h2 == h
    return jnp.concatenate([same & (t - s == j), same & (s - t == j)], axis=0).astype(F32)


def ssm_mats(lam_re, lam_im, log_dt, b_re, b_im, c_re, c_im):
    g, p, hh = SSM_G, SSM_P, SSM_H
    hp = lax.Precision.HIGHEST
    jj = jnp.arange(CH + 1, dtype=F32)
    dt = jnp.exp(log_dt)[..., None]
    mag = jnp.exp((lam_re * dt)[..., None] * jj)
    ang = (lam_im * dt)[..., None] * jj
    pr, pi = mag * jnp.cos(ang), mag * jnp.sin(ang)
    abr, abi = pr[..., 1], pi[..., 1]
    den = lam_re * lam_re + lam_im * lam_im
    zr = ((abr - 1.0) * lam_re + abi * lam_im) / den
    zi = (abi * lam_re - (abr - 1.0) * lam_im) / den
    bbr = zr[..., None] * b_re[None] - zi[..., None] * b_im[None]
    bbi = zr[..., None] * b_im[None] + zi[..., None] * b_re[None]
    crt, cit = c_re.transpose(0, 1, 3, 2), c_im.transpose(0, 1, 3, 2)
    car = pr[..., None] * crt[..., None, :] - pi[..., None] * cit[..., None, :]
    cai = pr[..., None] * cit[..., None, :] + pi[..., None] * crt[..., None, :]
    lhs = jnp.concatenate([bbr, -bbi], axis=2).transpose(0, 1, 3, 2)
    rhs = jnp.concatenate([car[..., :CH, :], cai[..., :CH, :]], axis=2).reshape(2, g, 2 * p, GW)
    kt = jnp.einsum("dgkp,dgpn->dgkn", lhs, rhs, precision=hp)
    kcat = jnp.concatenate([kt[0], kt[1]], axis=-1).reshape(g * hh, 2 * GW)
    tmat = jnp.dot(kcat, _toeplitz_select(), precision=hp)
    tmat = tmat.reshape(g, hh, CH, GW).transpose(0, 2, 1, 3).reshape(g, GW, GW)

    def e_part(d, pw_r, pw_i):
        pw_r, pw_i = pw_r.transpose(0, 2, 1)[:, :, None, :], pw_i.transpose(0, 2, 1)[:, :, None, :]
        br, bi = bbr[d].transpose(0, 2, 1)[:, None], bbi[d].transpose(0, 2, 1)[:, None]
        return [pw_r * br - pw_i * bi, pw_r * bi + pw_i * br]

    eparts = (e_part(0, pr[0, ..., :CH][..., ::-1], pi[0, ..., :CH][..., ::-1])
              + e_part(1, pr[1, ..., :CH], pi[1, ..., :CH]))
    emat = jnp.concatenate(eparts, axis=3).reshape(g, GW, 4 * p)

    oparts = [car[0, ..., 1:, :], -cai[0, ..., 1:, :], car[1, ..., 1:, :][..., ::-1, :], -cai[1, ..., 1:, :][..., ::-1, :]]
    omat = jnp.stack([v.reshape(g, p, GW) for v in oparts], axis=1).reshape(g, 4 * p, GW)
    amat = jnp.stack([pr[0, ..., CH], pi[0, ..., CH], pr[1, ..., CH], pi[1, ..., CH]], axis=0).reshape(4, 1, g * p)
    return tmat, emat, omat, amat


def outproj_fwd(x, att, ssm, wo, name):
    s, d = x.shape

    def body(x_ref, a_ref, s_ref, w_ref, o_ref):
        o_ref[...] = (x_ref[...] + _dot(a_ref[...].astype(BF16), w_ref[0:ATT_W, :])
                      + _dot(s_ref[...].astype(BF16), w_ref[ATT_W:, :]))

    (out,), _ = _call(body, name, (s // TM,),
                      [pl.BlockSpec((TM, d), _row), pl.BlockSpec((TM, ATT_W), _row), pl.BlockSpec((TM, SSM_W), _row),
                       pl.BlockSpec((ATT_W + SSM_W, d), _fixed)],
                      [pl.BlockSpec((TM, d), _row)], [jax.ShapeDtypeStruct((s, d), F32)], ("parallel",),
                      (x, att, ssm, wo))
    return out


def outproj_bwd(dx1, att, ssm, wo, name):
    s, d = dx1.shape

    def body(dx_ref, a_ref, s_ref, w_ref, da_ref, ds_ref, dw_ref):
        @pl.when(pl.program_id(0) == 0)
        def _():
            dw_ref[...] = jnp.zeros_like(dw_ref)

        dxb = dx_ref[...].astype(BF16)
        da_ref[...] = _dot_nt(dxb, w_ref[0:ATT_W, :])
        ds_ref[...] = _dot_nt(dxb, w_ref[ATT_W:, :])
        dw_ref[0:ATT_W, :] += _dot_tn(a_ref[...].astype(BF16), dxb)
        dw_ref[ATT_W:, :] += _dot_tn(s_ref[...].astype(BF16), dxb)

    out, _ = _call(body, name, (s // TM,),
                   [pl.BlockSpec((TM, d), _row), pl.BlockSpec((TM, ATT_W), _row), pl.BlockSpec((TM, SSM_W), _row),
                    pl.BlockSpec((ATT_W + SSM_W, d), _fixed)],
                   [pl.BlockSpec((TM, ATT_W), _row), pl.BlockSpec((TM, SSM_W), _row),
                    pl.BlockSpec((ATT_W + SSM_W, d), _fixed)],
                   [jax.ShapeDtypeStruct((s, ATT_W), F32), jax.ShapeDtypeStruct((s, SSM_W), F32),
                    jax.ShapeDtypeStruct((ATT_W + SSM_W, d), F32)], ("arbitrary",), (dx1, att, ssm, wo))
    return out


def ffn_fwd(x1, gain, w1, w2, name, xch=None):
    s, d = x1.shape
    nch, _, fc = w1.shape

    def body(x_ref, g_ref, w1_ref, w2_ref, o_ref, h_ref, a_ref):
        @pl.when(pl.program_id(1) == 0)
        def _():
            xv = x_ref[...]
            h_ref[...] = (xv * _rms(xv) * g_ref[...]).astype(BF16)
            o_ref[...] = xv

        a = _dot(h_ref[...], w1_ref[...])
        a_ref[...] = a.astype(BF16)
        o_ref[...] += _dot(jnp.square(jnp.maximum(a, 0.0)).astype(BF16), w2_ref[...])

    tm = min(TM_FFN, s)
    return _call(
        body, name, (s // tm, nch),
        [pl.BlockSpec((tm, d), lambda i, k: (i, 0)), pl.BlockSpec((1, d), lambda i, k: (0, 0)),
         pl.BlockSpec((None, d, fc), lambda i, k: (k, 0, 0)), pl.BlockSpec((None, fc, d), lambda i, k: (k, 0, 0))],
        [pl.BlockSpec((tm, d), lambda i, k: (i, 0)), pl.BlockSpec((tm, d), lambda i, k: (i, 0)),
         pl.BlockSpec((tm, fc), lambda i, k: (i, k))],
        [jax.ShapeDtypeStruct((s, d), F32), jax.ShapeDtypeStruct((s, d), BF16), jax.ShapeDtypeStruct((s, nch * fc), BF16)],
        ("arbitrary", "arbitrary"), (x1, gain, w1, w2), xch=xch)


def ffn_bwd_tok(dx2, x1, gain, a, w1, w2, name, xch=None):
    s, d = x1.shape
    nch, _, fc = w1.shape

    def body(dx_ref, x_ref, g_ref, a_ref, w1_ref, w2_ref, da_ref, dx1_ref, dg_ref, dxb_ref, dh_ref):
        i = pl.program_id(0)
        k = pl.program_id(1)

        @pl.when(jnp.logical_and(i == 0, k == 0))
        def _():
            dg_ref[...] = jnp.zeros_like(dg_ref)

        @pl.when(k == 0)
        def _():
            dxb_ref[...] = dx_ref[...].astype(BF16)
            dh_ref[...] = jnp.zeros_like(dh_ref)

        dr = _dot_nt(dxb_ref[...], w2_ref[...])
        da = (dr * (2.0 * jnp.maximum(a_ref[...].astype(F32), 0.0))).astype(BF16)
        da_ref[...] = da
        dh_ref[...] += _dot_nt(da, w1_ref[...])

        @pl.when(k == nch - 1)
        def _():
            xv = x_ref[...]
            r = _rms(xv)
            xhat = xv * r
            dh = dh_ref[...]
            dg_ref[...] += jnp.sum(dh * xhat, axis=0, keepdims=True)
            dx1_ref[...] = dx_ref[...] + _rms_bwd(xhat, r, dh * g_ref[...])

    tm = min(TM_FFN, s)
    return _call(
        body, name, (s // tm, nch),
        [pl.BlockSpec((tm, d), lambda i, k: (i, 0)), pl.BlockSpec((tm, d), lambda i, k: (i, 0)),
         pl.BlockSpec((1, d), lambda i, k: (0, 0)), pl.BlockSpec((tm, fc), lambda i, k: (i, k)),
         pl.BlockSpec((None, d, fc), lambda i, k: (k, 0, 0)), pl.BlockSpec((None, fc, d), lambda i, k: (k, 0, 0))],
        [pl.BlockSpec((tm, fc), lambda i, k: (i, k)), pl.BlockSpec((tm, d), lambda i, k: (i, 0)),
         pl.BlockSpec((1, d), lambda i, k: (0, 0)), pl.BlockSpec((tm, d), lambda i, k: (i, 0))],
        [jax.ShapeDtypeStruct((s, nch * fc), BF16), jax.ShapeDtypeStruct((s, d), F32),
         jax.ShapeDtypeStruct((1, d), F32), jax.ShapeDtypeStruct((s, d), BF16)],
        ("arbitrary", "arbitrary"), (dx2, x1, gain, a, w1, w2), scratch=[pltpu.VMEM((tm, d), F32)], xch=xch)


def ffn_bwd_w(h2, da, a, dxb, nch, name, xch=None):
    s, d = h2.shape
    fc = a.shape[1] // nch
    tm = min(TM_FFN, s)

    def body(h_ref, da_ref, a_ref, dx_ref, dw1_ref, dw2_ref, acc1_ref, acc2_ref):
        t = pl.program_id(1)

        @pl.when(t == 0)
        def _():
            acc1_ref[...] = jnp.zeros_like(acc1_ref)
            acc2_ref[...] = jnp.zeros_like(acc2_ref)

        acc1_ref[...] += _dot_tn(h_ref[...], da_ref[...])
        r = jnp.square(jnp.maximum(a_ref[...].astype(F32), 0.0)).astype(BF16)
        acc2_ref[...] += _dot_tn(r, dx_ref[...])

        @pl.when(t == s // tm - 1)
        def _():
            dw1_ref[...] = acc1_ref[...].astype(BF16)
            dw2_ref[...] = acc2_ref[...].astype(BF16)

    return _call(
        body, name, (nch, s // tm),
        [pl.BlockSpec((tm, d), lambda k, t: (t, 0)), pl.BlockSpec((tm, fc), lambda k, t: (t, k)),
         pl.BlockSpec((tm, fc), lambda k, t: (t, k)), pl.BlockSpec((tm, d), lambda k, t: (t, 0))],
        [pl.BlockSpec((None, d, fc), lambda k, t: (k, 0, 0)), pl.BlockSpec((None, fc, d), lambda k, t: (k, 0, 0))],
        [jax.ShapeDtypeStruct((nch, d, fc), BF16), jax.ShapeDtypeStruct((nch, fc, d), BF16)],
        ("arbitrary", "arbitrary"), (h2, da, a, dxb), scratch=[pltpu.VMEM((d, fc), F32), pltpu.VMEM((fc, d), F32)],
        xch=xch)


def loss_grad(xf, tgt, name):
    s, d = xf.shape
    nt = s // TM

    def body(x_ref, t_ref, dx_ref, l_ref, acc_ref):
        i = pl.program_id(0)

        @pl.when(i == 0)
        def _():
            acc_ref[...] = jnp.zeros_like(acc_ref)

        e = x_ref[...] - t_ref[...]
        dx_ref[...] = e * (1.0 / d)
        acc_ref[...] += jnp.sum(e * e, axis=0, keepdims=True)

        @pl.when(i == nt - 1)
        def _():
            l_ref[...] = jnp.sum(acc_ref[...], axis=1, keepdims=True) * (0.5 / d)

    out, _ = _call(body, name, (nt,), [pl.BlockSpec((TM, d), _row), pl.BlockSpec((TM, d), _row)],
                   [pl.BlockSpec((TM, d), _row), pl.BlockSpec((1, 1), _fixed)],
                   [jax.ShapeDtypeStruct((s, d), F32), jax.ShapeDtypeStruct((1, 1), F32)], ("arbitrary",),
                   (xf, tgt), scratch=[pltpu.VMEM((1, d), F32)])
    return out


def adamw_sum(parts, w, m, v, br, name):
    nl = len(parts)
    npart, r, c = parts[0].shape
    nb = r // br
    c1 = 1.0 - ADAM_B1 ** ADAM_STEP
    c2 = 1.0 - ADAM_B2 ** ADAM_STEP

    def body(*refs):
        p_refs = refs[:nl]
        w_ref, m_ref, v_ref, g_ref, d_ref, nm_ref, nv_ref = refs[nl:]
        for l in range(nl):
            @pl.when(pl.program_id(0) == l)
            def _(l=l):
                g = p_refs[l][0].astype(F32)
                for j in range(1, npart):
                    g = g + p_refs[l][j].astype(F32)
                m2 = ADAM_B1 * m_ref[...] + (1.0 - ADAM_B1) * g
                v2 = ADAM_B2 * v_ref[...] + (1.0 - ADAM_B2) * jnp.square(g)
                g_ref[...] = g
                nm_ref[...] = m2
                nv_ref[...] = v2
                d_ref[...] = -ADAM_LR * ((m2 / c1) / (jnp.sqrt(v2 / c2) + ADAM_EPS) + ADAM_WD * w_ref[...])

    blk = pl.BlockSpec((br, c), lambda l, i: (l * nb + i, 0))
    pspecs = [pl.BlockSpec((npart, br, c), lambda l, i, own=own: (0, jnp.where(l == own, i, 0), 0)) for own in range(nl)]
    sds = jax.ShapeDtypeStruct((nl * r, c), F32)
    out, _ = _call(body, name, (nl, nb), pspecs + [blk, blk, blk], [blk, blk, blk, blk], [sds, sds, sds, sds],
                   ("arbitrary", "arbitrary"), (*parts, w, m, v))
    return out


def sum_parts(parts, name):
    n, r, c = parts.shape

    def body(p_ref, o_ref):
        g = p_ref[0]
        for j in range(1, n):
            g = g + p_ref[j]
        o_ref[...] = g

    (out,), _ = _call(body, name, (1,), [pl.BlockSpec((n, r, c), lambda i: (0, 0, 0))], [pl.BlockSpec((r, c), _fixed)],
                      [jax.ShapeDtypeStruct((r, c), F32)], ("arbitrary",), (parts,))
    return out


def layer_fwd(x, w_in, p, mats, l, late, nxt):
    tmat, emat, omat, amat = mats
    z = norm_matmul(x, p["norm1"], w_in, f"in_fwd{l}")
    (att, lse), got = attn_fwd(z, p["bias"], p["q_gain"], p["k_gain"], p["sink"], f"attn_fwd{l}", (late, False))
    p = dict(p, **_whole_weights(dict(zip(BIG[-len(got):], got))))
    ug = to_groups(z, U0 // 128, f"ssm_to_groups{l}")
    s4 = ssm_in(ug, emat, l, f"ssm_in{l}")
    xp4 = chunk_scan(s4, amat, False, f"ssm_scan{l}")
    yc = from_groups(ssm_out(ug, tmat, xp4, omat, l, f"ssm_out{l}"), f"ssm_from_groups{l}")
    ssm, ypre, gpre = ssm_post_fwd(yc, z, p["d_skip"], p["w_glu"], f"ssm_post{l}")
    x1 = outproj_fwd(x, att, ssm, p["w_out"], f"out_fwd{l}")
    (x2, h2, a), gathered = ffn_fwd(x1, p["norm2"], p["w_ff1"], p["w_ff2"], f"ffn_fwd{l}",
                                    None if nxt is None else (nxt, False))
    saved = dict(x=x, z=z, att=att, lse=lse, ug=ug, xp4=xp4, ssm=ssm, ypre=ypre, gpre=gpre, x1=x1, h2=h2, a=a)
    return x2, saved, dict(p, w_in=w_in), gathered


def layer_bwd(dx2, p, mats, sv, l, above):
    tmat, emat, omat, amat = mats
    nch = p["w_ff1"].shape[0]
    last = l == 0
    (da, dx1, dnorm2, dxb), got_tok = ffn_bwd_tok(dx2, sv["x1"], p["norm2"], sv["a"], p["w_ff1"], p["w_ff2"],
                                                  f"ffn_bwd{l}", None if above is None else (above["tok"], True))
    (dw1, dw2), got_w = ffn_bwd_w(sv["h2"], da, sv["a"], dxb, nch, f"ffn_bwdw{l}",
                                  None if above is None else (above["w"], True))
    datt, dssm, dwo = outproj_bwd(dx1, sv["att"], sv["ssm"], p["w_out"], f"out_bwd{l}")
    dyc, du_skip, dwglu, ddskip = ssm_post_bwd(dssm, sv["gpre"], sv["ypre"], sv["z"], p["d_skip"], p["w_glu"],
                                               f"ssm_post_bwd{l}")
    dyg = to_groups(dyc, 0, f"ssm_to_groups_bwd{l}")
    domat, dxp4 = ssm_out_bwd(dyg, sv["xp4"], omat, l, f"ssm_out_bwd{l}")
    aconj = amat * jnp.array([1.0, -1.0, 1.0, -1.0], F32).reshape(4, 1, 1)
    ds4, damat = chunk_scan(dxp4, aconj, True, f"ssm_scan_bwd{l}", xp4=sv["xp4"])
    dtmat, demat, dug = ssm_in_bwd(sv["ug"], dyg, ds4, tmat, emat, l, f"ssm_in_bwd{l}")
    du_core = from_groups(dug, f"ssm_from_groups_bwd{l}")
    (dq, dk, dv, dqg, dkg, dsink), got_ff = attn_bwd(sv["z"], p["bias"], sv["att"], datt, sv["lse"], p["q_gain"], p["k_gain"],
                                                     p["sink"], f"attn_bwd{l}", ([dw1, dw2] if last else [dw1], True))
    blocks = _grad_blocks(dict(w_glu=dwglu, w_out=dwo))
    (dx, dwin, dnorm1), got_mix = in_bwd(sv["x"], p["norm1"], p["w_in"], du_skip, du_core, dq, dk, dv, dx1, f"in_bwd{l}",
                                         ([blocks["w_glu"], blocks["w_out"]], True) if last else None)
    grads = dict(norm1=dnorm1, q_gain=dqg, k_gain=dkg, sink=dsink, d_skip=ddskip, norm2=dnorm2)
    win_blocks = _grad_blocks(dict(w_in=dwin))["w_in"]
    mine = dict(w_ff1=got_ff[0])
    if last:
        mine.update(w_ff2=got_ff[1], w_glu=got_mix[0], w_out=got_mix[1])
        below = win_blocks
    else:
        below = dict(tok=[win_blocks, blocks["w_glu"], blocks["w_out"]], w=[dw2])
    theirs = {} if above is None else dict(w_in=got_tok[0], w_glu=got_tok[1], w_out=got_tok[2], w_ff2=got_w[0])
    return dx, grads, (dtmat, demat, domat, damat), mine, theirs, below


def _whole_weights(gathered):
    out = {}
    for n, g in gathered.items():
        if n == "w_in":
            w_in = g.transpose(1, 0, 2).reshape(D_MODEL, IN_W)
            out[n] = jnp.concatenate([w_in[:, V_END:], w_in[:, :V_END]], axis=1)
        elif n == "w_glu":
            out[n] = g.transpose(1, 0, 2).reshape(SSM_W, 2 * SSM_W)
        elif n == "w_out":
            out[n] = g.reshape(ATT_W + SSM_W, D_MODEL)
        else:
            out[n] = g
    return out


def _grad_blocks(grads):
    out = {}
    for n, g in grads.items():
        g = g.astype(BF16)
        if n == "w_in":
            g = jnp.concatenate([g[:, IN_W - V_END:], g[:, :IN_W - V_END]], axis=1)
            out[n] = g.reshape(D_MODEL, N_DEV, IN_W // N_DEV).transpose(1, 0, 2)
        elif n == "w_glu":
            out[n] = g.reshape(SSM_W, N_DEV, 2 * SSM_W // N_DEV).transpose(1, 0, 2)
        elif n == "w_out":
            out[n] = g.reshape(N_DEV, (ATT_W + SSM_W) // N_DEV, D_MODEL)
        else:
            out[n] = g
    return out


def _small_layout(like):
    layout, row = {}, 0
    for n in SMALL:
        size = int(np.prod(like[n].shape))
        nrow = -(-size // (8 * 128)) * 8
        layout[n] = (row, nrow, size)
        row += nrow
    return layout, -(-row // (8 * N_DEV)) * 8 * N_DEV


def _pack_small(vals, layout, rows):
    pieces, used = [], 0
    for n in SMALL:
        _, nrow, size = layout[n]
        flat = vals[n].reshape(-1).astype(F32)
        pieces.append(jnp.pad(flat, (0, nrow * 128 - size)).reshape(nrow, 128))
        used += nrow
    if rows > used:
        pieces.append(jnp.zeros((rows - used, 128), F32))
    return jnp.concatenate(pieces, axis=0)


def _unpack_small(packed, like, layout):
    out = {}
    for n in SMALL:
        row, nrow, size = layout[n]
        out[n] = packed[row:row + nrow].reshape(-1)[:size].reshape(like[n].shape)
    return out


def kernel(x, norm1, w_in, q_gain, k_gain, sink, lam_re, lam_im, log_dt, b_re, b_im, c_re, c_im, d_skip, w_glu, w_out, norm2, w_ff1, w_ff2, loss_target, m_norm1, m_w_in, m_q_gain, m_k_gain, m_sink, m_lam_re, m_lam_im, m_log_dt, m_b_re, m_b_im, m_c_re, m_c_im, m_d_skip, m_w_glu, m_w_out, m_norm2, m_w_ff1, m_w_ff2, v_norm1, v_w_in, v_q_gain, v_k_gain, v_sink, v_lam_re, v_lam_im, v_log_dt, v_b_re, v_b_im, v_c_re, v_c_im, v_d_skip, v_w_glu, v_w_out, v_norm2, v_w_ff1, v_w_ff2):
    w = dict(norm1=norm1, w_in=w_in, q_gain=q_gain, k_gain=k_gain, sink=sink, lam_re=lam_re, lam_im=lam_im,
             log_dt=log_dt, b_re=b_re, b_im=b_im, c_re=c_re, c_im=c_im, d_skip=d_skip, w_glu=w_glu, w_out=w_out,
             norm2=norm2, w_ff1=w_ff1, w_ff2=w_ff2)
    m = dict(norm1=m_norm1, w_in=m_w_in, q_gain=m_q_gain, k_gain=m_k_gain, sink=m_sink, lam_re=m_lam_re,
             lam_im=m_lam_im, log_dt=m_log_dt, b_re=m_b_re, b_im=m_b_im, c_re=m_c_re, c_im=m_c_im, d_skip=m_d_skip,
             w_glu=m_w_glu, w_out=m_w_out, norm2=m_norm2, w_ff1=m_w_ff1, w_ff2=m_w_ff2)
    v = dict(norm1=v_norm1, w_in=v_w_in, q_gain=v_q_gain, k_gain=v_k_gain, sink=v_sink, lam_re=v_lam_re,
             lam_im=v_lam_im, log_dt=v_log_dt, b_re=v_b_re, b_im=v_b_im, c_re=v_c_re, c_im=v_c_im, d_skip=v_d_skip,
             w_glu=v_w_glu, w_out=v_w_out, norm2=v_norm2, w_ff1=v_w_ff1, w_ff2=v_w_ff2)
    nl = w_in.shape[0]
    shards = [[w[n][l].astype(BF16) for n in BIG] for l in range(nl)]
    (tmat, emat, omat, amat), mats_vjp = jax.vjp(jax.vmap(ssm_mats), *[w[n] for n in S5])

    (g_in,) = exchange(shards[0][:1], False, "gather_w_in0")
    have = _whole_weights(dict(w_in=g_in))
    late = shards[0][1:]
    xs = x[0]
    bias = band_bias()
    saved, lp, lm = [], [], []
    for l in range(nl):
        p = {n: have[n] for n in have if n != "w_in"}
        for n in ("norm1", "q_gain", "k_gain", "d_skip", "norm2"):
            p[n] = w[n][l].reshape(1, -1)
        p["sink"] = sink[l]
        p["bias"] = bias
        mats = (tmat, emat, omat, amat[l])
        xs, sv, p, got = layer_fwd(xs, have["w_in"], p, mats, l, late, shards[l + 1][:4] if l + 1 < nl else None)
        if l + 1 < nl:
            have = _whole_weights(dict(zip(BIG[:4], got)))
            late = shards[l + 1][4:]
        saved.append(sv)
        lp.append(p)
        lm.append(mats)
    dx, loss_part = loss_grad(xs, loss_target[0], "loss")
    loss = lax.psum(loss_part[0, 0], ("x", "y", "c"))

    grads, dmats, parts = [None] * nl, [None] * nl, [dict() for _ in range(nl)]
    above = None
    for l in reversed(range(nl)):
        dx, grads[l], dmats[l], mine, theirs, above = layer_bwd(dx, lp[l], lm[l], saved[l], l, above)
        parts[l].update(mine)
        if l + 1 < nl:
            parts[l + 1].update(theirs)
    (parts[0]["w_in"],) = exchange([above], True, "exchange_g_in0")

    gs = {n: jnp.stack([grads[l][n].reshape(w[n].shape[1:]) for l in range(nl)])
          for n in ("norm1", "q_gain", "k_gain", "sink", "d_skip", "norm2")}
    ds5 = mats_vjp(tuple(jnp.stack([dmats[l][i] for l in range(nl)]) for i in range(4)))
    gs.update(zip(S5, ds5))
    layout, rows = _small_layout(w)
    (mine,) = exchange([_pack_small(gs, layout, rows).reshape(N_DEV, rows // N_DEV, 128)], True, "scatter_small_grads")
    (small_sum,) = exchange([sum_parts(mine, "sum_small_grads")], False, "gather_small_grads")

    out_g, out_d, out_m, out_v = {}, {}, {}, {}
    for n in BIG:
        c = w[n].shape[-1]
        r = int(np.prod(w[n].shape[:-1]))
        res = adamw_sum([parts[l][n] for l in range(nl)], w[n].reshape(r, c), m[n].reshape(r, c), v[n].reshape(r, c),
                        ADAM_ROWS[c], f"adamw_{n}")
        out_g[n], out_d[n], out_m[n], out_v[n] = (t.reshape(w[n].shape) for t in res)
    res = adamw_sum([small_sum.reshape(1, rows, 128)], _pack_small(w, layout, rows), _pack_small(m, layout, rows),
                    _pack_small(v, layout, rows), rows // N_DEV, "adamw_small")
    for dst, packed in zip((out_g, out_d, out_m, out_v), res):
        dst.update(_unpack_small(packed, w, layout))

    return (loss, dx[None], *[out_g[n] for n in WEIGHTS], *[out_d[n] for n in WEIGHTS],
            *[out_m[n] for n in WEIGHTS], *[out_v[n] for n in WEIGHTS])
```

```python
import numpy as np
import jax
import jax.numpy as jnp
from jax import lax
from jax.experimental import pallas as pl
from jax.experimental.pallas import tpu as pltpu

F32, BF16 = jnp.float32, jnp.bfloat16
EPS = 1e-6
D_MODEL = 1024
ATT_HEADS, KV_HEADS, GQA_GROUP, HEAD_DIM = 8, 2, 4, 64
ATT_W, KV_W, SSM_W, IN_W = 512, 128, 512, 1280
V_END = 768
U0, Q0, K0, V0 = 0, 512, 1024, 1152
BLK = 128
SCALE = 0.125
SSM_G, SSM_H, SSM_P = 32, 16, 64
CH = 16
GW = CH * SSM_H
SEGS = 8
N_DEV = 8
NEG = float(np.finfo(np.float32).min)
SLOPES = tuple(2.0 ** (-8.0 * (h + 1) / ATT_HEADS) for h in range(ATT_HEADS))
VMEM_LIMIT = 56 * 1024 * 1024
TM = 512
TM_FFN = 1024

ADAM_LR, ADAM_B1, ADAM_B2, ADAM_EPS, ADAM_WD, ADAM_STEP = 0.001, 0.9, 0.999, 1e-08, 0.01, 10

SMALL = ("norm1", "q_gain", "k_gain", "sink", "lam_re", "lam_im", "log_dt", "b_re", "b_im",
         "c_re", "c_im", "d_skip", "norm2")
S5 = ("lam_re", "lam_im", "log_dt", "b_re", "b_im", "c_re", "c_im")
BIG = ("w_in", "w_glu", "w_out", "w_ff1", "w_ff2")
WEIGHTS = ("norm1", "w_in", "q_gain", "k_gain", "sink", "lam_re", "lam_im", "log_dt", "b_re", "b_im",
           "c_re", "c_im", "d_skip", "w_glu", "w_out", "norm2", "w_ff1", "w_ff2")
ADAM_ROWS = {160: 256, 128: 512, 512: 128, 1024: 64}


def _dot(a, b):
    return jnp.dot(a, b, preferred_element_type=F32)


def _dot_nt(a, b):
    return lax.dot_general(a, b, (((1,), (1,)), ((), ())), preferred_element_type=F32)


def _dot_tn(a, b):
    return lax.dot_general(a, b, (((0,), (0,)), ((), ())), preferred_element_type=F32)


def _rms(x):
    return lax.rsqrt(jnp.mean(x * x, axis=-1, keepdims=True) + EPS)


def _rms_bwd(xhat, r, dxhat):
    return r * (dxhat - xhat * jnp.mean(dxhat * xhat, axis=-1, keepdims=True))


def _sigmoid(x):
    return 1.0 / (1.0 + jnp.exp(-x))


_GC = 0.7978845608028654
_GA = 0.044715


def _gelu(x):
    return 0.5 * x * (1.0 + jnp.tanh(_GC * (x + _GA * x * x * x)))


def _gelu_grad(x):
    t = jnp.tanh(_GC * (x + _GA * x * x * x))
    return 0.5 * (1.0 + t) + 0.5 * x * (1.0 - t * t) * _GC * (1.0 + 3.0 * _GA * x * x)


def _row(i):
    return (i, 0)


def _fixed(i):
    return (0, 0)


def _me_and_peers():
    x, y, c = lax.axis_index("x"), lax.axis_index("y"), lax.axis_index("c")
    me = 4 * x + 2 * y + c
    peers = []
    for k in range(1, N_DEV):
        px = jnp.bitwise_xor(x, (k >> 2) & 1)
        py = jnp.bitwise_xor(y, (k >> 1) & 1)
        pc = jnp.bitwise_xor(c, k & 1)
        peers.append(((px, py, pc), 4 * px + 2 * py + pc))
    return me, peers


def _xch_copies(ins, outs, send_sems, recv_sems, loc_sems, scatter):
    me, peers = _me_and_peers()
    local, sends, recvs = [], [], []
    for a in range(len(ins)):
        local.append(pltpu.make_async_copy(ins[a].at[me] if scatter else ins[a], outs[a].at[me], loc_sems.at[a]))
    for k, (dev, idx) in enumerate(peers):
        for a in range(len(ins)):
            src = ins[a].at[idx] if scatter else ins[a]
            for dst, group in ((outs[a].at[me], sends), (outs[a].at[idx], recvs)):
                group.append(pltpu.make_async_remote_copy(
                    src_ref=src, dst_ref=dst, send_sem=send_sems.at[a, k], recv_sem=recv_sems.at[a, k],
                    device_id=dev, device_id_type=pl.DeviceIdType.MESH))
    return local, sends, recvs


def _xch_start(copies):
    local, sends, _ = copies
    for cp in local + sends:
        cp.start()


def _xch_wait(copies):
    local, sends, recvs = copies
    for cp in recvs:
        cp.wait_recv()
    for cp in sends:
        cp.wait_send()
    for cp in local:
        cp.wait()


def _xch_shapes(arrays, scatter):
    return [jax.ShapeDtypeStruct(a.shape if scatter else (N_DEV,) + a.shape, a.dtype) for a in arrays]


def _xch_sems(n):
    return [pltpu.SemaphoreType.DMA((n, N_DEV - 1)), pltpu.SemaphoreType.DMA((n, N_DEV - 1)),
            pltpu.SemaphoreType.DMA((n,))]


_ANY = pl.BlockSpec(memory_space=pl.ANY)


def exchange(arrays, scatter, name):
    n = len(arrays)

    def body(*refs):
        copies = _xch_copies(refs[:n], refs[n:2 * n], *refs[2 * n:], scatter)
        _xch_start(copies)
        _xch_wait(copies)

    return pl.pallas_call(
        body, name=name, in_specs=[_ANY] * n, out_specs=[_ANY] * n, out_shape=_xch_shapes(arrays, scatter),
        scratch_shapes=_xch_sems(n), compiler_params=pltpu.CompilerParams(has_side_effects=True),
    )(*arrays)


def _call(body, name, grid, in_specs, out_specs, out_shape, sem, inputs, scratch=(), xch=None):
    params = pltpu.CompilerParams(dimension_semantics=sem, vmem_limit_bytes=VMEM_LIMIT)
    if xch is None:
        out = pl.pallas_call(body, name=name, grid=grid, in_specs=in_specs, out_specs=out_specs, out_shape=out_shape,
                             scratch_shapes=list(scratch), compiler_params=params)(*inputs)
        return list(out), None
    arrays, scatter = xch
    n, nin, nout, nsc = len(arrays), len(in_specs), len(out_specs), len(scratch)

    def wrapped(*refs):
        ins, refs = refs[:nin], refs[nin:]
        xin, refs = refs[:n], refs[n:]
        outs, refs = refs[:nout], refs[nout:]
        xout, refs = refs[:n], refs[n:]
        sc, sems = refs[:nsc], refs[nsc:]
        first = last = None
        for ax, size in enumerate(grid):
            f, e = pl.program_id(ax) == 0, pl.program_id(ax) == size - 1
            first = f if first is None else jnp.logical_and(first, f)
            last = e if last is None else jnp.logical_and(last, e)

        @pl.when(first)
        def _():
            _xch_start(_xch_copies(xin, xout, *sems, scatter))

        body(*ins, *outs, *sc)

        @pl.when(last)
        def _():
            _xch_wait(_xch_copies(xin, xout, *sems, scatter))

    out = pl.pallas_call(
        wrapped, name=name, grid=grid, in_specs=list(in_specs) + [_ANY] * n, out_specs=list(out_specs) + [_ANY] * n,
        out_shape=list(out_shape) + _xch_shapes(arrays, scatter), scratch_shapes=list(scratch) + _xch_sems(n),
        compiler_params=params)(*inputs, *arrays)
    return list(out[:nout]), list(out[nout:])


def norm_matmul(x, gain, w, name):
    s, d = x.shape
    n = w.shape[1]

    def body(x_ref, g_ref, w_ref, z_ref):
        xv = x_ref[...]
        h = (xv * _rms(xv) * g_ref[...]).astype(BF16)
        z_ref[...] = _dot(h, w_ref[...])

    (z,), _ = _call(body, name, (s // TM,),
                    [pl.BlockSpec((TM, d), _row), pl.BlockSpec((1, d), _fixed), pl.BlockSpec((d, n), _fixed)],
                    [pl.BlockSpec((TM, n), _row)], [jax.ShapeDtypeStruct((s, n), F32)], ("parallel",), (x, gain, w))
    return z


def in_bwd(x, gain, w, du_a, du_b, dq, dk, dv, dres, name, xch=None):
    s, d = x.shape
    n = w.shape[1]

    def body(x_ref, g_ref, w_ref, dua_ref, dub_ref, dq_ref, dk_ref, dv_ref, dres_ref, dx_ref, dw_ref, dg_ref):
        @pl.when(pl.program_id(0) == 0)
        def _():
            dw_ref[...] = jnp.zeros_like(dw_ref)
            dg_ref[...] = jnp.zeros_like(dg_ref)

        xv = x_ref[...]
        r = _rms(xv)
        xhat = xv * r
        g = g_ref[...]
        h = (xhat * g).astype(BF16)
        dz = jnp.concatenate([(dua_ref[...] + _lanes4(dub_ref)).astype(BF16), dq_ref[...].astype(BF16),
                              dk_ref[...].astype(BF16), dv_ref[...].astype(BF16)], axis=1)
        dh = _dot_nt(dz, w_ref[...])
        dw_ref[...] += _dot_tn(h, dz)
        dg_ref[...] += jnp.sum(dh * xhat, axis=0, keepdims=True)
        dx_ref[...] = dres_ref[...] + _rms_bwd(xhat, r, dh * g)

    return _call(
        body, name, (s // TM,),
        [pl.BlockSpec((TM, d), _row), pl.BlockSpec((1, d), _fixed), pl.BlockSpec((d, n), _fixed),
         pl.BlockSpec((TM, SSM_W), _row), pl.BlockSpec((4, TM, 128), lambda i: (0, i, 0)), pl.BlockSpec((TM, ATT_W), _row),
         pl.BlockSpec((TM, KV_W), _row), pl.BlockSpec((TM, KV_W), _row), pl.BlockSpec((TM, d), _row)],
        [pl.BlockSpec((TM, d), _row), pl.BlockSpec((d, n), _fixed), pl.BlockSpec((1, d), _fixed)],
        [jax.ShapeDtypeStruct((s, d), F32), jax.ShapeDtypeStruct((d, n), F32), jax.ShapeDtypeStruct((1, d), F32)],
        ("arbitrary",), (x, gain, w, du_a, du_b, dq, dk, dv, dres), xch=xch)


def _band_specs(nb):
    def w0(i):
        return jnp.clip(i - 1, 0, nb - 3)

    specs = [pl.BlockSpec((None, KV_HEADS, GQA_GROUP * BLK, 3 * BLK), lambda i: (i - w0(i), 0, 0, 0)),
             pl.BlockSpec((BLK, ATT_W), lambda i: (i, Q0 // ATT_W))]
    for col in (K0 // KV_W, V0 // KV_W):
        specs += [pl.BlockSpec((BLK, KV_W), lambda i, c=col, o=o: (w0(i) + o, c)) for o in range(3)]
    return specs


def band_bias():
    off = jnp.arange(3).reshape(3, 1, 1, 1)
    row = jnp.arange(GQA_GROUP * BLK).reshape(1, 1, -1, 1)
    dist = jnp.abs(off * BLK + row % BLK - jnp.arange(3 * BLK).reshape(1, 1, 1, -1))
    slope = jnp.asarray(SLOPES, F32).reshape(1, KV_HEADS, GQA_GROUP, 1)
    slope = jnp.repeat(slope, BLK, axis=2)
    return jnp.where(dist <= BLK, -slope * dist.astype(F32), NEG)


def _stack_heads(x, j):
    return jnp.concatenate([x[:, h * HEAD_DIM:(h + 1) * HEAD_DIM] for h in range(j * GQA_GROUP, (j + 1) * GQA_GROUP)],
                           axis=0)


def _per_head(vals):
    head = lax.broadcasted_iota(jnp.int32, (GQA_GROUP * BLK, 1), 0) // BLK
    out = jnp.full((GQA_GROUP * BLK, 1), vals[GQA_GROUP - 1], F32)
    for g in range(GQA_GROUP - 2, -1, -1):
        out = jnp.where(head == g, vals[g], out)
    return out


def attn_fwd(z, bias, qg, kg, sink, name, xch=None):
    s = z.shape[0]
    nb = s // BLK

    def body(sink_ref, b_ref, q_ref, k0, k1, k2, v0, v1, v2, qg_ref, kg_ref, o_ref, lse_ref):
        k3 = jnp.concatenate([k0[...], k1[...], k2[...]], axis=0)
        v3 = jnp.concatenate([v0[...], v1[...], v2[...]], axis=0).astype(BF16)
        q = q_ref[...]
        for j in range(KV_HEADS):
            heads = range(j * GQA_GROUP, (j + 1) * GQA_GROUP)
            kj = k3[:, j * HEAD_DIM:(j + 1) * HEAD_DIM]
            knj = (kj * _rms(kj) * kg_ref[...]).astype(BF16)
            vj = v3[:, j * HEAD_DIM:(j + 1) * HEAD_DIM]
            q4 = _stack_heads(q, j)
            qs = (q4 * _rms(q4) * (qg_ref[...] * SCALE)).astype(BF16)
            sc = _dot_nt(qs, knj) + b_ref[j]
            sk = _per_head([sink_ref[h] for h in heads])
            m = jnp.maximum(jnp.max(sc, axis=-1, keepdims=True), sk)
            p = jnp.exp(sc - m)
            den = jnp.sum(p, axis=-1, keepdims=True) + jnp.exp(sk - m)
            o4 = _dot(p.astype(BF16), vj) * (1.0 / den)
            lse4 = m + jnp.log(den)
            for g, h in enumerate(heads):
                o_ref[:, h * HEAD_DIM:(h + 1) * HEAD_DIM] = o4[g * BLK:(g + 1) * BLK]
                lse_ref[:, h:h + 1] = lse4[g * BLK:(g + 1) * BLK]

    return _call(
        body, name, (nb,),
        [pl.BlockSpec(memory_space=pltpu.SMEM)] + _band_specs(nb)
        + [pl.BlockSpec((1, HEAD_DIM), _fixed), pl.BlockSpec((1, HEAD_DIM), _fixed)],
        [pl.BlockSpec((BLK, ATT_W), _row), pl.BlockSpec((BLK, ATT_HEADS), _row)],
        [jax.ShapeDtypeStruct((s, ATT_W), F32), jax.ShapeDtypeStruct((s, ATT_HEADS), F32)],
        ("arbitrary",), (sink, bias, z, z, z, z, z, z, z, qg, kg), xch=xch)


def attn_bwd(z, bias, att, datt, lse, qg, kg, sink, name, xch=None):
    s = z.shape[0]
    nb = s // BLK

    def body(sink_ref, b_ref, q_ref, k0, k1, k2, v0, v1, v2, o_ref, do_ref, lse_ref, qg_ref, kg_ref,
             dq_ref, dk_ref, dv_ref, dqg_ref, dkg_ref, dsk_ref):
        i = pl.program_id(0)

        @pl.when(i == 0)
        def _():
            dk_ref[...] = jnp.zeros_like(dk_ref)
            dv_ref[...] = jnp.zeros_like(dv_ref)
            dqg_ref[...] = jnp.zeros_like(dqg_ref)
            dkg_ref[...] = jnp.zeros_like(dkg_ref)
            dsk_ref[...] = jnp.zeros_like(dsk_ref)

        w0 = jnp.clip(i - 1, 0, nb - 3)
        k3 = jnp.concatenate([k0[...], k1[...], k2[...]], axis=0)
        v3 = jnp.concatenate([v0[...], v1[...], v2[...]], axis=0).astype(BF16)
        q = q_ref[...]
        o = o_ref[...]
        do = do_ref[...]
        lse = lse_ref[...]
        qgv = qg_ref[...]
        kgv = kg_ref[...]
        rows = pl.ds(pl.multiple_of(w0 * BLK, BLK), 3 * BLK)
        dqg = jnp.zeros((1, HEAD_DIM), F32)
        dkg = jnp.zeros((1, HEAD_DIM), F32)
        for j in range(KV_HEADS):
            heads = range(j * GQA_GROUP, (j + 1) * GQA_GROUP)
            cols = slice(j * HEAD_DIM, (j + 1) * HEAD_DIM)
            kj = k3[:, cols]
            rk = _rms(kj)
            khat = kj * rk
            knj = (khat * kgv).astype(BF16)
            vj = v3[:, cols]
            q4 = _stack_heads(q, j)
            rq = _rms(q4)
            qhat = q4 * rq
            qs = (qhat * (qgv * SCALE)).astype(BF16)
            sc = _dot_nt(qs, knj) + b_ref[j]
            lse4 = jnp.concatenate([lse[:, h:h + 1] for h in heads], axis=0)
            p = jnp.exp(sc - lse4)
            do4 = _stack_heads(do, j)
            delta = jnp.sum(do4 * _stack_heads(o, j), axis=-1, keepdims=True)
            dob = do4.astype(BF16)
            ds = p * (_dot_nt(dob, vj) - delta)
            sunk = jnp.exp(_per_head([sink_ref[h] for h in heads]) - lse4) * delta
            dsb = ds.astype(BF16)
            dvj = _dot_tn(p.astype(BF16), dob)
            dqn = _dot(dsb, knj) * SCALE
            dkn = _dot_tn(dsb, qs)
            dqg = dqg + jnp.sum(dqn * qhat, axis=0, keepdims=True)
            dq4 = _rms_bwd(qhat, rq, dqn * qgv)
            for g, h in enumerate(heads):
                dq_ref[:, h * HEAD_DIM:(h + 1) * HEAD_DIM] = dq4[g * BLK:(g + 1) * BLK]
                dsk_ref[:, h:h + 1] += -jnp.sum(sunk[g * BLK:(g + 1) * BLK], axis=0, keepdims=True)
            dkg = dkg + jnp.sum(dkn * khat, axis=0, keepdims=True)
            dk_ref[rows, cols] += _rms_bwd(khat, rk, dkn * kgv)
            dv_ref[rows, cols] += dvj
        dqg_ref[...] += dqg
        dkg_ref[...] += dkg

    return _call(
        body, name, (nb,),
        [pl.BlockSpec(memory_space=pltpu.SMEM)] + _band_specs(nb)
        + [pl.BlockSpec((BLK, ATT_W), _row), pl.BlockSpec((BLK, ATT_W), _row), pl.BlockSpec((BLK, ATT_HEADS), _row),
           pl.BlockSpec((1, HEAD_DIM), _fixed), pl.BlockSpec((1, HEAD_DIM), _fixed)],
        [pl.BlockSpec((BLK, ATT_W), _row), pl.BlockSpec((s, KV_W), _fixed), pl.BlockSpec((s, KV_W), _fixed),
         pl.BlockSpec((1, HEAD_DIM), _fixed), pl.BlockSpec((1, HEAD_DIM), _fixed), pl.BlockSpec((1, ATT_HEADS), _fixed)],
        [jax.ShapeDtypeStruct((s, ATT_W), F32), jax.ShapeDtypeStruct((s, KV_W), F32), jax.ShapeDtypeStruct((s, KV_W), F32),
         jax.ShapeDtypeStruct((1, HEAD_DIM), F32), jax.ShapeDtypeStruct((1, HEAD_DIM), F32),
         jax.ShapeDtypeStruct((1, ATT_HEADS), F32)],
        ("arbitrary",), (sink, bias, z, z, z, z, z, z, z, att, datt, lse, qg, kg), xch=xch)


def _group_steps(nc):
    nstep = nc // SEGS
    return nstep, min(32, nstep)


def _lanes4(ref):
    return jnp.concatenate([ref[q] for q in range(4)], axis=1)


def _pair_split(x4):
    return [jnp.concatenate([x4[:, q * 128 + r * SSM_P:q * 128 + (r + 1) * SSM_P] for q in range(4)], axis=1)
            for r in range(2)]


def _pair_merge(a0, a1):
    return [jnp.concatenate([a[:, q * SSM_P:(q + 1) * SSM_P] for a in (a0, a1)], axis=1) for q in range(4)]


def _block_transpose(vals, slot):
    vals = list(vals)
    for k in (4, 2, 1):
        low = (slot & k) == 0
        for i in range(8):
            if i & k:
                continue
            a, b = vals[i], vals[i + k]
            vals[i] = jnp.where(low, a, pltpu.roll(b, k * SSM_H, 1))
            vals[i + k] = jnp.where(low, pltpu.roll(a, 128 - k * SSM_H, 1), b)
    return vals


def to_groups(src, col, name):
    s, w = src.shape
    nc = s // CH
    nstep, sb = _group_steps(nc)

    def body(u0, u1, u2, u3, o_ref):
        slot = lax.broadcasted_iota(jnp.int32, (sb, 128), 1) // SSM_H
        for seg in range(SEGS):
            for vc, u_ref in enumerate((u0, u1, u2, u3)):
                for sh in range(2):
                    pieces = [u_ref[seg, pl.ds(sh * 8 + sl, sb, stride=CH), :] for sl in range(8)]
                    for gl, blk in enumerate(_block_transpose(pieces, slot)):
                        o_ref[(vc * 8 + gl) * 2 + sh, pl.ds(seg, sb, stride=SEGS), :] = blk

    src3 = src.reshape(SEGS, s // SEGS, w)
    (out,), _ = _call(body, name, (nstep // sb,),
                      [pl.BlockSpec((SEGS, sb * CH, 128), lambda i, c=col + vc: (0, i, c)) for vc in range(4)],
                      [pl.BlockSpec((2 * SSM_G, sb * SEGS, 128), lambda i: (0, i, 0))],
                      [jax.ShapeDtypeStruct((2 * SSM_G, nc, 128), F32)], ("parallel",), (src3,) * 4)
    return out


def from_groups(yc, name):
    nc = yc.shape[1]
    s = nc * CH
    nstep, sb = _group_steps(nc)

    def body(y_ref, o_ref):
        slot = lax.broadcasted_iota(jnp.int32, (sb, 128), 1) // SSM_H
        for seg in range(SEGS):
            for vc in range(4):
                for sh in range(2):
                    pieces = [y_ref[(vc * 8 + gl) * 2 + sh, pl.ds(seg, sb, stride=SEGS), :] for gl in range(8)]
                    for sl, blk in enumerate(_block_transpose(pieces, slot)):
                        o_ref[vc, seg, pl.ds(sh * 8 + sl, sb, stride=CH), :] = blk

    (out,), _ = _call(body, name, (nstep // sb,),
                      [pl.BlockSpec((2 * SSM_G, sb * SEGS, 128), lambda i: (0, i, 0))],
                      [pl.BlockSpec((4, SEGS, sb * CH, 128), lambda i: (0, 0, i, 0))],
                      [jax.ShapeDtypeStruct((4, SEGS, s // SEGS, 128), F32)], ("parallel",), (yc,))
    return out.reshape(4, s, 128)


def _pair3(i):
    return (i, 0, 0)


def _op_spec(l, rows, cols):
    return pl.BlockSpec((None, 2, rows, cols), lambda i: (l, i, 0, 0))


def _state_blk(i):
    return (0, 0, i)


def ssm_in(ug, e, l, name):
    nc = ug.shape[1]

    def body(u_ref, e_ref, s_ref):
        u = _lanes4(u_ref).astype(BF16)
        for q, blk in enumerate(_pair_merge(_dot(u[:, :GW], e_ref[0].astype(BF16)), _dot(u[:, GW:], e_ref[1].astype(BF16)))):
            s_ref[q] = blk

    (out,), _ = _call(body, name, (SSM_G // 2,),
                      [pl.BlockSpec((4, nc, 128), _pair3), _op_spec(l, GW, 4 * SSM_P)],
                      [pl.BlockSpec((4, nc, 128), _state_blk)],
                      [jax.ShapeDtypeStruct((4, nc, SSM_G * SSM_P), F32)], ("parallel",), (ug, e))
    return out


def chunk_scan(s4, a4, flip, name, xp4=None):
    _, nc, gp = s4.shape
    nstep = nc // SEGS
    assert nstep & (nstep - 1) == 0
    ct = 512
    with_da = xp4 is not None

    def body(*refs):
        if with_da:
            s_ref, a_ref, xp_ref, o_ref, da_ref = refs
        else:
            s_ref, a_ref, o_ref = refs
        rows = lax.broadcasted_iota(jnp.int32, (SEGS, ct), 0)
        zero = jnp.zeros((SEGS, ct), F32)
        for pair in range(2):
            asc = (pair == 0) != flip
            ir, ii = 2 * pair, 2 * pair + 1
            ar1 = a_ref[ir]
            ai1 = a_ref[ii]
            ar = jnp.broadcast_to(ar1, (SEGS, ct))
            ai = jnp.broadcast_to(ai1, (SEGS, ct))

            def tile(t):
                tt = t if asc else nstep - 1 - t
                return pl.ds(pl.multiple_of(tt * SEGS, SEGS), SEGS)

            def local(t, c):
                xr, xi = c
                sl = tile(t)
                return (ar * xr - ai * xi + s_ref[ir, sl, :], ar * xi + ai * xr + s_ref[ii, sl, :])

            er, ei = lax.fori_loop(0, nstep, local, (zero, zero))
            pr, pi = ar1, ai1
            for _ in range(nstep.bit_length() - 1):
                pr, pi = pr * pr - pi * pi, 2.0 * pr * pi
            cr = jnp.zeros((1, ct), F32)
            ci = jnp.zeros((1, ct), F32)
            xin_r, xin_i = zero, zero
            for k in range(SEGS):
                sg = k if asc else SEGS - 1 - k
                here = rows == sg
                xin_r = jnp.where(here, cr, xin_r)
                xin_i = jnp.where(here, ci, xin_i)
                lr = jnp.sum(jnp.where(here, er, 0.0), axis=0, keepdims=True)
                li = jnp.sum(jnp.where(here, ei, 0.0), axis=0, keepdims=True)
                cr, ci = pr * cr - pi * ci + lr, pr * ci + pi * cr + li

            def final(t, c):
                xr, xi, acr, aci = c
                sl = tile(t)
                o_ref[ir, sl, :] = xr
                o_ref[ii, sl, :] = xi
                if with_da:
                    br = xp_ref[ir, sl, :]
                    bi = xp_ref[ii, sl, :]
                    acr = acr + br * xr + bi * xi
                    aci = aci + br * xi - bi * xr
                return (ar * xr - ai * xi + s_ref[ir, sl, :], ar * xi + ai * xr + s_ref[ii, sl, :], acr, aci)

            _, _, acr, aci = lax.fori_loop(0, nstep, final, (xin_r, xin_i, zero, zero))
            if with_da:
                da_ref[ir] = jnp.sum(acr, axis=0, keepdims=True)
                da_ref[ii] = jnp.sum(aci, axis=0, keepdims=True)

    blk = pl.BlockSpec((4, nc, ct), _state_blk)
    ablk = pl.BlockSpec((4, 1, ct), _state_blk)
    sds = jax.ShapeDtypeStruct((4, nc, gp), F32)
    if with_da:
        out, _ = _call(body, name, (gp // ct,), [blk, ablk, blk], [blk, ablk],
                       [sds, jax.ShapeDtypeStruct((4, 1, gp), F32)], ("parallel",), (s4, a4, xp4))
        return out
    (out,), _ = _call(body, name, (gp // ct,), [blk, ablk], [blk], [sds], ("parallel",), (s4, a4))
    return out


def _state_cat(ref):
    return _lanes4(ref).astype(BF16)


def ssm_out(ug, t, xp4, o, l, name):
    nc = ug.shape[1]

    def body(u_ref, t_ref, xp_ref, o_ref, y_ref):
        xs = _pair_split(_lanes4(xp_ref))
        u = _lanes4(u_ref).astype(BF16)
        for r in range(2):
            y = _dot(u[:, r * GW:(r + 1) * GW], t_ref[r].astype(BF16)) + _dot(xs[r].astype(BF16), o_ref[r].astype(BF16))
            y_ref[2 * r] = y[:, :128]
            y_ref[2 * r + 1] = y[:, 128:]

    (out,), _ = _call(body, name, (SSM_G // 2,),
                      [pl.BlockSpec((4, nc, 128), _pair3), _op_spec(l, GW, GW),
                       pl.BlockSpec((4, nc, 128), _state_blk), _op_spec(l, 4 * SSM_P, GW)],
                      [pl.BlockSpec((4, nc, 128), _pair3)],
                      [jax.ShapeDtypeStruct((2 * SSM_G, nc, 128), F32)], ("parallel",), (ug, t, xp4, o))
    return out


def ssm_out_bwd(dyg, xp4, o, l, name):
    nc = dyg.shape[1]

    def body(dy_ref, xp_ref, o_ref, do_ref, dxp_ref):
        xs = _pair_split(_lanes4(xp_ref))
        dy = _lanes4(dy_ref).astype(BF16)
        dxs = []
        for r in range(2):
            dyr = dy[:, r * GW:(r + 1) * GW]
            do_ref[r] = _dot_tn(xs[r].astype(BF16), dyr)
            dxs.append(_dot_nt(dyr, o_ref[r].astype(BF16)))
        for q, blk in enumerate(_pair_merge(*dxs)):
            dxp_ref[q] = blk

    out, _ = _call(body, name, (SSM_G // 2,),
                   [pl.BlockSpec((4, nc, 128), _pair3), pl.BlockSpec((4, nc, 128), _state_blk),
                    _op_spec(l, 4 * SSM_P, GW)],
                   [pl.BlockSpec((2, 4 * SSM_P, GW), _pair3), pl.BlockSpec((4, nc, 128), _state_blk)],
                   [jax.ShapeDtypeStruct((SSM_G, 4 * SSM_P, GW), F32), jax.ShapeDtypeStruct((4, nc, SSM_G * SSM_P), F32)],
                   ("parallel",), (dyg, xp4, o))
    return out


def ssm_in_bwd(ug, dyg, ds4, t, e, l, name):
    nc = ug.shape[1]

    def body(u_ref, dy_ref, ds_ref, t_ref, e_ref, dt_ref, de_ref, du_ref):
        dss = _pair_split(_lanes4(ds_ref))
        u = _lanes4(u_ref).astype(BF16)
        dy = _lanes4(dy_ref).astype(BF16)
        for r in range(2):
            cols = slice(r * GW, (r + 1) * GW)
            ds = dss[r].astype(BF16)
            dt_ref[r] = _dot_tn(u[:, cols], dy[:, cols])
            de_ref[r] = _dot_tn(u[:, cols], ds)
            du = _dot_nt(dy[:, cols], t_ref[r].astype(BF16)) + _dot_nt(ds, e_ref[r].astype(BF16))
            du_ref[2 * r] = du[:, :128]
            du_ref[2 * r + 1] = du[:, 128:]

    out, _ = _call(body, name, (SSM_G // 2,),
                   [pl.BlockSpec((4, nc, 128), _pair3), pl.BlockSpec((4, nc, 128), _pair3),
                    pl.BlockSpec((4, nc, 128), _state_blk), _op_spec(l, GW, GW), _op_spec(l, GW, 4 * SSM_P)],
                   [pl.BlockSpec((2, GW, GW), _pair3), pl.BlockSpec((2, GW, 4 * SSM_P), _pair3),
                    pl.BlockSpec((4, nc, 128), _pair3)],
                   [jax.ShapeDtypeStruct((SSM_G, GW, GW), F32), jax.ShapeDtypeStruct((SSM_G, GW, 4 * SSM_P), F32),
                    jax.ShapeDtypeStruct((2 * SSM_G, nc, 128), F32)], ("parallel",), (ug, dyg, ds4, t, e))
    return out


def ssm_post_fwd(yc, z, dskip, wglu, name):
    s = yc.shape[1]

    def body(y_ref, u_ref, d_ref, w_ref, o_ref, yp_ref, g_ref):
        yp = _lanes4(y_ref) + d_ref[...] * u_ref[...]
        yp_ref[...] = yp
        gv = _dot(_gelu(yp).astype(BF16), w_ref[...])
        g_ref[...] = gv
        o_ref[...] = gv[:, :SSM_W] * _sigmoid(gv[:, SSM_W:])

    out, _ = _call(body, name, (s // TM,),
                   [pl.BlockSpec((4, TM, 128), lambda i: (0, i, 0)), pl.BlockSpec((TM, SSM_W), lambda i: (i, U0 // SSM_W)),
                    pl.BlockSpec((1, SSM_W), _fixed), pl.BlockSpec((SSM_W, 2 * SSM_W), _fixed)],
                   [pl.BlockSpec((TM, SSM_W), _row), pl.BlockSpec((TM, SSM_W), _row), pl.BlockSpec((TM, 2 * SSM_W), _row)],
                   [jax.ShapeDtypeStruct((s, SSM_W), F32), jax.ShapeDtypeStruct((s, SSM_W), F32),
                    jax.ShapeDtypeStruct((s, 2 * SSM_W), F32)], ("parallel",), (yc, z, dskip, wglu))
    return out


def ssm_post_bwd(dssm, gpre, ypre, z, dskip, wglu, name):
    s = dssm.shape[0]

    def body(do_ref, g_ref, yp_ref, u_ref, d_ref, w_ref, dy_ref, du_ref, dw_ref, dd_ref):
        @pl.when(pl.program_id(0) == 0)
        def _():
            dw_ref[...] = jnp.zeros_like(dw_ref)
            dd_ref[...] = jnp.zeros_like(dd_ref)

        gv = g_ref[...]
        val = gv[:, :SSM_W]
        sg = _sigmoid(gv[:, SSM_W:])
        do = do_ref[...]
        dg = jnp.concatenate([do * sg, do * val * sg * (1.0 - sg)], axis=1).astype(BF16)
        yp = yp_ref[...]
        dgl = _dot_nt(dg, w_ref[...])
        dw_ref[...] += _dot_tn(_gelu(yp).astype(BF16), dg)
        dyp = dgl * _gelu_grad(yp)
        dy_ref[...] = dyp
        du_ref[...] = dyp * d_ref[...]
        dd_ref[...] += jnp.sum(dyp * u_ref[...], axis=0, keepdims=True)

    out, _ = _call(body, name, (s // TM,),
                   [pl.BlockSpec((TM, SSM_W), _row), pl.BlockSpec((TM, 2 * SSM_W), _row), pl.BlockSpec((TM, SSM_W), _row),
                    pl.BlockSpec((TM, SSM_W), lambda i: (i, U0 // SSM_W)), pl.BlockSpec((1, SSM_W), _fixed),
                    pl.BlockSpec((SSM_W, 2 * SSM_W), _fixed)],
                   [pl.BlockSpec((TM, SSM_W), _row), pl.BlockSpec((TM, SSM_W), _row),
                    pl.BlockSpec((SSM_W, 2 * SSM_W), _fixed), pl.BlockSpec((1, SSM_W), _fixed)],
                   [jax.ShapeDtypeStruct((s, SSM_W), F32), jax.ShapeDtypeStruct((s, SSM_W), F32),
                    jax.ShapeDtypeStruct((SSM_W, 2 * SSM_W), F32), jax.ShapeDtypeStruct((1, SSM_W), F32)],
                   ("arbitrary",), (dssm, gpre, ypre, z, dskip, wglu))
    return out


def _toeplitz_select():
    row = lax.broadcasted_iota(jnp.int32, (GW, CH * GW), 0)
    col = lax.broadcasted_iota(jnp.int32, (GW, CH * GW), 1)
    j, h2 = row // SSM_H, row % SSM_H
    s, t, h = col // GW, (col % GW) // SSM_H, col % SSM_H
    same = h2 == h
    return jnp.concatenate([same & (t - s == j), same & (s - t == j)], axis=0).astype(F32)


def ssm_mats(lam_re, lam_im, log_dt, b_re, b_im, c_re, c_im):
    g, p, hh = SSM_G, SSM_P, SSM_H
    hp = lax.Precision.HIGHEST
    jj = jnp.arange(CH + 1, dtype=F32)
    dt = jnp.exp(log_dt)[..., None]
    mag = jnp.exp((lam_re * dt)[..., None] * jj)
    ang = (lam_im * dt)[..., None] * jj
    pr, pi = mag * jnp.cos(ang), mag * jnp.sin(ang)
    abr, abi = pr[..., 1], pi[..., 1]
    den = lam_re * lam_re + lam_im * lam_im
    zr = ((abr - 1.0) * lam_re + abi * lam_im) / den
    zi = (abi * lam_re - (abr - 1.0) * lam_im) / den
    bbr = zr[..., None] * b_re[None] - zi[..., None] * b_im[None]
    bbi = zr[..., None] * b_im[None] + zi[..., None] * b_re[None]
    crt, cit = c_re.transpose(0, 1, 3, 2), c_im.transpose(0, 1, 3, 2)
    car = pr[..., None] * crt[..., None, :] - pi[..., None] * cit[..., None, :]
    cai = pr[..., None] * cit[..., None, :] + pi[..., None] * crt[..., None, :]
    lhs = jnp.concatenate([bbr, -bbi], axis=2).transpose(0, 1, 3, 2)
    rhs = jnp.concatenate([car[..., :CH, :], cai[..., :CH, :]], axis=2).reshape(2, g, 2 * p, GW)
    kt = jnp.einsum("dgkp,dgpn->dgkn", lhs, rhs, precision=hp)
    kcat = jnp.concatenate([kt[0], kt[1]], axis=-1).reshape(g * hh, 2 * GW)
    tmat = jnp.dot(kcat, _toeplitz_select(), precision=hp)
    tmat = tmat.reshape(g, hh, CH, GW).transpose(0, 2, 1, 3).reshape(g, GW, GW)

    def e_part(d, pw_r, pw_i):
        pw_r, pw_i = pw_r.transpose(0, 2, 1)[:, :, None, :], pw_i.transpose(0, 2, 1)[:, :, None, :]
        br, bi = bbr[d].transpose(0, 2, 1)[:, None], bbi[d].transpose(0, 2, 1)[:, None]
        return [pw_r * br - pw_i * bi, pw_r * bi + pw_i * br]

    eparts = (e_part(0, pr[0, ..., :CH][..., ::-1], pi[0, ..., :CH][..., ::-1])
              + e_part(1, pr[1, ..., :CH], pi[1, ..., :CH]))
    emat = jnp.concatenate(eparts, axis=3).reshape(g, GW, 4 * p)

    oparts = [car[0, ..., 1:, :], -cai[0, ..., 1:, :], car[1, ..., 1:, :][..., ::-1, :], -cai[1, ..., 1:, :][..., ::-1, :]]
    omat = jnp.stack([v.reshape(g, p, GW) for v in oparts], axis=1).reshape(g, 4 * p, GW)
    amat = jnp.stack([pr[0, ..., CH], pi[0, ..., CH], pr[1, ..., CH], pi[1, ..., CH]], axis=0).reshape(4, 1, g * p)
    return tmat, emat, omat, amat


def outproj_fwd(x, att, ssm, wo, name):
    s, d = x.shape

    def body(x_ref, a_ref, s_ref, w_ref, o_ref):
        o_ref[...] = (x_ref[...] + _dot(a_ref[...].astype(BF16), w_ref[0:ATT_W, :])
                      + _dot(s_ref[...].astype(BF16), w_ref[ATT_W:, :]))

    (out,), _ = _call(body, name, (s // TM,),
                      [pl.BlockSpec((TM, d), _row), pl.BlockSpec((TM, ATT_W), _row), pl.BlockSpec((TM, SSM_W), _row),
                       pl.BlockSpec((ATT_W + SSM_W, d), _fixed)],
                      [pl.BlockSpec((TM, d), _row)], [jax.ShapeDtypeStruct((s, d), F32)], ("parallel",),
                      (x, att, ssm, wo))
    return out


def outproj_bwd(dx1, att, ssm, wo, name):
    s, d = dx1.shape

    def body(dx_ref, a_ref, s_ref, w_ref, da_ref, ds_ref, dw_ref):
        @pl.when(pl.program_id(0) == 0)
        def _():
            dw_ref[...] = jnp.zeros_like(dw_ref)

        dxb = dx_ref[...].astype(BF16)
        da_ref[...] = _dot_nt(dxb, w_ref[0:ATT_W, :])
        ds_ref[...] = _dot_nt(dxb, w_ref[ATT_W:, :])
        dw_ref[0:ATT_W, :] += _dot_tn(a_ref[...].astype(BF16), dxb)
        dw_ref[ATT_W:, :] += _dot_tn(s_ref[...].astype(BF16), dxb)

    out, _ = _call(body, name, (s // TM,),
                   [pl.BlockSpec((TM, d), _row), pl.BlockSpec((TM, ATT_W), _row), pl.BlockSpec((TM, SSM_W), _row),
                    pl.BlockSpec((ATT_W + SSM_W, d), _fixed)],
                   [pl.BlockSpec((TM, ATT_W), _row), pl.BlockSpec((TM, SSM_W), _row),
                    pl.BlockSpec((ATT_W + SSM_W, d), _fixed)],
                   [jax.ShapeDtypeStruct((s, ATT_W), F32), jax.ShapeDtypeStruct((s, SSM_W), F32),
                    jax.ShapeDtypeStruct((ATT_W + SSM_W, d), F32)], ("arbitrary",), (dx1, att, ssm, wo))
    return out


def ffn_fwd(x1, gain, w1, w2, name, xch=None):
    s, d = x1.shape
    nch, _, fc = w1.shape

    def body(x_ref, g_ref, w1_ref, w2_ref, o_ref, h_ref, a_ref):
        @pl.when(pl.program_id(1) == 0)
        def _():
            xv = x_ref[...]
            h_ref[...] = (xv * _rms(xv) * g_ref[...]).astype(BF16)
            o_ref[...] = xv

        a = _dot(h_ref[...], w1_ref[...])
        a_ref[...] = a.astype(BF16)
        o_ref[...] += _dot(jnp.square(jnp.maximum(a, 0.0)).astype(BF16), w2_ref[...])

    tm = min(TM_FFN, s)
    return _call(
        body, name, (s // tm, nch),
        [pl.BlockSpec((tm, d), lambda i, k: (i, 0)), pl.BlockSpec((1, d), lambda i, k: (0, 0)),
         pl.BlockSpec((None, d, fc), lambda i, k: (k, 0, 0)), pl.BlockSpec((None, fc, d), lambda i, k: (k, 0, 0))],
        [pl.BlockSpec((tm, d), lambda i, k: (i, 0)), pl.BlockSpec((tm, d), lambda i, k: (i, 0)),
         pl.BlockSpec((tm, fc), lambda i, k: (i, k))],
        [jax.ShapeDtypeStruct((s, d), F32), jax.ShapeDtypeStruct((s, d), BF16), jax.ShapeDtypeStruct((s, nch * fc), BF16)],
        ("arbitrary", "arbitrary"), (x1, gain, w1, w2), xch=xch)


def ffn_bwd_tok(dx2, x1, gain, a, w1, w2, name, xch=None):
    s, d = x1.shape
    nch, _, fc = w1.shape

    def body(dx_ref, x_ref, g_ref, a_ref, w1_ref, w2_ref, da_ref, dx1_ref, dg_ref, dxb_ref, dh_ref):
        i = pl.program_id(0)
        k = pl.program_id(1)

        @pl.when(jnp.logical_and(i == 0, k == 0))
        def _():
            dg_ref[...] = jnp.zeros_like(dg_ref)

        @pl.when(k == 0)
        def _():
            dxb_ref[...] = dx_ref[...].astype(BF16)
            dh_ref[...] = jnp.zeros_like(dh_ref)

        dr = _dot_nt(dxb_ref[...], w2_ref[...])
        da = (dr * (2.0 * jnp.maximum(a_ref[...].astype(F32), 0.0))).astype(BF16)
        da_ref[...] = da
        dh_ref[...] += _dot_nt(da, w1_ref[...])

        @pl.when(k == nch - 1)
        def _():
            xv = x_ref[...]
            r = _rms(xv)
            xhat = xv * r
            dh = dh_ref[...]
            dg_ref[...] += jnp.sum(dh * xhat, axis=0, keepdims=True)
            dx1_ref[...] = dx_ref[...] + _rms_bwd(xhat, r, dh * g_ref[...])

    tm = min(TM_FFN, s)
    return _call(
        body, name, (s // tm, nch),
        [pl.BlockSpec((tm, d), lambda i, k: (i, 0)), pl.BlockSpec((tm, d), lambda i, k: (i, 0)),
         pl.BlockSpec((1, d), lambda i, k: (0, 0)), pl.BlockSpec((tm, fc), lambda i, k: (i, k)),
         pl.BlockSpec((None, d, fc), lambda i, k: (k, 0, 0)), pl.BlockSpec((None, fc, d), lambda i, k: (k, 0, 0))],
        [pl.BlockSpec((tm, fc), lambda i, k: (i, k)), pl.BlockSpec((tm, d), lambda i, k: (i, 0)),
         pl.BlockSpec((1, d), lambda i, k: (0, 0)), pl.BlockSpec((tm, d), lambda i, k: (i, 0))],
        [jax.ShapeDtypeStruct((s, nch * fc), BF16), jax.ShapeDtypeStruct((s, d), F32),
         jax.ShapeDtypeStruct((1, d), F32), jax.ShapeDtypeStruct((s, d), BF16)],
        ("arbitrary", "arbitrary"), (dx2, x1, gain, a, w1, w2), scratch=[pltpu.VMEM((tm, d), F32)], xch=xch)


def ffn_bwd_w(h2, da, a, dxb, nch, name, xch=None):
    s, d = h2.shape
    fc = a.shape[1] // nch
    tm = min(TM_FFN, s)

    def body(h_ref, da_ref, a_ref, dx_ref, dw1_ref, dw2_ref, acc1_ref, acc2_ref):
        t = pl.program_id(1)

        @pl.when(t == 0)
        def _():
            acc1_ref[...] = jnp.zeros_like(acc1_ref)
            acc2_ref[...] = jnp.zeros_like(acc2_ref)

        acc1_ref[...] += _dot_tn(h_ref[...], da_ref[...])
        r = jnp.square(jnp.maximum(a_ref[...].astype(F32), 0.0)).astype(BF16)
        acc2_ref[...] += _dot_tn(r, dx_ref[...])

        @pl.when(t == s // tm - 1)
        def _():
            dw1_ref[...] = acc1_ref[...].astype(BF16)
            dw2_ref[...] = acc2_ref[...].astype(BF16)

    return _call(
        body, name, (nch, s // tm),
        [pl.BlockSpec((tm, d), lambda k, t: (t, 0)), pl.BlockSpec((tm, fc), lambda k, t: (t, k)),
         pl.BlockSpec((tm, fc), lambda k, t: (t, k)), pl.BlockSpec((tm, d), lambda k, t: (t, 0))],
        [pl.BlockSpec((None, d, fc), lambda k, t: (k, 0, 0)), pl.BlockSpec((None, fc, d), lambda k, t: (k, 0, 0))],
        [jax.ShapeDtypeStruct((nch, d, fc), BF16), jax.ShapeDtypeStruct((nch, fc, d), BF16)],
        ("arbitrary", "arbitrary"), (h2, da, a, dxb), scratch=[pltpu.VMEM((d, fc), F32), pltpu.VMEM((fc, d), F32)],
        xch=xch)


def loss_grad(xf, tgt, name):
    s, d = xf.shape
    nt = s // TM

    def body(x_ref, t_ref, dx_ref, l_ref, acc_ref):
        i = pl.program_id(0)

        @pl.when(i == 0)
        def _():
            acc_ref[...] = jnp.zeros_like(acc_ref)

        e = x_ref[...] - t_ref[...]
        dx_ref[...] = e * (1.0 / d)
        acc_ref[...] += jnp.sum(e * e, axis=0, keepdims=True)

        @pl.when(i == nt - 1)
        def _():
            l_ref[...] = jnp.sum(acc_ref[...], axis=1, keepdims=True) * (0.5 / d)

    out, _ = _call(body, name, (nt,), [pl.BlockSpec((TM, d), _row), pl.BlockSpec((TM, d), _row)],
                   [pl.BlockSpec((TM, d), _row), pl.BlockSpec((1, 1), _fixed)],
                   [jax.ShapeDtypeStruct((s, d), F32), jax.ShapeDtypeStruct((1, 1), F32)], ("arbitrary",),
                   (xf, tgt), scratch=[pltpu.VMEM((1, d), F32)])
    return out


def adamw_sum(parts, w, m, v, br, name, xch=None):
    nl = len(parts)
    npart, r, c = parts[0].shape
    nb = r // br
    c1 = 1.0 - ADAM_B1 ** ADAM_STEP
    c2 = 1.0 - ADAM_B2 ** ADAM_STEP

    def body(*refs):
        p_refs = refs[:nl]
        w_ref, m_ref, v_ref, g_ref, d_ref, nm_ref, nv_ref = refs[nl:]
        for l in range(nl):
            @pl.when(pl.program_id(0) == l)
            def _(l=l):
                g = p_refs[l][0].astype(F32)
                for j in range(1, npart):
                    g = g + p_refs[l][j].astype(F32)
                m2 = ADAM_B1 * m_ref[...] + (1.0 - ADAM_B1) * g
                v2 = ADAM_B2 * v_ref[...] + (1.0 - ADAM_B2) * jnp.square(g)
                g_ref[...] = g
                nm_ref[...] = m2
                nv_ref[...] = v2
                d_ref[...] = -ADAM_LR * ((m2 / c1) / (jnp.sqrt(v2 / c2) + ADAM_EPS) + ADAM_WD * w_ref[...])

    blk = pl.BlockSpec((br, c), lambda l, i: (l * nb + i, 0))
    pspecs = [pl.BlockSpec((npart, br, c), lambda l, i, own=own: (0, jnp.where(l == own, i, 0), 0)) for own in range(nl)]
    sds = jax.ShapeDtypeStruct((nl * r, c), F32)
    return _call(body, name, (nl, nb), pspecs + [blk, blk, blk], [blk, blk, blk, blk], [sds, sds, sds, sds],
                 ("arbitrary", "arbitrary"), (*parts, w, m, v), xch=xch)


def sum_parts(parts, name):
    n, r, c = parts.shape

    def body(p_ref, o_ref):
        g = p_ref[0]
        for j in range(1, n):
            g = g + p_ref[j]
        o_ref[...] = g

    (out,), _ = _call(body, name, (1,), [pl.BlockSpec((n, r, c), lambda i: (0, 0, 0))], [pl.BlockSpec((r, c), _fixed)],
                      [jax.ShapeDtypeStruct((r, c), F32)], ("arbitrary",), (parts,))
    return out


def layer_fwd(x, w_in, p, mats, l, late, nxt):
    tmat, emat, omat, amat = mats
    z = norm_matmul(x, p["norm1"], w_in, f"in_fwd{l}")
    (att, lse), got = attn_fwd(z, p["bias"], p["q_gain"], p["k_gain"], p["sink"], f"attn_fwd{l}", (late, False))
    p = dict(p, **_whole_weights(dict(zip(BIG[-len(got):], got))))
    ug = to_groups(z, U0 // 128, f"ssm_to_groups{l}")
    s4 = ssm_in(ug, emat, l, f"ssm_in{l}")
    xp4 = chunk_scan(s4, amat, False, f"ssm_scan{l}")
    yc = from_groups(ssm_out(ug, tmat, xp4, omat, l, f"ssm_out{l}"), f"ssm_from_groups{l}")
    ssm, ypre, gpre = ssm_post_fwd(yc, z, p["d_skip"], p["w_glu"], f"ssm_post{l}")
    x1 = outproj_fwd(x, att, ssm, p["w_out"], f"out_fwd{l}")
    (x2, h2, a), gathered = ffn_fwd(x1, p["norm2"], p["w_ff1"], p["w_ff2"], f"ffn_fwd{l}",
                                    None if nxt is None else (nxt, False))
    saved = dict(x=x, z=z, att=att, lse=lse, ug=ug, xp4=xp4, ssm=ssm, ypre=ypre, gpre=gpre, x1=x1, h2=h2, a=a)
    return x2, saved, dict(p, w_in=w_in), gathered


def layer_bwd(dx2, p, mats, sv, l, above):
    tmat, emat, omat, amat = mats
    nch = p["w_ff1"].shape[0]
    last = l == 0
    (da, dx1, dnorm2, dxb), got_tok = ffn_bwd_tok(dx2, sv["x1"], p["norm2"], sv["a"], p["w_ff1"], p["w_ff2"],
                                                  f"ffn_bwd{l}", None if above is None else (above["tok"], True))
    (dw1, dw2), got_w = ffn_bwd_w(sv["h2"], da, sv["a"], dxb, nch, f"ffn_bwdw{l}",
                                  None if above is None else (above["w"], True))
    datt, dssm, dwo = outproj_bwd(dx1, sv["att"], sv["ssm"], p["w_out"], f"out_bwd{l}")
    dyc, du_skip, dwglu, ddskip = ssm_post_bwd(dssm, sv["gpre"], sv["ypre"], sv["z"], p["d_skip"], p["w_glu"],
                                               f"ssm_post_bwd{l}")
    dyg = to_groups(dyc, 0, f"ssm_to_groups_bwd{l}")
    domat, dxp4 = ssm_out_bwd(dyg, sv["xp4"], omat, l, f"ssm_out_bwd{l}")
    aconj = amat * jnp.array([1.0, -1.0, 1.0, -1.0], F32).reshape(4, 1, 1)
    ds4, damat = chunk_scan(dxp4, aconj, True, f"ssm_scan_bwd{l}", xp4=sv["xp4"])
    dtmat, demat, dug = ssm_in_bwd(sv["ug"], dyg, ds4, tmat, emat, l, f"ssm_in_bwd{l}")
    du_core = from_groups(dug, f"ssm_from_groups_bwd{l}")
    (dq, dk, dv, dqg, dkg, dsink), got_ff = attn_bwd(sv["z"], p["bias"], sv["att"], datt, sv["lse"], p["q_gain"], p["k_gain"],
                                                     p["sink"], f"attn_bwd{l}", ([dw1, dw2] if last else [dw1], True))
    blocks = _grad_blocks(dict(w_glu=dwglu, w_out=dwo))
    (dx, dwin, dnorm1), got_mix = in_bwd(sv["x"], p["norm1"], p["w_in"], du_skip, du_core, dq, dk, dv, dx1, f"in_bwd{l}",
                                         ([blocks["w_glu"], blocks["w_out"]], True) if last else None)
    grads = dict(norm1=dnorm1, q_gain=dqg, k_gain=dkg, sink=dsink, d_skip=ddskip, norm2=dnorm2)
    win_blocks = _grad_blocks(dict(w_in=dwin))["w_in"]
    mine = dict(w_ff1=got_ff[0])
    if last:
        mine.update(w_ff2=got_ff[1], w_glu=got_mix[0], w_out=got_mix[1])
        below = win_blocks
    else:
        below = dict(tok=[win_blocks, blocks["w_glu"], blocks["w_out"]], w=[dw2])
    theirs = {} if above is None else dict(w_in=got_tok[0], w_glu=got_tok[1], w_out=got_tok[2], w_ff2=got_w[0])
    return dx, grads, (dtmat, demat, domat, damat), mine, theirs, below


def _whole_weights(gathered):
    out = {}
    for n, g in gathered.items():
        if n == "w_in":
            w_in = g.transpose(1, 0, 2).reshape(D_MODEL, IN_W)
            out[n] = jnp.concatenate([w_in[:, V_END:], w_in[:, :V_END]], axis=1)
        elif n == "w_glu":
            out[n] = g.transpose(1, 0, 2).reshape(SSM_W, 2 * SSM_W)
        elif n == "w_out":
            out[n] = g.reshape(ATT_W + SSM_W, D_MODEL)
        else:
            out[n] = g
    return out


def _grad_blocks(grads):
    out = {}
    for n, g in grads.items():
        g = g.astype(BF16)
        if n == "w_in":
            g = jnp.concatenate([g[:, IN_W - V_END:], g[:, :IN_W - V_END]], axis=1)
            out[n] = g.reshape(D_MODEL, N_DEV, IN_W // N_DEV).transpose(1, 0, 2)
        elif n == "w_glu":
            out[n] = g.reshape(SSM_W, N_DEV, 2 * SSM_W // N_DEV).transpose(1, 0, 2)
        elif n == "w_out":
            out[n] = g.reshape(N_DEV, (ATT_W + SSM_W) // N_DEV, D_MODEL)
        else:
            out[n] = g
    return out


def _small_layout(like):
    layout, row = {}, 0
    for n in SMALL:
        size = int(np.prod(like[n].shape))
        nrow = -(-size // (8 * 128)) * 8
        layout[n] = (row, nrow, size)
        row += nrow
    return layout, -(-row // (8 * N_DEV)) * 8 * N_DEV


def _pack_small(vals, layout, rows):
    pieces, used = [], 0
    for n in SMALL:
        _, nrow, size = layout[n]
        flat = vals[n].reshape(-1).astype(F32)
        pieces.append(jnp.pad(flat, (0, nrow * 128 - size)).reshape(nrow, 128))
        used += nrow
    if rows > used:
        pieces.append(jnp.zeros((rows - used, 128), F32))
    return jnp.concatenate(pieces, axis=0)


def _unpack_small(packed, like, layout):
    out = {}
    for n in SMALL:
        row, nrow, size = layout[n]
        out[n] = packed[row:row + nrow].reshape(-1)[:size].reshape(like[n].shape)
    return out


def kernel(x, norm1, w_in, q_gain, k_gain, sink, lam_re, lam_im, log_dt, b_re, b_im, c_re, c_im, d_skip, w_glu, w_out, norm2, w_ff1, w_ff2, loss_target, m_norm1, m_w_in, m_q_gain, m_k_gain, m_sink, m_lam_re, m_lam_im, m_log_dt, m_b_re, m_b_im, m_c_re, m_c_im, m_d_skip, m_w_glu, m_w_out, m_norm2, m_w_ff1, m_w_ff2, v_norm1, v_w_in, v_q_gain, v_k_gain, v_sink, v_lam_re, v_lam_im, v_log_dt, v_b_re, v_b_im, v_c_re, v_c_im, v_d_skip, v_w_glu, v_w_out, v_norm2, v_w_ff1, v_w_ff2):
    w = dict(norm1=norm1, w_in=w_in, q_gain=q_gain, k_gain=k_gain, sink=sink, lam_re=lam_re, lam_im=lam_im,
             log_dt=log_dt, b_re=b_re, b_im=b_im, c_re=c_re, c_im=c_im, d_skip=d_skip, w_glu=w_glu, w_out=w_out,
             norm2=norm2, w_ff1=w_ff1, w_ff2=w_ff2)
    m = dict(norm1=m_norm1, w_in=m_w_in, q_gain=m_q_gain, k_gain=m_k_gain, sink=m_sink, lam_re=m_lam_re,
             lam_im=m_lam_im, log_dt=m_log_dt, b_re=m_b_re, b_im=m_b_im, c_re=m_c_re, c_im=m_c_im, d_skip=m_d_skip,
             w_glu=m_w_glu, w_out=m_w_out, norm2=m_norm2, w_ff1=m_w_ff1, w_ff2=m_w_ff2)
    v = dict(norm1=v_norm1, w_in=v_w_in, q_gain=v_q_gain, k_gain=v_k_gain, sink=v_sink, lam_re=v_lam_re,
             lam_im=v_lam_im, log_dt=v_log_dt, b_re=v_b_re, b_im=v_b_im, c_re=v_c_re, c_im=v_c_im, d_skip=v_d_skip,
             w_glu=v_w_glu, w_out=v_w_out, norm2=v_norm2, w_ff1=v_w_ff1, w_ff2=v_w_ff2)
    nl = w_in.shape[0]
    shards = [[w[n][l].astype(BF16) for n in BIG] for l in range(nl)]
    (tmat, emat, omat, amat), mats_vjp = jax.vjp(jax.vmap(ssm_mats), *[w[n] for n in S5])

    (g_in,) = exchange(shards[0][:1], False, "gather_w_in0")
    have = _whole_weights(dict(w_in=g_in))
    late = shards[0][1:]
    xs = x[0]
    bias = band_bias()
    saved, lp, lm = [], [], []
    for l in range(nl):
        p = {n: have[n] for n in have if n != "w_in"}
        for n in ("norm1", "q_gain", "k_gain", "d_skip", "norm2"):
            p[n] = w[n][l].reshape(1, -1)
        p["sink"] = sink[l]
        p["bias"] = bias
        mats = (tmat, emat, omat, amat[l])
        xs, sv, p, got = layer_fwd(xs, have["w_in"], p, mats, l, late, shards[l + 1][:4] if l + 1 < nl else None)
        if l + 1 < nl:
            have = _whole_weights(dict(zip(BIG[:4], got)))
            late = shards[l + 1][4:]
        saved.append(sv)
        lp.append(p)
        lm.append(mats)
    dx, loss_part = loss_grad(xs, loss_target[0], "loss")
    loss = lax.psum(loss_part[0, 0], ("x", "y", "c"))

    grads, dmats, parts = [None] * nl, [None] * nl, [dict() for _ in range(nl)]
    above = None
    for l in reversed(range(nl)):
        dx, grads[l], dmats[l], mine, theirs, above = layer_bwd(dx, lp[l], lm[l], saved[l], l, above)
        parts[l].update(mine)
        if l + 1 < nl:
            parts[l + 1].update(theirs)
    out_g, out_d, out_m, out_v = {}, {}, {}, {}

    def adamw_big(n, xch=None):
        c = w[n].shape[-1]
        r = int(np.prod(w[n].shape[:-1]))
        res, got = adamw_sum([parts[l][n] for l in range(nl)], w[n].reshape(r, c), m[n].reshape(r, c),
                             v[n].reshape(r, c), ADAM_ROWS[c], f"adamw_{n}", xch)
        out_g[n], out_d[n], out_m[n], out_v[n] = (t.reshape(w[n].shape) for t in res)
        return got

    (parts[0]["w_in"],) = adamw_big("w_ff1", ([above], True))

    gs = {n: jnp.stack([grads[l][n].reshape(w[n].shape[1:]) for l in range(nl)])
          for n in ("norm1", "q_gain", "k_gain", "sink", "d_skip", "norm2")}
    ds5 = mats_vjp(tuple(jnp.stack([dmats[l][i] for l in range(nl)]) for i in range(4)))
    gs.update(zip(S5, ds5))
    layout, rows = _small_layout(w)
    (mine,) = adamw_big("w_ff2", ([_pack_small(gs, layout, rows).reshape(N_DEV, rows // N_DEV, 128)], True))
    (small_sum,) = exchange([sum_parts(mine, "sum_small_grads")], False, "gather_small_grads")
    for n in ("w_in", "w_glu", "w_out"):
        adamw_big(n)
    res, _ = adamw_sum([small_sum.reshape(1, rows, 128)], _pack_small(w, layout, rows), _pack_small(m, layout, rows),
                       _pack_small(v, layout, rows), rows // N_DEV, "adamw_small")
    for dst, packed in zip((out_g, out_d, out_m, out_v), res):
        dst.update(_unpack_small(packed, w, layout))

    return (loss, dx[None], *[out_g[n] for n in WEIGHTS], *[out_d[n] for n in WEIGHTS],
            *[out_m[n] for n in WEIGHTS], *[out_v[n] for n in WEIGHTS])
```

```python
import numpy as np
import jax
import jax.numpy as jnp
from jax import lax
from jax.experimental import pallas as pl
from jax.experimental.pallas import tpu as pltpu

F32, BF16 = jnp.float32, jnp.bfloat16
EPS = 1e-6
D_MODEL = 1024
ATT_HEADS, KV_HEADS, GQA_GROUP, HEAD_DIM = 8, 2, 4, 64
ATT_W, KV_W, SSM_W, IN_W = 512, 128, 512, 1280
V_END = 768
U0, Q0, K0, V0 = 0, 512, 1024, 1152
BLK = 128
SCALE = 0.125
SSM_G, SSM_H, SSM_P = 32, 16, 64
CH = 16
GW = CH * SSM_H
SEGS = 8
N_DEV = 8
NEG = float(np.finfo(np.float32).min)
SLOPES = tuple(2.0 ** (-8.0 * (h + 1) / ATT_HEADS) for h in range(ATT_HEADS))
VMEM_LIMIT = 56 * 1024 * 1024
TM = 512
TM_FFN = 1024

ADAM_LR, ADAM_B1, ADAM_B2, ADAM_EPS, ADAM_WD, ADAM_STEP = 0.001, 0.9, 0.999, 1e-08, 0.01, 10

SMALL = ("norm1", "q_gain", "k_gain", "sink", "lam_re", "lam_im", "log_dt", "b_re", "b_im",
         "c_re", "c_im", "d_skip", "norm2")
S5 = ("lam_re", "lam_im", "log_dt", "b_re", "b_im", "c_re", "c_im")
BIG = ("w_in", "w_glu", "w_out", "w_ff1", "w_ff2")
WEIGHTS = ("norm1", "w_in", "q_gain", "k_gain", "sink", "lam_re", "lam_im", "log_dt", "b_re", "b_im",
           "c_re", "c_im", "d_skip", "w_glu", "w_out", "norm2", "w_ff1", "w_ff2")
ADAM_ROWS = {160: 256, 128: 512, 512: 128, 1024: 64}


def _dot(a, b):
    return jnp.dot(a, b, preferred_element_type=F32)


def _dot_nt(a, b):
    return lax.dot_general(a, b, (((1,), (1,)), ((), ())), preferred_element_type=F32)


def _dot_tn(a, b):
    return lax.dot_general(a, b, (((0,), (0,)), ((), ())), preferred_element_type=F32)


def _rms(x):
    return lax.rsqrt(jnp.mean(x * x, axis=-1, keepdims=True) + EPS)


def _rms_bwd(xhat, r, dxhat):
    return r * (dxhat - xhat * jnp.mean(dxhat * xhat, axis=-1, keepdims=True))


def _sigmoid(x):
    return 1.0 / (1.0 + jnp.exp(-x))


_GC = 0.7978845608028654
_GA = 0.044715


def _gelu(x):
    return 0.5 * x * (1.0 + jnp.tanh(_GC * (x + _GA * x * x * x)))


def _gelu_grad(x):
    t = jnp.tanh(_GC * (x + _GA * x * x * x))
    return 0.5 * (1.0 + t) + 0.5 * x * (1.0 - t * t) * _GC * (1.0 + 3.0 * _GA * x * x)


def _row(i):
    return (i, 0)


def _fixed(i):
    return (0, 0)


def _me_and_peers():
    x, y, c = lax.axis_index("x"), lax.axis_index("y"), lax.axis_index("c")
    me = 4 * x + 2 * y + c
    peers = []
    for k in range(1, N_DEV):
        px = jnp.bitwise_xor(x, (k >> 2) & 1)
        py = jnp.bitwise_xor(y, (k >> 1) & 1)
        pc = jnp.bitwise_xor(c, k & 1)
        peers.append(((px, py, pc), 4 * px + 2 * py + pc))
    return me, peers


def _xch_copies(ins, outs, send_sems, recv_sems, loc_sems, scatter):
    me, peers = _me_and_peers()
    local, sends, recvs = [], [], []
    for a in range(len(ins)):
        local.append(pltpu.make_async_copy(ins[a].at[me] if scatter else ins[a], outs[a].at[me], loc_sems.at[a]))
    for k, (dev, idx) in enumerate(peers):
        for a in range(len(ins)):
            src = ins[a].at[idx] if scatter else ins[a]
            for dst, group in ((outs[a].at[me], sends), (outs[a].at[idx], recvs)):
                group.append(pltpu.make_async_remote_copy(
                    src_ref=src, dst_ref=dst, send_sem=send_sems.at[a, k], recv_sem=recv_sems.at[a, k],
                    device_id=dev, device_id_type=pl.DeviceIdType.MESH))
    return local, sends, recvs


def _xch_start(copies):
    local, sends, _ = copies
    for cp in local + sends:
        cp.start()


def _xch_wait(copies):
    local, sends, recvs = copies
    for cp in recvs:
        cp.wait_recv()
    for cp in sends:
        cp.wait_send()
    for cp in local:
        cp.wait()


def _xch_shapes(arrays, scatter):
    return [jax.ShapeDtypeStruct(a.shape if scatter else (N_DEV,) + a.shape, a.dtype) for a in arrays]


def _xch_sems(n):
    return [pltpu.SemaphoreType.DMA((n, N_DEV - 1)), pltpu.SemaphoreType.DMA((n, N_DEV - 1)),
            pltpu.SemaphoreType.DMA((n,))]


_ANY = pl.BlockSpec(memory_space=pl.ANY)


def exchange(arrays, scatter, name):
    n = len(arrays)

    def body(*refs):
        copies = _xch_copies(refs[:n], refs[n:2 * n], *refs[2 * n:], scatter)
        _xch_start(copies)
        _xch_wait(copies)

    return pl.pallas_call(
        body, name=name, in_specs=[_ANY] * n, out_specs=[_ANY] * n, out_shape=_xch_shapes(arrays, scatter),
        scratch_shapes=_xch_sems(n), compiler_params=pltpu.CompilerParams(has_side_effects=True),
    )(*arrays)


def _call(body, name, grid, in_specs, out_specs, out_shape, sem, inputs, scratch=(), xch=None):
    params = pltpu.CompilerParams(dimension_semantics=sem, vmem_limit_bytes=VMEM_LIMIT)
    if xch is None:
        out = pl.pallas_call(body, name=name, grid=grid, in_specs=in_specs, out_specs=out_specs, out_shape=out_shape,
                             scratch_shapes=list(scratch), compiler_params=params)(*inputs)
        return list(out), None
    arrays, scatter = xch
    n, nin, nout, nsc = len(arrays), len(in_specs), len(out_specs), len(scratch)

    def wrapped(*refs):
        ins, refs = refs[:nin], refs[nin:]
        xin, refs = refs[:n], refs[n:]
        outs, refs = refs[:nout], refs[nout:]
        xout, refs = refs[:n], refs[n:]
        sc, sems = refs[:nsc], refs[nsc:]
        first = last = None
        for ax, size in enumerate(grid):
            f, e = pl.program_id(ax) == 0, pl.program_id(ax) == size - 1
            first = f if first is None else jnp.logical_and(first, f)
            last = e if last is None else jnp.logical_and(last, e)

        @pl.when(first)
        def _():
            _xch_start(_xch_copies(xin, xout, *sems, scatter))

        body(*ins, *outs, *sc)

        @pl.when(last)
        def _():
            _xch_wait(_xch_copies(xin, xout, *sems, scatter))

    out = pl.pallas_call(
        wrapped, name=name, grid=grid, in_specs=list(in_specs) + [_ANY] * n, out_specs=list(out_specs) + [_ANY] * n,
        out_shape=list(out_shape) + _xch_shapes(arrays, scatter), scratch_shapes=list(scratch) + _xch_sems(n),
        compiler_params=params)(*inputs, *arrays)
    return list(out[:nout]), list(out[nout:])


def norm_matmul(x, gain, w, name):
    s, d = x.shape
    n = w.shape[1]

    def body(x_ref, g_ref, w_ref, z_ref):
        xv = x_ref[...]
        h = (xv * _rms(xv) * g_ref[...]).astype(BF16)
        z_ref[...] = _dot(h, w_ref[...])

    (z,), _ = _call(body, name, (s // TM,),
                    [pl.BlockSpec((TM, d), _row), pl.BlockSpec((1, d), _fixed), pl.BlockSpec((d, n), _fixed)],
                    [pl.BlockSpec((TM, n), _row)], [jax.ShapeDtypeStruct((s, n), F32)], ("parallel",), (x, gain, w))
    return z


def in_bwd(x, gain, w, du_a, du_b, dq, dk, dv, dres, name, xch=None):
    s, d = x.shape
    n = w.shape[1]

    def body(x_ref, g_ref, w_ref, dua_ref, dub_ref, dq_ref, dk_ref, dv_ref, dres_ref, dx_ref, dw_ref, dg_ref):
        @pl.when(pl.program_id(0) == 0)
        def _():
            dw_ref[...] = jnp.zeros_like(dw_ref)
            dg_ref[...] = jnp.zeros_like(dg_ref)

        xv = x_ref[...]
        r = _rms(xv)
        xhat = xv * r
        g = g_ref[...]
        h = (xhat * g).astype(BF16)
        dz = jnp.concatenate([(dua_ref[...] + _lanes4(dub_ref)).astype(BF16), dq_ref[...].astype(BF16),
                              dk_ref[...].astype(BF16), dv_ref[...].astype(BF16)], axis=1)
        dh = _dot_nt(dz, w_ref[...])
        dw_ref[...] += _dot_tn(h, dz)
        dg_ref[...] += jnp.sum(dh * xhat, axis=0, keepdims=True)
        dx_ref[...] = dres_ref[...] + _rms_bwd(xhat, r, dh * g)

    return _call(
        body, name, (s // TM,),
        [pl.BlockSpec((TM, d), _row), pl.BlockSpec((1, d), _fixed), pl.BlockSpec((d, n), _fixed),
         pl.BlockSpec((TM, SSM_W), _row), pl.BlockSpec((4, TM, 128), lambda i: (0, i, 0)), pl.BlockSpec((TM, ATT_W), _row),
         pl.BlockSpec((TM, KV_W), _row), pl.BlockSpec((TM, KV_W), _row), pl.BlockSpec((TM, d), _row)],
        [pl.BlockSpec((TM, d), _row), pl.BlockSpec((d, n), _fixed), pl.BlockSpec((1, d), _fixed)],
        [jax.ShapeDtypeStruct((s, d), F32), jax.ShapeDtypeStruct((d, n), F32), jax.ShapeDtypeStruct((1, d), F32)],
        ("arbitrary",), (x, gain, w, du_a, du_b, dq, dk, dv, dres), xch=xch)


def _band_specs(nb):
    def w0(i):
        return jnp.clip(i - 1, 0, nb - 3)

    specs = [pl.BlockSpec((None, KV_HEADS, GQA_GROUP * BLK, 3 * BLK), lambda i: (i - w0(i), 0, 0, 0)),
             pl.BlockSpec((BLK, ATT_W), lambda i: (i, Q0 // ATT_W))]
    for col in (K0 // KV_W, V0 // KV_W):
        specs += [pl.BlockSpec((BLK, KV_W), lambda i, c=col, o=o: (w0(i) + o, c)) for o in range(3)]
    return specs


def band_bias():
    off = jnp.arange(3).reshape(3, 1, 1, 1)
    row = jnp.arange(GQA_GROUP * BLK).reshape(1, 1, -1, 1)
    dist = jnp.abs(off * BLK + row % BLK - jnp.arange(3 * BLK).reshape(1, 1, 1, -1))
    slope = jnp.asarray(SLOPES, F32).reshape(1, KV_HEADS, GQA_GROUP, 1)
    slope = jnp.repeat(slope, BLK, axis=2)
    return jnp.where(dist <= BLK, -slope * dist.astype(F32), NEG)


def _stack_heads(x, j):
    return jnp.concatenate([x[:, h * HEAD_DIM:(h + 1) * HEAD_DIM] for h in range(j * GQA_GROUP, (j + 1) * GQA_GROUP)],
                           axis=0)


def _per_head(vals):
    head = lax.broadcasted_iota(jnp.int32, (GQA_GROUP * BLK, 1), 0) // BLK
    out = jnp.full((GQA_GROUP * BLK, 1), vals[GQA_GROUP - 1], F32)
    for g in range(GQA_GROUP - 2, -1, -1):
        out = jnp.where(head == g, vals[g], out)
    return out


def attn_fwd(z, bias, qg, kg, sink, name, xch=None):
    s = z.shape[0]
    nb = s // BLK

    def body(sink_ref, b_ref, q_ref, k0, k1, k2, v0, v1, v2, qg_ref, kg_ref, o_ref, lse_ref):
        k3 = jnp.concatenate([k0[...], k1[...], k2[...]], axis=0)
        v3 = jnp.concatenate([v0[...], v1[...], v2[...]], axis=0).astype(BF16)
        q = q_ref[...]
        for j in range(KV_HEADS):
            heads = range(j * GQA_GROUP, (j + 1) * GQA_GROUP)
            kj = k3[:, j * HEAD_DIM:(j + 1) * HEAD_DIM]
            knj = (kj * _rms(kj) * kg_ref[...]).astype(BF16)
            vj = v3[:, j * HEAD_DIM:(j + 1) * HEAD_DIM]
            q4 = _stack_heads(q, j)
            qs = (q4 * _rms(q4) * (qg_ref[...] * SCALE)).astype(BF16)
            sc = _dot_nt(qs, knj) + b_ref[j]
            sk = _per_head([sink_ref[h] for h in heads])
            m = jnp.maximum(jnp.max(sc, axis=-1, keepdims=True), sk)
            p = jnp.exp(sc - m)
            den = jnp.sum(p, axis=-1, keepdims=True) + jnp.exp(sk - m)
            o4 = _dot(p.astype(BF16), vj) * (1.0 / den)
            lse4 = m + jnp.log(den)
            for g, h in enumerate(heads):
                o_ref[:, h * HEAD_DIM:(h + 1) * HEAD_DIM] = o4[g * BLK:(g + 1) * BLK]
                lse_ref[:, h:h + 1] = lse4[g * BLK:(g + 1) * BLK]

    return _call(
        body, name, (nb,),
        [pl.BlockSpec(memory_space=pltpu.SMEM)] + _band_specs(nb)
        + [pl.BlockSpec((1, HEAD_DIM), _fixed), pl.BlockSpec((1, HEAD_DIM), _fixed)],
        [pl.BlockSpec((BLK, ATT_W), _row), pl.BlockSpec((BLK, ATT_HEADS), _row)],
        [jax.ShapeDtypeStruct((s, ATT_W), F32), jax.ShapeDtypeStruct((s, ATT_HEADS), F32)],
        ("arbitrary",), (sink, bias, z, z, z, z, z, z, z, qg, kg), xch=xch)


def attn_bwd(z, bias, att, datt, lse, qg, kg, sink, name, xch=None):
    s = z.shape[0]
    nb = s // BLK

    def body(sink_ref, b_ref, q_ref, k0, k1, k2, v0, v1, v2, o_ref, do_ref, lse_ref, qg_ref, kg_ref,
             dq_ref, dk_ref, dv_ref, dqg_ref, dkg_ref, dsk_ref):
        i = pl.program_id(0)

        @pl.when(i == 0)
        def _():
            dk_ref[...] = jnp.zeros_like(dk_ref)
            dv_ref[...] = jnp.zeros_like(dv_ref)
            dqg_ref[...] = jnp.zeros_like(dqg_ref)
            dkg_ref[...] = jnp.zeros_like(dkg_ref)
            dsk_ref[...] = jnp.zeros_like(dsk_ref)

        w0 = jnp.clip(i - 1, 0, nb - 3)
        k3 = jnp.concatenate([k0[...], k1[...], k2[...]], axis=0)
        v3 = jnp.concatenate([v0[...], v1[...], v2[...]], axis=0).astype(BF16)
        q = q_ref[...]
        o = o_ref[...]
        do = do_ref[...]
        lse = lse_ref[...]
        qgv = qg_ref[...]
        kgv = kg_ref[...]
        rows = pl.ds(pl.multiple_of(w0 * BLK, BLK), 3 * BLK)
        dqg = jnp.zeros((1, HEAD_DIM), F32)
        dkg = jnp.zeros((1, HEAD_DIM), F32)
        for j in range(KV_HEADS):
            heads = range(j * GQA_GROUP, (j + 1) * GQA_GROUP)
            cols = slice(j * HEAD_DIM, (j + 1) * HEAD_DIM)
            kj = k3[:, cols]
            rk = _rms(kj)
            khat = kj * rk
            knj = (khat * kgv).astype(BF16)
            vj = v3[:, cols]
            q4 = _stack_heads(q, j)
            rq = _rms(q4)
            qhat = q4 * rq
            qs = (qhat * (qgv * SCALE)).astype(BF16)
            sc = _dot_nt(qs, knj) + b_ref[j]
            lse4 = jnp.concatenate([lse[:, h:h + 1] for h in heads], axis=0)
            p = jnp.exp(sc - lse4)
            do4 = _stack_heads(do, j)
            delta = jnp.sum(do4 * _stack_heads(o, j), axis=-1, keepdims=True)
            dob = do4.astype(BF16)
            ds = p * (_dot_nt(dob, vj) - delta)
            sunk = jnp.exp(_per_head([sink_ref[h] for h in heads]) - lse4) * delta
            dsb = ds.astype(BF16)
            dvj = _dot_tn(p.astype(BF16), dob)
            dqn = _dot(dsb, knj) * SCALE
            dkn = _dot_tn(dsb, qs)
            dqg = dqg + jnp.sum(dqn * qhat, axis=0, keepdims=True)
            dq4 = _rms_bwd(qhat, rq, dqn * qgv)
            for g, h in enumerate(heads):
                dq_ref[:, h * HEAD_DIM:(h + 1) * HEAD_DIM] = dq4[g * BLK:(g + 1) * BLK]
                dsk_ref[:, h:h + 1] += -jnp.sum(sunk[g * BLK:(g + 1) * BLK], axis=0, keepdims=True)
            dkg = dkg + jnp.sum(dkn * khat, axis=0, keepdims=True)
            dk_ref[rows, cols] += _rms_bwd(khat, rk, dkn * kgv)
            dv_ref[rows, cols] += dvj
        dqg_ref[...] += dqg
        dkg_ref[...] += dkg

    return _call(
        body, name, (nb,),
        [pl.BlockSpec(memory_space=pltpu.SMEM)] + _band_specs(nb)
        + [pl.BlockSpec((BLK, ATT_W), _row), pl.BlockSpec((BLK, ATT_W), _row), pl.BlockSpec((BLK, ATT_HEADS), _row),
           pl.BlockSpec((1, HEAD_DIM), _fixed), pl.BlockSpec((1, HEAD_DIM), _fixed)],
        [pl.BlockSpec((BLK, ATT_W), _row), pl.BlockSpec((s, KV_W), _fixed), pl.BlockSpec((s, KV_W), _fixed),
         pl.BlockSpec((1, HEAD_DIM), _fixed), pl.BlockSpec((1, HEAD_DIM), _fixed), pl.BlockSpec((1, ATT_HEADS), _fixed)],
        [jax.ShapeDtypeStruct((s, ATT_W), F32), jax.ShapeDtypeStruct((s, KV_W), F32), jax.ShapeDtypeStruct((s, KV_W), F32),
         jax.ShapeDtypeStruct((1, HEAD_DIM), F32), jax.ShapeDtypeStruct((1, HEAD_DIM), F32),
         jax.ShapeDtypeStruct((1, ATT_HEADS), F32)],
        ("arbitrary",), (sink, bias, z, z, z, z, z, z, z, att, datt, lse, qg, kg), xch=xch)


def _group_steps(nc):
    nstep = nc // SEGS
    return nstep, min(32, nstep)


def _lanes4(ref):
    return jnp.concatenate([ref[q] for q in range(4)], axis=1)


def _pair_split(x4):
    return [jnp.concatenate([x4[:, q * 128 + r * SSM_P:q * 128 + (r + 1) * SSM_P] for q in range(4)], axis=1)
            for r in range(2)]


def _pair_merge(a0, a1):
    return [jnp.concatenate([a[:, q * SSM_P:(q + 1) * SSM_P] for a in (a0, a1)], axis=1) for q in range(4)]


def _block_transpose(vals, slot):
    vals = list(vals)
    for k in (4, 2, 1):
        low = (slot & k) == 0
        for i in range(8):
            if i & k:
                continue
            a, b = vals[i], vals[i + k]
            vals[i] = jnp.where(low, a, pltpu.roll(b, k * SSM_H, 1))
            vals[i + k] = jnp.where(low, pltpu.roll(a, 128 - k * SSM_H, 1), b)
    return vals


def to_groups(src, col, name):
    s, w = src.shape
    nc = s // CH
    nstep, sb = _group_steps(nc)

    def body(u0, u1, u2, u3, o_ref):
        slot = lax.broadcasted_iota(jnp.int32, (sb, 128), 1) // SSM_H
        for seg in range(SEGS):
            for vc, u_ref in enumerate((u0, u1, u2, u3)):
                for sh in range(2):
                    pieces = [u_ref[seg, pl.ds(sh * 8 + sl, sb, stride=CH), :] for sl in range(8)]
                    for gl, blk in enumerate(_block_transpose(pieces, slot)):
                        o_ref[(vc * 8 + gl) * 2 + sh, pl.ds(seg, sb, stride=SEGS), :] = blk

    src3 = src.reshape(SEGS, s // SEGS, w)
    (out,), _ = _call(body, name, (nstep // sb,),
                      [pl.BlockSpec((SEGS, sb * CH, 128), lambda i, c=col + vc: (0, i, c)) for vc in range(4)],
                      [pl.BlockSpec((2 * SSM_G, sb * SEGS, 128), lambda i: (0, i, 0))],
                      [jax.ShapeDtypeStruct((2 * SSM_G, nc, 128), F32)], ("parallel",), (src3,) * 4)
    return out


def from_groups(yc, name):
    nc = yc.shape[1]
    s = nc * CH
    nstep, sb = _group_steps(nc)

    def body(y_ref, o_ref):
        slot = lax.broadcasted_iota(jnp.int32, (sb, 128), 1) // SSM_H
        for seg in range(SEGS):
            for vc in range(4):
                for sh in range(2):
                    pieces = [y_ref[(vc * 8 + gl) * 2 + sh, pl.ds(seg, sb, stride=SEGS), :] for gl in range(8)]
                    for sl, blk in enumerate(_block_transpose(pieces, slot)):
                        o_ref[vc, seg, pl.ds(sh * 8 + sl, sb, stride=CH), :] = blk

    (out,), _ = _call(body, name, (nstep // sb,),
                      [pl.BlockSpec((2 * SSM_G, sb * SEGS, 128), lambda i: (0, i, 0))],
                      [pl.BlockSpec((4, SEGS, sb * CH, 128), lambda i: (0, 0, i, 0))],
                      [jax.ShapeDtypeStruct((4, SEGS, s // SEGS, 128), F32)], ("parallel",), (yc,))
    return out.reshape(4, s, 128)


def _pair3(i):
    return (i, 0, 0)


def _op_spec(l, rows, cols):
    return pl.BlockSpec((None, 2, rows, cols), lambda i: (l, i, 0, 0))


def _state_blk(i):
    return (0, 0, i)


def ssm_in(ug, e, l, name):
    nc = ug.shape[1]

    def body(u_ref, e_ref, s_ref):
        u = _lanes4(u_ref).astype(BF16)
        for q, blk in enumerate(_pair_merge(_dot(u[:, :GW], e_ref[0].astype(BF16)), _dot(u[:, GW:], e_ref[1].astype(BF16)))):
            s_ref[q] = blk

    (out,), _ = _call(body, name, (SSM_G // 2,),
                      [pl.BlockSpec((4, nc, 128), _pair3), _op_spec(l, GW, 4 * SSM_P)],
                      [pl.BlockSpec((4, nc, 128), _state_blk)],
                      [jax.ShapeDtypeStruct((4, nc, SSM_G * SSM_P), F32)], ("parallel",), (ug, e))
    return out


def chunk_scan(s4, a4, flip, name, xp4=None):
    _, nc, gp = s4.shape
    nstep = nc // SEGS
    assert nstep & (nstep - 1) == 0
    ct = 512
    with_da = xp4 is not None

    def body(*refs):
        if with_da:
            s_ref, a_ref, xp_ref, o_ref, da_ref = refs
        else:
            s_ref, a_ref, o_ref = refs
        rows = lax.broadcasted_iota(jnp.int32, (SEGS, ct), 0)
        zero = jnp.zeros((SEGS, ct), F32)
        for pair in range(2):
            asc = (pair == 0) != flip
            ir, ii = 2 * pair, 2 * pair + 1
            ar1 = a_ref[ir]
            ai1 = a_ref[ii]
            ar = jnp.broadcast_to(ar1, (SEGS, ct))
            ai = jnp.broadcast_to(ai1, (SEGS, ct))

            def tile(t):
                tt = t if asc else nstep - 1 - t
                return pl.ds(pl.multiple_of(tt * SEGS, SEGS), SEGS)

            def local(t, c):
                xr, xi = c
                sl = tile(t)
                return (ar * xr - ai * xi + s_ref[ir, sl, :], ar * xi + ai * xr + s_ref[ii, sl, :])

            er, ei = lax.fori_loop(0, nstep, local, (zero, zero))
            pr, pi = ar1, ai1
            for _ in range(nstep.bit_length() - 1):
                pr, pi = pr * pr - pi * pi, 2.0 * pr * pi
            cr = jnp.zeros((1, ct), F32)
            ci = jnp.zeros((1, ct), F32)
            xin_r, xin_i = zero, zero
            for k in range(SEGS):
                sg = k if asc else SEGS - 1 - k
                here = rows == sg
                xin_r = jnp.where(here, cr, xin_r)
                xin_i = jnp.where(here, ci, xin_i)
                lr = jnp.sum(jnp.where(here, er, 0.0), axis=0, keepdims=True)
                li = jnp.sum(jnp.where(here, ei, 0.0), axis=0, keepdims=True)
                cr, ci = pr * cr - pi * ci + lr, pr * ci + pi * cr + li

            def final(t, c):
                xr, xi, acr, aci = c
                sl = tile(t)
                o_ref[ir, sl, :] = xr
                o_ref[ii, sl, :] = xi
                if with_da:
                    br = xp_ref[ir, sl, :]
                    bi = xp_ref[ii, sl, :]
                    acr = acr + br * xr + bi * xi
                    aci = aci + br * xi - bi * xr
                return (ar * xr - ai * xi + s_ref[ir, sl, :], ar * xi + ai * xr + s_ref[ii, sl, :], acr, aci)

            _, _, acr, aci = lax.fori_loop(0, nstep, final, (xin_r, xin_i, zero, zero))
            if with_da:
                da_ref[ir] = jnp.sum(acr, axis=0, keepdims=True)
                da_ref[ii] = jnp.sum(aci, axis=0, keepdims=True)

    blk = pl.BlockSpec((4, nc, ct), _state_blk)
    ablk = pl.BlockSpec((4, 1, ct), _state_blk)
    sds = jax.ShapeDtypeStruct((4, nc, gp), F32)
    if with_da:
        out, _ = _call(body, name, (gp // ct,), [blk, ablk, blk], [blk, ablk],
                       [sds, jax.ShapeDtypeStruct((4, 1, gp), F32)], ("parallel",), (s4, a4, xp4))
        return out
    (out,), _ = _call(body, name, (gp // ct,), [blk, ablk], [blk], [sds], ("parallel",), (s4, a4))
    return out


def _state_cat(ref):
    return _lanes4(ref).astype(BF16)


def ssm_out(ug, t, xp4, o, l, name):
    nc = ug.shape[1]

    def body(u_ref, t_ref, xp_ref, o_ref, y_ref):
        xs = _pair_split(_lanes4(xp_ref))
        u = _lanes4(u_ref).astype(BF16)
        for r in range(2):
            y = _dot(u[:, r * GW:(r + 1) * GW], t_ref[r].astype(BF16)) + _dot(xs[r].astype(BF16), o_ref[r].astype(BF16))
            y_ref[2 * r] = y[:, :128]
            y_ref[2 * r + 1] = y[:, 128:]

    (out,), _ = _call(body, name, (SSM_G // 2,),
                      [pl.BlockSpec((4, nc, 128), _pair3), _op_spec(l, GW, GW),
                       pl.BlockSpec((4, nc, 128), _state_blk), _op_spec(l, 4 * SSM_P, GW)],
                      [pl.BlockSpec((4, nc, 128), _pair3)],
                      [jax.ShapeDtypeStruct((2 * SSM_G, nc, 128), F32)], ("parallel",), (ug, t, xp4, o))
    return out


def ssm_out_bwd(dyg, xp4, o, l, name):
    nc = dyg.shape[1]

    def body(dy_ref, xp_ref, o_ref, do_ref, dxp_ref):
        xs = _pair_split(_lanes4(xp_ref))
        dy = _lanes4(dy_ref).astype(BF16)
        dxs = []
        for r in range(2):
            dyr = dy[:, r * GW:(r + 1) * GW]
            do_ref[r] = _dot_tn(xs[r].astype(BF16), dyr)
            dxs.append(_dot_nt(dyr, o_ref[r].astype(BF16)))
        for q, blk in enumerate(_pair_merge(*dxs)):
            dxp_ref[q] = blk

    out, _ = _call(body, name, (SSM_G // 2,),
                   [pl.BlockSpec((4, nc, 128), _pair3), pl.BlockSpec((4, nc, 128), _state_blk),
                    _op_spec(l, 4 * SSM_P, GW)],
                   [pl.BlockSpec((2, 4 * SSM_P, GW), _pair3), pl.BlockSpec((4, nc, 128), _state_blk)],
                   [jax.ShapeDtypeStruct((SSM_G, 4 * SSM_P, GW), F32), jax.ShapeDtypeStruct((4, nc, SSM_G * SSM_P), F32)],
                   ("parallel",), (dyg, xp4, o))
    return out


def ssm_in_bwd(ug, dyg, ds4, t, e, l, name):
    nc = ug.shape[1]

    def body(u_ref, dy_ref, ds_ref, t_ref, e_ref, dt_ref, de_ref, du_ref):
        dss = _pair_split(_lanes4(ds_ref))
        u = _lanes4(u_ref).astype(BF16)
        dy = _lanes4(dy_ref).astype(BF16)
        for r in range(2):
            cols = slice(r * GW, (r + 1) * GW)
            ds = dss[r].astype(BF16)
            dt_ref[r] = _dot_tn(u[:, cols], dy[:, cols])
            de_ref[r] = _dot_tn(u[:, cols], ds)
            du = _dot_nt(dy[:, cols], t_ref[r].astype(BF16)) + _dot_nt(ds, e_ref[r].astype(BF16))
            du_ref[2 * r] = du[:, :128]
            du_ref[2 * r + 1] = du[:, 128:]

    out, _ = _call(body, name, (SSM_G // 2,),
                   [pl.BlockSpec((4, nc, 128), _pair3), pl.BlockSpec((4, nc, 128), _pair3),
                    pl.BlockSpec((4, nc, 128), _state_blk), _op_spec(l, GW, GW), _op_spec(l, GW, 4 * SSM_P)],
                   [pl.BlockSpec((2, GW, GW), _pair3), pl.BlockSpec((2, GW, 4 * SSM_P), _pair3),
                    pl.BlockSpec((4, nc, 128), _pair3)],
                   [jax.ShapeDtypeStruct((SSM_G, GW, GW), F32), jax.ShapeDtypeStruct((SSM_G, GW, 4 * SSM_P), F32),
                    jax.ShapeDtypeStruct((2 * SSM_G, nc, 128), F32)], ("parallel",), (ug, dyg, ds4, t, e))
    return out


def ssm_post_fwd(yc, z, dskip, wglu, name):
    s = yc.shape[1]

    def body(y_ref, u_ref, d_ref, w_ref, o_ref, yp_ref, g_ref):
        yp = _lanes4(y_ref) + d_ref[...] * u_ref[...]
        yp_ref[...] = yp
        gv = _dot(_gelu(yp).astype(BF16), w_ref[...])
        g_ref[...] = gv
        o_ref[...] = gv[:, :SSM_W] * _sigmoid(gv[:, SSM_W:])

    out, _ = _call(body, name, (s // TM,),
                   [pl.BlockSpec((4, TM, 128), lambda i: (0, i, 0)), pl.BlockSpec((TM, SSM_W), lambda i: (i, U0 // SSM_W)),
                    pl.BlockSpec((1, SSM_W), _fixed), pl.BlockSpec((SSM_W, 2 * SSM_W), _fixed)],
                   [pl.BlockSpec((TM, SSM_W), _row), pl.BlockSpec((TM, SSM_W), _row), pl.BlockSpec((TM, 2 * SSM_W), _row)],
                   [jax.ShapeDtypeStruct((s, SSM_W), F32), jax.ShapeDtypeStruct((s, SSM_W), F32),
                    jax.ShapeDtypeStruct((s, 2 * SSM_W), F32)], ("parallel",), (yc, z, dskip, wglu))
    return out


def ssm_post_bwd(dssm, gpre, ypre, z, dskip, wglu, name):
    s = dssm.shape[0]

    def body(do_ref, g_ref, yp_ref, u_ref, d_ref, w_ref, dy_ref, du_ref, dw_ref, dd_ref):
        @pl.when(pl.program_id(0) == 0)
        def _():
            dw_ref[...] = jnp.zeros_like(dw_ref)
            dd_ref[...] = jnp.zeros_like(dd_ref)

        gv = g_ref[...]
        val = gv[:, :SSM_W]
        sg = _sigmoid(gv[:, SSM_W:])
        do = do_ref[...]
        dg = jnp.concatenate([do * sg, do * val * sg * (1.0 - sg)], axis=1).astype(BF16)
        yp = yp_ref[...]
        dgl = _dot_nt(dg, w_ref[...])
        dw_ref[...] += _dot_tn(_gelu(yp).astype(BF16), dg)
        dyp = dgl * _gelu_grad(yp)
        dy_ref[...] = dyp
        du_ref[...] = dyp * d_ref[...]
        dd_ref[...] += jnp.sum(dyp * u_ref[...], axis=0, keepdims=True)

    out, _ = _call(body, name, (s // TM,),
                   [pl.BlockSpec((TM, SSM_W), _row), pl.BlockSpec((TM, 2 * SSM_W), _row), pl.BlockSpec((TM, SSM_W), _row),
                    pl.BlockSpec((TM, SSM_W), lambda i: (i, U0 // SSM_W)), pl.BlockSpec((1, SSM_W), _fixed),
                    pl.BlockSpec((SSM_W, 2 * SSM_W), _fixed)],
                   [pl.BlockSpec((TM, SSM_W), _row), pl.BlockSpec((TM, SSM_W), _row),
                    pl.BlockSpec((SSM_W, 2 * SSM_W), _fixed), pl.BlockSpec((1, SSM_W), _fixed)],
                   [jax.ShapeDtypeStruct((s, SSM_W), F32), jax.ShapeDtypeStruct((s, SSM_W), F32),
                    jax.ShapeDtypeStruct((SSM_W, 2 * SSM_W), F32), jax.ShapeDtypeStruct((1, SSM_W), F32)],
                   ("arbitrary",), (dssm, gpre, ypre, z, dskip, wglu))
    return out


def _toeplitz_select():
    row = lax.broadcasted_iota(jnp.int32, (GW, CH * GW), 0)
    col = lax.broadcasted_iota(jnp.int32, (GW, CH * GW), 1)
    j, h2 = row // SSM_H, row % SSM_H
    s, t, h = col // GW, (col % GW) // SSM_H, col % SSM_H
    same = h2 == h
    return jnp.concatenate([same & (t - s == j), same & (s - t == j)], axis=0).astype(F32)


def ssm_mats(lam_re, lam_im, log_dt, b_re, b_im, c_re, c_im):
    g, p, hh = SSM_G, SSM_P, SSM_H
    hp = lax.Precision.HIGHEST
    jj = jnp.arange(CH + 1, dtype=F32)
    dt = jnp.exp(log_dt)[..., None]
    mag = jnp.exp((lam_re * dt)[..., None] * jj)
    ang = (lam_im * dt)[..., None] * jj
    pr, pi = mag * jnp.cos(ang), mag * jnp.sin(ang)
    abr, abi = pr[..., 1], pi[..., 1]
    den = lam_re * lam_re + lam_im * lam_im
    zr = ((abr - 1.0) * lam_re + abi * lam_im) / den
    zi = (abi * lam_re - (abr - 1.0) * lam_im) / den
    bbr = zr[..., None] * b_re[None] - zi[..., None] * b_im[None]
    bbi = zr[..., None] * b_im[None] + zi[..., None] * b_re[None]
    crt, cit = c_re.transpose(0, 1, 3, 2), c_im.transpose(0, 1, 3, 2)
    car = pr[..., None] * crt[..., None, :] - pi[..., None] * cit[..., None, :]
    cai = pr[..., None] * cit[..., None, :] + pi[..., None] * crt[..., None, :]
    lhs = jnp.concatenate([bbr, -bbi], axis=2).transpose(0, 1, 3, 2)
    rhs = jnp.concatenate([car[..., :CH, :], cai[..., :CH, :]], axis=2).reshape(2, g, 2 * p, GW)
    kt = jnp.einsum("dgkp,dgpn->dgkn", lhs, rhs, precision=hp)
    kcat = jnp.concatenate([kt[0], kt[1]], axis=-1).reshape(g * hh, 2 * GW)
    tmat = jnp.dot(kcat, _toeplitz_select(), precision=hp)
    tmat = tmat.reshape(g, hh, CH, GW).transpose(0, 2, 1, 3).reshape(g, GW, GW)

    def e_part(d, pw_r, pw_i):
        pw_r, pw_i = pw_r.transpose(0, 2, 1)[:, :, None, :], pw_i.transpose(0, 2, 1)[:, :, None, :]
        br, bi = bbr[d].transpose(0, 2, 1)[:, None], bbi[d].transpose(0, 2, 1)[:, None]
        return [pw_r * br - pw_i * bi, pw_r * bi + pw_i * br]

    eparts = (e_part(0, pr[0, ..., :CH][..., ::-1], pi[0, ..., :CH][..., ::-1])
              + e_part(1, pr[1, ..., :CH], pi[1, ..., :CH]))
    emat = jnp.concatenate(eparts, axis=3).reshape(g, GW, 4 * p)

    oparts = [car[0, ..., 1:, :], -cai[0, ..., 1:, :], car[1, ..., 1:, :][..., ::-1, :], -cai[1, ..., 1:, :][..., ::-1, :]]
    omat = jnp.stack([v.reshape(g, p, GW) for v in oparts], axis=1).reshape(g, 4 * p, GW)
    amat = jnp.stack([pr[0, ..., CH], pi[0, ..., CH], pr[1, ..., CH], pi[1, ..., CH]], axis=0).reshape(4, 1, g * p)
    return tmat, emat, omat, amat


def outproj_fwd(x, att, ssm, wo, name):
    s, d = x.shape

    def body(x_ref, a_ref, s_ref, w_ref, o_ref):
        o_ref[...] = (x_ref[...] + _dot(a_ref[...].astype(BF16), w_ref[0:ATT_W, :])
                      + _dot(s_ref[...].astype(BF16), w_ref[ATT_W:, :]))

    (out,), _ = _call(body, name, (s // TM,),
                      [pl.BlockSpec((TM, d), _row), pl.BlockSpec((TM, ATT_W), _row), pl.BlockSpec((TM, SSM_W), _row),
                       pl.BlockSpec((ATT_W + SSM_W, d), _fixed)],
                      [pl.BlockSpec((TM, d), _row)], [jax.ShapeDtypeStruct((s, d), F32)], ("parallel",),
                      (x, att, ssm, wo))
    return out


def outproj_bwd(dx1, att, ssm, wo, name):
    s, d = dx1.shape

    def body(dx_ref, a_ref, s_ref, w_ref, da_ref, ds_ref, dw_ref):
        @pl.when(pl.program_id(0) == 0)
        def _():
            dw_ref[...] = jnp.zeros_like(dw_ref)

        dxb = dx_ref[...].astype(BF16)
        da_ref[...] = _dot_nt(dxb, w_ref[0:ATT_W, :])
        ds_ref[...] = _dot_nt(dxb, w_ref[ATT_W:, :])
        dw_ref[0:ATT_W, :] += _dot_tn(a_ref[...].astype(BF16), dxb)
        dw_ref[ATT_W:, :] += _dot_tn(s_ref[...].astype(BF16), dxb)

    out, _ = _call(body, name, (s // TM,),
                   [pl.BlockSpec((TM, d), _row), pl.BlockSpec((TM, ATT_W), _row), pl.BlockSpec((TM, SSM_W), _row),
                    pl.BlockSpec((ATT_W + SSM_W, d), _fixed)],
                   [pl.BlockSpec((TM, ATT_W), _row), pl.BlockSpec((TM, SSM_W), _row),
                    pl.BlockSpec((ATT_W + SSM_W, d), _fixed)],
                   [jax.ShapeDtypeStruct((s, ATT_W), F32), jax.ShapeDtypeStruct((s, SSM_W), F32),
                    jax.ShapeDtypeStruct((ATT_W + SSM_W, d), F32)], ("arbitrary",), (dx1, att, ssm, wo))
    return out


def ffn_fwd(x1, gain, w1, w2, name, xch=None):
    s, d = x1.shape
    nch, _, fc = w1.shape

    kc = 2 if nch % 2 == 0 else 1

    def body(x_ref, g_ref, w1_ref, w2_ref, o_ref, h_ref, a_ref):
        @pl.when(pl.program_id(1) == 0)
        def _():
            xv = x_ref[...]
            h_ref[...] = (xv * _rms(xv) * g_ref[...]).astype(BF16)
            o_ref[...] = xv

        h = h_ref[...]
        acc = None
        for c in range(kc):
            a = _dot(h, w1_ref[c])
            a_ref[:, c * fc:(c + 1) * fc] = a.astype(BF16)
            t = _dot(jnp.square(jnp.maximum(a, 0.0)).astype(BF16), w2_ref[c])
            acc = t if acc is None else acc + t
        o_ref[...] += acc

    tm = min(TM_FFN, s)
    return _call(
        body, name, (s // tm, nch // kc),
        [pl.BlockSpec((tm, d), lambda i, k: (i, 0)), pl.BlockSpec((1, d), lambda i, k: (0, 0)),
         pl.BlockSpec((kc, d, fc), lambda i, k: (k, 0, 0)), pl.BlockSpec((kc, fc, d), lambda i, k: (k, 0, 0))],
        [pl.BlockSpec((tm, d), lambda i, k: (i, 0)), pl.BlockSpec((tm, d), lambda i, k: (i, 0)),
         pl.BlockSpec((tm, kc * fc), lambda i, k: (i, k))],
        [jax.ShapeDtypeStruct((s, d), F32), jax.ShapeDtypeStruct((s, d), BF16), jax.ShapeDtypeStruct((s, nch * fc), BF16)],
        ("arbitrary", "arbitrary"), (x1, gain, w1, w2), xch=xch)


def ffn_bwd_tok(dx2, x1, gain, a, w1, w2, name, xch=None):
    s, d = x1.shape
    nch, _, fc = w1.shape

    def body(dx_ref, x_ref, g_ref, a_ref, w1_ref, w2_ref, da_ref, dx1_ref, dg_ref, dxb_ref, dh_ref):
        i = pl.program_id(0)
        k = pl.program_id(1)

        @pl.when(jnp.logical_and(i == 0, k == 0))
        def _():
            dg_ref[...] = jnp.zeros_like(dg_ref)

        @pl.when(k == 0)
        def _():
            dxb_ref[...] = dx_ref[...].astype(BF16)
            dh_ref[...] = jnp.zeros_like(dh_ref)

        dr = _dot_nt(dxb_ref[...], w2_ref[...])
        da = (dr * (2.0 * jnp.maximum(a_ref[...].astype(F32), 0.0))).astype(BF16)
        da_ref[...] = da
        dh_ref[...] += _dot_nt(da, w1_ref[...])

        @pl.when(k == nch - 1)
        def _():
            xv = x_ref[...]
            r = _rms(xv)
            xhat = xv * r
            dh = dh_ref[...]
            dg_ref[...] += jnp.sum(dh * xhat, axis=0, keepdims=True)
            dx1_ref[...] = dx_ref[...] + _rms_bwd(xhat, r, dh * g_ref[...])

    tm = min(TM_FFN, s)
    return _call(
        body, name, (s // tm, nch),
        [pl.BlockSpec((tm, d), lambda i, k: (i, 0)), pl.BlockSpec((tm, d), lambda i, k: (i, 0)),
         pl.BlockSpec((1, d), lambda i, k: (0, 0)), pl.BlockSpec((tm, fc), lambda i, k: (i, k)),
         pl.BlockSpec((None, d, fc), lambda i, k: (k, 0, 0)), pl.BlockSpec((None, fc, d), lambda i, k: (k, 0, 0))],
        [pl.BlockSpec((tm, fc), lambda i, k: (i, k)), pl.BlockSpec((tm, d), lambda i, k: (i, 0)),
         pl.BlockSpec((1, d), lambda i, k: (0, 0)), pl.BlockSpec((tm, d), lambda i, k: (i, 0))],
        [jax.ShapeDtypeStruct((s, nch * fc), BF16), jax.ShapeDtypeStruct((s, d), F32),
         jax.ShapeDtypeStruct((1, d), F32), jax.ShapeDtypeStruct((s, d), BF16)],
        ("arbitrary", "arbitrary"), (dx2, x1, gain, a, w1, w2), scratch=[pltpu.VMEM((tm, d), F32)], xch=xch)


def ffn_bwd_w(h2, da, a, dxb, nch, name, xch=None):
    s, d = h2.shape
    fc = a.shape[1] // nch
    tm = min(TM_FFN, s)

    def body(h_ref, da_ref, a_ref, dx_ref, dw1_ref, dw2_ref, acc1_ref, acc2_ref):
        t = pl.program_id(1)

        @pl.when(t == 0)
        def _():
            acc1_ref[...] = jnp.zeros_like(acc1_ref)
            acc2_ref[...] = jnp.zeros_like(acc2_ref)

        acc1_ref[...] += _dot_tn(h_ref[...], da_ref[...])
        r = jnp.square(jnp.maximum(a_ref[...].astype(F32), 0.0)).astype(BF16)
        acc2_ref[...] += _dot_tn(r, dx_ref[...])

        @pl.when(t == s // tm - 1)
        def _():
            dw1_ref[...] = acc1_ref[...].astype(BF16)
            dw2_ref[...] = acc2_ref[...].astype(BF16)

    return _call(
        body, name, (nch, s // tm),
        [pl.BlockSpec((tm, d), lambda k, t: (t, 0)), pl.BlockSpec((tm, fc), lambda k, t: (t, k)),
         pl.BlockSpec((tm, fc), lambda k, t: (t, k)), pl.BlockSpec((tm, d), lambda k, t: (t, 0))],
        [pl.BlockSpec((None, d, fc), lambda k, t: (k, 0, 0)), pl.BlockSpec((None, fc, d), lambda k, t: (k, 0, 0))],
        [jax.ShapeDtypeStruct((nch, d, fc), BF16), jax.ShapeDtypeStruct((nch, fc, d), BF16)],
        ("arbitrary", "arbitrary"), (h2, da, a, dxb), scratch=[pltpu.VMEM((d, fc), F32), pltpu.VMEM((fc, d), F32)],
        xch=xch)


def loss_grad(xf, tgt, name):
    s, d = xf.shape
    nt = s // TM

    def body(x_ref, t_ref, dx_ref, l_ref, acc_ref):
        i = pl.program_id(0)

        @pl.when(i == 0)
        def _():
            acc_ref[...] = jnp.zeros_like(acc_ref)

        e = x_ref[...] - t_ref[...]
        dx_ref[...] = e * (1.0 / d)
        acc_ref[...] += jnp.sum(e * e, axis=0, keepdims=True)

        @pl.when(i == nt - 1)
        def _():
            l_ref[...] = jnp.sum(acc_ref[...], axis=1, keepdims=True) * (0.5 / d)

    out, _ = _call(body, name, (nt,), [pl.BlockSpec((TM, d), _row), pl.BlockSpec((TM, d), _row)],
                   [pl.BlockSpec((TM, d), _row), pl.BlockSpec((1, 1), _fixed)],
                   [jax.ShapeDtypeStruct((s, d), F32), jax.ShapeDtypeStruct((1, 1), F32)], ("arbitrary",),
                   (xf, tgt), scratch=[pltpu.VMEM((1, d), F32)])
    return out


def adamw_sum(parts, w, m, v, br, name, xch=None):
    nl = len(parts)
    npart, r, c = parts[0].shape
    nb = r // br
    c1 = 1.0 - ADAM_B1 ** ADAM_STEP
    c2 = 1.0 - ADAM_B2 ** ADAM_STEP

    def body(*refs):
        p_refs = refs[:nl]
        w_ref, m_ref, v_ref, g_ref, d_ref, nm_ref, nv_ref = refs[nl:]
        for l in range(nl):
            @pl.when(pl.program_id(0) == l)
            def _(l=l):
                g = p_refs[l][0].astype(F32)
                for j in range(1, npart):
                    g = g + p_refs[l][j].astype(F32)
                m2 = ADAM_B1 * m_ref[...] + (1.0 - ADAM_B1) * g
                v2 = ADAM_B2 * v_ref[...] + (1.0 - ADAM_B2) * jnp.square(g)
                g_ref[...] = g
                nm_ref[...] = m2
                nv_ref[...] = v2
                d_ref[...] = -ADAM_LR * ((m2 / c1) / (jnp.sqrt(v2 / c2) + ADAM_EPS) + ADAM_WD * w_ref[...])

    blk = pl.BlockSpec((br, c), lambda l, i: (l * nb + i, 0))
    pspecs = [pl.BlockSpec((npart, br, c), lambda l, i, own=own: (0, jnp.where(l == own, i, 0), 0)) for own in range(nl)]
    sds = jax.ShapeDtypeStruct((nl * r, c), F32)
    return _call(body, name, (nl, nb), pspecs + [blk, blk, blk], [blk, blk, blk, blk], [sds, sds, sds, sds],
                 ("arbitrary", "arbitrary"), (*parts, w, m, v), xch=xch)


def sum_parts(parts, name):
    n, r, c = parts.shape

    def body(p_ref, o_ref):
        g = p_ref[0]
        for j in range(1, n):
            g = g + p_ref[j]
        o_ref[...] = g

    (out,), _ = _call(body, name, (1,), [pl.BlockSpec((n, r, c), lambda i: (0, 0, 0))], [pl.BlockSpec((r, c), _fixed)],
                      [jax.ShapeDtypeStruct((r, c), F32)], ("arbitrary",), (parts,))
    return out


def layer_fwd(x, w_in, p, mats, l, late, nxt):
    tmat, emat, omat, amat = mats
    z = norm_matmul(x, p["norm1"], w_in, f"in_fwd{l}")
    (att, lse), got = attn_fwd(z, p["bias"], p["q_gain"], p["k_gain"], p["sink"], f"attn_fwd{l}", (late, False))
    p = dict(p, **_whole_weights(dict(zip(BIG[-len(got):], got))))
    ug = to_groups(z, U0 // 128, f"ssm_to_groups{l}")
    s4 = ssm_in(ug, emat, l, f"ssm_in{l}")
    xp4 = chunk_scan(s4, amat, False, f"ssm_scan{l}")
    yc = from_groups(ssm_out(ug, tmat, xp4, omat, l, f"ssm_out{l}"), f"ssm_from_groups{l}")
    ssm, ypre, gpre = ssm_post_fwd(yc, z, p["d_skip"], p["w_glu"], f"ssm_post{l}")
    x1 = outproj_fwd(x, att, ssm, p["w_out"], f"out_fwd{l}")
    (x2, h2, a), gathered = ffn_fwd(x1, p["norm2"], p["w_ff1"], p["w_ff2"], f"ffn_fwd{l}",
                                    None if nxt is None else (nxt, False))
    saved = dict(x=x, z=z, att=att, lse=lse, ug=ug, xp4=xp4, ssm=ssm, ypre=ypre, gpre=gpre, x1=x1, h2=h2, a=a)
    return x2, saved, dict(p, w_in=w_in), gathered


def layer_bwd(dx2, p, mats, sv, l, above):
    tmat, emat, omat, amat = mats
    nch = p["w_ff1"].shape[0]
    last = l == 0
    (da, dx1, dnorm2, dxb), got_tok = ffn_bwd_tok(dx2, sv["x1"], p["norm2"], sv["a"], p["w_ff1"], p["w_ff2"],
                                                  f"ffn_bwd{l}", None if above is None else (above["tok"], True))
    (dw1, dw2), got_w = ffn_bwd_w(sv["h2"], da, sv["a"], dxb, nch, f"ffn_bwdw{l}",
                                  None if above is None else (above["w"], True))
    datt, dssm, dwo = outproj_bwd(dx1, sv["att"], sv["ssm"], p["w_out"], f"out_bwd{l}")
    dyc, du_skip, dwglu, ddskip = ssm_post_bwd(dssm, sv["gpre"], sv["ypre"], sv["z"], p["d_skip"], p["w_glu"],
                                               f"ssm_post_bwd{l}")
    dyg = to_groups(dyc, 0, f"ssm_to_groups_bwd{l}")
    domat, dxp4 = ssm_out_bwd(dyg, sv["xp4"], omat, l, f"ssm_out_bwd{l}")
    aconj = amat * jnp.array([1.0, -1.0, 1.0, -1.0], F32).reshape(4, 1, 1)
    ds4, damat = chunk_scan(dxp4, aconj, True, f"ssm_scan_bwd{l}", xp4=sv["xp4"])
    dtmat, demat, dug = ssm_in_bwd(sv["ug"], dyg, ds4, tmat, emat, l, f"ssm_in_bwd{l}")
    du_core = from_groups(dug, f"ssm_from_groups_bwd{l}")
    (dq, dk, dv, dqg, dkg, dsink), got_ff = attn_bwd(sv["z"], p["bias"], sv["att"], datt, sv["lse"], p["q_gain"], p["k_gain"],
                                                     p["sink"], f"attn_bwd{l}", ([dw1, dw2] if last else [dw1], True))
    blocks = _grad_blocks(dict(w_glu=dwglu, w_out=dwo))
    (dx, dwin, dnorm1), got_mix = in_bwd(sv["x"], p["norm1"], p["w_in"], du_skip, du_core, dq, dk, dv, dx1, f"in_bwd{l}",
                                         ([blocks["w_glu"], blocks["w_out"]], True) if last else None)
    grads = dict(norm1=dnorm1, q_gain=dqg, k_gain=dkg, sink=dsink, d_skip=ddskip, norm2=dnorm2)
    win_blocks = _grad_blocks(dict(w_in=dwin))["w_in"]
    mine = dict(w_ff1=got_ff[0])
    if last:
        mine.update(w_ff2=got_ff[1], w_glu=got_mix[0], w_out=got_mix[1])
        below = win_blocks
    else:
        below = dict(tok=[win_blocks, blocks["w_glu"], blocks["w_out"]], w=[dw2])
    theirs = {} if above is None else dict(w_in=got_tok[0], w_glu=got_tok[1], w_out=got_tok[2], w_ff2=got_w[0])
    return dx, grads, (dtmat, demat, domat, damat), mine, theirs, below


def _whole_weights(gathered):
    out = {}
    for n, g in gathered.items():
        if n == "w_in":
            w_in = g.transpose(1, 0, 2).reshape(D_MODEL, IN_W)
            out[n] = jnp.concatenate([w_in[:, V_END:], w_in[:, :V_END]], axis=1)
        elif n == "w_glu":
            out[n] = g.transpose(1, 0, 2).reshape(SSM_W, 2 * SSM_W)
        elif n == "w_out":
            out[n] = g.reshape(ATT_W + SSM_W, D_MODEL)
        else:
            out[n] = g
    return out


def _grad_blocks(grads):
    out = {}
    for n, g in grads.items():
        g = g.astype(BF16)
        if n == "w_in":
            g = jnp.concatenate([g[:, IN_W - V_END:], g[:, :IN_W - V_END]], axis=1)
            out[n] = g.reshape(D_MODEL, N_DEV, IN_W // N_DEV).transpose(1, 0, 2)
        elif n == "w_glu":
            out[n] = g.reshape(SSM_W, N_DEV, 2 * SSM_W // N_DEV).transpose(1, 0, 2)
        elif n == "w_out":
            out[n] = g.reshape(N_DEV, (ATT_W + SSM_W) // N_DEV, D_MODEL)
        else:
            out[n] = g
    return out


def _small_layout(like):
    layout, row = {}, 0
    for n in SMALL:
        size = int(np.prod(like[n].shape))
        nrow = -(-size // (8 * 128)) * 8
        layout[n] = (row, nrow, size)
        row += nrow
    return layout, -(-row // (8 * N_DEV)) * 8 * N_DEV


def _pack_small(vals, layout, rows):
    pieces, used = [], 0
    for n in SMALL:
        _, nrow, size = layout[n]
        flat = vals[n].reshape(-1).astype(F32)
        pieces.append(jnp.pad(flat, (0, nrow * 128 - size)).reshape(nrow, 128))
        used += nrow
    if rows > used:
        pieces.append(jnp.zeros((rows - used, 128), F32))
    return jnp.concatenate(pieces, axis=0)


def _unpack_small(packed, like, layout):
    out = {}
    for n in SMALL:
        row, nrow, size = layout[n]
        out[n] = packed[row:row + nrow].reshape(-1)[:size].reshape(like[n].shape)
    return out


def kernel(x, norm1, w_in, q_gain, k_gain, sink, lam_re, lam_im, log_dt, b_re, b_im, c_re, c_im, d_skip, w_glu, w_out, norm2, w_ff1, w_ff2, loss_target, m_norm1, m_w_in, m_q_gain, m_k_gain, m_sink, m_lam_re, m_lam_im, m_log_dt, m_b_re, m_b_im, m_c_re, m_c_im, m_d_skip, m_w_glu, m_w_out, m_norm2, m_w_ff1, m_w_ff2, v_norm1, v_w_in, v_q_gain, v_k_gain, v_sink, v_lam_re, v_lam_im, v_log_dt, v_b_re, v_b_im, v_c_re, v_c_im, v_d_skip, v_w_glu, v_w_out, v_norm2, v_w_ff1, v_w_ff2):
    w = dict(norm1=norm1, w_in=w_in, q_gain=q_gain, k_gain=k_gain, sink=sink, lam_re=lam_re, lam_im=lam_im,
             log_dt=log_dt, b_re=b_re, b_im=b_im, c_re=c_re, c_im=c_im, d_skip=d_skip, w_glu=w_glu, w_out=w_out,
             norm2=norm2, w_ff1=w_ff1, w_ff2=w_ff2)
    m = dict(norm1=m_norm1, w_in=m_w_in, q_gain=m_q_gain, k_gain=m_k_gain, sink=m_sink, lam_re=m_lam_re,
             lam_im=m_lam_im, log_dt=m_log_dt, b_re=m_b_re, b_im=m_b_im, c_re=m_c_re, c_im=m_c_im, d_skip=m_d_skip,
             w_glu=m_w_glu, w_out=m_w_out, norm2=m_norm2, w_ff1=m_w_ff1, w_ff2=m_w_ff2)
    v = dict(norm1=v_norm1, w_in=v_w_in, q_gain=v_q_gain, k_gain=v_k_gain, sink=v_sink, lam_re=v_lam_re,
             lam_im=v_lam_im, log_dt=v_log_dt, b_re=v_b_re, b_im=v_b_im, c_re=v_c_re, c_im=v_c_im, d_skip=v_d_skip,
             w_glu=v_w_glu, w_out=v_w_out, norm2=v_norm2, w_ff1=v_w_ff1, w_ff2=v_w_ff2)
    nl = w_in.shape[0]
    shards = [[w[n][l].astype(BF16) for n in BIG] for l in range(nl)]
    (tmat, emat, omat, amat), mats_vjp = jax.vjp(jax.vmap(ssm_mats), *[w[n] for n in S5])

    (g_in,) = exchange(shards[0][:1], False, "gather_w_in0")
    have = _whole_weights(dict(w_in=g_in))
    late = shards[0][1:]
    xs = x[0]
    bias = band_bias()
    saved, lp, lm = [], [], []
    for l in range(nl):
        p = {n: have[n] for n in have if n != "w_in"}
        for n in ("norm1", "q_gain", "k_gain", "d_skip", "norm2"):
            p[n] = w[n][l].reshape(1, -1)
        p["sink"] = sink[l]
        p["bias"] = bias
        mats = (tmat, emat, omat, amat[l])
        xs, sv, p, got = layer_fwd(xs, have["w_in"], p, mats, l, late, shards[l + 1][:4] if l + 1 < nl else None)
        if l + 1 < nl:
            have = _whole_weights(dict(zip(BIG[:4], got)))
            late = shards[l + 1][4:]
        saved.append(sv)
        lp.append(p)
        lm.append(mats)
    dx, loss_part = loss_grad(xs, loss_target[0], "loss")
    loss = lax.psum(loss_part[0, 0], ("x", "y", "c"))

    grads, dmats, parts = [None] * nl, [None] * nl, [dict() for _ in range(nl)]
    above = None
    for l in reversed(range(nl)):
        dx, grads[l], dmats[l], mine, theirs, above = layer_bwd(dx, lp[l], lm[l], saved[l], l, above)
        parts[l].update(mine)
        if l + 1 < nl:
            parts[l + 1].update(theirs)
    out_g, out_d, out_m, out_v = {}, {}, {}, {}

    def adamw_big(n, xch=None):
        c = w[n].shape[-1]
        r = int(np.prod(w[n].shape[:-1]))
        res, got = adamw_sum([parts[l][n] for l in range(nl)], w[n].reshape(r, c), m[n].reshape(r, c),
                             v[n].reshape(r, c), ADAM_ROWS[c], f"adamw_{n}", xch)
        out_g[n], out_d[n], out_m[n], out_v[n] = (t.reshape(w[n].shape) for t in res)
        return got

    (parts[0]["w_in"],) = adamw_big("w_ff1", ([above], True))

    gs = {n: jnp.stack([grads[l][n].reshape(w[n].shape[1:]) for l in range(nl)])
          for n in ("norm1", "q_gain", "k_gain", "sink", "d_skip", "norm2")}
    ds5 = mats_vjp(tuple(jnp.stack([dmats[l][i] for l in range(nl)]) for i in range(4)))
    gs.update(zip(S5, ds5))
    layout, rows = _small_layout(w)
    (mine,) = adamw_big("w_ff2", ([_pack_small(gs, layout, rows).reshape(N_DEV, rows // N_DEV, 128)], True))
    (small_sum,) = exchange([sum_parts(mine, "sum_small_grads")], False, "gather_small_grads")
    for n in ("w_in", "w_glu", "w_out"):
        adamw_big(n)
    res, _ = adamw_sum([small_sum.reshape(1, rows, 128)], _pack_small(w, layout, rows), _pack_small(m, layout, rows),
                       _pack_small(v, layout, rows), rows // N_DEV, "adamw_small")
    for dst, packed in zip((out_g, out_d, out_m, out_v), res):
        dst.update(_unpack_small(packed, w, layout))

    return (loss, dx[None], *[out_g[n] for n in WEIGHTS], *[out_d[n] for n in WEIGHTS],
            *[out_m[n] for n in WEIGHTS], *[out_v[n] for n in WEIGHTS])
```
